```python
import jax, jax.numpy as jnp
from jax import lax
import numpy as np

D_MODEL = 2048
BATCH = 4
SEQ = 8192
DEPTH = 2
DEC_BATCH = 32
DEC_SEQ = 16
PAST_LEN = 2048

CHUNK = 64
HEAD_DIM = 64
H_A = 8
W_A = H_A * HEAD_DIM
N_PREV_CHUNKS = 8
WINDOW_A = N_PREV_CHUNKS * CHUNK
BAND = WINDOW_A + CHUNK
REL_CLIP = 128
H_B = 8
W_B = H_B * HEAD_DIM
Q_BLOCK = 128
C_CONV = D_MODEL // 2
CONV_W = 31
N_BRANCH = 3
N_GROUPS = 4
EXPERTS_PER_GROUP = 8
N_EXPERTS = N_GROUPS * EXPERTS_PER_GROUP
TOP_K = 2
D_EXPERT = D_MODEL // 8
EXPERT_BLOCK = 128
IN_SIZES = (W_A, W_A, W_A, W_B, W_B, W_B, H_B, 2 * C_CONV, N_BRANCH * D_MODEL)
N_IN = sum(IN_SIZES)
FORGET_COL = 3 * W_A + 3 * W_B
SCALE = HEAD_DIM ** -0.5
EPS = 1e-6
NEG_INF = -1e30

kernel_name = "hybrid_streaming_encoder_step"


def rmsnorm(x, g):
    xf = x.astype(jnp.float32)
    xf = xf * lax.rsqrt(jnp.mean(xf * xf, axis=-1, keepdims=True) + EPS)
    return (xf * g.astype(jnp.float32)).astype(x.dtype)


def layernorm(x, g, b):
    xf = x.astype(jnp.float32)
    mu = jnp.mean(xf, axis=-1, keepdims=True)
    var = jnp.mean(jnp.square(xf - mu), axis=-1, keepdims=True)
    y = (xf - mu) * lax.rsqrt(var + EPS) * g.astype(jnp.float32) + b.astype(jnp.float32)
    return y.astype(x.dtype)


def project_mixers(h, w_in, b_in):
    z = h @ w_in + b_in
    qa, ka, va, qb, kb, vb, zf, zc, zg = jnp.split(z, np.cumsum(IN_SIZES)[:-1].tolist(), axis=-1)
    heads = lambda t, n: t.reshape(*t.shape[:-1], n, HEAD_DIM)
    logf = jax.nn.log_sigmoid(zf.astype(jnp.float32))
    val, gate = jnp.split(zc, 2, axis=-1)
    u = val * jax.nn.sigmoid(gate)
    gates = jax.nn.sigmoid(zg).reshape(*zg.shape[:-1], N_BRANCH, D_MODEL)
    return (heads(qa, H_A), heads(ka, H_A), heads(va, H_A),
            heads(qb, H_B), heads(kb, H_B), heads(vb, H_B), logf, u, gates)


def rel_bias_lookup(rel_bias, rel):
    return rel_bias.astype(jnp.float32)[:, jnp.clip(rel, -REL_CLIP, REL_CLIP) + REL_CLIP]


def band_attention_prompt(q, k, v, rel_bias):
    B, T, H, Dh = q.shape
    nc = T // CHUNK
    pad = ((0, 0), (WINDOW_A, 0), (0, 0), (0, 0))
    kp = jnp.pad(k, pad).reshape(B, nc + N_PREV_CHUNKS, CHUNK, H, Dh)
    vp = jnp.pad(v, pad).reshape(B, nc + N_PREV_CHUNKS, CHUNK, H, Dh)
    k_band = jnp.concatenate([kp[:, o:o + nc] for o in range(N_PREV_CHUNKS + 1)], axis=2)
    v_band = jnp.concatenate([vp[:, o:o + nc] for o in range(N_PREV_CHUNKS + 1)], axis=2)
    qc = q.reshape(B, nc, CHUNK, H, Dh)
    s = jnp.einsum('bcqhd,bckhd->bchqk', qc, k_band).astype(jnp.float32) * SCALE
    rel = jnp.arange(CHUNK)[:, None] + WINDOW_A - jnp.arange(BAND)[None, :]
    bias = rel_bias_lookup(rel_bias, rel)
    key_pos = jnp.arange(nc)[:, None] * CHUNK - WINDOW_A + jnp.arange(BAND)[None, :]
    s = jnp.where((key_pos >= 0)[None, :, None, None, :], s + bias[None, None], NEG_INF)
    p = jax.nn.softmax(s, axis=-1).astype(v.dtype)
    o = jnp.einsum('bchqk,bckhd->bcqhd', p, v_band)
    return o.reshape(B, T, H * Dh)


def band_attention_sample(q, k, v, k_cache, v_cache, rel_bias):
    B, S, H, Dh = q.shape
    Lc = k_cache.shape[1]
    kk = jnp.concatenate([k_cache, k], axis=1)
    vv = jnp.concatenate([v_cache, v], axis=1)
    s = jnp.einsum('bqhd,bkhd->bhqk', q, kk).astype(jnp.float32) * SCALE
    rel = Lc + jnp.arange(S)[:, None] - jnp.arange(Lc + S)[None, :]
    p = jax.nn.softmax(s + rel_bias_lookup(rel_bias, rel)[None], axis=-1).astype(vv.dtype)
    o = jnp.einsum('bhqk,bkhd->bqhd', p, vv)
    return o.reshape(B, S, H * Dh)


def forgetting_attention_prompt(q, k, v, logf):
    B, T, H, Dh = q.shape
    nb = T // Q_BLOCK
    F = jnp.cumsum(logf, axis=1).transpose(0, 2, 1)
    qb = q.reshape(B, nb, Q_BLOCK, H, Dh).transpose(1, 0, 2, 3, 4)
    Fq = F.reshape(B, H, nb, Q_BLOCK).transpose(2, 0, 1, 3)
    kpos = jnp.arange(T)

    def block(args):
        qblk, fblk, i = args
        s = jnp.einsum('bqhd,bkhd->bhqk', qblk, k).astype(jnp.float32) * SCALE
        s = s + (fblk[..., :, None] - F[:, :, None, :])
        qpos = i * Q_BLOCK + jnp.arange(Q_BLOCK)
        s = jnp.where(kpos[None, :] <= qpos[:, None], s, NEG_INF)
        p = jax.nn.softmax(s, axis=-1).astype(v.dtype)
        return jnp.einsum('bhqk,bkhd->bqhd', p, v)

    o = lax.map(block, (qb, Fq, jnp.arange(nb)))
    return o.transpose(1, 0, 2, 3, 4).reshape(B, T, H * Dh)


def forgetting_attention_sample(q, k, v, logf, k_cache, v_cache, logf_cache):
    B, S, H, Dh = q.shape
    P = k_cache.shape[1]
    kk = jnp.concatenate([k_cache, k], axis=1)
    vv = jnp.concatenate([v_cache, v], axis=1)
    F = jnp.cumsum(jnp.concatenate([logf_cache.astype(jnp.float32), logf], axis=1), axis=1)
    F = F.transpose(0, 2, 1)
    s = jnp.einsum('bqhd,bkhd->bhqk', q, kk).astype(jnp.float32) * SCALE
    s = s + (F[:, :, P:, None] - F[:, :, None, :])
    mask = jnp.arange(P + S)[None, :] <= (P + jnp.arange(S))[:, None]
    p = jax.nn.softmax(jnp.where(mask, s, NEG_INF), axis=-1).astype(vv.dtype)
    o = jnp.einsum('bhqk,bkhd->bqhd', p, vv)
    return o.reshape(B, S, H * Dh)


def conv_module(u_pad, conv_w, conv_b, ln_g, ln_b, w_proj_c):
    c = lax.conv_general_dilated(u_pad, conv_w.astype(u_pad.dtype)[:, None, :], window_strides=(1,),
                                 padding='VALID', dimension_numbers=('NWC', 'WIO', 'NWC'),
                                 feature_group_count=C_CONV)
    c = jax.nn.silu(layernorm(c + conv_b, ln_g, ln_b))
    return c @ w_proj_c


def merge(gates, ya, yb, yc, w_out):
    mixed = gates[..., 0, :] * ya + gates[..., 1, :] * yb + gates[..., 2, :] * yc
    return mixed @ w_out


def mixer_prompt(h, lw):
    qa, ka, va, qb, kb, vb, logf, u, gates = project_mixers(h, lw['w_in'], lw['b_in'])
    ya = band_attention_prompt(qa, ka, va, lw['rel_bias']) @ lw['w_proj_a']
    yb = forgetting_attention_prompt(qb, kb, vb, logf) @ lw['w_proj_b']
    u_pad = jnp.pad(u, ((0, 0), (CONV_W - 1, 0), (0, 0)))
    yc = conv_module(u_pad, lw['conv_w'], lw['conv_b'], lw['conv_ln_g'], lw['conv_ln_b'], lw['w_proj_c'])
    out = merge(gates, ya, yb, yc, lw['w_out'])
    a_rows = min(WINDOW_A, h.shape[1])
    state = (ka[:, -a_rows:], va[:, -a_rows:], kb, vb, logf, u_pad[:, -(CONV_W - 1):])
    return out, state


def mixer_sample(h, ca_k, ca_v, cb_k, cb_v, cb_logf, c_conv, lw):
    qa, ka, va, qb, kb, vb, logf, u, gates = project_mixers(h, lw['w_in'], lw['b_in'])
    ya = band_attention_sample(qa, ka, va, ca_k, ca_v, lw['rel_bias']) @ lw['w_proj_a']
    yb = forgetting_attention_sample(qb, kb, vb, logf, cb_k, cb_v, cb_logf) @ lw['w_proj_b']
    u_pad = jnp.concatenate([c_conv, u], axis=1)
    yc = conv_module(u_pad, lw['conv_w'], lw['conv_b'], lw['conv_ln_g'], lw['conv_ln_b'], lw['w_proj_c'])
    out = merge(gates, ya, yb, yc, lw['w_out'])
    state = (ka, va, kb, vb, logf, u_pad[:, -(CONV_W - 1):])
    return out, state


def routed_ffn(h, w_rg, b_rg, w_re, b_re, w_gate, w_up, w_down):
    lead = h.shape[:-1]
    x = h.reshape(-1, D_MODEL)
    M = x.shape[0]
    g_logits = (x @ w_rg).astype(jnp.float32) + b_rg.astype(jnp.float32)
    g_idx = jnp.argmax(g_logits, axis=-1).astype(jnp.int32)
    g_w = jnp.take_along_axis(jax.nn.softmax(g_logits, axis=-1), g_idx[:, None], axis=1)
    e_logits = ((x @ w_re).astype(jnp.float32) + b_re.astype(jnp.float32)).reshape(M, N_GROUPS, EXPERTS_PER_GROUP)
    e_sel = jnp.take_along_axis(e_logits, g_idx[:, None, None], axis=1)[:, 0]
    top_v, top_i = lax.top_k(e_sel, TOP_K)
    comb = g_w * jax.nn.softmax(top_v, axis=-1)
    expert = g_idx[:, None] * EXPERTS_PER_GROUP + top_i.astype(jnp.int32)
    n_assign = M * TOP_K
    flat_e = expert.reshape(-1)
    flat_w = comb.reshape(-1)
    flat_tok = jnp.arange(n_assign, dtype=jnp.int32) // TOP_K
    order = jnp.argsort(flat_e)
    e_sorted = flat_e[order]
    counts = jnp.zeros((N_EXPERTS,), jnp.int32).at[flat_e].add(1)
    n_blk_e = (counts + EXPERT_BLOCK - 1) // EXPERT_BLOCK
    blk_end = jnp.cumsum(n_blk_e)
    blk_start = blk_end - n_blk_e
    row_start = jnp.cumsum(counts) - counts
    rank = jnp.arange(n_assign, dtype=jnp.int32) - row_start[e_sorted]
    dest = blk_start[e_sorted] * EXPERT_BLOCK + rank
    n_blocks = -(-n_assign // EXPERT_BLOCK) + N_EXPERTS
    n_rows = n_blocks * EXPERT_BLOCK
    row_tok = jnp.full((n_rows,), M, jnp.int32).at[dest].set(flat_tok[order])
    row_w = jnp.zeros((n_rows,), jnp.float32).at[dest].set(flat_w[order])
    blk_expert = jnp.minimum(jnp.searchsorted(blk_end, jnp.arange(n_blocks, dtype=jnp.int32), side='right'),
                             N_EXPERTS - 1)
    x_pad = jnp.concatenate([x, jnp.zeros((1, D_MODEL), x.dtype)], axis=0)
    xb = x_pad[row_tok].reshape(n_blocks, EXPERT_BLOCK, D_MODEL)

    def expert_block(args):
        xblk, e = args
        return (jax.nn.silu(xblk @ w_gate[e]) * (xblk @ w_up[e])) @ w_down[e]

    yb = lax.map(expert_block, (xb, blk_expert)).reshape(n_rows, D_MODEL)
    y = jax.ops.segment_sum(yb * row_w[:, None].astype(yb.dtype), row_tok, num_segments=M + 1)[:M]
    return y.reshape(*lead, D_MODEL)


def setup_inputs(seed: int = 0) -> dict:
    key = jax.random.key(seed)
    ks = jax.random.split(key, 32)

    def nrm(k, shape, scale):
        return jax.random.normal(k, shape, jnp.float32) * scale

    a_rows = min(WINDOW_A, PAST_LEN)
    b_in = nrm(ks[9], (DEPTH, N_IN), 0.02)
    b_in = b_in.at[:, FORGET_COL:FORGET_COL + H_B].set(
        jax.random.uniform(ks[10], (DEPTH, H_B), jnp.float32, 1.0, 4.0))
    return {
        'x_prompt': nrm(ks[0], (BATCH, SEQ, D_MODEL), 1.0),
        'x_sample': nrm(ks[1], (DEC_BATCH, DEC_SEQ, D_MODEL), 1.0),
        'cache_a_k': nrm(ks[2], (DEPTH, DEC_BATCH, a_rows, H_A, HEAD_DIM), 1.0),
        'cache_a_v': nrm(ks[3], (DEPTH, DEC_BATCH, a_rows, H_A, HEAD_DIM), 1.0),
        'cache_b_k': nrm(ks[4], (DEPTH, DEC_BATCH, PAST_LEN, H_B, HEAD_DIM), 1.0),
        'cache_b_v': nrm(ks[5], (DEPTH, DEC_BATCH, PAST_LEN, H_B, HEAD_DIM), 1.0),
        'cache_b_logf': jax.nn.log_sigmoid(2.5 + nrm(ks[6], (DEPTH, DEC_BATCH, PAST_LEN, H_B), 1.0)),
        'state_conv': nrm(ks[7], (DEPTH, DEC_BATCH, CONV_W - 1, C_CONV), 0.5),
        'norm_mix_g': 1.0 + nrm(ks[8], (DEPTH, D_MODEL), 0.02),
        'w_in': nrm(ks[11], (DEPTH, D_MODEL, N_IN), D_MODEL ** -0.5),
        'b_in': b_in,
        'rel_bias': nrm(ks[12], (DEPTH, H_A, 2 * REL_CLIP + 1), 0.1),
        'conv_w': nrm(ks[13], (DEPTH, CONV_W, C_CONV), CONV_W ** -0.5),
        'conv_b': nrm(ks[14], (DEPTH, C_CONV), 0.02),
        'conv_ln_g': 1.0 + nrm(ks[15], (DEPTH, C_CONV), 0.02),
        'conv_ln_b': nrm(ks[16], (DEPTH, C_CONV), 0.02),
        'w_proj_a': nrm(ks[17], (DEPTH, W_A, D_MODEL), W_A ** -0.5),
        'w_proj_b': nrm(ks[18], (DEPTH, W_B, D_MODEL), W_B ** -0.5),
        'w_proj_c': nrm(ks[19], (DEPTH, C_CONV, D_MODEL), C_CONV ** -0.5),
        'w_out': nrm(ks[20], (DEPTH, D_MODEL, D_MODEL), D_MODEL ** -0.5),
        'norm_ffn_g': 1.0 + nrm(ks[21], (DEPTH, D_MODEL), 0.02),
        'w_router_group': nrm(ks[22], (DEPTH, D_MODEL, N_GROUPS), D_MODEL ** -0.5),
        'b_router_group': nrm(ks[23], (DEPTH, N_GROUPS), 0.01),
        'w_router_expert': nrm(ks[24], (DEPTH, D_MODEL, N_EXPERTS), D_MODEL ** -0.5),
        'b_router_expert': nrm(ks[25], (DEPTH, N_EXPERTS), 0.01),
        'w_e_gate': nrm(ks[26], (DEPTH, N_EXPERTS, D_MODEL, D_EXPERT), D_MODEL ** -0.5),
        'w_e_up': nrm(ks[27], (DEPTH, N_EXPERTS, D_MODEL, D_EXPERT), D_MODEL ** -0.5),
        'w_e_down': nrm(ks[28], (DEPTH, N_EXPERTS, D_EXPERT, D_MODEL), D_EXPERT ** -0.5),
        'norm_final_g': 1.0 + nrm(ks[29], (D_MODEL,), 0.02),
    }


def reference(x_prompt, x_sample, cache_a_k, cache_a_v, cache_b_k, cache_b_v, cache_b_logf, state_conv,
              norm_mix_g, w_in, b_in, rel_bias, conv_w, conv_b, conv_ln_g, conv_ln_b,
              w_proj_a, w_proj_b, w_proj_c, w_out, norm_ffn_g,
              w_router_group, b_router_group, w_router_expert, b_router_expert,
              w_e_gate, w_e_up, w_e_down, norm_final_g):
    xp, xs = x_prompt, x_sample
    p_states, s_states = [], []
    for l in range(DEPTH):
        lw = {'w_in': w_in[l], 'b_in': b_in[l], 'rel_bias': rel_bias[l], 'conv_w': conv_w[l],
              'conv_b': conv_b[l], 'conv_ln_g': conv_ln_g[l], 'conv_ln_b': conv_ln_b[l],
              'w_proj_a': w_proj_a[l], 'w_proj_b': w_proj_b[l], 'w_proj_c': w_proj_c[l], 'w_out': w_out[l]}
        fw = (w_router_group[l], b_router_group[l], w_router_expert[l], b_router_expert[l],
              w_e_gate[l], w_e_up[l], w_e_down[l])
        mp, sp = mixer_prompt(rmsnorm(xp, norm_mix_g[l]), lw)
        ms, ss = mixer_sample(rmsnorm(xs, norm_mix_g[l]), cache_a_k[l], cache_a_v[l], cache_b_k[l],
                              cache_b_v[l], cache_b_logf[l], state_conv[l], lw)
        xp = xp + mp
        xs = xs + ms
        xp = xp + routed_ffn(rmsnorm(xp, norm_ffn_g[l]), *fw)
        xs = xs + routed_ffn(rmsnorm(xs, norm_ffn_g[l]), *fw)
        p_states.append(sp)
        s_states.append(ss)
    y_prompt = rmsnorm(xp, norm_final_g)
    y_sample = rmsnorm(xs, norm_final_g)
    stack = lambda states, i: jnp.stack([st[i] for st in states], axis=0)
    p_a_k, p_a_v, p_b_k = stack(p_states, 0), stack(p_states, 1), stack(p_states, 2)
    p_b_v, p_b_logf, p_conv = stack(p_states, 3), stack(p_states, 4), stack(p_states, 5)
    s_a_k, s_a_v, s_b_k = stack(s_states, 0), stack(s_states, 1), stack(s_states, 2)
    s_b_v, s_b_logf, s_conv = stack(s_states, 3), stack(s_states, 4), stack(s_states, 5)
    return (y_prompt, y_sample, p_a_k, p_a_v, p_b_k, p_b_v, p_b_logf, p_conv,
            s_a_k, s_a_v, s_b_k, s_b_v, s_b_logf, s_conv)
```

```python
import functools

import jax
import jax.numpy as jnp
import numpy as np
from jax import lax
from jax.experimental import pallas as pl
from jax.experimental.pallas import tpu as pltpu

f32 = jnp.float32
bf16 = jnp.bfloat16

HEAD_DIM = 64
N_HEADS = 8
W_HEADS = N_HEADS * HEAD_DIM
CHUNK = 64
WINDOW_A = 8 * CHUNK
REL_CLIP = 128
CONV_W = 31
CONV_HALO = 32
N_GROUPS = 4
EXPERTS_PER_GROUP = 8
N_EXPERTS = N_GROUPS * EXPERTS_PER_GROUP
TOP_K = 2
SCALE = HEAD_DIM ** -0.5
EPS = 1e-6
NEG_INF = -1e30
LANES = 128
MIB = 1024 * 1024


def _params(sem, vmem_mib=48):
    return pltpu.CompilerParams(dimension_semantics=sem, vmem_limit_bytes=vmem_mib * MIB)


def _row_tile(m, cap):
    t = cap
    while m % t:
        t //= 2
    return t


def _sigmoid(z):
    return 1.0 / (1.0 + jnp.exp(-z))


def _rms(x, g):
    return x * lax.rsqrt(jnp.mean(x * x, axis=-1, keepdims=True) + EPS) * g


def _when_segment(i, bounds, fn):
    for k in range(len(bounds) - 1):
        pl.when((i >= bounds[k]) & (i < bounds[k + 1]))(functools.partial(fn, k))


def _seg_spec(block, start, count, width_axes=1):
    zeros = (0,) * width_axes
    return pl.BlockSpec(block, lambda i, *_: (jnp.clip(i - start, 0, count - 1),) + zeros)


TN = 512


def _inproj_kernel(n_seg, bounds, n_conv, *refs):
    xs = refs[:n_seg]
    (g_ref, w_ref, b_ref, wf_ref, bf_ref) = refs[n_seg:n_seg + 5]
    outs = refs[n_seg + 5:-1]
    h_scr = refs[-1]
    (qa, ka, ka16, va, va16, qb, kb, kb16, vb, vb16, u, gates, logf) = outs
    i = pl.program_id(0)
    j = pl.program_id(1)

    def norm(k):
        h_scr[...] = _rms(xs[k][...], g_ref[...]).astype(bf16)

    @pl.when(j == 0)
    def _():
        _when_segment(i, bounds, norm)
        zf = jnp.dot(h_scr[...], wf_ref[...], preferred_element_type=f32) + bf_ref[...]
        lf = jnp.minimum(zf, 0.0) - jnp.log1p(jnp.exp(-jnp.abs(zf)))
        logf[...] = lf[:, :N_HEADS]

    z = jnp.dot(h_scr[...], w_ref[...], preferred_element_type=f32) + b_ref[...]

    def store(dsts):
        def go():
            for d in dsts:
                d[...] = z.astype(d.dtype)
        return go

    for t, dsts in enumerate(((qa,), (ka, ka16), (va, va16), (qb,), (kb, kb16), (vb, vb16))):
        pl.when(j == t)(store(dsts))

    half = TN // 2
    for c in range(n_conv):
        @pl.when(j == 6 + c)
        def _(c=c):
            u[:, c * half:(c + 1) * half] = z[:, :half] * _sigmoid(z[:, half:])

    @pl.when(j >= 6 + n_conv)
    def _():
        gates[...] = _sigmoid(z).astype(bf16)


def _inproj(x_segs, g, w_main, b_main, w_f, b_f, tm):
    d = x_segs[0].shape[1]
    m = sum(a.shape[0] for a in x_segs)
    c_conv = d // 2
    n_conv = c_conv // (TN // 2)
    n_gate = 3 * d // TN
    n_col = 6 + n_conv + n_gate
    assert w_main.shape == (d, n_col * TN)
    counts = [a.shape[0] // tm for a in x_segs]
    bounds = [0]
    for cnt in counts:
        bounds.append(bounds[-1] + cnt)
    n_seg = len(x_segs)

    in_specs = [pl.BlockSpec((tm, d), (lambda i, j, s=bounds[k], n=counts[k]: (jnp.clip(i - s, 0, n - 1), 0)))
                for k in range(n_seg)]
    in_specs += [
        pl.BlockSpec((1, d), lambda i, j: (0, 0)),
        pl.BlockSpec((d, TN), lambda i, j: (0, j)),
        pl.BlockSpec((1, TN), lambda i, j: (0, j)),
        pl.BlockSpec((d, LANES), lambda i, j: (0, 0)),
        pl.BlockSpec((1, LANES), lambda i, j: (0, 0)),
    ]
    row = lambda width: pl.BlockSpec((tm, width), lambda i, j: (i, 0))
    out_specs = [row(W_HEADS)] * 10 + [
        row(c_conv),
        pl.BlockSpec((tm, TN), lambda i, j: (i, jnp.clip(j - (6 + n_conv), 0, n_gate - 1))),
        row(N_HEADS),
    ]
    sds = jax.ShapeDtypeStruct
    out_shape = [sds((m, W_HEADS), bf16), sds((m, W_HEADS), f32), sds((m, W_HEADS), bf16),
                 sds((m, W_HEADS), f32), sds((m, W_HEADS), bf16),
                 sds((m, W_HEADS), bf16), sds((m, W_HEADS), f32), sds((m, W_HEADS), bf16),
                 sds((m, W_HEADS), f32), sds((m, W_HEADS), bf16),
                 sds((m, c_conv), f32), sds((m, 3 * d), bf16), sds((m, N_HEADS), f32)]
    return pl.pallas_call(
        functools.partial(_inproj_kernel, n_seg, tuple(bounds), n_conv),
        grid=(m // tm, n_col),
        in_specs=in_specs, out_specs=out_specs, out_shape=out_shape,
        scratch_shapes=[pltpu.VMEM((tm, d), bf16)],
        compiler_params=_params(("arbitrary", "arbitrary")),
        name="inproj",
    )(*x_segs, g, w_main, b_main, w_f, b_f)


TC = 512


def _cumsum_kernel(x_ref, o_ref, carry):
    @pl.when(pl.program_id(1) == 0)
    def _():
        carry[...] = jnp.zeros_like(carry)

    blk = x_ref[0]
    r = lax.broadcasted_iota(jnp.int32, (TC, TC), 0)
    c = lax.broadcasted_iota(jnp.int32, (TC, TC), 1)
    tri = jnp.where(r <= c, 1.0, 0.0).astype(f32)
    cs = jnp.dot(blk, tri, precision=lax.Precision.HIGHEST, preferred_element_type=f32) + carry[:, 0:1]
    o_ref[0] = cs
    carry[...] = jnp.broadcast_to(cs[:, TC - 1:TC], carry.shape)


def _cumsum_time(x):
    nb, h, t = x.shape
    tp = -(-t // TC) * TC
    xp = jnp.pad(x, ((0, 0), (0, 0), (0, tp - t)))
    out = pl.pallas_call(
        _cumsum_kernel,
        grid=(nb, tp // TC),
        in_specs=[pl.BlockSpec((1, h, TC), lambda b, k: (b, 0, k))],
        out_specs=pl.BlockSpec((1, h, TC), lambda b, k: (b, 0, k)),
        out_shape=jax.ShapeDtypeStruct((nb, h, tp), f32),
        scratch_shapes=[pltpu.VMEM((h, LANES), f32)],
        compiler_params=_params(("arbitrary", "arbitrary")),
        name="cumsum",
    )(xp)
    return out[:, :, :t]


def _fox_kernel(tq, tk, off, q_ref, k_ref, v_ref, fq_ref, fk_ref, o_ref, m_scr, l_scr, acc_scr):
    i = pl.program_id(1)
    j = pl.program_id(2)

    @pl.when(j == 0)
    def _():
        m_scr[...] = jnp.full_like(m_scr, NEG_INF)
        l_scr[...] = jnp.zeros_like(l_scr)
        acc_scr[...] = jnp.zeros_like(acc_scr)

    q_first = i * tq + off
    q_last = q_first + tq - 1
    k_first = j * tk
    k_last = k_first + tk - 1
    lane = lax.broadcasted_iota(jnp.int32, (tq, LANES), 1)
    low = lane < HEAD_DIM

    def body(masked):
        if masked:
            kpos = k_first + lax.broadcasted_iota(jnp.int32, (tq, tk), 1)
            qpos = q_first + lax.broadcasted_iota(jnp.int32, (tq, tk), 0)
            vis = kpos <= qpos
        for p in range(N_HEADS // 2):
            cols = slice(p * LANES, (p + 1) * LANES)
            q2 = q_ref[:, cols]
            k2 = k_ref[:, cols]
            v2 = v_ref[:, cols]
            upd = []
            for half in range(2):
                h = 2 * p + half
                qm = jnp.where(low if half == 0 else ~low, q2, jnp.zeros_like(q2))
                s = lax.dot_general(qm, k2, (((1,), (1,)), ((), ())), preferred_element_type=f32) * SCALE
                s = s + (fq_ref[:, h:h + 1] - fk_ref[0, h:h + 1, :])
                if masked:
                    s = jnp.where(vis, s, NEG_INF)
                m_old = m_scr[h]
                m_new = jnp.maximum(m_old, jnp.max(s, axis=-1, keepdims=True))
                alpha = jnp.exp(m_old - m_new)
                pr = jnp.exp(s - m_new)
                l_scr[h] = alpha * l_scr[h] + jnp.sum(pr, axis=-1, keepdims=True)
                m_scr[h] = m_new
                pv = jnp.dot(pr.astype(bf16), v2, preferred_element_type=f32)
                upd.append((alpha, pv))
            acc = acc_scr[:, cols]
            acc_scr[:, cols] = jnp.where(low, upd[0][0] * acc + upd[0][1], upd[1][0] * acc + upd[1][1])

    pl.when(k_last <= q_first)(functools.partial(body, False))
    pl.when((k_first <= q_last) & (k_last > q_first))(functools.partial(body, True))

    @pl.when(j == pl.num_programs(2) - 1)
    def _():
        for p in range(N_HEADS // 2):
            cols = slice(p * LANES, (p + 1) * LANES)
            inv = jnp.where(low, 1.0 / l_scr[2 * p], 1.0 / l_scr[2 * p + 1])
            o_ref[:, cols] = (acc_scr[:, cols] * inv).astype(o_ref.dtype)


def _fox_attention(q, k, v, fq, fk, nb, t_q, t_k, tq, tk, q_blk0=0, kv_blk0=0):
    nq, nk = t_q // tq, t_k // tk
    off = t_k - t_q

    def last_k(i):
        return jnp.minimum((i * tq + tq - 1 + off) // tk, nk - 1)

    kv_spec = pl.BlockSpec((tk, W_HEADS), lambda b, i, j: (kv_blk0 + b * nk + jnp.minimum(j, last_k(i)), 0))
    return pl.pallas_call(
        functools.partial(_fox_kernel, tq, tk, off),
        grid=(nb, nq, nk),
        in_specs=[pl.BlockSpec((tq, W_HEADS), lambda b, i, j: (q_blk0 + b * nq + i, 0)),
                  kv_spec, kv_spec,
                  pl.BlockSpec((tq, N_HEADS), lambda b, i, j: (b * nq + i, 0)),
                  pl.BlockSpec((1, N_HEADS, tk), lambda b, i, j: (b, 0, jnp.minimum(j, last_k(i))))],
        out_specs=pl.BlockSpec((tq, W_HEADS), lambda b, i, j: (b * nq + i, 0)),
        out_shape=jax.ShapeDtypeStruct((nb * t_q, W_HEADS), bf16),
        scratch_shapes=[pltpu.VMEM((N_HEADS, tq, 1), f32), pltpu.VMEM((N_HEADS, tq, 1), f32),
                        pltpu.VMEM((tq, W_HEADS), f32)],
        compiler_params=_params(("arbitrary", "arbitrary", "arbitrary")),
        name="fox",
    )(q, k, v, fq, fk)


def _band_kernel(rows, gq, wk, has_prev, *refs):
    if has_prev:
        q_ref, kp_ref, kc_ref, vp_ref, vc_ref, bias_ref, o_ref, k_scr, v_scr = refs
        k_scr[0:WINDOW_A] = kp_ref[...]
        k_scr[WINDOW_A:WINDOW_A + rows] = kc_ref[...]
        v_scr[0:WINDOW_A] = vp_ref[...]
        v_scr[WINDOW_A:WINDOW_A + rows] = vc_ref[...]
        k_src, v_src = k_scr, v_scr
    else:
        q_ref, k_src, v_src, bias_ref, o_ref = refs
    i = pl.program_id(1)
    lane = lax.broadcasted_iota(jnp.int32, (gq, LANES), 1)
    low = lane < HEAD_DIM
    for g in range(rows // gq):
        r0 = g * gq
        if has_prev:
            key_pos = (i - 1) * WINDOW_A + r0 + lax.broadcasted_iota(jnp.int32, (gq, wk), 1)
            vis = key_pos >= 0
        for p in range(N_HEADS // 2):
            cols = slice(p * LANES, (p + 1) * LANES)
            q2 = q_ref[r0:r0 + gq, cols]
            kw = k_src[r0:r0 + wk, cols]
            vw = v_src[r0:r0 + wk, cols]
            outs = []
            for half in range(2):
                h = 2 * p + half
                qm = jnp.where(low if half == 0 else ~low, q2, jnp.zeros_like(q2))
                s = lax.dot_general(qm, kw, (((1,), (1,)), ((), ())), preferred_element_type=f32) * SCALE
                s = s + bias_ref[h]
                if has_prev:
                    s = jnp.where(vis, s, NEG_INF)
                m = jnp.max(s, axis=-1, keepdims=True)
                pr = jnp.exp(s - m)
                l = jnp.sum(pr, axis=-1, keepdims=True)
                pv = jnp.dot(pr.astype(bf16), vw, preferred_element_type=f32)
                outs.append(pv / l)
            o_ref[r0:r0 + gq, cols] = jnp.where(low, outs[0], outs[1]).astype(o_ref.dtype)


def _band_bias(rel_bias, gq, wk, q_shift):
    r = jnp.arange(gq)[:, None]
    s = jnp.arange(wk)[None, :]
    rel = r + q_shift - s
    table = rel_bias.astype(f32)[:, jnp.clip(rel, -REL_CLIP, REL_CLIP) + REL_CLIP]
    band0 = (r // CHUNK) * CHUNK + q_shift - WINDOW_A
    ok = (s >= band0) & (s < band0 + WINDOW_A + CHUNK)
    return jnp.where(ok[None], table, NEG_INF)


def _band_attention_prompt(q, k, v, rel_bias, nb, t, gq):
    rows = WINDOW_A
    wk = WINDOW_A + gq
    n_steps = t // rows
    bias = _band_bias(rel_bias, gq, wk, WINDOW_A)
    cur = pl.BlockSpec((rows, W_HEADS), lambda b, i: (b * n_steps + i, 0))
    prev = pl.BlockSpec((rows, W_HEADS), lambda b, i: (b * n_steps + jnp.maximum(i - 1, 0), 0))
    return pl.pallas_call(
        functools.partial(_band_kernel, rows, gq, wk, True),
        grid=(nb, n_steps),
        in_specs=[cur, prev, cur, prev, cur,
                  pl.BlockSpec((N_HEADS, gq, wk), lambda b, i: (0, 0, 0))],
        out_specs=cur,
        out_shape=jax.ShapeDtypeStruct((nb * t, W_HEADS), bf16),
        scratch_shapes=[pltpu.VMEM((2 * rows, W_HEADS), bf16), pltpu.VMEM((2 * rows, W_HEADS), bf16)],
        compiler_params=_params(("arbitrary", "arbitrary")),
        name="band_prompt",
    )(q, k, k, v, v, bias)


def _band_attention_sample(q, kk, vv, rel_bias, nb, s_new, l_cache, q_blk0):
    wk = l_cache + s_new
    r = jnp.arange(s_new)[:, None]
    s = jnp.arange(wk)[None, :]
    bias = rel_bias.astype(f32)[:, jnp.clip(l_cache + r - s, -REL_CLIP, REL_CLIP) + REL_CLIP]
    return pl.pallas_call(
        functools.partial(_band_kernel, s_new, s_new, wk, False),
        grid=(nb, 1),
        in_specs=[pl.BlockSpec((s_new, W_HEADS), lambda b, i: (q_blk0 + b, 0)),
                  pl.BlockSpec((wk, W_HEADS), lambda b, i: (b, 0)),
                  pl.BlockSpec((wk, W_HEADS), lambda b, i: (b, 0)),
                  pl.BlockSpec((N_HEADS, s_new, wk), lambda b, i: (0, 0, 0))],
        out_specs=pl.BlockSpec((s_new, W_HEADS), lambda b, i: (b, 0)),
        out_shape=jax.ShapeDtypeStruct((nb * s_new, W_HEADS), bf16),
        compiler_params=_params(("arbitrary", "arbitrary")),
        name="band_sample",
    )(q, kk, vv, bias)


CONV_STRIP = 32


def _conv_kernel(tt, init_ref, u_ref, w_ref, cb_ref, g_ref, b_ref, o_ref, ubuf):
    @pl.when(pl.program_id(1) == 0)
    def _():
        ubuf[0:CONV_HALO] = init_ref[0]

    ubuf[CONV_HALO:CONV_HALO + tt] = u_ref[...]
    rs = min(CONV_STRIP, tt)
    first = CONV_HALO - (CONV_W - 1)
    for s in range(tt // rs):
        acc = cb_ref[...] + w_ref[0:1, :] * ubuf[s * rs + first:s * rs + first + rs, :]
        for j in range(1, CONV_W):
            acc = acc + w_ref[j:j + 1, :] * ubuf[s * rs + first + j:s * rs + first + j + rs, :]
        mu = jnp.mean(acc, axis=-1, keepdims=True)
        cen = acc - mu
        var = jnp.mean(cen * cen, axis=-1, keepdims=True)
        y = cen * lax.rsqrt(var + EPS) * g_ref[...] + b_ref[...]
        o_ref[s * rs:(s + 1) * rs, :] = (y * _sigmoid(y)).astype(o_ref.dtype)
    if tt >= CONV_HALO:
        ubuf[0:CONV_HALO] = ubuf[tt:tt + CONV_HALO]


def _conv_module(u, init, conv_w, conv_b, ln_g, ln_b, nb, t, tt, blk0):
    c = u.shape[1]
    n_t = t // tt
    vec = pl.BlockSpec((1, c), lambda b, i: (0, 0))
    return pl.pallas_call(
        functools.partial(_conv_kernel, tt),
        grid=(nb, n_t),
        in_specs=[pl.BlockSpec((1, CONV_HALO, c), lambda b, i: (b, 0, 0)),
                  pl.BlockSpec((tt, c), lambda b, i: (blk0 + b * n_t + i, 0)),
                  pl.BlockSpec((CONV_W, c), lambda b, i: (0, 0)), vec, vec, vec],
        out_specs=pl.BlockSpec((tt, c), lambda b, i: (b * n_t + i, 0)),
        out_shape=jax.ShapeDtypeStruct((nb * t, c), bf16),
        scratch_shapes=[pltpu.VMEM((CONV_HALO + tt, c), f32)],
        compiler_params=_params(("arbitrary", "arbitrary")),
        name="conv",
    )(init, u, conv_w, conv_b, ln_g, ln_b)


def _mix_kernel(bounds, ya_p, ya_s, yb_p, yb_s, c_p, c_s, gates, pa, pb, pc, o_ref):
    d = o_ref.shape[1]

    def go(k):
        ya, yb, c = ((ya_p, yb_p, c_p), (ya_s, yb_s, c_s))[k]
        a = jnp.dot(ya[...], pa[...], preferred_element_type=f32)
        mixed = gates[:, 0:d].astype(f32) * a
        b = jnp.dot(yb[...], pb[...], preferred_element_type=f32)
        mixed = mixed + gates[:, d:2 * d].astype(f32) * b
        cc = jnp.dot(c[...], pc[...], preferred_element_type=f32)
        mixed = mixed + gates[:, 2 * d:3 * d].astype(f32) * cc
        o_ref[...] = mixed.astype(o_ref.dtype)

    _when_segment(pl.program_id(0), bounds, go)


def _mix(ya_p, ya_s, yb_p, yb_s, c_p, c_s, gates, pa, pb, pc, tm):
    m, d3 = gates.shape
    d = d3 // 3
    n_p, n_s = ya_p.shape[0] // tm, ya_s.shape[0] // tm
    bounds = (0, n_p, n_p + n_s)
    c_conv = c_p.shape[1]
    const = lambda shape: pl.BlockSpec(shape, lambda i: (0, 0))
    return pl.pallas_call(
        functools.partial(_mix_kernel, bounds),
        grid=(m // tm,),
        in_specs=[_seg_spec((tm, W_HEADS), 0, n_p), _seg_spec((tm, W_HEADS), n_p, n_s),
                  _seg_spec((tm, W_HEADS), 0, n_p), _seg_spec((tm, W_HEADS), n_p, n_s),
                  _seg_spec((tm, c_conv), 0, n_p), _seg_spec((tm, c_conv), n_p, n_s),
                  pl.BlockSpec((tm, d3), lambda i: (i, 0)),
                  const((W_HEADS, d)), const((W_HEADS, d)), const((c_conv, d))],
        out_specs=pl.BlockSpec((tm, d), lambda i: (i, 0)),
        out_shape=jax.ShapeDtypeStruct((m, d), bf16),
        compiler_params=_params(("arbitrary",)),
        name="mix",
    )(ya_p, ya_s, yb_p, yb_s, c_p, c_s, gates, pa, pb, pc)


def _route(logits):
    shape = logits.shape
    lane = lax.broadcasted_iota(jnp.int32, shape, 1)
    lane_f = lane.astype(f32)
    big = float(LANES)
    gl = jnp.where(lane < N_GROUPS, logits, -jnp.inf)
    g_max = jnp.max(gl, axis=-1, keepdims=True)
    g_idx = jnp.min(jnp.where(gl == g_max, lane_f, big), axis=-1, keepdims=True)
    g_sum = jnp.sum(jnp.exp(gl - g_max), axis=-1, keepdims=True)
    g_w = 1.0 / g_sum
    lo = N_GROUPS + g_idx * EXPERTS_PER_GROUP
    el = jnp.where((lane_f >= lo) & (lane_f < lo + EXPERTS_PER_GROUP), logits, -jnp.inf)
    m1 = jnp.max(el, axis=-1, keepdims=True)
    i1 = jnp.min(jnp.where(el == m1, lane_f, big), axis=-1, keepdims=True)
    el2 = jnp.where(lane_f == i1, -jnp.inf, el)
    m2 = jnp.max(el2, axis=-1, keepdims=True)
    i2 = jnp.min(jnp.where(el2 == m2, lane_f, big), axis=-1, keepdims=True)
    e21 = jnp.exp(m2 - m1)
    den = 1.0 + e21
    w1 = g_w * (1.0 / den)
    w2 = g_w * (e21 / den)
    eid = jnp.where(lane == 0, i1 - N_GROUPS, jnp.where(lane == 1, i2 - N_GROUPS, 0.0)).astype(jnp.int32)
    wgt = jnp.where(lane == 0, w1, jnp.where(lane == 1, w2, 0.0))
    return eid, wgt


def _outproj_kernel(n_slab, mixed, x, wo, g2, wr, br, xo, h2o, eid_o, wgt_o):
    tm = x.shape[0]
    xn = x[...] + jnp.dot(mixed[...], wo[...], preferred_element_type=f32)
    xo[...] = xn
    h2 = _rms(xn, g2[...])
    for s in range(n_slab):
        h2o[pl.ds(s, tm, stride=n_slab), :] = h2[:, s * LANES:(s + 1) * LANES]
    logits = jnp.dot(h2, wr[...], precision=lax.Precision.HIGHEST, preferred_element_type=f32) + br[...]
    eid, wgt = _route(logits)
    eid_o[...] = eid
    wgt_o[...] = wgt


def _outproj(mixed, x, wo, g2, wr, br, tm):
    m, d = x.shape
    n_slab = d // LANES
    const = lambda shape: pl.BlockSpec(shape, lambda i: (0, 0))
    row = lambda w: pl.BlockSpec((tm, w), lambda i: (i, 0))
    sds = jax.ShapeDtypeStruct
    return pl.pallas_call(
        functools.partial(_outproj_kernel, n_slab),
        grid=(m // tm,),
        in_specs=[row(d), row(d), const((d, d)), const((1, d)), const((d, LANES)), const((1, LANES))],
        out_specs=[row(d), pl.BlockSpec((tm * n_slab, LANES), lambda i: (i, 0)), row(LANES), row(LANES)],
        out_shape=[sds((m, d), f32), sds((m * n_slab, LANES), f32), sds((m, LANES), jnp.int32),
                   sds((m, LANES), f32)],
        compiler_params=_params(("arbitrary",)),
        name="outproj",
    )(mixed, x, wo, g2, wr, br)


TB = 256
TD = 256


def _plan(eid):
    flat_e = eid.reshape(-1)
    n_assign = flat_e.shape[0]
    onehot = (flat_e[:, None] == jnp.arange(N_EXPERTS, dtype=jnp.int32)[None, :]).astype(jnp.int32)
    csum = jnp.cumsum(onehot, axis=0)
    counts = csum[-1]
    rank = jnp.sum(onehot * csum, axis=1) - 1
    n_blk_e = (counts + TB - 1) // TB
    blk_end = jnp.cumsum(n_blk_e)
    blk_start = blk_end - n_blk_e
    dest = blk_start[flat_e] * TB + rank
    n_blocks = -(-n_assign // TB) + N_EXPERTS
    blk_expert = jnp.minimum(jnp.searchsorted(blk_end, jnp.arange(n_blocks, dtype=jnp.int32), side='right'),
                             N_EXPERTS - 1).astype(jnp.int32)
    return dest.astype(jnp.int32), blk_expert, blk_end[-1:].astype(jnp.int32), n_blocks


def _dispatch_kernel(n_slab, dest_ref, h_ref, zero_ref, xs_ref, sem):
    del zero_ref
    n = dest_ref.shape[2]

    def copy(a):
        src = h_ref.at[pl.ds(pl.multiple_of((a // TOP_K) * n_slab, n_slab), n_slab), :]
        dst = xs_ref.at[pl.ds(pl.multiple_of(dest_ref[0, 0, a] * n_slab, n_slab), n_slab), :]
        return pltpu.make_async_copy(src, dst, sem)

    def start(a, carry):
        copy(a).start()
        return carry

    def wait(a, carry):
        copy(a).wait()
        return carry

    lax.fori_loop(0, n, start, 0)
    lax.fori_loop(0, n, wait, 0)


def _dispatch(h2_slab, dest, n_rows, n_slab):
    m = h2_slab.shape[0] // n_slab
    n_steps = m // TD
    dest3 = dest.reshape(n_steps, 1, TD * TOP_K)
    zeros = jnp.zeros((n_rows * n_slab, LANES), f32)
    return pl.pallas_call(
        functools.partial(_dispatch_kernel, n_slab),
        grid=(n_steps,),
        in_specs=[pl.BlockSpec((1, 1, TD * TOP_K), lambda i: (i, 0, 0), memory_space=pltpu.SMEM),
                  pl.BlockSpec((TD * n_slab, LANES), lambda i: (i, 0)),
                  pl.BlockSpec(memory_space=pl.ANY)],
        out_specs=pl.BlockSpec(memory_space=pl.ANY),
        out_shape=jax.ShapeDtypeStruct((n_rows * n_slab, LANES), f32),
        scratch_shapes=[pltpu.SemaphoreType.DMA(())],
        input_output_aliases={2: 0},
        compiler_params=_params(("arbitrary",)),
        name="dispatch",
    )(dest3, h2_slab, zeros)


def _expert_kernel(n_slab, be_ref, nu_ref, xs_ref, wgu_ref, wd_ref, o_ref):
    del be_ref

    @pl.when(pl.program_id(0) < nu_ref[0])
    def _():
        de = wd_ref.shape[1]
        x = jnp.concatenate([xs_ref[pl.ds(s, TB, stride=n_slab), :] for s in range(n_slab)], axis=1)
        gu = jnp.dot(x.astype(bf16), wgu_ref[0], preferred_element_type=f32)
        g = gu[:, :de]
        hmid = (g * _sigmoid(g)) * gu[:, de:]
        y = jnp.dot(hmid.astype(bf16), wd_ref[0], preferred_element_type=f32)
        for s in range(n_slab):
            o_ref[pl.ds(s, TB, stride=n_slab), :] = y[:, s * LANES:(s + 1) * LANES]


def _experts(xs, blk_expert, n_used, wgu, wd, n_blocks, n_slab):
    d, de2 = wgu.shape[1], wgu.shape[2]
    blk = lambda i, be, nu: (jnp.minimum(i, nu[0] - 1), 0)
    grid_spec = pltpu.PrefetchScalarGridSpec(
        num_scalar_prefetch=2,
        grid=(n_blocks,),
        in_specs=[pl.BlockSpec((TB * n_slab, LANES), blk),
                  pl.BlockSpec((1, d, de2), lambda i, be, nu: (be[jnp.minimum(i, nu[0] - 1)], 0, 0)),
                  pl.BlockSpec((1, de2 // 2, d), lambda i, be, nu: (be[jnp.minimum(i, nu[0] - 1)], 0, 0))],
        out_specs=pl.BlockSpec((TB * n_slab, LANES), blk),
    )
    return pl.pallas_call(
        functools.partial(_expert_kernel, n_slab),
        grid_spec=grid_spec,
        out_shape=jax.ShapeDtypeStruct(xs.shape, f32),
        input_output_aliases={2: 0},
        compiler_params=_params(("arbitrary",)),
        name="experts",
    )(blk_expert, n_used, xs, wgu, wd)


def _combine_kernel(n_slab, final, bounds, dest_ref, x_ref, wgt_ref, g_ref, ys_ref, *rest):
    outs, (gbuf, sem) = rest[:-2], rest[-2:]
    n = dest_ref.shape[2]
    tm = x_ref.shape[0]

    def copy(a):
        src = ys_ref.at[pl.ds(pl.multiple_of(dest_ref[0, 0, a] * n_slab, n_slab), n_slab), :]
        dst = gbuf.at[pl.ds(pl.multiple_of(a * n_slab, n_slab), n_slab), :]
        return pltpu.make_async_copy(src, dst, sem)

    def start(a, carry):
        copy(a).start()
        return carry

    def wait(a, carry):
        copy(a).wait()
        return carry

    lax.fori_loop(0, n, start, 0)
    lax.fori_loop(0, n, wait, 0)
    ys = []
    for k in range(TOP_K):
        y = jnp.concatenate([gbuf[pl.ds(k * n_slab + s, tm, stride=TOP_K * n_slab), :] for s in range(n_slab)],
                            axis=1)
        ys.append(y * wgt_ref[:, k:k + 1])
    x = x_ref[...] + (ys[0] + ys[1])
    if final:
        x = _rms(x, g_ref[...])

    def store(k):
        outs[k][...] = x

    _when_segment(pl.program_id(0), bounds, store)


def _combine(x, ys, dest, wgt, g, n_slab, final, seg_rows):
    m, d = x.shape
    n_steps = m // TD
    dest3 = dest.reshape(n_steps, 1, TD * TOP_K)
    counts = [r // TD for r in seg_rows]
    bounds = [0]
    for cnt in counts:
        bounds.append(bounds[-1] + cnt)
    out_specs = [_seg_spec((TD, d), bounds[k], counts[k]) for k in range(len(seg_rows))]
    out_shape = [jax.ShapeDtypeStruct((r, d), f32) for r in seg_rows]
    return pl.pallas_call(
        functools.partial(_combine_kernel, n_slab, final, tuple(bounds)),
        grid=(n_steps,),
        in_specs=[pl.BlockSpec((1, 1, TD * TOP_K), lambda i: (i, 0, 0), memory_space=pltpu.SMEM),
                  pl.BlockSpec((TD, d), lambda i: (i, 0)),
                  pl.BlockSpec((TD, LANES), lambda i: (i, 0)),
                  pl.BlockSpec((1, d), lambda i: (0, 0)),
                  pl.BlockSpec(memory_space=pl.ANY)],
        out_specs=out_specs, out_shape=out_shape,
        scratch_shapes=[pltpu.VMEM((TD * TOP_K * n_slab, LANES), f32), pltpu.SemaphoreType.DMA(())],
        compiler_params=_params(("arbitrary",)),
        name="combine",
    )(dest3, x, wgt, g, ys)


def _pack_w_in(w_in, b_in, d):
    c_conv = d // 2
    half = TN // 2
    n_qkv = 6 * W_HEADS
    f0 = n_qkv
    c0 = f0 + N_HEADS
    g0 = c0 + 2 * c_conv
    cols = [np.arange(n_qkv)]
    for c in range(c_conv // half):
        cols.append(c0 + c * half + np.arange(half))
        cols.append(c0 + c_conv + c * half + np.arange(half))
    cols.append(g0 + np.arange(3 * d))
    cols = np.concatenate(cols)
    w_main = w_in[:, cols].astype(bf16)
    b_main = b_in[cols][None, :].astype(f32)
    w_f = jnp.pad(w_in[:, f0:f0 + N_HEADS], ((0, 0), (0, LANES - N_HEADS))).astype(bf16)
    b_f = jnp.pad(b_in[f0:f0 + N_HEADS], (0, LANES - N_HEADS))[None, :].astype(f32)
    return w_main, b_main, w_f, b_f


def kernel(x_prompt, x_sample, cache_a_k, cache_a_v, cache_b_k, cache_b_v, cache_b_logf, state_conv, norm_mix_g, w_in, b_in, rel_bias, conv_w, conv_b, conv_ln_g, conv_ln_b, w_proj_a, w_proj_b, w_proj_c, w_out, norm_ffn_g, w_router_group, b_router_group, w_router_expert, b_router_expert, w_e_gate, w_e_up, w_e_down, norm_final_g):
    nb_p, t_p, d = x_prompt.shape
    nb_s, t_s, _ = x_sample.shape
    depth = w_in.shape[0]
    past = cache_b_k.shape[2]
    a_rows = cache_a_k.shape[2]
    m_p, m_s = nb_p * t_p, nb_s * t_s
    m = m_p + m_s
    c_conv = d // 2
    n_slab = d // LANES
    tm = _row_tile(np.gcd(m_p, m_s), 512)
    tm_mix = _row_tile(np.gcd(m_p, m_s), 256)
    assert m_p % TD == 0 and m_s % TD == 0 and t_s % 16 == 0 and m_p % t_s == 0

    x_segs = [x_prompt.reshape(m_p, d), x_sample.reshape(m_s, d)]
    p_states, s_states = [], []
    for l in range(depth):
        w_main, b_main, w_f, b_f = _pack_w_in(w_in[l], b_in[l], d)
        (qa, ka, ka16, va, va16, qb, kb, kb16, vb, vb16, u, gates, logf) = _inproj(
            x_segs, norm_mix_g[l][None, :], w_main, b_main, w_f, b_f, tm)

        ya_p = _band_attention_prompt(qa, ka16, va16, rel_bias[l], nb_p, t_p, 4 * CHUNK)
        kk = jnp.concatenate([cache_a_k[l].reshape(nb_s, a_rows, W_HEADS).astype(bf16),
                              ka16[m_p:].reshape(nb_s, t_s, W_HEADS)], axis=1).reshape(-1, W_HEADS)
        vv = jnp.concatenate([cache_a_v[l].reshape(nb_s, a_rows, W_HEADS).astype(bf16),
                              va16[m_p:].reshape(nb_s, t_s, W_HEADS)], axis=1).reshape(-1, W_HEADS)
        ya_s = _band_attention_sample(qa, kk, vv, rel_bias[l], nb_s, t_s, a_rows, m_p // t_s)

        logf_p = logf[:m_p].reshape(nb_p, t_p, N_HEADS)
        logf_s = logf[m_p:].reshape(nb_s, t_s, N_HEADS)
        cum_p = _cumsum_time(logf_p.transpose(0, 2, 1))
        tq_p = _row_tile(t_p, 512)
        yb_p = _fox_attention(qb, kb16, vb16, cum_p.transpose(0, 2, 1).reshape(m_p, N_HEADS), cum_p,
                              nb_p, t_p, t_p, tq_p, tq_p)
        cum_s = _cumsum_time(jnp.concatenate([cache_b_logf[l].astype(f32), logf_s], axis=1).transpose(0, 2, 1))
        kk = jnp.concatenate([cache_b_k[l].reshape(nb_s, past, W_HEADS).astype(bf16),
                              kb16[m_p:].reshape(nb_s, t_s, W_HEADS)], axis=1).reshape(-1, W_HEADS)
        vv = jnp.concatenate([cache_b_v[l].reshape(nb_s, past, W_HEADS).astype(bf16),
                              vb16[m_p:].reshape(nb_s, t_s, W_HEADS)], axis=1).reshape(-1, W_HEADS)
        yb_s = _fox_attention(qb, kk, vv, cum_s[:, :, past:].transpose(0, 2, 1).reshape(m_s, N_HEADS), cum_s,
                              nb_s, t_s, past + t_s, t_s, past + t_s, q_blk0=m_p // t_s)

        conv_args = (conv_w[l], conv_b[l][None, :], conv_ln_g[l][None, :], conv_ln_b[l][None, :])
        c_p = _conv_module(u, jnp.zeros((nb_p, CONV_HALO, c_conv), f32), *conv_args,
                           nb_p, t_p, _row_tile(t_p, 256), 0)
        init_s = jnp.pad(state_conv[l], ((0, 0), (CONV_HALO - (CONV_W - 1), 0), (0, 0)))
        c_s = _conv_module(u, init_s, *conv_args, nb_s, t_s, t_s, m_p // t_s)

        mixed = _mix(ya_p, ya_s, yb_p, yb_s, c_p, c_s, gates, w_proj_a[l].astype(bf16),
                     w_proj_b[l].astype(bf16), w_proj_c[l].astype(bf16), tm_mix)
        x_cat = x_segs[0] if len(x_segs) == 1 else jnp.concatenate(x_segs, axis=0)
        wr = jnp.pad(jnp.concatenate([w_router_group[l], w_router_expert[l]], axis=1),
                     ((0, 0), (0, LANES - N_GROUPS - N_EXPERTS)))
        br = jnp.pad(jnp.concatenate([b_router_group[l], b_router_expert[l]]),
                     (0, LANES - N_GROUPS - N_EXPERTS))[None, :]
        x_mid, h2_slab, eid, wgt = _outproj(mixed, x_cat, w_out[l].astype(bf16), norm_ffn_g[l][None, :],
                                            wr, br, tm_mix)

        dest, blk_expert, n_used, n_blocks = _plan(eid[:, :TOP_K])
        xs = _dispatch(h2_slab, dest, n_blocks * TB, n_slab)
        wgu = jnp.concatenate([w_e_gate[l], w_e_up[l]], axis=2).astype(bf16)
        ys = _experts(xs, blk_expert, n_used, wgu, w_e_down[l].astype(bf16), n_blocks, n_slab)
        final = l == depth - 1
        x_segs = _combine(x_mid, ys, dest, wgt, norm_final_g[None, :], n_slab, final,
                          (m_p, m_s) if final else (m,))

        heads = lambda a, nb, t: a.reshape(nb, t, N_HEADS, HEAD_DIM)
        u_p = u[:m_p].reshape(nb_p, t_p, c_conv)
        u_s = u[m_p:].reshape(nb_s, t_s, c_conv)
        a_keep = min(WINDOW_A, t_p)
        p_states.append((heads(ka[:m_p], nb_p, t_p)[:, -a_keep:], heads(va[:m_p], nb_p, t_p)[:, -a_keep:],
                         heads(kb[:m_p], nb_p, t_p), heads(vb[:m_p], nb_p, t_p), logf_p,
                         u_p[:, -(CONV_W - 1):]))
        s_states.append((heads(ka[m_p:], nb_s, t_s), heads(va[m_p:], nb_s, t_s),
                         heads(kb[m_p:], nb_s, t_s), heads(vb[m_p:], nb_s, t_s), logf_s,
                         jnp.concatenate([state_conv[l], u_s], axis=1)[:, -(CONV_W - 1):]))

    y_prompt = x_segs[0].reshape(nb_p, t_p, d)
    y_sample = x_segs[1].reshape(nb_s, t_s, d)
    stack = lambda states, k: jnp.stack([st[k] for st in states], axis=0)
    return (y_prompt, y_sample) + tuple(stack(p_states, k) for k in range(6)) + \
        tuple(stack(s_states, k) for k in range(6))
```

```python
import functools

import jax
import jax.numpy as jnp
import numpy as np
from jax import lax
from jax.experimental import pallas as pl
from jax.experimental.pallas import tpu as pltpu

f32 = jnp.float32
bf16 = jnp.bfloat16

HEAD_DIM = 64
N_HEADS = 8
W_HEADS = N_HEADS * HEAD_DIM
CHUNK = 64
WINDOW_A = 8 * CHUNK
REL_CLIP = 128
CONV_W = 31
CONV_HALO = 32
N_GROUPS = 4
EXPERTS_PER_GROUP = 8
N_EXPERTS = N_GROUPS * EXPERTS_PER_GROUP
TOP_K = 2
SCALE = HEAD_DIM ** -0.5
EPS = 1e-6
NEG_INF = -1e30
LANES = 128
MIB = 1024 * 1024


def _params(sem, vmem_mib=48):
    return pltpu.CompilerParams(dimension_semantics=sem, vmem_limit_bytes=vmem_mib * MIB)


def _row_tile(m, cap):
    t = cap
    while m % t:
        t //= 2
    return t


def _sigmoid(z):
    return 1.0 / (1.0 + jnp.exp(-z))


def _rms(x, g):
    return x * lax.rsqrt(jnp.mean(x * x, axis=-1, keepdims=True) + EPS) * g


def _when_segment(i, bounds, fn):
    for k in range(len(bounds) - 1):
        pl.when((i >= bounds[k]) & (i < bounds[k + 1]))(functools.partial(fn, k))


def _seg_spec(block, start, count, width_axes=1):
    zeros = (0,) * width_axes
    return pl.BlockSpec(block, lambda i, *_: (jnp.clip(i - start, 0, count - 1),) + zeros)


TN = 1024
SEG_PER_TILE = TN // W_HEADS
N_QKV_TILES = 6 // SEG_PER_TILE


def _inproj_kernel(n_seg, bounds, n_conv, n_prompt, *refs):
    xs = refs[:n_seg]
    (g_ref, w_ref, b_ref, wf_ref, bf_ref) = refs[n_seg:n_seg + 5]
    outs = refs[n_seg + 5 + 8:-1]
    h_scr = refs[-1]
    (qa, ka16, va16, qb, kb16, vb16, u, gates, logf, ka_p, va_p, kb_p, vb_p, ka_s, va_s, kb_s, vb_s) = outs
    i = pl.program_id(0)
    j = pl.program_id(1)

    def norm(k):
        h_scr[...] = _rms(xs[k][...], g_ref[...]).astype(bf16)

    @pl.when(j == 0)
    def _():
        _when_segment(i, bounds, norm)
        zf = jnp.dot(h_scr[...], wf_ref[...], preferred_element_type=f32) + bf_ref[...]
        lf = jnp.minimum(zf, 0.0) - jnp.log1p(jnp.exp(-jnp.abs(zf)))
        logf[...] = lf[:, :N_HEADS]

    def project():
        return jnp.dot(h_scr[...], w_ref[...], preferred_element_type=f32) + b_ref[...]

    segments = ((qa, None, None), (ka16, ka_p, ka_s), (va16, va_p, va_s),
                (qb, None, None), (kb16, kb_p, kb_s), (vb16, vb_p, vb_s))
    is_prompt = i < n_prompt
    for t in range(N_QKV_TILES):
        for prompt_rows in (True, False):
            @pl.when((j == t) & (is_prompt if prompt_rows else ~is_prompt))
            def _(t=t, prompt_rows=prompt_rows):
                z = project()
                for n in range(SEG_PER_TILE):
                    zn = z[:, n * W_HEADS:(n + 1) * W_HEADS]
                    copy, state_p, state_s = segments[t * SEG_PER_TILE + n]
                    copy[...] = zn.astype(bf16)
                    state = state_p if prompt_rows else state_s
                    if state is not None:
                        state[...] = zn

    half = TN // 2
    for c in range(n_conv):
        @pl.when(j == N_QKV_TILES + c)
        def _(c=c):
            z = project()
            u[:, c * half:(c + 1) * half] = z[:, :half] * _sigmoid(z[:, half:])

    @pl.when(j >= N_QKV_TILES + n_conv)
    def _():
        gates[...] = _sigmoid(project()).astype(bf16)


def _inproj(x_segs, g, w_main, b_main, w_f, b_f, states, layer, m_p, t_p, a_keep, tm):
    d = x_segs[0].shape[1]
    m = sum(a.shape[0] for a in x_segs)
    c_conv = d // 2
    n_conv = c_conv // (TN // 2)
    n_gate = 3 * d // TN
    n_col = N_QKV_TILES + n_conv + n_gate
    assert w_main.shape == (d, n_col * TN)
    counts = [a.shape[0] // tm for a in x_segs]
    bounds = [0]
    for cnt in counts:
        bounds.append(bounds[-1] + cnt)
    n_seg = len(x_segs)
    n_p, n_s = m_p // tm, (m - m_p) // tm
    per_seq, keep = t_p // tm, a_keep // tm

    in_specs = [pl.BlockSpec((tm, d), (lambda i, j, s=bounds[k], n=counts[k]: (jnp.clip(i - s, 0, n - 1), 0)))
                for k in range(n_seg)]
    in_specs += [
        pl.BlockSpec((1, d), lambda i, j: (0, 0)),
        pl.BlockSpec((d, TN), lambda i, j: (0, j)),
        pl.BlockSpec((1, TN), lambda i, j: (0, j)),
        pl.BlockSpec((d, LANES), lambda i, j: (0, 0)),
        pl.BlockSpec((1, LANES), lambda i, j: (0, 0)),
    ] + [pl.BlockSpec(memory_space=pl.ANY)] * 8
    row = lambda width: pl.BlockSpec((tm, width), lambda i, j: (i, 0))

    def tail_rows(i, j):
        ip = jnp.minimum(i, n_p - 1)
        return (layer, (ip // per_seq) * keep + jnp.maximum(ip % per_seq - (per_seq - keep), 0), 0)

    state_block = (None, tm, W_HEADS)
    tail_spec = pl.BlockSpec(state_block, tail_rows)
    prompt_spec = pl.BlockSpec(state_block, lambda i, j: (layer, jnp.minimum(i, n_p - 1), 0))
    sample_spec = pl.BlockSpec(state_block, lambda i, j: (layer, jnp.clip(i - n_p, 0, n_s - 1), 0))
    out_specs = [row(W_HEADS)] * 6 + [
        row(c_conv),
        pl.BlockSpec((tm, TN), lambda i, j: (i, jnp.clip(j - (N_QKV_TILES + n_conv), 0, n_gate - 1))),
        row(N_HEADS),
        tail_spec, tail_spec, prompt_spec, prompt_spec, sample_spec, sample_spec, sample_spec, sample_spec]
    sds = jax.ShapeDtypeStruct
    out_shape = [sds((m, W_HEADS), bf16)] * 6 + [sds((m, c_conv), f32), sds((m, 3 * d), bf16),
                                                 sds((m, N_HEADS), f32)]
    out_shape += [sds(s.shape, s.dtype) for s in states]
    n_in = n_seg + 5
    outs = pl.pallas_call(
        functools.partial(_inproj_kernel, n_seg, tuple(bounds), n_conv, n_p),
        grid=(m // tm, n_col),
        in_specs=in_specs, out_specs=out_specs, out_shape=out_shape,
        scratch_shapes=[pltpu.VMEM((tm, d), bf16)],
        input_output_aliases={n_in + k: 9 + k for k in range(8)},
        compiler_params=_params(("arbitrary", "arbitrary"), 58),
        name="inproj",
    )(*x_segs, g, w_main, b_main, w_f, b_f, *states)
    return outs[:9], outs[9:]


TC = 512


def _cumsum_kernel(x_ref, o_ref, carry):
    @pl.when(pl.program_id(1) == 0)
    def _():
        carry[...] = jnp.zeros_like(carry)

    blk = x_ref[0]
    r = lax.broadcasted_iota(jnp.int32, (TC, TC), 0)
    c = lax.broadcasted_iota(jnp.int32, (TC, TC), 1)
    tri = jnp.where(r <= c, 1.0, 0.0).astype(f32)
    cs = jnp.dot(blk, tri, precision=lax.Precision.HIGHEST, preferred_element_type=f32) + carry[:, 0:1]
    o_ref[0] = cs
    carry[...] = jnp.broadcast_to(cs[:, TC - 1:TC], carry.shape)


def _cumsum_time(x):
    nb, h, t = x.shape
    tp = -(-t // TC) * TC
    xp = jnp.pad(x, ((0, 0), (0, 0), (0, tp - t)))
    out = pl.pallas_call(
        _cumsum_kernel,
        grid=(nb, tp // TC),
        in_specs=[pl.BlockSpec((1, h, TC), lambda b, k: (b, 0, k))],
        out_specs=pl.BlockSpec((1, h, TC), lambda b, k: (b, 0, k)),
        out_shape=jax.ShapeDtypeStruct((nb, h, tp), f32),
        scratch_shapes=[pltpu.VMEM((h, LANES), f32)],
        compiler_params=_params(("arbitrary", "arbitrary")),
        name="cumsum",
    )(xp)
    return out[:, :, :t]


F_PARTS = 3


def _fox_expand_kernel(kind, x_ref, f_ref, o_ref):
    tm = x_ref.shape[0]
    lane = lax.broadcasted_iota(jnp.int32, (tm, LANES), 1)
    for h in range(N_HEADS):
        pair = x_ref[:, (h // 2) * LANES:(h // 2 + 1) * LANES].astype(f32)
        if h % 2:
            pair = pltpu.roll(pair, HEAD_DIM, axis=1)
        if kind == "v":
            spare = jnp.where(lane == HEAD_DIM, 1.0, 0.0)
        else:
            if kind == "q":
                pair = pair * SCALE
            ones_at, parts_at, sign = (F_PARTS, 0, 1.0) if kind == "q" else (0, F_PARTS, -1.0)
            ones = (lane >= HEAD_DIM + ones_at) & (lane < HEAD_DIM + ones_at + F_PARTS)
            spare = jnp.where(ones, 1.0, 0.0)
            rest = f_ref[:, h:h + 1] * sign
            for n in range(F_PARTS):
                part = rest.astype(bf16).astype(f32)
                rest = rest - part
                spare = jnp.where(lane == HEAD_DIM + parts_at + n, part, spare)
        o_ref[h] = jnp.where(lane < HEAD_DIM, pair, spare).astype(o_ref.dtype)


def _fox_expand(kind, x, f, rows, tm, blk0=0):
    if f is None:
        f = jnp.zeros((rows, N_HEADS), f32)
    return pl.pallas_call(
        functools.partial(_fox_expand_kernel, kind),
        grid=(rows // tm,),
        in_specs=[pl.BlockSpec((tm, W_HEADS), lambda i: (blk0 + i, 0)),
                  pl.BlockSpec((tm, N_HEADS), lambda i: (i, 0))],
        out_specs=pl.BlockSpec((N_HEADS, tm, LANES), lambda i: (0, i, 0)),
        out_shape=jax.ShapeDtypeStruct((N_HEADS, rows, LANES), bf16),
        compiler_params=_params(("arbitrary",)),
        name="fox_expand_" + kind,
    )(x, f)


FOX_HEADS_PER_TRIP = 2


def _fox_kernel(tq, tk, off, q_ref, k_ref, v_ref, o_ref, m_scr, acc_scr):
    i = pl.program_id(1)
    j = pl.program_id(2)

    @pl.when(j == 0)
    def _():
        m_scr[...] = jnp.full_like(m_scr, NEG_INF)
        acc_scr[...] = jnp.zeros_like(acc_scr)

    q_first = i * tq + off
    q_last = q_first + tq - 1
    k_first = j * tk
    k_last = k_first + tk - 1

    def body(masked):
        if masked:
            kpos = k_first + lax.broadcasted_iota(jnp.int32, (tq, tk), 1)
            qpos = q_first + lax.broadcasted_iota(jnp.int32, (tq, tk), 0)
            vis = kpos <= qpos

        def head(h):
            s = lax.dot_general(q_ref[h], k_ref[h], (((1,), (1,)), ((), ())), preferred_element_type=f32)
            if masked:
                s = jnp.where(vis, s, NEG_INF)
            m_old = m_scr[h]
            m_new = jnp.maximum(m_old, jnp.max(s, axis=-1, keepdims=True))
            pr = jnp.exp(s - m_new[:, 0:1])
            pv = jnp.dot(pr.astype(bf16), v_ref[h], preferred_element_type=f32)
            acc_scr[h] = jnp.exp(m_old - m_new) * acc_scr[h] + pv
            m_scr[h] = m_new

        def trip(g, carry):
            for n in range(FOX_HEADS_PER_TRIP):
                head(g * FOX_HEADS_PER_TRIP + n)
            return carry

        lax.fori_loop(0, N_HEADS // FOX_HEADS_PER_TRIP, trip, 0)

    pl.when(k_last <= q_first)(functools.partial(body, False))
    pl.when((k_first <= q_last) & (k_last > q_first))(functools.partial(body, True))

    @pl.when(j == pl.num_programs(2) - 1)
    def _():
        lane = lax.broadcasted_iota(jnp.int32, (tq, LANES), 1)
        for p in range(N_HEADS // 2):
            even = acc_scr[2 * p]
            odd = acc_scr[2 * p + 1]
            even = even / even[:, HEAD_DIM:HEAD_DIM + 1]
            odd = odd / odd[:, HEAD_DIM:HEAD_DIM + 1]
            o_ref[:, p * LANES:(p + 1) * LANES] = jnp.where(
                lane < HEAD_DIM, even, pltpu.roll(odd, HEAD_DIM, axis=1)).astype(o_ref.dtype)


def _fox_attention(q, k, v, nb, t_q, t_k, tq, tk):
    nq, nk = t_q // tq, t_k // tk
    off = t_k - t_q

    def last_k(i):
        return jnp.minimum((i * tq + tq - 1 + off) // tk, nk - 1)

    kv_spec = pl.BlockSpec((N_HEADS, tk, LANES), lambda b, i, j: (0, b * nk + jnp.minimum(j, last_k(i)), 0))
    return pl.pallas_call(
        functools.partial(_fox_kernel, tq, tk, off),
        grid=(nb, nq, nk),
        in_specs=[pl.BlockSpec((N_HEADS, tq, LANES), lambda b, i, j: (0, b * nq + i, 0)), kv_spec, kv_spec],
        out_specs=pl.BlockSpec((tq, W_HEADS), lambda b, i, j: (b * nq + i, 0)),
        out_shape=jax.ShapeDtypeStruct((nb * t_q, W_HEADS), bf16),
        scratch_shapes=[pltpu.VMEM((N_HEADS, tq, LANES), f32), pltpu.VMEM((N_HEADS, tq, LANES), f32)],
        compiler_params=_params(("arbitrary", "arbitrary", "arbitrary")),
        name="fox",
    )(q, k, v)


def _band_kernel(rows, gq, wk, has_prev, *refs):
    if has_prev:
        q_ref, kp_ref, kc_ref, vp_ref, vc_ref, bias_ref, o_ref, k_scr, v_scr = refs
        k_scr[0:WINDOW_A] = kp_ref[...]
        k_scr[WINDOW_A:WINDOW_A + rows] = kc_ref[...]
        v_scr[0:WINDOW_A] = vp_ref[...]
        v_scr[WINDOW_A:WINDOW_A + rows] = vc_ref[...]
        k_src, v_src = k_scr, v_scr
    else:
        q_ref, k_src, v_src, bias_ref, o_ref = refs
    i = pl.program_id(1)
    lane = lax.broadcasted_iota(jnp.int32, (gq, LANES), 1)
    low = lane < HEAD_DIM
    for g in range(rows // gq):
        r0 = g * gq
        if has_prev:
            key_pos = (i - 1) * WINDOW_A + r0 + lax.broadcasted_iota(jnp.int32, (gq, wk), 1)
            vis = key_pos >= 0
        for p in range(N_HEADS // 2):
            cols = slice(p * LANES, (p + 1) * LANES)
            q2 = q_ref[r0:r0 + gq, cols]
            kw = k_src[r0:r0 + wk, cols]
            vw = v_src[r0:r0 + wk, cols]
            outs = []
            for half in range(2):
                h = 2 * p + half
                qm = jnp.where(low if half == 0 else ~low, q2, jnp.zeros_like(q2))
                s = lax.dot_general(qm, kw, (((1,), (1,)), ((), ())), preferred_element_type=f32) * SCALE
                s = s + bias_ref[h]
                if has_prev:
                    s = jnp.where(vis, s, NEG_INF)
                m = jnp.max(s, axis=-1, keepdims=True)
                pr = jnp.exp(s - m)
                l = jnp.sum(pr, axis=-1, keepdims=True)
                pv = jnp.dot(pr.astype(bf16), vw, preferred_element_type=f32)
                outs.append(pv / l)
            o_ref[r0:r0 + gq, cols] = jnp.where(low, outs[0], outs[1]).astype(o_ref.dtype)


def _band_bias(rel_bias, gq, wk, q_shift):
    r = jnp.arange(gq)[:, None]
    s = jnp.arange(wk)[None, :]
    rel = r + q_shift - s
    table = rel_bias.astype(f32)[:, jnp.clip(rel, -REL_CLIP, REL_CLIP) + REL_CLIP]
    band0 = (r // CHUNK) * CHUNK + q_shift - WINDOW_A
    ok = (s >= band0) & (s < band0 + WINDOW_A + CHUNK)
    return jnp.where(ok[None], table, NEG_INF)


def _band_attention_prompt(q, k, v, rel_bias, nb, t, gq):
    rows = WINDOW_A
    wk = WINDOW_A + gq
    n_steps = t // rows
    bias = _band_bias(rel_bias, gq, wk, WINDOW_A)
    cur = pl.BlockSpec((rows, W_HEADS), lambda b, i: (b * n_steps + i, 0))
    prev = pl.BlockSpec((rows, W_HEADS), lambda b, i: (b * n_steps + jnp.maximum(i - 1, 0), 0))
    return pl.pallas_call(
        functools.partial(_band_kernel, rows, gq, wk, True),
        grid=(nb, n_steps),
        in_specs=[cur, prev, cur, prev, cur,
                  pl.BlockSpec((N_HEADS, gq, wk), lambda b, i: (0, 0, 0))],
        out_specs=cur,
        out_shape=jax.ShapeDtypeStruct((nb * t, W_HEADS), bf16),
        scratch_shapes=[pltpu.VMEM((2 * rows, W_HEADS), bf16), pltpu.VMEM((2 * rows, W_HEADS), bf16)],
        compiler_params=_params(("arbitrary", "arbitrary")),
        name="band_prompt",
    )(q, k, k, v, v, bias)


def _band_attention_sample(q, kk, vv, rel_bias, nb, s_new, l_cache, q_blk0):
    wk = l_cache + s_new
    r = jnp.arange(s_new)[:, None]
    s = jnp.arange(wk)[None, :]
    bias = rel_bias.astype(f32)[:, jnp.clip(l_cache + r - s, -REL_CLIP, REL_CLIP) + REL_CLIP]
    return pl.pallas_call(
        functools.partial(_band_kernel, s_new, s_new, wk, False),
        grid=(nb, 1),
        in_specs=[pl.BlockSpec((s_new, W_HEADS), lambda b, i: (q_blk0 + b, 0)),
                  pl.BlockSpec((wk, W_HEADS), lambda b, i: (b, 0)),
                  pl.BlockSpec((wk, W_HEADS), lambda b, i: (b, 0)),
                  pl.BlockSpec((N_HEADS, s_new, wk), lambda b, i: (0, 0, 0))],
        out_specs=pl.BlockSpec((s_new, W_HEADS), lambda b, i: (b, 0)),
        out_shape=jax.ShapeDtypeStruct((nb * s_new, W_HEADS), bf16),
        compiler_params=_params(("arbitrary", "arbitrary")),
        name="band_sample",
    )(q, kk, vv, bias)


CONV_STRIP = 32


def _conv_kernel(tt, init_ref, u_ref, w_ref, cb_ref, g_ref, b_ref, o_ref, ubuf):
    @pl.when(pl.program_id(1) == 0)
    def _():
        ubuf[0:CONV_HALO] = init_ref[0]

    ubuf[CONV_HALO:CONV_HALO + tt] = u_ref[...]
    rs = min(CONV_STRIP, tt)
    first = CONV_HALO - (CONV_W - 1)
    for s in range(tt // rs):
        acc = cb_ref[...] + w_ref[0:1, :] * ubuf[s * rs + first:s * rs + first + rs, :]
        for j in range(1, CONV_W):
            acc = acc + w_ref[j:j + 1, :] * ubuf[s * rs + first + j:s * rs + first + j + rs, :]
        mu = jnp.mean(acc, axis=-1, keepdims=True)
        cen = acc - mu
        var = jnp.mean(cen * cen, axis=-1, keepdims=True)
        y = cen * lax.rsqrt(var + EPS) * g_ref[...] + b_ref[...]
        o_ref[s * rs:(s + 1) * rs, :] = (y * _sigmoid(y)).astype(o_ref.dtype)
    if tt >= CONV_HALO:
        ubuf[0:CONV_HALO] = ubuf[tt:tt + CONV_HALO]


def _conv_module(u, init, conv_w, conv_b, ln_g, ln_b, nb, t, tt, blk0):
    c = u.shape[1]
    n_t = t // tt
    vec = pl.BlockSpec((1, c), lambda b, i: (0, 0))
    return pl.pallas_call(
        functools.partial(_conv_kernel, tt),
        grid=(nb, n_t),
        in_specs=[pl.BlockSpec((1, CONV_HALO, c), lambda b, i: (b, 0, 0)),
                  pl.BlockSpec((tt, c), lambda b, i: (blk0 + b * n_t + i, 0)),
                  pl.BlockSpec((CONV_W, c), lambda b, i: (0, 0)), vec, vec, vec],
        out_specs=pl.BlockSpec((tt, c), lambda b, i: (b * n_t + i, 0)),
        out_shape=jax.ShapeDtypeStruct((nb * t, c), bf16),
        scratch_shapes=[pltpu.VMEM((CONV_HALO + tt, c), f32)],
        compiler_params=_params(("arbitrary", "arbitrary")),
        name="conv",
    )(init, u, conv_w, conv_b, ln_g, ln_b)


def _mix_kernel(bounds, ya_p, ya_s, yb_p, yb_s, c_p, c_s, gates, pa, pb, pc, o_ref):
    d = o_ref.shape[1]

    def go(k):
        ya, yb, c = ((ya_p, yb_p, c_p), (ya_s, yb_s, c_s))[k]
        a = jnp.dot(ya[...], pa[...], preferred_element_type=f32)
        mixed = gates[:, 0:d].astype(f32) * a
        b = jnp.dot(yb[...], pb[...], preferred_element_type=f32)
        mixed = mixed + gates[:, d:2 * d].astype(f32) * b
        cc = jnp.dot(c[...], pc[...], preferred_element_type=f32)
        mixed = mixed + gates[:, 2 * d:3 * d].astype(f32) * cc
        o_ref[...] = mixed.astype(o_ref.dtype)

    _when_segment(pl.program_id(0), bounds, go)


def _mix(ya_p, ya_s, yb_p, yb_s, c_p, c_s, gates, pa, pb, pc, tm):
    m, d3 = gates.shape
    d = d3 // 3
    n_p, n_s = ya_p.shape[0] // tm, ya_s.shape[0] // tm
    bounds = (0, n_p, n_p + n_s)
    c_conv = c_p.shape[1]
    const = lambda shape: pl.BlockSpec(shape, lambda i: (0, 0))
    return pl.pallas_call(
        functools.partial(_mix_kernel, bounds),
        grid=(m // tm,),
        in_specs=[_seg_spec((tm, W_HEADS), 0, n_p), _seg_spec((tm, W_HEADS), n_p, n_s),
                  _seg_spec((tm, W_HEADS), 0, n_p), _seg_spec((tm, W_HEADS), n_p, n_s),
                  _seg_spec((tm, c_conv), 0, n_p), _seg_spec((tm, c_conv), n_p, n_s),
                  pl.BlockSpec((tm, d3), lambda i: (i, 0)),
                  const((W_HEADS, d)), const((W_HEADS, d)), const((c_conv, d))],
        out_specs=pl.BlockSpec((tm, d), lambda i: (i, 0)),
        out_shape=jax.ShapeDtypeStruct((m, d), bf16),
        compiler_params=_params(("arbitrary",)),
        name="mix",
    )(ya_p, ya_s, yb_p, yb_s, c_p, c_s, gates, pa, pb, pc)


def _route(logits):
    shape = logits.shape
    lane = lax.broadcasted_iota(jnp.int32, shape, 1)
    lane_f = lane.astype(f32)
    big = float(LANES)
    gl = jnp.where(lane < N_GROUPS, logits, -jnp.inf)
    g_max = jnp.max(gl, axis=-1, keepdims=True)
    g_idx = jnp.min(jnp.where(gl == g_max, lane_f, big), axis=-1, keepdims=True)
    g_sum = jnp.sum(jnp.exp(gl - g_max), axis=-1, keepdims=True)
    g_w = 1.0 / g_sum
    lo = N_GROUPS + g_idx * EXPERTS_PER_GROUP
    el = jnp.where((lane_f >= lo) & (lane_f < lo + EXPERTS_PER_GROUP), logits, -jnp.inf)
    m1 = jnp.max(el, axis=-1, keepdims=True)
    i1 = jnp.min(jnp.where(el == m1, lane_f, big), axis=-1, keepdims=True)
    el2 = jnp.where(lane_f == i1, -jnp.inf, el)
    m2 = jnp.max(el2, axis=-1, keepdims=True)
    i2 = jnp.min(jnp.where(el2 == m2, lane_f, big), axis=-1, keepdims=True)
    e21 = jnp.exp(m2 - m1)
    den = 1.0 + e21
    w1 = g_w * (1.0 / den)
    w2 = g_w * (e21 / den)
    eid = jnp.where(lane == 0, i1 - N_GROUPS, jnp.where(lane == 1, i2 - N_GROUPS, 0.0)).astype(jnp.int32)
    wgt = jnp.where(lane == 0, w1, jnp.where(lane == 1, w2, 0.0))
    return eid, wgt


def _outproj_kernel(n_slab, n_seg, bounds, *refs):
    xs = refs[:n_seg]
    mixed, wo, g2, wr, br, xo, h2o, eid_o, wgt_o = refs[n_seg:]
    tm = mixed.shape[0]
    i = pl.program_id(0)
    x = xs[0][...]
    for k in range(1, n_seg):
        x = jnp.where(i >= bounds[k], xs[k][...], x)
    xn = x + jnp.dot(mixed[...], wo[...], preferred_element_type=f32)
    xo[...] = xn
    h2 = _rms(xn, g2[...])
    for s in range(n_slab):
        h2o[pl.ds(s, tm, stride=n_slab), :] = h2[:, s * LANES:(s + 1) * LANES]
    h_hi = h2.astype(bf16)
    h_lo = (h2 - h_hi.astype(f32)).astype(bf16)
    hi = jnp.dot(h_hi, wr[...], preferred_element_type=f32)
    lo = jnp.dot(h_lo, wr[:, :LANES], preferred_element_type=f32)
    logits = hi[:, :LANES] + (hi[:, LANES:] + lo) + br[...]
    eid, wgt = _route(logits)
    eid_o[...] = eid
    wgt_o[...] = wgt


def _outproj(mixed, x_segs, wo, g2, wr, br, tm):
    m, d = mixed.shape
    n_slab = d // LANES
    counts = [a.shape[0] // tm for a in x_segs]
    bounds = [0]
    for cnt in counts:
        bounds.append(bounds[-1] + cnt)
    const = lambda shape: pl.BlockSpec(shape, lambda i: (0, 0))
    row = lambda w: pl.BlockSpec((tm, w), lambda i: (i, 0))
    sds = jax.ShapeDtypeStruct
    return pl.pallas_call(
        functools.partial(_outproj_kernel, n_slab, len(x_segs), tuple(bounds)),
        grid=(m // tm,),
        in_specs=[_seg_spec((tm, d), bounds[k], counts[k]) for k in range(len(x_segs))] +
                 [row(d), const((d, d)), const((1, d)), const((d, 2 * LANES)), const((1, LANES))],
        out_specs=[row(d), pl.BlockSpec((tm * n_slab, LANES), lambda i: (i, 0)), row(LANES), row(LANES)],
        out_shape=[sds((m, d), f32), sds((m * n_slab, LANES), f32), sds((m, LANES), jnp.int32),
                   sds((m, LANES), f32)],
        compiler_params=_params(("arbitrary",)),
        name="outproj",
    )(*x_segs, mixed, wo, g2, wr, br)


TB = 256
TD = 256


def _plan(eid):
    flat_e = eid.reshape(-1)
    n_assign = flat_e.shape[0]
    onehot = (flat_e[:, None] == jnp.arange(N_EXPERTS, dtype=jnp.int32)[None, :]).astype(jnp.int32)
    csum = jnp.cumsum(onehot, axis=0)
    counts = csum[-1]
    rank = jnp.sum(onehot * csum, axis=1) - 1
    n_blk_e = (counts + TB - 1) // TB
    blk_end = jnp.cumsum(n_blk_e)
    blk_start = blk_end - n_blk_e
    dest = blk_start[flat_e] * TB + rank
    n_blocks = -(-n_assign // TB) + N_EXPERTS
    blk_expert = jnp.minimum(jnp.searchsorted(blk_end, jnp.arange(n_blocks, dtype=jnp.int32), side='right'),
                             N_EXPERTS - 1).astype(jnp.int32)
    return dest.astype(jnp.int32), blk_expert, blk_end[-1:].astype(jnp.int32), n_blocks


DMA_UNROLL = 8


def _issue_rows(n, copy):
    def trip(t, carry):
        for r in range(DMA_UNROLL):
            copy(t * DMA_UNROLL + r).start()
        return carry

    lax.fori_loop(0, n // DMA_UNROLL, trip, 0)


def _dispatch_kernel(n_slab, dest_ref, h_ref, zero_ref, xs_ref, sems):
    del zero_ref
    i = pl.program_id(0)
    n = dest_ref.shape[2]
    slot = i % 2

    def copy(a):
        tok = i * (n // TOP_K) + a // TOP_K
        src = h_ref.at[pl.ds(pl.multiple_of(tok * n_slab, n_slab), n_slab), :]
        dst = xs_ref.at[pl.ds(pl.multiple_of(dest_ref[0, 0, a] * n_slab, n_slab), n_slab), :]
        return pltpu.make_async_copy(src, dst, sems.at[slot])

    def wait_step(s):
        rows = pl.ds(0, n * n_slab)
        pltpu.make_async_copy(h_ref.at[rows, :], xs_ref.at[rows, :], sems.at[s]).wait()

    _issue_rows(n, copy)
    pl.when(i > 0)(lambda: wait_step(1 - slot))
    pl.when(i == pl.num_programs(0) - 1)(lambda: wait_step(slot))


def _dispatch(h2_slab, dest, n_rows, n_slab):
    m = h2_slab.shape[0] // n_slab
    n_steps = m // TD
    dest3 = dest.reshape(n_steps, 1, TD * TOP_K)
    zeros = jnp.zeros((n_rows * n_slab, LANES), f32)
    return pl.pallas_call(
        functools.partial(_dispatch_kernel, n_slab),
        grid=(n_steps,),
        in_specs=[pl.BlockSpec((1, 1, TD * TOP_K), lambda i: (i, 0, 0), memory_space=pltpu.SMEM),
                  pl.BlockSpec(memory_space=pl.ANY),
                  pl.BlockSpec(memory_space=pl.ANY)],
        out_specs=pl.BlockSpec(memory_space=pl.ANY),
        out_shape=jax.ShapeDtypeStruct((n_rows * n_slab, LANES), f32),
        scratch_shapes=[pltpu.SemaphoreType.DMA((2,))],
        input_output_aliases={2: 0},
        compiler_params=_params(("arbitrary",)),
        name="dispatch",
    )(dest3, h2_slab, zeros)


def _expert_kernel(n_slab, be_ref, nu_ref, xs_ref, wgu_ref, wd_ref, o_ref):
    del be_ref

    @pl.when(pl.program_id(0) < nu_ref[0])
    def _():
        de = wd_ref.shape[1]
        x = jnp.concatenate([xs_ref[pl.ds(s, TB, stride=n_slab), :] for s in range(n_slab)], axis=1)
        gu = jnp.dot(x.astype(bf16), wgu_ref[0], preferred_element_type=f32)
        g = gu[:, :de]
        hmid = (g * _sigmoid(g)) * gu[:, de:]
        y = jnp.dot(hmid.astype(bf16), wd_ref[0], preferred_element_type=f32)
        for s in range(n_slab):
            o_ref[pl.ds(s, TB, stride=n_slab), :] = y[:, s * LANES:(s + 1) * LANES]


def _experts(xs, blk_expert, n_used, wgu, wd, n_blocks, n_slab):
    d, de2 = wgu.shape[1], wgu.shape[2]
    blk = lambda i, be, nu: (jnp.minimum(i, nu[0] - 1), 0)
    grid_spec = pltpu.PrefetchScalarGridSpec(
        num_scalar_prefetch=2,
        grid=(n_blocks,),
        in_specs=[pl.BlockSpec((TB * n_slab, LANES), blk),
                  pl.BlockSpec((1, d, de2), lambda i, be, nu: (be[jnp.minimum(i, nu[0] - 1)], 0, 0)),
                  pl.BlockSpec((1, de2 // 2, d), lambda i, be, nu: (be[jnp.minimum(i, nu[0] - 1)], 0, 0))],
        out_specs=pl.BlockSpec((TB * n_slab, LANES), blk),
    )
    return pl.pallas_call(
        functools.partial(_expert_kernel, n_slab),
        grid_spec=grid_spec,
        out_shape=jax.ShapeDtypeStruct(xs.shape, f32),
        input_output_aliases={2: 0},
        compiler_params=_params(("arbitrary",)),
        name="experts",
    )(blk_expert, n_used, xs, wgu, wd)


def _combine_kernel(n_slab, final, bounds, dest_ref, next_ref, x_ref, wgt_ref, g_ref, ys_ref, *rest):
    outs, (gbuf, sems) = rest[:-2], rest[-2:]
    i = pl.program_id(0)
    n = dest_ref.shape[2]
    tm = x_ref.shape[0]
    slot = i % 2

    def gather(idx_ref, s):
        def copy(a):
            src = ys_ref.at[pl.ds(pl.multiple_of(idx_ref[0, 0, a] * n_slab, n_slab), n_slab), :]
            dst = gbuf.at[s, pl.ds(pl.multiple_of(a * n_slab, n_slab), n_slab), :]
            return pltpu.make_async_copy(src, dst, sems.at[s])
        _issue_rows(n, copy)

    pl.when(i == 0)(lambda: gather(dest_ref, slot))
    pl.when(i + 1 < pl.num_programs(0))(lambda: gather(next_ref, 1 - slot))
    rows = pl.ds(0, n * n_slab)
    pltpu.make_async_copy(ys_ref.at[rows, :], gbuf.at[slot], sems.at[slot]).wait()

    ys = []
    for k in range(TOP_K):
        y = jnp.concatenate([gbuf[slot, pl.ds(k * n_slab + s, tm, stride=TOP_K * n_slab), :]
                             for s in range(n_slab)], axis=1)
        ys.append(y * wgt_ref[:, k:k + 1])
    x = x_ref[...] + (ys[0] + ys[1])
    if final:
        x = _rms(x, g_ref[...])

    def store(k):
        outs[k][...] = x

    _when_segment(i, bounds, store)


def _combine(x, ys, dest, wgt, g, n_slab, final, seg_rows):
    m, d = x.shape
    n_steps = m // TD
    dest3 = dest.reshape(n_steps, 1, TD * TOP_K)
    counts = [r // TD for r in seg_rows]
    bounds = [0]
    for cnt in counts:
        bounds.append(bounds[-1] + cnt)
    out_specs = [_seg_spec((TD, d), bounds[k], counts[k]) for k in range(len(seg_rows))]
    out_shape = [jax.ShapeDtypeStruct((r, d), f32) for r in seg_rows]
    idx_block = (1, 1, TD * TOP_K)
    return pl.pallas_call(
        functools.partial(_combine_kernel, n_slab, final, tuple(bounds)),
        grid=(n_steps,),
        in_specs=[pl.BlockSpec(idx_block, lambda i: (i, 0, 0), memory_space=pltpu.SMEM),
                  pl.BlockSpec(idx_block, lambda i: (jnp.minimum(i + 1, n_steps - 1), 0, 0),
                               memory_space=pltpu.SMEM),
                  pl.BlockSpec((TD, d), lambda i: (i, 0)),
                  pl.BlockSpec((TD, LANES), lambda i: (i, 0)),
                  pl.BlockSpec((1, d), lambda i: (0, 0)),
                  pl.BlockSpec(memory_space=pl.ANY)],
        out_specs=out_specs, out_shape=out_shape,
        scratch_shapes=[pltpu.VMEM((2, TD * TOP_K * n_slab, LANES), f32), pltpu.SemaphoreType.DMA((2,))],
        compiler_params=_params(("arbitrary",)),
        name="combine",
    )(dest3, dest3, x, wgt, g, ys)


def _pack_w_in(w_in, b_in, d):
    c_conv = d // 2
    half = TN // 2
    n_qkv = 6 * W_HEADS
    f0 = n_qkv
    c0 = f0 + N_HEADS
    g0 = c0 + 2 * c_conv
    def pack(a):
        parts = [a[..., :n_qkv]]
        for c in range(c_conv // half):
            parts.append(a[..., c0 + c * half:c0 + (c + 1) * half])
            parts.append(a[..., c0 + c_conv + c * half:c0 + c_conv + (c + 1) * half])
        parts.append(a[..., g0:])
        return jnp.concatenate(parts, axis=-1)

    w_main = pack(w_in.astype(bf16))
    b_main = pack(b_in)[None, :].astype(f32)
    w_f = jnp.pad(w_in[:, f0:f0 + N_HEADS], ((0, 0), (0, LANES - N_HEADS))).astype(bf16)
    b_f = jnp.pad(b_in[f0:f0 + N_HEADS], (0, LANES - N_HEADS))[None, :].astype(f32)
    return w_main, b_main, w_f, b_f


def kernel(x_prompt, x_sample, cache_a_k, cache_a_v, cache_b_k, cache_b_v, cache_b_logf, state_conv, norm_mix_g, w_in, b_in, rel_bias, conv_w, conv_b, conv_ln_g, conv_ln_b, w_proj_a, w_proj_b, w_proj_c, w_out, norm_ffn_g, w_router_group, b_router_group, w_router_expert, b_router_expert, w_e_gate, w_e_up, w_e_down, norm_final_g):
    nb_p, t_p, d = x_prompt.shape
    nb_s, t_s, _ = x_sample.shape
    depth = w_in.shape[0]
    past = cache_b_k.shape[2]
    a_rows = cache_a_k.shape[2]
    m_p, m_s = nb_p * t_p, nb_s * t_s
    m = m_p + m_s
    c_conv = d // 2
    n_slab = d // LANES
    tm = _row_tile(np.gcd(m_p, m_s), 512)
    tm_mix = _row_tile(np.gcd(m_p, m_s), 256)
    assert m_p % TD == 0 and m_s % TD == 0 and t_s % 16 == 0 and m_p % t_s == 0

    a_keep = min(WINDOW_A, t_p)
    x_segs = [x_prompt.reshape(m_p, d), x_sample.reshape(m_s, d)]
    kv_states = [jnp.zeros((depth, rows, W_HEADS), f32)
                 for rows in (nb_p * a_keep, nb_p * a_keep, m_p, m_p, m_s, m_s, m_s, m_s)]
    p_states, s_states = [], []
    for l in range(depth):
        w_main, b_main, w_f, b_f = _pack_w_in(w_in[l], b_in[l], d)
        (qa, ka16, va16, qb, kb16, vb16, u, gates, logf), kv_states = _inproj(
            x_segs, norm_mix_g[l][None, :], w_main, b_main, w_f, b_f, kv_states, l, m_p, t_p, a_keep, tm)

        ya_p = _band_attention_prompt(qa, ka16, va16, rel_bias[l], nb_p, t_p, 4 * CHUNK)
        kk = jnp.concatenate([cache_a_k[l].reshape(nb_s, a_rows, W_HEADS).astype(bf16),
                              ka16[m_p:].reshape(nb_s, t_s, W_HEADS)], axis=1).reshape(-1, W_HEADS)
        vv = jnp.concatenate([cache_a_v[l].reshape(nb_s, a_rows, W_HEADS).astype(bf16),
                              va16[m_p:].reshape(nb_s, t_s, W_HEADS)], axis=1).reshape(-1, W_HEADS)
        ya_s = _band_attention_sample(qa, kk, vv, rel_bias[l], nb_s, t_s, a_rows, m_p // t_s)

        logf_p = logf[:m_p].reshape(nb_p, t_p, N_HEADS)
        logf_s = logf[m_p:].reshape(nb_s, t_s, N_HEADS)
        cum_p = _cumsum_time(logf_p.transpose(0, 2, 1))
        f_p = cum_p.transpose(0, 2, 1).reshape(m_p, N_HEADS)
        yb_p = _fox_attention(_fox_expand("q", qb, f_p, m_p, tm), _fox_expand("k", kb16, f_p, m_p, tm),
                              _fox_expand("v", vb16, None, m_p, tm),
                              nb_p, t_p, t_p, _row_tile(t_p, 2048), _row_tile(t_p, 512))
        cum_s = _cumsum_time(jnp.concatenate([cache_b_logf[l].astype(f32), logf_s], axis=1).transpose(0, 2, 1))
        t_ks = past + t_s
        kk = jnp.concatenate([cache_b_k[l].reshape(nb_s, past, W_HEADS).astype(bf16),
                              kb16[m_p:].reshape(nb_s, t_s, W_HEADS)], axis=1).reshape(-1, W_HEADS)
        vv = jnp.concatenate([cache_b_v[l].reshape(nb_s, past, W_HEADS).astype(bf16),
                              vb16[m_p:].reshape(nb_s, t_s, W_HEADS)], axis=1).reshape(-1, W_HEADS)
        f_ks = cum_s.transpose(0, 2, 1)
        yb_s = _fox_attention(
            _fox_expand("q", qb, f_ks[:, past:].reshape(m_s, N_HEADS), m_s, t_s, blk0=m_p // t_s),
            _fox_expand("k", kk, f_ks.reshape(nb_s * t_ks, N_HEADS), nb_s * t_ks, t_ks),
            _fox_expand("v", vv, None, nb_s * t_ks, t_ks),
            nb_s, t_s, t_ks, t_s, t_ks)

        conv_args = (conv_w[l], conv_b[l][None, :], conv_ln_g[l][None, :], conv_ln_b[l][None, :])
        c_p = _conv_module(u, jnp.zeros((nb_p, CONV_HALO, c_conv), f32), *conv_args,
                           nb_p, t_p, _row_tile(t_p, 256), 0)
        init_s = jnp.pad(state_conv[l], ((0, 0), (CONV_HALO - (CONV_W - 1), 0), (0, 0)))
        c_s = _conv_module(u, init_s, *conv_args, nb_s, t_s, t_s, m_p // t_s)

        mixed = _mix(ya_p, ya_s, yb_p, yb_s, c_p, c_s, gates, w_proj_a[l].astype(bf16),
                     w_proj_b[l].astype(bf16), w_proj_c[l].astype(bf16), tm_mix)
        wr = jnp.pad(jnp.concatenate([w_router_group[l], w_router_expert[l]], axis=1),
                     ((0, 0), (0, LANES - N_GROUPS - N_EXPERTS)))
        wr_hi = wr.astype(bf16)
        wr_parts = jnp.concatenate([wr_hi, (wr - wr_hi.astype(f32)).astype(bf16)], axis=1)
        br = jnp.pad(jnp.concatenate([b_router_group[l], b_router_expert[l]]),
                     (0, LANES - N_GROUPS - N_EXPERTS))[None, :]
        x_mid, h2_slab, eid, wgt = _outproj(mixed, x_segs, w_out[l].astype(bf16), norm_ffn_g[l][None, :],
                                            wr_parts, br, tm_mix)

        dest, blk_expert, n_used, n_blocks = _plan(eid[:, :TOP_K])
        xs = _dispatch(h2_slab, dest, n_blocks * TB, n_slab)
        wgu = jnp.concatenate([w_e_gate[l], w_e_up[l]], axis=2).astype(bf16)
        ys = _experts(xs, blk_expert, n_used, wgu, w_e_down[l].astype(bf16), n_blocks, n_slab)
        final = l == depth - 1
        x_segs = _combine(x_mid, ys, dest, wgt, norm_final_g[None, :], n_slab, final,
                          (m_p, m_s) if final else (m,))

        u_p = u[:m_p].reshape(nb_p, t_p, c_conv)
        u_s = u[m_p:].reshape(nb_s, t_s, c_conv)
        p_states.append((logf_p, u_p[:, -(CONV_W - 1):]))
        s_states.append((logf_s, jnp.concatenate([state_conv[l], u_s], axis=1)[:, -(CONV_W - 1):]))

    y_prompt = x_segs[0].reshape(nb_p, t_p, d)
    y_sample = x_segs[1].reshape(nb_s, t_s, d)
    stack = lambda states, k: jnp.stack([st[k] for st in states], axis=0)
    heads = lambda a, nb, t: a.reshape(depth, nb, t, N_HEADS, HEAD_DIM)
    ka_p, va_p, kb_p, vb_p, ka_s, va_s, kb_s, vb_s = kv_states
    return (y_prompt, y_sample,
            heads(ka_p, nb_p, a_keep), heads(va_p, nb_p, a_keep), heads(kb_p, nb_p, t_p), heads(vb_p, nb_p, t_p),
            stack(p_states, 0), stack(p_states, 1),
            heads(ka_s, nb_s, t_s), heads(va_s, nb_s, t_s), heads(kb_s, nb_s, t_s), heads(vb_s, nb_s, t_s),
            stack(s_states, 0), stack(s_states, 1))
```

```python
import functools

import jax
import jax.numpy as jnp
import numpy as np
from jax import lax
from jax.experimental import pallas as pl
from jax.experimental.pallas import tpu as pltpu

f32 = jnp.float32
bf16 = jnp.bfloat16

HEAD_DIM = 64
N_HEADS = 8
W_HEADS = N_HEADS * HEAD_DIM
CHUNK = 64
WINDOW_A = 8 * CHUNK
REL_CLIP = 128
CONV_W = 31
CONV_HALO = 32
N_GROUPS = 4
EXPERTS_PER_GROUP = 8
N_EXPERTS = N_GROUPS * EXPERTS_PER_GROUP
TOP_K = 2
SCALE = HEAD_DIM ** -0.5
EPS = 1e-6
NEG_INF = -1e30
LANES = 128
MIB = 1024 * 1024


def _params(sem, vmem_mib=48):
    return pltpu.CompilerParams(dimension_semantics=sem, vmem_limit_bytes=vmem_mib * MIB)


def _row_tile(m, cap):
    t = cap
    while m % t:
        t //= 2
    return t


def _sigmoid(z):
    return 1.0 / (1.0 + jnp.exp(-z))


def _rms(x, g):
    return x * lax.rsqrt(jnp.mean(x * x, axis=-1, keepdims=True) + EPS) * g


def _when_segment(i, bounds, fn):
    for k in range(len(bounds) - 1):
        pl.when((i >= bounds[k]) & (i < bounds[k + 1]))(functools.partial(fn, k))


def _seg_spec(block, start, count, width_axes=1):
    zeros = (0,) * width_axes
    return pl.BlockSpec(block, lambda i, *_: (jnp.clip(i - start, 0, count - 1),) + zeros)


TN = 1024
SEG_PER_TILE = TN // W_HEADS
N_QKV_TILES = 6 // SEG_PER_TILE


def _inproj_kernel(n_seg, bounds, n_conv, n_prompt, *refs):
    xs = refs[:n_seg]
    (g_ref, w_ref, b_ref, wf_ref, bf_ref) = refs[n_seg:n_seg + 5]
    outs = refs[n_seg + 5 + 8:-1]
    h_scr = refs[-1]
    (qa, ka16, va16, qb, kb16, vb16, u, gates, logf, ka_p, va_p, kb_p, vb_p, ka_s, va_s, kb_s, vb_s) = outs
    i = pl.program_id(0)
    j = pl.program_id(1)

    def norm(k):
        h_scr[...] = _rms(xs[k][...], g_ref[...]).astype(bf16)

    @pl.when(j == 0)
    def _():
        _when_segment(i, bounds, norm)
        zf = jnp.dot(h_scr[...], wf_ref[...], preferred_element_type=f32) + bf_ref[...]
        lf = jnp.minimum(zf, 0.0) - jnp.log1p(jnp.exp(-jnp.abs(zf)))
        logf[...] = lf[:, :N_HEADS]

    def project():
        return jnp.dot(h_scr[...], w_ref[...], preferred_element_type=f32) + b_ref[...]

    segments = ((qa, None, None), (ka16, ka_p, ka_s), (va16, va_p, va_s),
                (qb, None, None), (kb16, kb_p, kb_s), (vb16, vb_p, vb_s))
    is_prompt = i < n_prompt
    for t in range(N_QKV_TILES):
        for prompt_rows in (True, False):
            @pl.when((j == t) & (is_prompt if prompt_rows else ~is_prompt))
            def _(t=t, prompt_rows=prompt_rows):
                z = project()
                for n in range(SEG_PER_TILE):
                    zn = z[:, n * W_HEADS:(n + 1) * W_HEADS]
                    copy, state_p, state_s = segments[t * SEG_PER_TILE + n]
                    copy[...] = zn.astype(bf16)
                    state = state_p if prompt_rows else state_s
                    if state is not None:
                        state[...] = zn

    half = TN // 2
    for c in range(n_conv):
        @pl.when(j == N_QKV_TILES + c)
        def _(c=c):
            z = project()
            glu = z[:, :half] * _sigmoid(z[:, half:])
            tm = glu.shape[0]
            n_slab = n_conv * half // LANES
            for q in range(half // LANES):
                u[pl.ds(c * (half // LANES) + q, tm, stride=n_slab), :] = glu[:, q * LANES:(q + 1) * LANES]

    @pl.when(j >= N_QKV_TILES + n_conv)
    def _():
        gates[...] = _sigmoid(project()).astype(bf16)


def _inproj(x_segs, g, w_main, b_main, w_f, b_f, states, layer, m_p, t_p, a_keep, tm):
    d = x_segs[0].shape[1]
    m = sum(a.shape[0] for a in x_segs)
    c_conv = d // 2
    n_conv = c_conv // (TN // 2)
    n_gate = 3 * d // TN
    n_col = N_QKV_TILES + n_conv + n_gate
    assert w_main.shape == (d, n_col * TN)
    counts = [a.shape[0] // tm for a in x_segs]
    bounds = [0]
    for cnt in counts:
        bounds.append(bounds[-1] + cnt)
    n_seg = len(x_segs)
    n_p, n_s = m_p // tm, (m - m_p) // tm
    per_seq, keep = t_p // tm, a_keep // tm

    in_specs = [pl.BlockSpec((tm, d), (lambda i, j, s=bounds[k], n=counts[k]: (jnp.clip(i - s, 0, n - 1), 0)))
                for k in range(n_seg)]
    in_specs += [
        pl.BlockSpec((1, d), lambda i, j: (0, 0)),
        pl.BlockSpec((d, TN), lambda i, j: (0, j)),
        pl.BlockSpec((1, TN), lambda i, j: (0, j)),
        pl.BlockSpec((d, LANES), lambda i, j: (0, 0)),
        pl.BlockSpec((1, LANES), lambda i, j: (0, 0)),
    ] + [pl.BlockSpec(memory_space=pl.ANY)] * 8
    row = lambda width: pl.BlockSpec((tm, width), lambda i, j: (i, 0))

    def tail_rows(i, j):
        ip = jnp.minimum(i, n_p - 1)
        return (layer, (ip // per_seq) * keep + jnp.maximum(ip % per_seq - (per_seq - keep), 0), 0)

    state_block = (None, tm, W_HEADS)
    tail_spec = pl.BlockSpec(state_block, tail_rows)
    prompt_spec = pl.BlockSpec(state_block, lambda i, j: (layer, jnp.minimum(i, n_p - 1), 0))
    sample_spec = pl.BlockSpec(state_block, lambda i, j: (layer, jnp.clip(i - n_p, 0, n_s - 1), 0))
    out_specs = [row(W_HEADS)] * 6 + [
        pl.BlockSpec((tm * (c_conv // LANES), LANES), lambda i, j: (i, 0)),
        pl.BlockSpec((tm, TN), lambda i, j: (i, jnp.clip(j - (N_QKV_TILES + n_conv), 0, n_gate - 1))),
        row(N_HEADS),
        tail_spec, tail_spec, prompt_spec, prompt_spec, sample_spec, sample_spec, sample_spec, sample_spec]
    sds = jax.ShapeDtypeStruct
    out_shape = [sds((m, W_HEADS), bf16)] * 6 + [sds((m * (c_conv // LANES), LANES), f32), sds((m, 3 * d), bf16),
                                                 sds((m, N_HEADS), f32)]
    out_shape += [sds(s.shape, s.dtype) for s in states]
    n_in = n_seg + 5
    outs = pl.pallas_call(
        functools.partial(_inproj_kernel, n_seg, tuple(bounds), n_conv, n_p),
        grid=(m // tm, n_col),
        in_specs=in_specs, out_specs=out_specs, out_shape=out_shape,
        scratch_shapes=[pltpu.VMEM((tm, d), bf16)],
        input_output_aliases={n_in + k: 9 + k for k in range(8)},
        compiler_params=_params(("arbitrary", "arbitrary"), 58),
        name="inproj",
    )(*x_segs, g, w_main, b_main, w_f, b_f, *states)
    return outs[:9], outs[9:]


TC = 512


def _cumsum_kernel(x_ref, o_ref, carry):
    @pl.when(pl.program_id(1) == 0)
    def _():
        carry[...] = jnp.zeros_like(carry)

    blk = x_ref[0]
    r = lax.broadcasted_iota(jnp.int32, (TC, TC), 0)
    c = lax.broadcasted_iota(jnp.int32, (TC, TC), 1)
    tri = jnp.where(r <= c, 1.0, 0.0).astype(f32)
    cs = jnp.dot(blk, tri, precision=lax.Precision.HIGHEST, preferred_element_type=f32) + carry[:, 0:1]
    o_ref[0] = cs
    carry[...] = jnp.broadcast_to(cs[:, TC - 1:TC], carry.shape)


def _cumsum_time(x):
    nb, h, t = x.shape
    tp = -(-t // TC) * TC
    xp = jnp.pad(x, ((0, 0), (0, 0), (0, tp - t)))
    out = pl.pallas_call(
        _cumsum_kernel,
        grid=(nb, tp // TC),
        in_specs=[pl.BlockSpec((1, h, TC), lambda b, k: (b, 0, k))],
        out_specs=pl.BlockSpec((1, h, TC), lambda b, k: (b, 0, k)),
        out_shape=jax.ShapeDtypeStruct((nb, h, tp), f32),
        scratch_shapes=[pltpu.VMEM((h, LANES), f32)],
        compiler_params=_params(("arbitrary", "arbitrary")),
        name="cumsum",
    )(xp)
    return out[:, :, :t]


F_PARTS = 3


def _fox_expand_kernel(kind, x_ref, f_ref, o_ref):
    tm = x_ref.shape[0]
    lane = lax.broadcasted_iota(jnp.int32, (tm, LANES), 1)
    for h in range(N_HEADS):
        pair = x_ref[:, (h // 2) * LANES:(h // 2 + 1) * LANES].astype(f32)
        if h % 2:
            pair = pltpu.roll(pair, HEAD_DIM, axis=1)
        if kind == "v":
            spare = jnp.where(lane == HEAD_DIM, 1.0, 0.0)
        else:
            if kind == "q":
                pair = pair * SCALE
            ones_at, parts_at, sign = (F_PARTS, 0, 1.0) if kind == "q" else (0, F_PARTS, -1.0)
            ones = (lane >= HEAD_DIM + ones_at) & (lane < HEAD_DIM + ones_at + F_PARTS)
            spare = jnp.where(ones, 1.0, 0.0)
            rest = f_ref[:, h:h + 1] * sign
            for n in range(F_PARTS):
                part = rest.astype(bf16).astype(f32)
                rest = rest - part
                spare = jnp.where(lane == HEAD_DIM + parts_at + n, part, spare)
        o_ref[h] = jnp.where(lane < HEAD_DIM, pair, spare).astype(o_ref.dtype)


def _fox_expand(kind, x, f, rows, tm, blk0=0):
    if f is None:
        f = jnp.zeros((rows, N_HEADS), f32)
    return pl.pallas_call(
        functools.partial(_fox_expand_kernel, kind),
        grid=(rows // tm,),
        in_specs=[pl.BlockSpec((tm, W_HEADS), lambda i: (blk0 + i, 0)),
                  pl.BlockSpec((tm, N_HEADS), lambda i: (i, 0))],
        out_specs=pl.BlockSpec((N_HEADS, tm, LANES), lambda i: (0, i, 0)),
        out_shape=jax.ShapeDtypeStruct((N_HEADS, rows, LANES), bf16),
        compiler_params=_params(("arbitrary",)),
        name="fox_expand_" + kind,
    )(x, f)


FOX_HEADS_PER_TRIP = 2


def _fox_kernel(tq, tk, off, q_ref, k_ref, v_ref, o_ref, m_scr, acc_scr):
    i = pl.program_id(1)
    j = pl.program_id(2)

    @pl.when(j == 0)
    def _():
        m_scr[...] = jnp.full_like(m_scr, NEG_INF)
        acc_scr[...] = jnp.zeros_like(acc_scr)

    q_first = i * tq + off
    q_last = q_first + tq - 1
    k_first = j * tk
    k_last = k_first + tk - 1

    def body(masked):
        if masked:
            kpos = k_first + lax.broadcasted_iota(jnp.int32, (tq, tk), 1)
            qpos = q_first + lax.broadcasted_iota(jnp.int32, (tq, tk), 0)
            vis = kpos <= qpos

        def head(h):
            s = lax.dot_general(q_ref[h], k_ref[h], (((1,), (1,)), ((), ())), preferred_element_type=f32)
            if masked:
                s = jnp.where(vis, s, NEG_INF)
            m_old = m_scr[h]
            m_new = jnp.maximum(m_old, jnp.max(s, axis=-1, keepdims=True))
            pr = jnp.exp(s - m_new[:, 0:1])
            pv = jnp.dot(pr.astype(bf16), v_ref[h], preferred_element_type=f32)
            acc_scr[h] = jnp.exp(m_old - m_new) * acc_scr[h] + pv
            m_scr[h] = m_new

        def trip(g, carry):
            for n in range(FOX_HEADS_PER_TRIP):
                head(g * FOX_HEADS_PER_TRIP + n)
            return carry

        lax.fori_loop(0, N_HEADS // FOX_HEADS_PER_TRIP, trip, 0)

    pl.when(k_last <= q_first)(functools.partial(body, False))
    pl.when((k_first <= q_last) & (k_last > q_first))(functools.partial(body, True))

    @pl.when(j == pl.num_programs(2) - 1)
    def _():
        lane = lax.broadcasted_iota(jnp.int32, (tq, LANES), 1)
        for p in range(N_HEADS // 2):
            even = acc_scr[2 * p]
            odd = acc_scr[2 * p + 1]
            even = even / even[:, HEAD_DIM:HEAD_DIM + 1]
            odd = odd / odd[:, HEAD_DIM:HEAD_DIM + 1]
            o_ref[:, p * LANES:(p + 1) * LANES] = jnp.where(
                lane < HEAD_DIM, even, pltpu.roll(odd, HEAD_DIM, axis=1)).astype(o_ref.dtype)


def _fox_attention(q, k, v, nb, t_q, t_k, tq, tk):
    nq, nk = t_q // tq, t_k // tk
    off = t_k - t_q

    def last_k(i):
        return jnp.minimum((i * tq + tq - 1 + off) // tk, nk - 1)

    kv_spec = pl.BlockSpec((N_HEADS, tk, LANES), lambda b, i, j: (0, b * nk + jnp.minimum(j, last_k(i)), 0))
    return pl.pallas_call(
        functools.partial(_fox_kernel, tq, tk, off),
        grid=(nb, nq, nk),
        in_specs=[pl.BlockSpec((N_HEADS, tq, LANES), lambda b, i, j: (0, b * nq + i, 0)), kv_spec, kv_spec],
        out_specs=pl.BlockSpec((tq, W_HEADS), lambda b, i, j: (b * nq + i, 0)),
        out_shape=jax.ShapeDtypeStruct((nb * t_q, W_HEADS), bf16),
        scratch_shapes=[pltpu.VMEM((N_HEADS, tq, LANES), f32), pltpu.VMEM((N_HEADS, tq, LANES), f32)],
        compiler_params=_params(("arbitrary", "arbitrary", "arbitrary")),
        name="fox",
    )(q, k, v)


def _band_kernel(rows, gq, wk, has_prev, *refs):
    if has_prev:
        q_ref, kp_ref, kc_ref, vp_ref, vc_ref, bias_ref, o_ref, k_scr, v_scr = refs
        k_scr[0:WINDOW_A] = kp_ref[...]
        k_scr[WINDOW_A:WINDOW_A + rows] = kc_ref[...]
        v_scr[0:WINDOW_A] = vp_ref[...]
        v_scr[WINDOW_A:WINDOW_A + rows] = vc_ref[...]
        k_src, v_src = k_scr, v_scr
    else:
        q_ref, k_src, v_src, bias_ref, o_ref = refs
    i = pl.program_id(1)
    lane = lax.broadcasted_iota(jnp.int32, (gq, LANES), 1)
    low = lane < HEAD_DIM
    for g in range(rows // gq):
        r0 = g * gq
        if has_prev:
            key_pos = (i - 1) * WINDOW_A + r0 + lax.broadcasted_iota(jnp.int32, (gq, wk), 1)
            vis = key_pos >= 0
        for p in range(N_HEADS // 2):
            cols = slice(p * LANES, (p + 1) * LANES)
            q2 = q_ref[r0:r0 + gq, cols]
            kw = k_src[r0:r0 + wk, cols]
            vw = v_src[r0:r0 + wk, cols]
            outs = []
            for half in range(2):
                h = 2 * p + half
                qm = jnp.where(low if half == 0 else ~low, q2, jnp.zeros_like(q2))
                s = lax.dot_general(qm, kw, (((1,), (1,)), ((), ())), preferred_element_type=f32) * SCALE
                s = s + bias_ref[h]
                if has_prev:
                    s = jnp.where(vis, s, NEG_INF)
                m = jnp.max(s, axis=-1, keepdims=True)
                pr = jnp.exp(s - m)
                l = jnp.sum(pr, axis=-1, keepdims=True)
                pv = jnp.dot(pr.astype(bf16), vw, preferred_element_type=f32)
                outs.append(pv / l)
            o_ref[r0:r0 + gq, cols] = jnp.where(low, outs[0], outs[1]).astype(o_ref.dtype)


def _rel_bias_table(rel_bias, gq, wk, q_shift):
    period = wk + gq
    j = np.arange(period)
    k = np.where(j < wk, j, j - period)
    line = rel_bias.astype(f32)[:, np.clip(q_shift - k, -REL_CLIP, REL_CLIP) + REL_CLIP]
    tiled = jnp.tile(line, (1, gq))[:, :gq * (period - 1)]
    return tiled.reshape(-1, gq, period - 1)[:, :, :wk]


def _band_bias(rel_bias, gq, wk, q_shift):
    r = np.arange(gq)[:, None]
    s = np.arange(wk)[None, :]
    band0 = (r // CHUNK) * CHUNK + q_shift - WINDOW_A
    ok = (s >= band0) & (s < band0 + WINDOW_A + CHUNK)
    return jnp.where(ok[None], _rel_bias_table(rel_bias, gq, wk, q_shift), NEG_INF)


def _band_attention_prompt(q, k, v, rel_bias, nb, t, gq):
    rows = WINDOW_A
    wk = WINDOW_A + gq
    n_steps = t // rows
    bias = _band_bias(rel_bias, gq, wk, WINDOW_A)
    cur = pl.BlockSpec((rows, W_HEADS), lambda b, i: (b * n_steps + i, 0))
    prev = pl.BlockSpec((rows, W_HEADS), lambda b, i: (b * n_steps + jnp.maximum(i - 1, 0), 0))
    return pl.pallas_call(
        functools.partial(_band_kernel, rows, gq, wk, True),
        grid=(nb, n_steps),
        in_specs=[cur, prev, cur, prev, cur,
                  pl.BlockSpec((N_HEADS, gq, wk), lambda b, i: (0, 0, 0))],
        out_specs=cur,
        out_shape=jax.ShapeDtypeStruct((nb * t, W_HEADS), bf16),
        scratch_shapes=[pltpu.VMEM((2 * rows, W_HEADS), bf16), pltpu.VMEM((2 * rows, W_HEADS), bf16)],
        compiler_params=_params(("arbitrary", "arbitrary")),
        name="band_prompt",
    )(q, k, k, v, v, bias)


def _band_attention_sample(q, kk, vv, rel_bias, nb, s_new, l_cache, q_blk0):
    wk = l_cache + s_new
    bias = _rel_bias_table(rel_bias, s_new, wk, l_cache)
    return pl.pallas_call(
        functools.partial(_band_kernel, s_new, s_new, wk, False),
        grid=(nb, 1),
        in_specs=[pl.BlockSpec((s_new, W_HEADS), lambda b, i: (q_blk0 + b, 0)),
                  pl.BlockSpec((wk, W_HEADS), lambda b, i: (b, 0)),
                  pl.BlockSpec((wk, W_HEADS), lambda b, i: (b, 0)),
                  pl.BlockSpec((N_HEADS, s_new, wk), lambda b, i: (0, 0, 0))],
        out_specs=pl.BlockSpec((s_new, W_HEADS), lambda b, i: (b, 0)),
        out_shape=jax.ShapeDtypeStruct((nb * s_new, W_HEADS), bf16),
        compiler_params=_params(("arbitrary", "arbitrary")),
        name="band_sample",
    )(q, kk, vv, bias)


CONV_STRIP = 32


def _conv_kernel(tt, n_slab, init_ref, u_ref, w_ref, cb_ref, g_ref, b_ref, o_ref, ubuf, acc_scr):
    halo = CONV_HALO * n_slab

    @pl.when(pl.program_id(1) == 0)
    def _():
        ubuf[0:halo] = init_ref[0]

    ubuf[halo:halo + tt * n_slab] = u_ref[...]
    rs = min(CONV_STRIP, tt)
    first = CONV_HALO - (CONV_W - 1)

    def per_step(slab):
        return jnp.broadcast_to(slab[None], (rs, n_slab, LANES)).reshape(rs * n_slab, LANES)

    for s in range(tt // rs):
        acc = per_step(cb_ref[...])
        for j in range(CONV_W):
            r0 = (s * rs + first + j) * n_slab
            acc = acc + per_step(w_ref[j * n_slab:(j + 1) * n_slab, :]) * ubuf[r0:r0 + rs * n_slab, :]
        acc_scr[...] = acc
        rows = jnp.concatenate([acc_scr[pl.ds(q, rs, stride=n_slab), :] for q in range(n_slab)], axis=1)
        mu = jnp.mean(rows, axis=-1, keepdims=True)
        cen = rows - mu
        var = jnp.mean(cen * cen, axis=-1, keepdims=True)
        y = cen * lax.rsqrt(var + EPS) * g_ref[...] + b_ref[...]
        o_ref[s * rs:(s + 1) * rs, :] = (y * _sigmoid(y)).astype(o_ref.dtype)
    if tt >= CONV_HALO:
        ubuf[0:halo] = ubuf[tt * n_slab:tt * n_slab + halo]


def _conv_module(u_slab, init, conv_w, conv_b, ln_g, ln_b, nb, t, tt, blk0):
    c = conv_w.shape[1]
    n_slab = c // LANES
    n_t = t // tt
    vec = pl.BlockSpec((1, c), lambda b, i: (0, 0))
    return pl.pallas_call(
        functools.partial(_conv_kernel, tt, n_slab),
        grid=(nb, n_t),
        in_specs=[pl.BlockSpec((1, CONV_HALO * n_slab, LANES), lambda b, i: (b, 0, 0)),
                  pl.BlockSpec((tt * n_slab, LANES), lambda b, i: (blk0 + b * n_t + i, 0)),
                  pl.BlockSpec((CONV_W * n_slab, LANES), lambda b, i: (0, 0)),
                  pl.BlockSpec((n_slab, LANES), lambda b, i: (0, 0)), vec, vec],
        out_specs=pl.BlockSpec((tt, c), lambda b, i: (b * n_t + i, 0)),
        out_shape=jax.ShapeDtypeStruct((nb * t, c), bf16),
        scratch_shapes=[pltpu.VMEM(((CONV_HALO + tt) * n_slab, LANES), f32),
                        pltpu.VMEM((min(CONV_STRIP, tt) * n_slab, LANES), f32)],
        compiler_params=_params(("arbitrary", "arbitrary")),
        name="conv",
    )(init.reshape(nb, CONV_HALO * n_slab, LANES), u_slab, conv_w.reshape(CONV_W * n_slab, LANES),
      conv_b.reshape(n_slab, LANES), ln_g, ln_b)


def _mix_kernel(bounds, ya_p, ya_s, yb_p, yb_s, c_p, c_s, gates, pa, pb, pc, o_ref):
    d = o_ref.shape[1]

    def go(k):
        ya, yb, c = ((ya_p, yb_p, c_p), (ya_s, yb_s, c_s))[k]
        a = jnp.dot(ya[...], pa[...], preferred_element_type=f32)
        mixed = gates[:, 0:d].astype(f32) * a
        b = jnp.dot(yb[...], pb[...], preferred_element_type=f32)
        mixed = mixed + gates[:, d:2 * d].astype(f32) * b
        cc = jnp.dot(c[...], pc[...], preferred_element_type=f32)
        mixed = mixed + gates[:, 2 * d:3 * d].astype(f32) * cc
        o_ref[...] = mixed.astype(o_ref.dtype)

    _when_segment(pl.program_id(0), bounds, go)


def _mix(ya_p, ya_s, yb_p, yb_s, c_p, c_s, gates, pa, pb, pc, tm):
    m, d3 = gates.shape
    d = d3 // 3
    n_p, n_s = ya_p.shape[0] // tm, ya_s.shape[0] // tm
    bounds = (0, n_p, n_p + n_s)
    c_conv = c_p.shape[1]
    const = lambda shape: pl.BlockSpec(shape, lambda i: (0, 0))
    return pl.pallas_call(
        functools.partial(_mix_kernel, bounds),
        grid=(m // tm,),
        in_specs=[_seg_spec((tm, W_HEADS), 0, n_p), _seg_spec((tm, W_HEADS), n_p, n_s),
                  _seg_spec((tm, W_HEADS), 0, n_p), _seg_spec((tm, W_HEADS), n_p, n_s),
                  _seg_spec((tm, c_conv), 0, n_p), _seg_spec((tm, c_conv), n_p, n_s),
                  pl.BlockSpec((tm, d3), lambda i: (i, 0)),
                  const((W_HEADS, d)), const((W_HEADS, d)), const((c_conv, d))],
        out_specs=pl.BlockSpec((tm, d), lambda i: (i, 0)),
        out_shape=jax.ShapeDtypeStruct((m, d), bf16),
        compiler_params=_params(("arbitrary",)),
        name="mix",
    )(ya_p, ya_s, yb_p, yb_s, c_p, c_s, gates, pa, pb, pc)


def _route(logits):
    shape = logits.shape
    lane = lax.broadcasted_iota(jnp.int32, shape, 1)
    lane_f = lane.astype(f32)
    big = float(LANES)
    gl = jnp.where(lane < N_GROUPS, logits, -jnp.inf)
    g_max = jnp.max(gl, axis=-1, keepdims=True)
    g_idx = jnp.min(jnp.where(gl == g_max, lane_f, big), axis=-1, keepdims=True)
    g_sum = jnp.sum(jnp.exp(gl - g_max), axis=-1, keepdims=True)
    g_w = 1.0 / g_sum
    lo = N_GROUPS + g_idx * EXPERTS_PER_GROUP
    el = jnp.where((lane_f >= lo) & (lane_f < lo + EXPERTS_PER_GROUP), logits, -jnp.inf)
    m1 = jnp.max(el, axis=-1, keepdims=True)
    i1 = jnp.min(jnp.where(el == m1, lane_f, big), axis=-1, keepdims=True)
    el2 = jnp.where(lane_f == i1, -jnp.inf, el)
    m2 = jnp.max(el2, axis=-1, keepdims=True)
    i2 = jnp.min(jnp.where(el2 == m2, lane_f, big), axis=-1, keepdims=True)
    e21 = jnp.exp(m2 - m1)
    den = 1.0 + e21
    w1 = g_w * (1.0 / den)
    w2 = g_w * (e21 / den)
    eid = jnp.where(lane == 0, i1 - N_GROUPS, jnp.where(lane == 1, i2 - N_GROUPS, 0.0)).astype(jnp.int32)
    wgt = jnp.where(lane == 0, w1, jnp.where(lane == 1, w2, 0.0))
    return eid, wgt


def _outproj_kernel(n_slab, n_seg, bounds, *refs):
    xs = refs[:n_seg]
    mixed, wo, g2, wr, br, xo, h2o, eid_o, wgt_o = refs[n_seg:]
    tm = mixed.shape[0]
    i = pl.program_id(0)
    x = xs[0][...]
    for k in range(1, n_seg):
        x = jnp.where(i >= bounds[k], xs[k][...], x)
    xn = x + jnp.dot(mixed[...], wo[...], preferred_element_type=f32)
    xo[...] = xn
    h2 = _rms(xn, g2[...])
    for s in range(n_slab):
        h2o[pl.ds(s, tm, stride=n_slab), :] = h2[:, s * LANES:(s + 1) * LANES]
    h_hi = h2.astype(bf16)
    h_lo = (h2 - h_hi.astype(f32)).astype(bf16)
    hi = jnp.dot(h_hi, wr[...], preferred_element_type=f32)
    lo = jnp.dot(h_lo, wr[:, :LANES], preferred_element_type=f32)
    logits = hi[:, :LANES] + (hi[:, LANES:] + lo) + br[...]
    eid, wgt = _route(logits)
    eid_o[...] = eid
    wgt_o[...] = wgt


def _outproj(mixed, x_segs, wo, g2, wr, br, tm):
    m, d = mixed.shape
    n_slab = d // LANES
    counts = [a.shape[0] // tm for a in x_segs]
    bounds = [0]
    for cnt in counts:
        bounds.append(bounds[-1] + cnt)
    const = lambda shape: pl.BlockSpec(shape, lambda i: (0, 0))
    row = lambda w: pl.BlockSpec((tm, w), lambda i: (i, 0))
    sds = jax.ShapeDtypeStruct
    return pl.pallas_call(
        functools.partial(_outproj_kernel, n_slab, len(x_segs), tuple(bounds)),
        grid=(m // tm,),
        in_specs=[_seg_spec((tm, d), bounds[k], counts[k]) for k in range(len(x_segs))] +
                 [row(d), const((d, d)), const((1, d)), const((d, 2 * LANES)), const((1, LANES))],
        out_specs=[row(d), pl.BlockSpec((tm * n_slab, LANES), lambda i: (i, 0)), row(LANES), row(LANES)],
        out_shape=[sds((m, d), f32), sds((m * n_slab, LANES), f32), sds((m, LANES), jnp.int32),
                   sds((m, LANES), f32)],
        compiler_params=_params(("arbitrary",)),
        name="outproj",
    )(*x_segs, mixed, wo, g2, wr, br)


TB = 256
TD = 256


def _plan(eid):
    flat_e = eid.reshape(-1)
    n_assign = flat_e.shape[0]
    onehot = (flat_e[:, None] == jnp.arange(N_EXPERTS, dtype=jnp.int32)[None, :]).astype(jnp.int32)
    csum = jnp.cumsum(onehot, axis=0)
    counts = csum[-1]
    rank = jnp.sum(onehot * csum, axis=1) - 1
    n_blk_e = (counts + TB - 1) // TB
    blk_end = jnp.cumsum(n_blk_e)
    blk_start = blk_end - n_blk_e
    dest = blk_start[flat_e] * TB + rank
    n_blocks = -(-n_assign // TB) + N_EXPERTS
    blk_expert = jnp.minimum(jnp.searchsorted(blk_end, jnp.arange(n_blocks, dtype=jnp.int32), side='right'),
                             N_EXPERTS - 1).astype(jnp.int32)
    return dest.astype(jnp.int32), blk_expert, blk_end[-1:].astype(jnp.int32), n_blocks


DMA_UNROLL = 8


def _issue_rows(n, copy):
    def trip(t, carry):
        for r in range(DMA_UNROLL):
            copy(t * DMA_UNROLL + r).start()
        return carry

    lax.fori_loop(0, n // DMA_UNROLL, trip, 0)


def _dispatch_kernel(n_slab, dest_ref, h_ref, zero_ref, xs_ref, sem):
    del zero_ref
    n = dest_ref.shape[2]

    def copy(a):
        src = h_ref.at[pl.ds(pl.multiple_of((a // TOP_K) * n_slab, n_slab), n_slab), :]
        dst = xs_ref.at[pl.ds(pl.multiple_of(dest_ref[0, 0, a] * n_slab, n_slab), n_slab), :]
        return pltpu.make_async_copy(src, dst, sem)

    _issue_rows(n, copy)
    for half in range(TOP_K):
        rows = pl.ds(0, (n // TOP_K) * n_slab)
        pltpu.make_async_copy(h_ref, xs_ref.at[rows, :], sem).wait()


def _dispatch(h2_slab, dest, n_rows, n_slab, td):
    m = h2_slab.shape[0] // n_slab
    n_steps = m // td
    dest3 = dest.reshape(n_steps, 1, td * TOP_K)
    zeros = jnp.zeros((n_rows * n_slab, LANES), f32)
    return pl.pallas_call(
        functools.partial(_dispatch_kernel, n_slab),
        grid=(n_steps,),
        in_specs=[pl.BlockSpec((1, 1, td * TOP_K), lambda i: (i, 0, 0), memory_space=pltpu.SMEM),
                  pl.BlockSpec((td * n_slab, LANES), lambda i: (i, 0)),
                  pl.BlockSpec(memory_space=pl.ANY)],
        out_specs=pl.BlockSpec(memory_space=pl.ANY),
        out_shape=jax.ShapeDtypeStruct((n_rows * n_slab, LANES), f32),
        scratch_shapes=[pltpu.SemaphoreType.DMA(())],
        input_output_aliases={2: 0},
        compiler_params=_params(("arbitrary",)),
        name="dispatch",
    )(dest3, h2_slab, zeros)


def _expert_kernel(n_slab, be_ref, nu_ref, xs_ref, wgu_ref, wd_ref, o_ref):
    del be_ref

    @pl.when(pl.program_id(0) < nu_ref[0])
    def _():
        de = wd_ref.shape[1]
        x = jnp.concatenate([xs_ref[pl.ds(s, TB, stride=n_slab), :] for s in range(n_slab)], axis=1)
        gu = jnp.dot(x.astype(bf16), wgu_ref[0], preferred_element_type=f32)
        g = gu[:, :de]
        hmid = (g * _sigmoid(g)) * gu[:, de:]
        y = jnp.dot(hmid.astype(bf16), wd_ref[0], preferred_element_type=f32)
        for s in range(n_slab):
            o_ref[pl.ds(s, TB, stride=n_slab), :] = y[:, s * LANES:(s + 1) * LANES]


def _experts(xs, blk_expert, n_used, wgu, wd, n_blocks, n_slab):
    d, de2 = wgu.shape[1], wgu.shape[2]
    blk = lambda i, be, nu: (jnp.minimum(i, nu[0] - 1), 0)
    grid_spec = pltpu.PrefetchScalarGridSpec(
        num_scalar_prefetch=2,
        grid=(n_blocks,),
        in_specs=[pl.BlockSpec((TB * n_slab, LANES), blk),
                  pl.BlockSpec((1, d, de2), lambda i, be, nu: (be[jnp.minimum(i, nu[0] - 1)], 0, 0)),
                  pl.BlockSpec((1, de2 // 2, d), lambda i, be, nu: (be[jnp.minimum(i, nu[0] - 1)], 0, 0))],
        out_specs=pl.BlockSpec((TB * n_slab, LANES), blk),
    )
    return pl.pallas_call(
        functools.partial(_expert_kernel, n_slab),
        grid_spec=grid_spec,
        out_shape=jax.ShapeDtypeStruct(xs.shape, f32),
        input_output_aliases={2: 0},
        compiler_params=_params(("arbitrary",)),
        name="experts",
    )(blk_expert, n_used, xs, wgu, wd)


def _combine_kernel(n_slab, final, bounds, dest_ref, next_ref, x_ref, wgt_ref, g_ref, ys_ref, *rest):
    outs, (gbuf, sems) = rest[:-2], rest[-2:]
    i = pl.program_id(0)
    n = dest_ref.shape[2]
    tm = x_ref.shape[0]
    slot = i % 2

    def gather(idx_ref, s):
        def copy(a):
            src = ys_ref.at[pl.ds(pl.multiple_of(idx_ref[0, 0, a] * n_slab, n_slab), n_slab), :]
            dst = gbuf.at[s, pl.ds(pl.multiple_of(a * n_slab, n_slab), n_slab), :]
            return pltpu.make_async_copy(src, dst, sems.at[s])
        _issue_rows(n, copy)

    pl.when(i == 0)(lambda: gather(dest_ref, slot))
    pl.when(i + 1 < pl.num_programs(0))(lambda: gather(next_ref, 1 - slot))
    rows = pl.ds(0, n * n_slab)
    pltpu.make_async_copy(ys_ref.at[rows, :], gbuf.at[slot], sems.at[slot]).wait()

    ys = []
    for k in range(TOP_K):
        y = jnp.concatenate([gbuf[slot, pl.ds(k * n_slab + s, tm, stride=TOP_K * n_slab), :]
                             for s in range(n_slab)], axis=1)
        ys.append(y * wgt_ref[:, k:k + 1])
    x = x_ref[...] + (ys[0] + ys[1])
    if final:
        x = _rms(x, g_ref[...])

    def store(k):
        outs[k][...] = x

    _when_segment(i, bounds, store)


def _combine(x, ys, dest, wgt, g, n_slab, final, seg_rows):
    m, d = x.shape
    n_steps = m // TD
    dest3 = dest.reshape(n_steps, 1, TD * TOP_K)
    counts = [r // TD for r in seg_rows]
    bounds = [0]
    for cnt in counts:
        bounds.append(bounds[-1] + cnt)
    out_specs = [_seg_spec((TD, d), bounds[k], counts[k]) for k in range(len(seg_rows))]
    out_shape = [jax.ShapeDtypeStruct((r, d), f32) for r in seg_rows]
    idx_block = (1, 1, TD * TOP_K)
    return pl.pallas_call(
        functools.partial(_combine_kernel, n_slab, final, tuple(bounds)),
        grid=(n_steps,),
        in_specs=[pl.BlockSpec(idx_block, lambda i: (i, 0, 0), memory_space=pltpu.SMEM),
                  pl.BlockSpec(idx_block, lambda i: (jnp.minimum(i + 1, n_steps - 1), 0, 0),
                               memory_space=pltpu.SMEM),
                  pl.BlockSpec((TD, d), lambda i: (i, 0)),
                  pl.BlockSpec((TD, LANES), lambda i: (i, 0)),
                  pl.BlockSpec((1, d), lambda i: (0, 0)),
                  pl.BlockSpec(memory_space=pl.ANY)],
        out_specs=out_specs, out_shape=out_shape,
        scratch_shapes=[pltpu.VMEM((2, TD * TOP_K * n_slab, LANES), f32), pltpu.SemaphoreType.DMA((2,))],
        compiler_params=_params(("arbitrary",)),
        name="combine",
    )(dest3, dest3, x, wgt, g, ys)


def _pack_w_in(w_in, b_in, d):
    c_conv = d // 2
    half = TN // 2
    n_qkv = 6 * W_HEADS
    f0 = n_qkv
    c0 = f0 + N_HEADS
    g0 = c0 + 2 * c_conv
    def pack(a):
        parts = [a[..., :n_qkv]]
        for c in range(c_conv // half):
            parts.append(a[..., c0 + c * half:c0 + (c + 1) * half])
            parts.append(a[..., c0 + c_conv + c * half:c0 + c_conv + (c + 1) * half])
        parts.append(a[..., g0:])
        return jnp.concatenate(parts, axis=-1)

    w_main = pack(w_in.astype(bf16))
    b_main = pack(b_in)[None, :].astype(f32)
    w_f = jnp.pad(w_in[:, f0:f0 + N_HEADS], ((0, 0), (0, LANES - N_HEADS))).astype(bf16)
    b_f = jnp.pad(b_in[f0:f0 + N_HEADS], (0, LANES - N_HEADS))[None, :].astype(f32)
    return w_main, b_main, w_f, b_f


def kernel(x_prompt, x_sample, cache_a_k, cache_a_v, cache_b_k, cache_b_v, cache_b_logf, state_conv, norm_mix_g, w_in, b_in, rel_bias, conv_w, conv_b, conv_ln_g, conv_ln_b, w_proj_a, w_proj_b, w_proj_c, w_out, norm_ffn_g, w_router_group, b_router_group, w_router_expert, b_router_expert, w_e_gate, w_e_up, w_e_down, norm_final_g):
    nb_p, t_p, d = x_prompt.shape
    nb_s, t_s, _ = x_sample.shape
    depth = w_in.shape[0]
    past = cache_b_k.shape[2]
    a_rows = cache_a_k.shape[2]
    m_p, m_s = nb_p * t_p, nb_s * t_s
    m = m_p + m_s
    c_conv = d // 2
    n_slab = d // LANES
    tm = _row_tile(np.gcd(m_p, m_s), 512)
    tm_mix = _row_tile(np.gcd(m_p, m_s), 256)
    assert m_p % TD == 0 and m_s % TD == 0 and t_s % 16 == 0 and m_p % t_s == 0

    a_keep = min(WINDOW_A, t_p)
    x_segs = [x_prompt.reshape(m_p, d), x_sample.reshape(m_s, d)]
    kv_states = [jnp.zeros((depth, rows, W_HEADS), f32)
                 for rows in (nb_p * a_keep, nb_p * a_keep, m_p, m_p, m_s, m_s, m_s, m_s)]
    p_states, s_states = [], []
    for l in range(depth):
        w_main, b_main, w_f, b_f = _pack_w_in(w_in[l], b_in[l], d)
        (qa, ka16, va16, qb, kb16, vb16, u, gates, logf), kv_states = _inproj(
            x_segs, norm_mix_g[l][None, :], w_main, b_main, w_f, b_f, kv_states, l, m_p, t_p, a_keep, tm)

        ya_p = _band_attention_prompt(qa, ka16, va16, rel_bias[l], nb_p, t_p, 4 * CHUNK)
        kk = jnp.concatenate([cache_a_k[l].reshape(nb_s, a_rows, W_HEADS).astype(bf16),
                              ka16[m_p:].reshape(nb_s, t_s, W_HEADS)], axis=1).reshape(-1, W_HEADS)
        vv = jnp.concatenate([cache_a_v[l].reshape(nb_s, a_rows, W_HEADS).astype(bf16),
                              va16[m_p:].reshape(nb_s, t_s, W_HEADS)], axis=1).reshape(-1, W_HEADS)
        ya_s = _band_attention_sample(qa, kk, vv, rel_bias[l], nb_s, t_s, a_rows, m_p // t_s)

        logf_p = logf[:m_p].reshape(nb_p, t_p, N_HEADS)
        logf_s = logf[m_p:].reshape(nb_s, t_s, N_HEADS)
        cum_p = _cumsum_time(logf_p.transpose(0, 2, 1))
        f_p = cum_p.transpose(0, 2, 1).reshape(m_p, N_HEADS)
        yb_p = _fox_attention(_fox_expand("q", qb, f_p, m_p, tm), _fox_expand("k", kb16, f_p, m_p, tm),
                              _fox_expand("v", vb16, None, m_p, tm),
                              nb_p, t_p, t_p, _row_tile(t_p, 2048), _row_tile(t_p, 512))
        cum_s = _cumsum_time(jnp.concatenate([cache_b_logf[l].astype(f32), logf_s], axis=1).transpose(0, 2, 1))
        t_ks = past + t_s
        kk = jnp.concatenate([cache_b_k[l].reshape(nb_s, past, W_HEADS).astype(bf16),
                              kb16[m_p:].reshape(nb_s, t_s, W_HEADS)], axis=1).reshape(-1, W_HEADS)
        vv = jnp.concatenate([cache_b_v[l].reshape(nb_s, past, W_HEADS).astype(bf16),
                              vb16[m_p:].reshape(nb_s, t_s, W_HEADS)], axis=1).reshape(-1, W_HEADS)
        f_ks = cum_s.transpose(0, 2, 1)
        yb_s = _fox_attention(
            _fox_expand("q", qb, f_ks[:, past:].reshape(m_s, N_HEADS), m_s, t_s, blk0=m_p // t_s),
            _fox_expand("k", kk, f_ks.reshape(nb_s * t_ks, N_HEADS), nb_s * t_ks, t_ks),
            _fox_expand("v", vv, None, nb_s * t_ks, t_ks),
            nb_s, t_s, t_ks, t_s, t_ks)

        conv_args = (conv_w[l], conv_b[l][None, :], conv_ln_g[l][None, :], conv_ln_b[l][None, :])
        c_p = _conv_module(u, jnp.zeros((nb_p, CONV_HALO, c_conv), f32), *conv_args,
                           nb_p, t_p, _row_tile(t_p, 256), 0)
        init_s = jnp.pad(state_conv[l], ((0, 0), (CONV_HALO - (CONV_W - 1), 0), (0, 0)))
        c_s = _conv_module(u, init_s, *conv_args, nb_s, t_s, t_s, m_p // t_s)

        mixed = _mix(ya_p, ya_s, yb_p, yb_s, c_p, c_s, gates, w_proj_a[l].astype(bf16),
                     w_proj_b[l].astype(bf16), w_proj_c[l].astype(bf16), tm_mix)
        wr = jnp.pad(jnp.concatenate([w_router_group[l], w_router_expert[l]], axis=1),
                     ((0, 0), (0, LANES - N_GROUPS - N_EXPERTS)))
        wr_hi = wr.astype(bf16)
        wr_parts = jnp.concatenate([wr_hi, (wr - wr_hi.astype(f32)).astype(bf16)], axis=1)
        br = jnp.pad(jnp.concatenate([b_router_group[l], b_router_expert[l]]),
                     (0, LANES - N_GROUPS - N_EXPERTS))[None, :]
        x_mid, h2_slab, eid, wgt = _outproj(mixed, x_segs, w_out[l].astype(bf16), norm_ffn_g[l][None, :],
                                            wr_parts, br, tm_mix)

        dest, blk_expert, n_used, n_blocks = _plan(eid[:, :TOP_K])
        xs = _dispatch(h2_slab, dest, n_blocks * TB, n_slab, tm)
        wgu = jnp.concatenate([w_e_gate[l], w_e_up[l]], axis=2).astype(bf16)
        ys = _experts(xs, blk_expert, n_used, wgu, w_e_down[l].astype(bf16), n_blocks, n_slab)
        final = l == depth - 1
        x_segs = _combine(x_mid, ys, dest, wgt, norm_final_g[None, :], n_slab, final,
                          (m_p, m_s) if final else (m,))

        u_steps = u.reshape(m, c_conv // LANES, LANES)
        u_p = u_steps[:m_p].reshape(nb_p, t_p, -1, LANES)[:, -(CONV_W - 1):].reshape(nb_p, CONV_W - 1, c_conv)
        u_s = u_steps[m_p:].reshape(nb_s, t_s, c_conv)
        p_states.append((logf_p, u_p))
        s_states.append((logf_s, jnp.concatenate([state_conv[l], u_s], axis=1)[:, -(CONV_W - 1):]))

    y_prompt = x_segs[0].reshape(nb_p, t_p, d)
    y_sample = x_segs[1].reshape(nb_s, t_s, d)
    stack = lambda states, k: jnp.stack([st[k] for st in states], axis=0)
    heads = lambda a, nb, t: a.reshape(depth, nb, t, N_HEADS, HEAD_DIM)
    ka_p, va_p, kb_p, vb_p, ka_s, va_s, kb_s, vb_s = kv_states
    return (y_prompt, y_sample,
            heads(ka_p, nb_p, a_keep), heads(va_p, nb_p, a_keep), heads(kb_p, nb_p, t_p), heads(vb_p, nb_p, t_p),
            stack(p_states, 0), stack(p_states, 1),
            heads(ka_s, nb_s, t_s), heads(va_s, nb_s, t_s), heads(kb_s, nb_s, t_s), heads(vb_s, nb_s, t_s),
            stack(s_states, 0), stack(s_states, 1))
```

```python
import functools

import jax
import jax.numpy as jnp
import numpy as np
from jax import lax
from jax.experimental import pallas as pl
from jax.experimental.pallas import tpu as pltpu

f32 = jnp.float32
bf16 = jnp.bfloat16

HEAD_DIM = 64
N_HEADS = 8
W_HEADS = N_HEADS * HEAD_DIM
CHUNK = 64
WINDOW_A = 8 * CHUNK
REL_CLIP = 128
CONV_W = 31
CONV_HALO = 32
N_GROUPS = 4
EXPERTS_PER_GROUP = 8
N_EXPERTS = N_GROUPS * EXPERTS_PER_GROUP
TOP_K = 2
SCALE = HEAD_DIM ** -0.5
EPS = 1e-6
NEG_INF = -1e30
LANES = 128
SUBLANES = 8
MIB = 1024 * 1024


def _params(sem, vmem_mib=48):
    return pltpu.CompilerParams(dimension_semantics=sem, vmem_limit_bytes=vmem_mib * MIB)


def _row_tile(m, cap):
    t = cap
    while m % t:
        t //= 2
    return t


def _sigmoid(z):
    return 0.5 * jnp.tanh(0.5 * z) + 0.5


def _rms(x, g):
    return x * lax.rsqrt(jnp.mean(x * x, axis=-1, keepdims=True) + EPS) * g


def _when_segment(i, bounds, fn):
    for k in range(len(bounds) - 1):
        pl.when((i >= bounds[k]) & (i < bounds[k + 1]))(functools.partial(fn, k))


def _seg_spec(block, start, count, width_axes=1):
    zeros = (0,) * width_axes
    return pl.BlockSpec(block, lambda i, *_: (jnp.clip(i - start, 0, count - 1),) + zeros)


TN = 1024
SEG_PER_TILE = TN // W_HEADS
N_QKV_TILES = 6 // SEG_PER_TILE


def _inproj_kernel(n_seg, bounds, n_conv, n_prompt, *refs):
    xs = refs[:n_seg]
    (g_ref, w_ref, b_ref, wf_ref, bf_ref) = refs[n_seg:n_seg + 5]
    outs = refs[n_seg + 5 + 8:-1]
    h_scr = refs[-1]
    (qa, ka16, va16, qb, kb16, vb16, u, gates, logf, ka_p, va_p, kb_p, vb_p, ka_s, va_s, kb_s, vb_s) = outs
    i = pl.program_id(0)
    j = pl.program_id(1)

    def norm(k):
        h_scr[...] = _rms(xs[k][...], g_ref[...]).astype(bf16)

    @pl.when(j == 0)
    def _():
        _when_segment(i, bounds, norm)
        zf = jnp.dot(h_scr[...], wf_ref[...], preferred_element_type=f32) + bf_ref[...]
        lf = jnp.minimum(zf, 0.0) - jnp.log1p(jnp.exp(-jnp.abs(zf)))
        logf[...] = lf[:, :N_HEADS]

    def project():
        return jnp.dot(h_scr[...], w_ref[...], preferred_element_type=f32) + b_ref[...]

    segments = ((qa, None, None), (ka16, ka_p, ka_s), (va16, va_p, va_s),
                (qb, None, None), (kb16, kb_p, kb_s), (vb16, vb_p, vb_s))
    is_prompt = i < n_prompt
    for t in range(N_QKV_TILES):
        for prompt_rows in (True, False):
            @pl.when((j == t) & (is_prompt if prompt_rows else ~is_prompt))
            def _(t=t, prompt_rows=prompt_rows):
                z = project()
                for n in range(SEG_PER_TILE):
                    zn = z[:, n * W_HEADS:(n + 1) * W_HEADS]
                    copy, state_p, state_s = segments[t * SEG_PER_TILE + n]
                    copy[...] = zn.astype(bf16)
                    state = state_p if prompt_rows else state_s
                    if state is not None:
                        state[...] = zn

    half = TN // 2
    for c in range(n_conv):
        @pl.when(j == N_QKV_TILES + c)
        def _(c=c):
            z = project()
            glu = z[:, :half] * _sigmoid(z[:, half:])
            tm = glu.shape[0]
            n_slab = n_conv * half // LANES
            for q in range(half // LANES):
                u[pl.ds(c * (half // LANES) + q, tm, stride=n_slab), :] = glu[:, q * LANES:(q + 1) * LANES]

    @pl.when(j >= N_QKV_TILES + n_conv)
    def _():
        gates[...] = _sigmoid(project()).astype(bf16)


def _inproj(x_segs, g, w_main, b_main, w_f, b_f, states, layer, m_p, t_p, a_keep, tm):
    d = x_segs[0].shape[1]
    m = sum(a.shape[0] for a in x_segs)
    c_conv = d // 2
    n_conv = c_conv // (TN // 2)
    n_gate = 3 * d // TN
    n_col = N_QKV_TILES + n_conv + n_gate
    assert w_main.shape == (d, n_col * TN)
    counts = [a.shape[0] // tm for a in x_segs]
    bounds = [0]
    for cnt in counts:
        bounds.append(bounds[-1] + cnt)
    n_seg = len(x_segs)
    n_p, n_s = m_p // tm, (m - m_p) // tm
    per_seq, keep = t_p // tm, a_keep // tm

    in_specs = [pl.BlockSpec((tm, d), (lambda i, j, s=bounds[k], n=counts[k]: (jnp.clip(i - s, 0, n - 1), 0)))
                for k in range(n_seg)]
    in_specs += [
        pl.BlockSpec((1, d), lambda i, j: (0, 0)),
        pl.BlockSpec((d, TN), lambda i, j: (0, j)),
        pl.BlockSpec((1, TN), lambda i, j: (0, j)),
        pl.BlockSpec((d, LANES), lambda i, j: (0, 0)),
        pl.BlockSpec((1, LANES), lambda i, j: (0, 0)),
    ] + [pl.BlockSpec(memory_space=pl.ANY)] * 8
    row = lambda width: pl.BlockSpec((tm, width), lambda i, j: (i, 0))

    def tail_rows(i, j):
        ip = jnp.minimum(i, n_p - 1)
        return (layer, (ip // per_seq) * keep + jnp.maximum(ip % per_seq - (per_seq - keep), 0), 0)

    state_block = (None, tm, W_HEADS)
    tail_spec = pl.BlockSpec(state_block, tail_rows)
    prompt_spec = pl.BlockSpec(state_block, lambda i, j: (layer, jnp.minimum(i, n_p - 1), 0))
    sample_spec = pl.BlockSpec(state_block, lambda i, j: (layer, jnp.clip(i - n_p, 0, n_s - 1), 0))
    out_specs = [row(W_HEADS)] * 6 + [
        pl.BlockSpec((tm * (c_conv // LANES), LANES), lambda i, j: (i, 0)),
        pl.BlockSpec((tm, TN), lambda i, j: (i, jnp.clip(j - (N_QKV_TILES + n_conv), 0, n_gate - 1))),
        row(N_HEADS),
        tail_spec, tail_spec, prompt_spec, prompt_spec, sample_spec, sample_spec, sample_spec, sample_spec]
    sds = jax.ShapeDtypeStruct
    out_shape = [sds((m, W_HEADS), bf16)] * 6 + [sds((m * (c_conv // LANES), LANES), f32), sds((m, 3 * d), bf16),
                                                 sds((m, N_HEADS), f32)]
    out_shape += [sds(s.shape, s.dtype) for s in states]
    n_in = n_seg + 5
    outs = pl.pallas_call(
        functools.partial(_inproj_kernel, n_seg, tuple(bounds), n_conv, n_p),
        grid=(m // tm, n_col),
        in_specs=in_specs, out_specs=out_specs, out_shape=out_shape,
        scratch_shapes=[pltpu.VMEM((tm, d), bf16)],
        input_output_aliases={n_in + k: 9 + k for k in range(8)},
        compiler_params=_params(("arbitrary", "arbitrary"), 58),
        name="inproj",
    )(*x_segs, g, w_main, b_main, w_f, b_f, *states)
    return outs[:9], outs[9:]


TC = 512


def _cumsum_kernel(x_ref, o_ref, carry):
    @pl.when(pl.program_id(1) == 0)
    def _():
        carry[...] = jnp.zeros_like(carry)

    blk = x_ref[0]
    r = lax.broadcasted_iota(jnp.int32, (TC, TC), 0)
    c = lax.broadcasted_iota(jnp.int32, (TC, TC), 1)
    tri = jnp.where(r <= c, 1.0, 0.0).astype(f32)
    cs = jnp.dot(blk, tri, precision=lax.Precision.HIGHEST, preferred_element_type=f32) + carry[:, 0:1]
    o_ref[0] = cs
    carry[...] = jnp.broadcast_to(cs[:, TC - 1:TC], carry.shape)


def _cumsum_time(x):
    nb, h, t = x.shape
    tp = -(-t // TC) * TC
    xp = jnp.pad(x, ((0, 0), (0, 0), (0, tp - t)))
    out = pl.pallas_call(
        _cumsum_kernel,
        grid=(nb, tp // TC),
        in_specs=[pl.BlockSpec((1, h, TC), lambda b, k: (b, 0, k))],
        out_specs=pl.BlockSpec((1, h, TC), lambda b, k: (b, 0, k)),
        out_shape=jax.ShapeDtypeStruct((nb, h, tp), f32),
        scratch_shapes=[pltpu.VMEM((h, LANES), f32)],
        compiler_params=_params(("arbitrary", "arbitrary")),
        name="cumsum",
    )(xp)
    return out[:, :, :t]


F_PARTS = 3


def _spare_base(h):
    return HEAD_DIM * (1 - h % 2)


def _fox_expand_kernel(kind, x_ref, f_ref, place_ref, o_ref):
    tm = x_ref.shape[0]
    lane = lax.broadcasted_iota(jnp.int32, (tm, LANES), 1)
    if kind != "v":
        rest = f_ref[...]
        stack = jnp.where(lane < (F_PARTS + 1) * N_HEADS, 1.0, 0.0)
        for n in range(F_PARTS):
            part = rest.astype(bf16).astype(f32)
            rest = rest - part
            stack = jnp.where((lane >= n * N_HEADS) & (lane < (n + 1) * N_HEADS), part, stack)
        spare_all = jnp.dot(stack.astype(bf16), place_ref[...], preferred_element_type=f32)
    for h in range(N_HEADS):
        pair = x_ref[:, (h // 2) * LANES:(h // 2 + 1) * LANES].astype(f32)
        if kind == "q":
            pair = pair * SCALE
        if kind == "v":
            spare = jnp.where(lane == _spare_base(h), 1.0, 0.0)
        else:
            spare = spare_all[:, h * LANES:(h + 1) * LANES]
        own = (lane < HEAD_DIM) if h % 2 == 0 else (lane >= HEAD_DIM)
        o_ref[h] = jnp.where(own, pair, spare).astype(o_ref.dtype)


def _fox_placement(kind):
    place = np.zeros((LANES, N_HEADS * LANES), np.float32)
    for h in range(N_HEADS):
        base = h * LANES + _spare_base(h)
        for n in range(F_PARTS):
            if kind == "q":
                place[n * N_HEADS + h, base + n] = 1.0
                place[F_PARTS * N_HEADS + h, base + F_PARTS + n] = 1.0
            else:
                place[F_PARTS * N_HEADS + h, base + n] = 1.0
                place[n * N_HEADS + h, base + F_PARTS + n] = -1.0
    return jnp.asarray(place, bf16)


def _fox_expand(kind, x, f, rows, tm, blk0=0):
    f_lanes = jnp.zeros((rows, LANES), f32) if f is None else jnp.tile(f, (1, LANES // N_HEADS))
    return pl.pallas_call(
        functools.partial(_fox_expand_kernel, kind),
        grid=(rows // tm,),
        in_specs=[pl.BlockSpec((tm, W_HEADS), lambda i: (blk0 + i, 0)),
                  pl.BlockSpec((tm, LANES), lambda i: (i, 0)),
                  pl.BlockSpec((LANES, N_HEADS * LANES), lambda i: (0, 0))],
        out_specs=pl.BlockSpec((N_HEADS, tm, LANES), lambda i: (0, i, 0)),
        out_shape=jax.ShapeDtypeStruct((N_HEADS, rows, LANES), bf16),
        compiler_params=_params(("arbitrary",)),
        name="fox_expand_" + kind,
    )(x, f_lanes, _fox_placement(kind))


FOX_HEADS_PER_TRIP = 2


def _fox_kernel(tq, tk, off, q_ref, k_ref, v_ref, o_ref, m_scr, acc_scr):
    i = pl.program_id(1)
    j = pl.program_id(2)

    @pl.when(j == 0)
    def _():
        m_scr[...] = jnp.full_like(m_scr, NEG_INF)
        acc_scr[...] = jnp.zeros_like(acc_scr)

    q_first = i * tq + off
    q_last = q_first + tq - 1
    k_first = j * tk
    k_last = k_first + tk - 1

    def body(row0, masked):
        rows = slice(row0, tq)
        if masked:
            kpos = k_first + lax.broadcasted_iota(jnp.int32, (tq - row0, tk), 1)
            qpos = q_first + row0 + lax.broadcasted_iota(jnp.int32, (tq - row0, tk), 0)
            vis = kpos <= qpos

        def head(h):
            s = lax.dot_general(q_ref[h, rows], k_ref[h], (((1,), (1,)), ((), ())), preferred_element_type=f32)
            if masked:
                s = jnp.where(vis, s, NEG_INF)
            m_old = m_scr[h, rows]
            m_new = jnp.maximum(m_old, jnp.max(s, axis=-1, keepdims=True))
            pr = jnp.exp(s - m_new[:, 0:1])
            pv = jnp.dot(pr.astype(bf16), v_ref[h], preferred_element_type=f32)
            acc_scr[h, rows] = jnp.exp(m_old - m_new) * acc_scr[h, rows] + pv
            m_scr[h, rows] = m_new

        def trip(g, carry):
            for n in range(FOX_HEADS_PER_TRIP):
                head(g * FOX_HEADS_PER_TRIP + n)
            return carry

        lax.fori_loop(0, N_HEADS // FOX_HEADS_PER_TRIP, trip, 0)

    pl.when(k_last <= q_first)(functools.partial(body, 0, False))
    if tq % tk == 0 and off % tk == 0:
        for c in range(tq // tk):
            pl.when(k_first == q_first + c * tk)(functools.partial(body, c * tk, True))
    else:
        pl.when((k_first <= q_last) & (k_last > q_first))(functools.partial(body, 0, True))

    @pl.when(j == pl.num_programs(2) - 1)
    def _():
        lane = lax.broadcasted_iota(jnp.int32, (tq, LANES), 1)
        for p in range(N_HEADS // 2):
            even = acc_scr[2 * p]
            odd = acc_scr[2 * p + 1]
            even = even / even[:, _spare_base(0):_spare_base(0) + 1]
            odd = odd / odd[:, _spare_base(1):_spare_base(1) + 1]
            o_ref[:, p * LANES:(p + 1) * LANES] = jnp.where(lane < HEAD_DIM, even, odd).astype(o_ref.dtype)


def _fox_attention(q, k, v, nb, t_q, t_k, tq, tk):
    nq, nk = t_q // tq, t_k // tk
    off = t_k - t_q

    def last_k(i):
        return jnp.minimum((i * tq + tq - 1 + off) // tk, nk - 1)

    kv_spec = pl.BlockSpec((N_HEADS, tk, LANES), lambda b, i, j: (0, b * nk + jnp.minimum(j, last_k(i)), 0))
    return pl.pallas_call(
        functools.partial(_fox_kernel, tq, tk, off),
        grid=(nb, nq, nk),
        in_specs=[pl.BlockSpec((N_HEADS, tq, LANES), lambda b, i, j: (0, b * nq + i, 0)), kv_spec, kv_spec],
        out_specs=pl.BlockSpec((tq, W_HEADS), lambda b, i, j: (b * nq + i, 0)),
        out_shape=jax.ShapeDtypeStruct((nb * t_q, W_HEADS), bf16),
        scratch_shapes=[pltpu.VMEM((N_HEADS, tq, LANES), f32), pltpu.VMEM((N_HEADS, tq, LANES), f32)],
        compiler_params=_params(("arbitrary", "arbitrary", "arbitrary")),
        name="fox",
    )(q, k, v)


def _band_kernel(rows, gq, wk, has_prev, *refs):
    if has_prev:
        q_ref, kp_ref, kc_ref, vp_ref, vc_ref, bias_ref, o_ref, k_scr, v_scr = refs
        k_scr[0:WINDOW_A] = kp_ref[...]
        k_scr[WINDOW_A:WINDOW_A + rows] = kc_ref[...]
        v_scr[0:WINDOW_A] = vp_ref[...]
        v_scr[WINDOW_A:WINDOW_A + rows] = vc_ref[...]
        k_src, v_src = k_scr, v_scr
    else:
        q_ref, k_src, v_src, bias_ref, o_ref = refs
    i = pl.program_id(1)
    lane = lax.broadcasted_iota(jnp.int32, (gq, LANES), 1)
    low = lane < HEAD_DIM
    for g in range(rows // gq):
        r0 = g * gq
        if has_prev:
            key_pos = (i - 1) * WINDOW_A + r0 + lax.broadcasted_iota(jnp.int32, (gq, wk), 1)
            vis = key_pos >= 0
        for p in range(N_HEADS // 2):
            cols = slice(p * LANES, (p + 1) * LANES)
            q2 = q_ref[r0:r0 + gq, cols]
            kw = k_src[r0:r0 + wk, cols]
            vw = v_src[r0:r0 + wk, cols]
            outs = []
            for half in range(2):
                h = 2 * p + half
                qm = jnp.where(low if half == 0 else ~low, q2, jnp.zeros_like(q2))
                s = lax.dot_general(qm, kw, (((1,), (1,)), ((), ())), preferred_element_type=f32) * SCALE
                s = s + bias_ref[h]
                if has_prev:
                    s = jnp.where(vis, s, NEG_INF)
                m = jnp.max(s, axis=-1, keepdims=True)
                pr = jnp.exp(s - m)
                l = jnp.sum(pr, axis=-1, keepdims=True)
                pv = jnp.dot(pr.astype(bf16), vw, preferred_element_type=f32)
                outs.append(pv / l)
            o_ref[r0:r0 + gq, cols] = jnp.where(low, outs[0], outs[1]).astype(o_ref.dtype)


def _rel_bias_table(rel_bias, gq, wk, q_shift):
    period = wk + gq
    j = np.arange(period)
    k = np.where(j < wk, j, j - period)
    line = rel_bias.astype(f32)[:, np.clip(q_shift - k, -REL_CLIP, REL_CLIP) + REL_CLIP]
    tiled = jnp.tile(line, (1, gq))[:, :gq * (period - 1)]
    return tiled.reshape(-1, gq, period - 1)[:, :, :wk]


def _band_bias(rel_bias, gq, wk, q_shift):
    r = np.arange(gq)[:, None]
    s = np.arange(wk)[None, :]
    band0 = (r // CHUNK) * CHUNK + q_shift - WINDOW_A
    ok = (s >= band0) & (s < band0 + WINDOW_A + CHUNK)
    return jnp.where(ok[None], _rel_bias_table(rel_bias, gq, wk, q_shift), NEG_INF)


def _band_attention_prompt(q, k, v, rel_bias, nb, t, gq):
    rows = WINDOW_A
    wk = WINDOW_A + gq
    n_steps = t // rows
    bias = _band_bias(rel_bias, gq, wk, WINDOW_A)
    cur = pl.BlockSpec((rows, W_HEADS), lambda b, i: (b * n_steps + i, 0))
    prev = pl.BlockSpec((rows, W_HEADS), lambda b, i: (b * n_steps + jnp.maximum(i - 1, 0), 0))
    return pl.pallas_call(
        functools.partial(_band_kernel, rows, gq, wk, True),
        grid=(nb, n_steps),
        in_specs=[cur, prev, cur, prev, cur,
                  pl.BlockSpec((N_HEADS, gq, wk), lambda b, i: (0, 0, 0))],
        out_specs=cur,
        out_shape=jax.ShapeDtypeStruct((nb * t, W_HEADS), bf16),
        scratch_shapes=[pltpu.VMEM((2 * rows, W_HEADS), bf16), pltpu.VMEM((2 * rows, W_HEADS), bf16)],
        compiler_params=_params(("arbitrary", "arbitrary")),
        name="band_prompt",
    )(q, k, k, v, v, bias)


def _band_attention_sample(q, kk, vv, rel_bias, nb, s_new, l_cache, q_blk0):
    wk = l_cache + s_new
    bias = _rel_bias_table(rel_bias, s_new, wk, l_cache)
    return pl.pallas_call(
        functools.partial(_band_kernel, s_new, s_new, wk, False),
        grid=(nb, 1),
        in_specs=[pl.BlockSpec((s_new, W_HEADS), lambda b, i: (q_blk0 + b, 0)),
                  pl.BlockSpec((wk, W_HEADS), lambda b, i: (b, 0)),
                  pl.BlockSpec((wk, W_HEADS), lambda b, i: (b, 0)),
                  pl.BlockSpec((N_HEADS, s_new, wk), lambda b, i: (0, 0, 0))],
        out_specs=pl.BlockSpec((s_new, W_HEADS), lambda b, i: (b, 0)),
        out_shape=jax.ShapeDtypeStruct((nb * s_new, W_HEADS), bf16),
        compiler_params=_params(("arbitrary", "arbitrary")),
        name="band_sample",
    )(q, kk, vv, bias)


CONV_STRIP = 32


def _conv_kernel(tt, n_slab, init_ref, u_ref, w_ref, cb_ref, g_ref, b_ref, o_ref, ubuf, acc_scr):
    halo = CONV_HALO * n_slab

    @pl.when(pl.program_id(1) == 0)
    def _():
        ubuf[0:halo] = init_ref[0]

    ubuf[halo:halo + tt * n_slab] = u_ref[...]
    rs = min(CONV_STRIP, tt)
    first = CONV_HALO - (CONV_W - 1)

    def per_step(slab):
        return jnp.broadcast_to(slab[None], (rs, n_slab, LANES)).reshape(rs * n_slab, LANES)

    for s in range(tt // rs):
        acc = per_step(cb_ref[...])
        for j in range(CONV_W):
            r0 = (s * rs + first + j) * n_slab
            acc = acc + per_step(w_ref[j * n_slab:(j + 1) * n_slab, :]) * ubuf[r0:r0 + rs * n_slab, :]
        acc_scr[...] = acc
        rows = jnp.concatenate([acc_scr[pl.ds(q, rs, stride=n_slab), :] for q in range(n_slab)], axis=1)
        mu = jnp.mean(rows, axis=-1, keepdims=True)
        cen = rows - mu
        var = jnp.mean(cen * cen, axis=-1, keepdims=True)
        y = cen * lax.rsqrt(var + EPS) * g_ref[...] + b_ref[...]
        o_ref[s * rs:(s + 1) * rs, :] = (y * _sigmoid(y)).astype(o_ref.dtype)
    if tt >= CONV_HALO:
        ubuf[0:halo] = ubuf[tt * n_slab:tt * n_slab + halo]


def _conv_module(u_slab, init, conv_w, conv_b, ln_g, ln_b, nb, t, tt, blk0):
    c = conv_w.shape[1]
    n_slab = c // LANES
    n_t = t // tt
    vec = pl.BlockSpec((1, c), lambda b, i: (0, 0))
    return pl.pallas_call(
        functools.partial(_conv_kernel, tt, n_slab),
        grid=(nb, n_t),
        in_specs=[pl.BlockSpec((1, CONV_HALO * n_slab, LANES), lambda b, i: (b, 0, 0)),
                  pl.BlockSpec((tt * n_slab, LANES), lambda b, i: (blk0 + b * n_t + i, 0)),
                  pl.BlockSpec((CONV_W * n_slab, LANES), lambda b, i: (0, 0)),
                  pl.BlockSpec((n_slab, LANES), lambda b, i: (0, 0)), vec, vec],
        out_specs=pl.BlockSpec((tt, c), lambda b, i: (b * n_t + i, 0)),
        out_shape=jax.ShapeDtypeStruct((nb * t, c), bf16),
        scratch_shapes=[pltpu.VMEM(((CONV_HALO + tt) * n_slab, LANES), f32),
                        pltpu.VMEM((min(CONV_STRIP, tt) * n_slab, LANES), f32)],
        compiler_params=_params(("arbitrary", "arbitrary")),
        name="conv",
    )(init.reshape(nb, CONV_HALO * n_slab, LANES), u_slab, conv_w.reshape(CONV_W * n_slab, LANES),
      conv_b.reshape(n_slab, LANES), ln_g, ln_b)


def _mix_kernel(bounds, ya_p, ya_s, yb_p, yb_s, c_p, c_s, gates, pa, pb, pc, o_ref):
    d = o_ref.shape[1]

    def go(k):
        ya, yb, c = ((ya_p, yb_p, c_p), (ya_s, yb_s, c_s))[k]
        a = jnp.dot(ya[...], pa[...], preferred_element_type=f32)
        mixed = gates[:, 0:d].astype(f32) * a
        b = jnp.dot(yb[...], pb[...], preferred_element_type=f32)
        mixed = mixed + gates[:, d:2 * d].astype(f32) * b
        cc = jnp.dot(c[...], pc[...], preferred_element_type=f32)
        mixed = mixed + gates[:, 2 * d:3 * d].astype(f32) * cc
        o_ref[...] = mixed.astype(o_ref.dtype)

    _when_segment(pl.program_id(0), bounds, go)


def _mix(ya_p, ya_s, yb_p, yb_s, c_p, c_s, gates, pa, pb, pc, tm):
    m, d3 = gates.shape
    d = d3 // 3
    n_p, n_s = ya_p.shape[0] // tm, ya_s.shape[0] // tm
    bounds = (0, n_p, n_p + n_s)
    c_conv = c_p.shape[1]
    const = lambda shape: pl.BlockSpec(shape, lambda i: (0, 0))
    return pl.pallas_call(
        functools.partial(_mix_kernel, bounds),
        grid=(m // tm,),
        in_specs=[_seg_spec((tm, W_HEADS), 0, n_p), _seg_spec((tm, W_HEADS), n_p, n_s),
                  _seg_spec((tm, W_HEADS), 0, n_p), _seg_spec((tm, W_HEADS), n_p, n_s),
                  _seg_spec((tm, c_conv), 0, n_p), _seg_spec((tm, c_conv), n_p, n_s),
                  pl.BlockSpec((tm, d3), lambda i: (i, 0)),
                  const((W_HEADS, d)), const((W_HEADS, d)), const((c_conv, d))],
        out_specs=pl.BlockSpec((tm, d), lambda i: (i, 0)),
        out_shape=jax.ShapeDtypeStruct((m, d), bf16),
        compiler_params=_params(("arbitrary",)),
        name="mix",
    )(ya_p, ya_s, yb_p, yb_s, c_p, c_s, gates, pa, pb, pc)


def _route(logits):
    shape = logits.shape
    lane = lax.broadcasted_iota(jnp.int32, shape, 1)
    lane_f = lane.astype(f32)
    big = float(LANES)
    gl = jnp.where(lane < N_GROUPS, logits, -jnp.inf)
    g_max = jnp.max(gl, axis=-1, keepdims=True)
    g_idx = jnp.min(jnp.where(gl == g_max, lane_f, big), axis=-1, keepdims=True)
    g_sum = jnp.sum(jnp.exp(gl - g_max), axis=-1, keepdims=True)
    g_w = 1.0 / g_sum
    lo = N_GROUPS + g_idx * EXPERTS_PER_GROUP
    el = jnp.where((lane_f >= lo) & (lane_f < lo + EXPERTS_PER_GROUP), logits, -jnp.inf)
    m1 = jnp.max(el, axis=-1, keepdims=True)
    i1 = jnp.min(jnp.where(el == m1, lane_f, big), axis=-1, keepdims=True)
    el2 = jnp.where(lane_f == i1, -jnp.inf, el)
    m2 = jnp.max(el2, axis=-1, keepdims=True)
    i2 = jnp.min(jnp.where(el2 == m2, lane_f, big), axis=-1, keepdims=True)
    e21 = jnp.exp(m2 - m1)
    den = 1.0 + e21
    w1 = g_w * (1.0 / den)
    w2 = g_w * (e21 / den)
    eid = jnp.where(lane == 0, i1 - N_GROUPS, jnp.where(lane == 1, i2 - N_GROUPS, 0.0)).astype(jnp.int32)
    wgt = jnp.where(lane == 0, w1, jnp.where(lane == 1, w2, 0.0))
    return eid, wgt


def _outproj_kernel(n_slab, n_seg, bounds, *refs):
    xs = refs[:n_seg]
    mixed, wo, g2, wr, br, xo, h2o, eid_o, wgt_o = refs[n_seg:]
    tm = mixed.shape[0]
    i = pl.program_id(0)
    x = xs[0][...]
    for k in range(1, n_seg):
        x = jnp.where(i >= bounds[k], xs[k][...], x)
    xn = x + jnp.dot(mixed[...], wo[...], preferred_element_type=f32)
    xo[...] = xn
    h2 = _rms(xn, g2[...])
    for s in range(n_slab):
        h2o[pl.ds(s, tm, stride=n_slab), :] = h2[:, s * LANES:(s + 1) * LANES]
    h_hi = h2.astype(bf16)
    h_lo = (h2 - h_hi.astype(f32)).astype(bf16)
    hi = jnp.dot(h_hi, wr[...], preferred_element_type=f32)
    lo = jnp.dot(h_lo, wr[:, :LANES], preferred_element_type=f32)
    logits = hi[:, :LANES] + (hi[:, LANES:] + lo) + br[...]
    eid, wgt = _route(logits)
    eid_o[...] = eid
    wgt_o[...] = wgt


def _outproj(mixed, x_segs, wo, g2, wr, br, tm):
    m, d = mixed.shape
    n_slab = d // LANES
    counts = [a.shape[0] // tm for a in x_segs]
    bounds = [0]
    for cnt in counts:
        bounds.append(bounds[-1] + cnt)
    const = lambda shape: pl.BlockSpec(shape, lambda i: (0, 0))
    row = lambda w: pl.BlockSpec((tm, w), lambda i: (i, 0))
    sds = jax.ShapeDtypeStruct
    return pl.pallas_call(
        functools.partial(_outproj_kernel, n_slab, len(x_segs), tuple(bounds)),
        grid=(m // tm,),
        in_specs=[_seg_spec((tm, d), bounds[k], counts[k]) for k in range(len(x_segs))] +
                 [row(d), const((d, d)), const((1, d)), const((d, 2 * LANES)), const((1, LANES))],
        out_specs=[row(d), pl.BlockSpec((tm * n_slab, LANES), lambda i: (i, 0)), row(LANES), row(LANES)],
        out_shape=[sds((m, d), f32), sds((m * n_slab, LANES), f32), sds((m, LANES), jnp.int32),
                   sds((m, LANES), f32)],
        compiler_params=_params(("arbitrary",)),
        name="outproj",
    )(*x_segs, mixed, wo, g2, wr, br)


TB = 256
TD = 256


def _plan(eid):
    flat_e = eid.reshape(-1)
    n_assign = flat_e.shape[0]
    onehot = (flat_e[:, None] == jnp.arange(N_EXPERTS, dtype=jnp.int32)[None, :]).astype(jnp.int32)
    csum = jnp.cumsum(onehot, axis=0)
    counts = csum[-1]
    rank = jnp.sum(onehot * csum, axis=1) - 1
    n_blk_e = (counts + TB - 1) // TB
    blk_end = jnp.cumsum(n_blk_e)
    blk_start = blk_end - n_blk_e
    dest = blk_start[flat_e] * TB + rank
    n_blocks = -(-n_assign // TB) + N_EXPERTS
    blk_expert = jnp.minimum(jnp.searchsorted(blk_end, jnp.arange(n_blocks, dtype=jnp.int32), side='right'),
                             N_EXPERTS - 1).astype(jnp.int32)
    return dest.astype(jnp.int32), blk_expert, blk_end[-1:].astype(jnp.int32), n_blocks


DMA_UNROLL = 8


def _issue_rows(n, copy):
    per_trip = DMA_UNROLL // TOP_K

    def trip(t, carry):
        for r in range(per_trip):
            for k in range(TOP_K):
                copy(t * per_trip + r, k).start()
        return carry

    lax.fori_loop(0, n // DMA_UNROLL, trip, 0)


def _dispatch_kernel(n_slab, dest_ref, h_ref, zero_ref, xs_ref, sem):
    del zero_ref
    n = dest_ref.shape[2]

    def copy(tok, k):
        src = h_ref.at[pl.ds(pl.multiple_of(tok * n_slab, n_slab), n_slab), :]
        dst = xs_ref.at[pl.ds(pl.multiple_of(dest_ref[0, 0, tok * TOP_K + k] * n_slab, n_slab), n_slab), :]
        return pltpu.make_async_copy(src, dst, sem)

    _issue_rows(n, copy)
    for half in range(TOP_K):
        rows = pl.ds(0, (n // TOP_K) * n_slab)
        pltpu.make_async_copy(h_ref, xs_ref.at[rows, :], sem).wait()


def _dispatch(h2_slab, dest, n_rows, n_slab, td):
    m = h2_slab.shape[0] // n_slab
    n_steps = m // td
    dest3 = dest.reshape(n_steps, 1, td * TOP_K)
    zeros = jnp.zeros((n_rows * n_slab, LANES), f32)
    return pl.pallas_call(
        functools.partial(_dispatch_kernel, n_slab),
        grid=(n_steps,),
        in_specs=[pl.BlockSpec((1, 1, td * TOP_K), lambda i: (i, 0, 0), memory_space=pltpu.SMEM),
                  pl.BlockSpec((td * n_slab, LANES), lambda i: (i, 0)),
                  pl.BlockSpec(memory_space=pl.ANY)],
        out_specs=pl.BlockSpec(memory_space=pl.ANY),
        out_shape=jax.ShapeDtypeStruct((n_rows * n_slab, LANES), f32),
        scratch_shapes=[pltpu.SemaphoreType.DMA(())],
        input_output_aliases={2: 0},
        compiler_params=_params(("arbitrary",)),
        name="dispatch",
    )(dest3, h2_slab, zeros)


def _expert_kernel(n_slab, be_ref, nu_ref, xs_ref, wgu_ref, wd_ref, o_ref):
    del be_ref

    @pl.when(pl.program_id(0) < nu_ref[0])
    def _():
        de = wd_ref.shape[1]
        x = jnp.concatenate([xs_ref[pl.ds(s, TB, stride=n_slab), :] for s in range(n_slab)], axis=1)
        gu = jnp.dot(x.astype(bf16), wgu_ref[0], preferred_element_type=f32)
        g = gu[:, :de]
        hmid = (g * _sigmoid(g)) * gu[:, de:]
        y = jnp.dot(hmid.astype(bf16), wd_ref[0], preferred_element_type=f32)
        for s in range(n_slab):
            o_ref[pl.ds(s, TB, stride=n_slab), :] = y[:, s * LANES:(s + 1) * LANES]


def _experts(xs, blk_expert, n_used, wgu, wd, n_blocks, n_slab):
    d, de2 = wgu.shape[1], wgu.shape[2]
    blk = lambda i, be, nu: (jnp.minimum(i, nu[0] - 1), 0)
    grid_spec = pltpu.PrefetchScalarGridSpec(
        num_scalar_prefetch=2,
        grid=(n_blocks,),
        in_specs=[pl.BlockSpec((TB * n_slab, LANES), blk),
                  pl.BlockSpec((1, d, de2), lambda i, be, nu: (be[jnp.minimum(i, nu[0] - 1)], 0, 0)),
                  pl.BlockSpec((1, de2 // 2, d), lambda i, be, nu: (be[jnp.minimum(i, nu[0] - 1)], 0, 0))],
        out_specs=pl.BlockSpec((TB * n_slab, LANES), blk),
    )
    return pl.pallas_call(
        functools.partial(_expert_kernel, n_slab),
        grid_spec=grid_spec,
        out_shape=jax.ShapeDtypeStruct(xs.shape, f32),
        input_output_aliases={2: 0},
        compiler_params=_params(("arbitrary",)),
        name="experts",
    )(blk_expert, n_used, xs, wgu, wd)


def _combine_kernel(n_slab, final, bounds, dest_ref, next_ref, x_ref, wgt_ref, g_ref, ys_ref, *rest):
    outs, (gbuf, sems) = rest[:-2], rest[-2:]
    i = pl.program_id(0)
    n = dest_ref.shape[2]
    tm = x_ref.shape[0]
    slot = i % 2
    pitch = n_slab + SUBLANES

    def gather(idx_ref, s):
        def copy(tok, k):
            src = ys_ref.at[pl.ds(pl.multiple_of(idx_ref[0, 0, tok * TOP_K + k] * n_slab, n_slab), n_slab), :]
            row = k * tm * pitch + tok * pitch
            dst = gbuf.at[s, pl.ds(pl.multiple_of(row, SUBLANES), n_slab), :]
            return pltpu.make_async_copy(src, dst, sems.at[s])
        _issue_rows(n, copy)

    pl.when(i == 0)(lambda: gather(dest_ref, slot))
    pl.when(i + 1 < pl.num_programs(0))(lambda: gather(next_ref, 1 - slot))
    rows = pl.ds(0, n * n_slab)
    pltpu.make_async_copy(ys_ref.at[rows, :], gbuf.at[slot, rows, :], sems.at[slot]).wait()

    ys = []
    for k in range(TOP_K):
        y = jnp.concatenate([gbuf[slot, pl.ds(k * tm * pitch + s, tm, stride=pitch), :]
                             for s in range(n_slab)], axis=1)
        ys.append(y * wgt_ref[:, k:k + 1])
    x = x_ref[...] + (ys[0] + ys[1])
    if final:
        x = _rms(x, g_ref[...])

    def store(k):
        outs[k][...] = x

    _when_segment(i, bounds, store)


def _combine(x, ys, dest, wgt, g, n_slab, final, seg_rows):
    m, d = x.shape
    n_steps = m // TD
    dest3 = dest.reshape(n_steps, 1, TD * TOP_K)
    counts = [r // TD for r in seg_rows]
    bounds = [0]
    for cnt in counts:
        bounds.append(bounds[-1] + cnt)
    out_specs = [_seg_spec((TD, d), bounds[k], counts[k]) for k in range(len(seg_rows))]
    out_shape = [jax.ShapeDtypeStruct((r, d), f32) for r in seg_rows]
    idx_block = (1, 1, TD * TOP_K)
    return pl.pallas_call(
        functools.partial(_combine_kernel, n_slab, final, tuple(bounds)),
        grid=(n_steps,),
        in_specs=[pl.BlockSpec(idx_block, lambda i: (i, 0, 0), memory_space=pltpu.SMEM),
                  pl.BlockSpec(idx_block, lambda i: (jnp.minimum(i + 1, n_steps - 1), 0, 0),
                               memory_space=pltpu.SMEM),
                  pl.BlockSpec((TD, d), lambda i: (i, 0)),
                  pl.BlockSpec((TD, LANES), lambda i: (i, 0)),
                  pl.BlockSpec((1, d), lambda i: (0, 0)),
                  pl.BlockSpec(memory_space=pl.ANY)],
        out_specs=out_specs, out_shape=out_shape,
        scratch_shapes=[pltpu.VMEM((2, TD * TOP_K * (n_slab + SUBLANES), LANES), f32),
                        pltpu.SemaphoreType.DMA((2,))],
        compiler_params=_params(("arbitrary",)),
        name="combine",
    )(dest3, dest3, x, wgt, g, ys)


def _pack_w_in(w_in, b_in, d):
    c_conv = d // 2
    half = TN // 2
    n_qkv = 6 * W_HEADS
    f0 = n_qkv
    c0 = f0 + N_HEADS
    g0 = c0 + 2 * c_conv
    def pack(a):
        parts = [a[..., :n_qkv]]
        for c in range(c_conv // half):
            parts.append(a[..., c0 + c * half:c0 + (c + 1) * half])
            parts.append(a[..., c0 + c_conv + c * half:c0 + c_conv + (c + 1) * half])
        parts.append(a[..., g0:])
        return jnp.concatenate(parts, axis=-1)

    w_main = pack(w_in.astype(bf16))
    b_main = pack(b_in)[None, :].astype(f32)
    w_f = jnp.pad(w_in[:, f0:f0 + N_HEADS], ((0, 0), (0, LANES - N_HEADS))).astype(bf16)
    b_f = jnp.pad(b_in[f0:f0 + N_HEADS], (0, LANES - N_HEADS))[None, :].astype(f32)
    return w_main, b_main, w_f, b_f


def kernel(x_prompt, x_sample, cache_a_k, cache_a_v, cache_b_k, cache_b_v, cache_b_logf, state_conv, norm_mix_g, w_in, b_in, rel_bias, conv_w, conv_b, conv_ln_g, conv_ln_b, w_proj_a, w_proj_b, w_proj_c, w_out, norm_ffn_g, w_router_group, b_router_group, w_router_expert, b_router_expert, w_e_gate, w_e_up, w_e_down, norm_final_g):
    nb_p, t_p, d = x_prompt.shape
    nb_s, t_s, _ = x_sample.shape
    depth = w_in.shape[0]
    past = cache_b_k.shape[2]
    a_rows = cache_a_k.shape[2]
    m_p, m_s = nb_p * t_p, nb_s * t_s
    m = m_p + m_s
    c_conv = d // 2
    n_slab = d // LANES
    tm = _row_tile(np.gcd(m_p, m_s), 512)
    tm_mix = _row_tile(np.gcd(m_p, m_s), 256)
    assert m_p % TD == 0 and m_s % TD == 0 and t_s % 16 == 0 and m_p % t_s == 0

    a_keep = min(WINDOW_A, t_p)
    x_segs = [x_prompt.reshape(m_p, d), x_sample.reshape(m_s, d)]
    kv_states = [jnp.zeros((depth, rows, W_HEADS), f32)
                 for rows in (nb_p * a_keep, nb_p * a_keep, m_p, m_p, m_s, m_s, m_s, m_s)]
    p_states, s_states = [], []
    for l in range(depth):
        w_main, b_main, w_f, b_f = _pack_w_in(w_in[l], b_in[l], d)
        (qa, ka16, va16, qb, kb16, vb16, u, gates, logf), kv_states = _inproj(
            x_segs, norm_mix_g[l][None, :], w_main, b_main, w_f, b_f, kv_states, l, m_p, t_p, a_keep, tm)

        ya_p = _band_attention_prompt(qa, ka16, va16, rel_bias[l], nb_p, t_p, 4 * CHUNK)
        kk = jnp.concatenate([cache_a_k[l].reshape(nb_s, a_rows, W_HEADS).astype(bf16),
                              ka16[m_p:].reshape(nb_s, t_s, W_HEADS)], axis=1).reshape(-1, W_HEADS)
        vv = jnp.concatenate([cache_a_v[l].reshape(nb_s, a_rows, W_HEADS).astype(bf16),
                              va16[m_p:].reshape(nb_s, t_s, W_HEADS)], axis=1).reshape(-1, W_HEADS)
        ya_s = _band_attention_sample(qa, kk, vv, rel_bias[l], nb_s, t_s, a_rows, m_p // t_s)

        logf_p = logf[:m_p].reshape(nb_p, t_p, N_HEADS)
        logf_s = logf[m_p:].reshape(nb_s, t_s, N_HEADS)
        cum_p = _cumsum_time(logf_p.transpose(0, 2, 1))
        f_p = cum_p.transpose(0, 2, 1).reshape(m_p, N_HEADS)
        yb_p = _fox_attention(_fox_expand("q", qb, f_p, m_p, tm), _fox_expand("k", kb16, f_p, m_p, tm),
                              _fox_expand("v", vb16, None, m_p, tm),
                              nb_p, t_p, t_p, _row_tile(t_p, 2048), _row_tile(t_p, 512))
        cum_s = _cumsum_time(jnp.concatenate([cache_b_logf[l].astype(f32), logf_s], axis=1).transpose(0, 2, 1))
        t_ks = past + t_s
        kk = jnp.concatenate([cache_b_k[l].reshape(nb_s, past, W_HEADS).astype(bf16),
                              kb16[m_p:].reshape(nb_s, t_s, W_HEADS)], axis=1).reshape(-1, W_HEADS)
        vv = jnp.concatenate([cache_b_v[l].reshape(nb_s, past, W_HEADS).astype(bf16),
                              vb16[m_p:].reshape(nb_s, t_s, W_HEADS)], axis=1).reshape(-1, W_HEADS)
        f_ks = cum_s.transpose(0, 2, 1)
        yb_s = _fox_attention(
            _fox_expand("q", qb, f_ks[:, past:].reshape(m_s, N_HEADS), m_s, t_s, blk0=m_p // t_s),
            _fox_expand("k", kk, f_ks.reshape(nb_s * t_ks, N_HEADS), nb_s * t_ks, t_ks),
            _fox_expand("v", vv, None, nb_s * t_ks, t_ks),
            nb_s, t_s, t_ks, t_s, t_ks)

        conv_args = (conv_w[l], conv_b[l][None, :], conv_ln_g[l][None, :], conv_ln_b[l][None, :])
        c_p = _conv_module(u, jnp.zeros((nb_p, CONV_HALO, c_conv), f32), *conv_args,
                           nb_p, t_p, _row_tile(t_p, 256), 0)
        init_s = jnp.pad(state_conv[l], ((0, 0), (CONV_HALO - (CONV_W - 1), 0), (0, 0)))
        c_s = _conv_module(u, init_s, *conv_args, nb_s, t_s, t_s, m_p // t_s)

        mixed = _mix(ya_p, ya_s, yb_p, yb_s, c_p, c_s, gates, w_proj_a[l].astype(bf16),
                     w_proj_b[l].astype(bf16), w_proj_c[l].astype(bf16), tm_mix)
        wr = jnp.pad(jnp.concatenate([w_router_group[l], w_router_expert[l]], axis=1),
                     ((0, 0), (0, LANES - N_GROUPS - N_EXPERTS)))
        wr_hi = wr.astype(bf16)
        wr_parts = jnp.concatenate([wr_hi, (wr - wr_hi.astype(f32)).astype(bf16)], axis=1)
        br = jnp.pad(jnp.concatenate([b_router_group[l], b_router_expert[l]]),
                     (0, LANES - N_GROUPS - N_EXPERTS))[None, :]
        x_mid, h2_slab, eid, wgt = _outproj(mixed, x_segs, w_out[l].astype(bf16), norm_ffn_g[l][None, :],
                                            wr_parts, br, tm_mix)

        dest, blk_expert, n_used, n_blocks = _plan(eid[:, :TOP_K])
        xs = _dispatch(h2_slab, dest, n_blocks * TB, n_slab, tm)
        wgu = jnp.concatenate([w_e_gate[l], w_e_up[l]], axis=2).astype(bf16)
        ys = _experts(xs, blk_expert, n_used, wgu, w_e_down[l].astype(bf16), n_blocks, n_slab)
        final = l == depth - 1
        x_segs = _combine(x_mid, ys, dest, wgt, norm_final_g[None, :], n_slab, final,
                          (m_p, m_s) if final else (m,))

        n_cs = c_conv // LANES
        u_p = jnp.stack([u[((b + 1) * t_p - (CONV_W - 1)) * n_cs:(b + 1) * t_p * n_cs] for b in range(nb_p)])
        u_p = u_p.reshape(nb_p, CONV_W - 1, c_conv)
        u_s = u[m_p * n_cs:].reshape(nb_s, t_s, c_conv)
        p_states.append((logf_p, u_p))
        s_states.append((logf_s, jnp.concatenate([state_conv[l], u_s], axis=1)[:, -(CONV_W - 1):]))

    y_prompt = x_segs[0].reshape(nb_p, t_p, d)
    y_sample = x_segs[1].reshape(nb_s, t_s, d)
    stack = lambda states, k: jnp.stack([st[k] for st in states], axis=0)
    heads = lambda a, nb, t: a.reshape(depth, nb, t, N_HEADS, HEAD_DIM)
    ka_p, va_p, kb_p, vb_p, ka_s, va_s, kb_s, vb_s = kv_states
    return (y_prompt, y_sample,
            heads(ka_p, nb_p, a_keep), heads(va_p, nb_p, a_keep), heads(kb_p, nb_p, t_p), heads(vb_p, nb_p, t_p),
            stack(p_states, 0), stack(p_states, 1),
            heads(ka_s, nb_s, t_s), heads(va_s, nb_s, t_s), heads(kb_s, nb_s, t_s), heads(vb_s, nb_s, t_s),
            stack(s_states, 0), stack(s_states, 1))
```

```python
import functools

import jax
import jax.numpy as jnp
import numpy as np
from jax import lax
from jax.experimental import pallas as pl
from jax.experimental.pallas import tpu as pltpu

f32 = jnp.float32
bf16 = jnp.bfloat16

HEAD_DIM = 64
N_HEADS = 8
W_HEADS = N_HEADS * HEAD_DIM
CHUNK = 64
WINDOW_A = 8 * CHUNK
REL_CLIP = 128
CONV_W = 31
CONV_HALO = 32
N_GROUPS = 4
EXPERTS_PER_GROUP = 8
N_EXPERTS = N_GROUPS * EXPERTS_PER_GROUP
TOP_K = 2
SCALE = HEAD_DIM ** -0.5
EPS = 1e-6
NEG_INF = -1e30
LANES = 128
SUBLANES = 8
MIB = 1024 * 1024


def _params(sem, vmem_mib=48):
    return pltpu.CompilerParams(dimension_semantics=sem, vmem_limit_bytes=vmem_mib * MIB)


def _row_tile(m, cap):
    t = cap
    while m % t:
        t //= 2
    return t


def _sigmoid(z):
    return 0.5 * jnp.tanh(0.5 * z) + 0.5


def _rms(x, g):
    return x * lax.rsqrt(jnp.mean(x * x, axis=-1, keepdims=True) + EPS) * g


def _when_segment(i, bounds, fn):
    for k in range(len(bounds) - 1):
        pl.when((i >= bounds[k]) & (i < bounds[k + 1]))(functools.partial(fn, k))


def _seg_spec(block, start, count, width_axes=1):
    zeros = (0,) * width_axes
    return pl.BlockSpec(block, lambda i, *_: (jnp.clip(i - start, 0, count - 1),) + zeros)


TN = 1024
SEG_PER_TILE = TN // W_HEADS
N_QKV_TILES = 6 // SEG_PER_TILE


def _inproj_kernel(n_seg, bounds, n_conv, n_prompt, *refs):
    xs = refs[:n_seg]
    (g_ref, w_ref, b_ref, wf_ref, bf_ref) = refs[n_seg:n_seg + 5]
    outs = refs[n_seg + 5 + 8:-1]
    h_scr = refs[-1]
    (qa, ka16, va16, qb, kb16, vb16, u, gates, logf, ka_p, va_p, kb_p, vb_p, ka_s, va_s, kb_s, vb_s) = outs
    i = pl.program_id(0)
    j = pl.program_id(1)

    def norm(k):
        h_scr[...] = _rms(xs[k][...], g_ref[...]).astype(bf16)

    @pl.when(j == 0)
    def _():
        _when_segment(i, bounds, norm)
        zf = jnp.dot(h_scr[...], wf_ref[...], preferred_element_type=f32) + bf_ref[...]
        lf = jnp.minimum(zf, 0.0) - jnp.log1p(jnp.exp(-jnp.abs(zf)))
        logf[...] = lf[:, :N_HEADS]

    def project():
        return jnp.dot(h_scr[...], w_ref[...], preferred_element_type=f32) + b_ref[...]

    segments = ((qa, None, None), (ka16, ka_p, ka_s), (va16, va_p, va_s),
                (qb, None, None), (kb16, kb_p, kb_s), (vb16, vb_p, vb_s))
    is_prompt = i < n_prompt
    for t in range(N_QKV_TILES):
        for prompt_rows in (True, False):
            @pl.when((j == t) & (is_prompt if prompt_rows else ~is_prompt))
            def _(t=t, prompt_rows=prompt_rows):
                z = project()
                for n in range(SEG_PER_TILE):
                    zn = z[:, n * W_HEADS:(n + 1) * W_HEADS]
                    copy, state_p, state_s = segments[t * SEG_PER_TILE + n]
                    copy[...] = zn.astype(bf16)
                    state = state_p if prompt_rows else state_s
                    if state is not None:
                        state[...] = zn

    half = TN // 2
    for c in range(n_conv):
        @pl.when(j == N_QKV_TILES + c)
        def _(c=c):
            z = project()
            glu = z[:, :half] * _sigmoid(z[:, half:])
            tm = glu.shape[0]
            n_slab = n_conv * half // LANES
            for q in range(half // LANES):
                u[pl.ds(c * (half // LANES) + q, tm, stride=n_slab), :] = glu[:, q * LANES:(q + 1) * LANES]

    @pl.when(j >= N_QKV_TILES + n_conv)
    def _():
        gates[...] = _sigmoid(project()).astype(bf16)


def _inproj(x_segs, g, w_main, b_main, w_f, b_f, states, layer, m_p, t_p, a_keep, tm):
    d = x_segs[0].shape[1]
    m = sum(a.shape[0] for a in x_segs)
    c_conv = d // 2
    n_conv = c_conv // (TN // 2)
    n_gate = 3 * d // TN
    n_col = N_QKV_TILES + n_conv + n_gate
    assert w_main.shape == (d, n_col * TN)
    counts = [a.shape[0] // tm for a in x_segs]
    bounds = [0]
    for cnt in counts:
        bounds.append(bounds[-1] + cnt)
    n_seg = len(x_segs)
    n_p, n_s = m_p // tm, (m - m_p) // tm
    per_seq, keep = t_p // tm, a_keep // tm

    in_specs = [pl.BlockSpec((tm, d), (lambda i, j, s=bounds[k], n=counts[k]: (jnp.clip(i - s, 0, n - 1), 0)))
                for k in range(n_seg)]
    in_specs += [
        pl.BlockSpec((1, d), lambda i, j: (0, 0)),
        pl.BlockSpec((d, TN), lambda i, j: (0, j)),
        pl.BlockSpec((1, TN), lambda i, j: (0, j)),
        pl.BlockSpec((d, LANES), lambda i, j: (0, 0)),
        pl.BlockSpec((1, LANES), lambda i, j: (0, 0)),
    ] + [pl.BlockSpec(memory_space=pl.ANY)] * 8
    row = lambda width: pl.BlockSpec((tm, width), lambda i, j: (i, 0))

    def tail_rows(i, j):
        ip = jnp.minimum(i, n_p - 1)
        return (layer, (ip // per_seq) * keep + jnp.maximum(ip % per_seq - (per_seq - keep), 0), 0)

    state_block = (None, tm, W_HEADS)
    tail_spec = pl.BlockSpec(state_block, tail_rows)
    prompt_spec = pl.BlockSpec(state_block, lambda i, j: (layer, jnp.minimum(i, n_p - 1), 0))
    sample_spec = pl.BlockSpec(state_block, lambda i, j: (layer, jnp.clip(i - n_p, 0, n_s - 1), 0))
    out_specs = [row(W_HEADS)] * 6 + [
        pl.BlockSpec((tm * (c_conv // LANES), LANES), lambda i, j: (i, 0)),
        pl.BlockSpec((tm, TN), lambda i, j: (i, jnp.clip(j - (N_QKV_TILES + n_conv), 0, n_gate - 1))),
        row(N_HEADS),
        tail_spec, tail_spec, prompt_spec, prompt_spec, sample_spec, sample_spec, sample_spec, sample_spec]
    sds = jax.ShapeDtypeStruct
    out_shape = [sds((m, W_HEADS), bf16)] * 6 + [sds((m * (c_conv // LANES), LANES), f32), sds((m, 3 * d), bf16),
                                                 sds((m, N_HEADS), f32)]
    out_shape += [sds(s.shape, s.dtype) for s in states]
    n_in = n_seg + 5
    outs = pl.pallas_call(
        functools.partial(_inproj_kernel, n_seg, tuple(bounds), n_conv, n_p),
        grid=(m // tm, n_col),
        in_specs=in_specs, out_specs=out_specs, out_shape=out_shape,
        scratch_shapes=[pltpu.VMEM((tm, d), bf16)],
        input_output_aliases={n_in + k: 9 + k for k in range(8)},
        compiler_params=_params(("arbitrary", "arbitrary"), 58),
        name="inproj",
    )(*x_segs, g, w_main, b_main, w_f, b_f, *states)
    return outs[:9], outs[9:]


TC = 512


def _cumsum_kernel(x_ref, o_ref, carry):
    @pl.when(pl.program_id(1) == 0)
    def _():
        carry[...] = jnp.zeros_like(carry)

    blk = x_ref[0]
    r = lax.broadcasted_iota(jnp.int32, (TC, TC), 0)
    c = lax.broadcasted_iota(jnp.int32, (TC, TC), 1)
    tri = jnp.where(r <= c, 1.0, 0.0).astype(f32)
    cs = jnp.dot(blk, tri, precision=lax.Precision.HIGHEST, preferred_element_type=f32) + carry[:, 0:1]
    o_ref[0] = cs
    carry[...] = jnp.broadcast_to(cs[:, TC - 1:TC], carry.shape)


def _cumsum_time(x):
    nb, h, t = x.shape
    tp = -(-t // TC) * TC
    xp = jnp.pad(x, ((0, 0), (0, 0), (0, tp - t)))
    out = pl.pallas_call(
        _cumsum_kernel,
        grid=(nb, tp // TC),
        in_specs=[pl.BlockSpec((1, h, TC), lambda b, k: (b, 0, k))],
        out_specs=pl.BlockSpec((1, h, TC), lambda b, k: (b, 0, k)),
        out_shape=jax.ShapeDtypeStruct((nb, h, tp), f32),
        scratch_shapes=[pltpu.VMEM((h, LANES), f32)],
        compiler_params=_params(("arbitrary", "arbitrary")),
        name="cumsum",
    )(xp)
    return out[:, :, :t]


F_PARTS = 3


def _spare_base(h):
    return HEAD_DIM * (1 - h % 2)


def _fox_expand_kernel(kind, x_ref, f_ref, place_ref, o_ref):
    tm = x_ref.shape[0]
    lane = lax.broadcasted_iota(jnp.int32, (tm, LANES), 1)
    if kind != "v":
        rest = f_ref[...]
        stack = jnp.where(lane < (F_PARTS + 1) * N_HEADS, 1.0, 0.0)
        for n in range(F_PARTS):
            part = rest.astype(bf16).astype(f32)
            rest = rest - part
            stack = jnp.where((lane >= n * N_HEADS) & (lane < (n + 1) * N_HEADS), part, stack)
        spare_all = jnp.dot(stack.astype(bf16), place_ref[...], preferred_element_type=f32)
    for h in range(N_HEADS):
        pair = x_ref[:, (h // 2) * LANES:(h // 2 + 1) * LANES].astype(f32)
        if kind == "q":
            pair = pair * SCALE
        if kind == "v":
            spare = jnp.where(lane == _spare_base(h), 1.0, 0.0)
        else:
            spare = spare_all[:, h * LANES:(h + 1) * LANES]
        own = (lane < HEAD_DIM) if h % 2 == 0 else (lane >= HEAD_DIM)
        o_ref[h] = jnp.where(own, pair, spare).astype(o_ref.dtype)


def _fox_placement(kind):
    place = np.zeros((LANES, N_HEADS * LANES), np.float32)
    for h in range(N_HEADS):
        base = h * LANES + _spare_base(h)
        for n in range(F_PARTS):
            if kind == "q":
                place[n * N_HEADS + h, base + n] = 1.0
                place[F_PARTS * N_HEADS + h, base + F_PARTS + n] = 1.0
            else:
                place[F_PARTS * N_HEADS + h, base + n] = 1.0
                place[n * N_HEADS + h, base + F_PARTS + n] = -1.0
    return jnp.asarray(place, bf16)


def _fox_expand(kind, x, f, rows, tm, blk0=0):
    f_lanes = jnp.zeros((rows, LANES), f32) if f is None else jnp.tile(f, (1, LANES // N_HEADS))
    return pl.pallas_call(
        functools.partial(_fox_expand_kernel, kind),
        grid=(rows // tm,),
        in_specs=[pl.BlockSpec((tm, W_HEADS), lambda i: (blk0 + i, 0)),
                  pl.BlockSpec((tm, LANES), lambda i: (i, 0)),
                  pl.BlockSpec((LANES, N_HEADS * LANES), lambda i: (0, 0))],
        out_specs=pl.BlockSpec((N_HEADS, tm, LANES), lambda i: (0, i, 0)),
        out_shape=jax.ShapeDtypeStruct((N_HEADS, rows, LANES), bf16),
        compiler_params=_params(("arbitrary",)),
        name="fox_expand_" + kind,
    )(x, f_lanes, _fox_placement(kind))


FOX_HEADS_PER_TRIP = 2


def _fox_kernel(tq, tk, off, q_ref, k_ref, v_ref, o_ref, m_scr, acc_scr):
    i = pl.program_id(1)
    j = pl.program_id(2)

    @pl.when(j == 0)
    def _():
        m_scr[...] = jnp.full_like(m_scr, NEG_INF)
        acc_scr[...] = jnp.zeros_like(acc_scr)

    q_first = i * tq + off
    q_last = q_first + tq - 1
    k_first = j * tk
    k_last = k_first + tk - 1

    def body(row0, masked):
        rows = slice(row0, tq)
        if masked:
            kpos = k_first + lax.broadcasted_iota(jnp.int32, (tq - row0, tk), 1)
            qpos = q_first + row0 + lax.broadcasted_iota(jnp.int32, (tq - row0, tk), 0)
            vis = kpos <= qpos

        def head(h):
            s = lax.dot_general(q_ref[h, rows], k_ref[h], (((1,), (1,)), ((), ())), preferred_element_type=f32)
            if masked:
                s = jnp.where(vis, s, NEG_INF)
            m_old = m_scr[h, rows]
            m_new = jnp.maximum(m_old, jnp.max(s, axis=-1, keepdims=True))
            pr = jnp.exp(s - m_new[:, 0:1])
            pv = jnp.dot(pr.astype(bf16), v_ref[h], preferred_element_type=f32)
            acc_scr[h, rows] = jnp.exp(m_old - m_new) * acc_scr[h, rows] + pv
            m_scr[h, rows] = m_new

        def trip(g, carry):
            for n in range(FOX_HEADS_PER_TRIP):
                head(g * FOX_HEADS_PER_TRIP + n)
            return carry

        lax.fori_loop(0, N_HEADS // FOX_HEADS_PER_TRIP, trip, 0)

    pl.when(k_last <= q_first)(functools.partial(body, 0, False))
    if tq % tk == 0 and off % tk == 0:
        for c in range(tq // tk):
            pl.when(k_first == q_first + c * tk)(functools.partial(body, c * tk, True))
    else:
        pl.when((k_first <= q_last) & (k_last > q_first))(functools.partial(body, 0, True))

    @pl.when(j == pl.num_programs(2) - 1)
    def _():
        lane = lax.broadcasted_iota(jnp.int32, (tq, LANES), 1)
        for p in range(N_HEADS // 2):
            even = acc_scr[2 * p]
            odd = acc_scr[2 * p + 1]
            even = even / even[:, _spare_base(0):_spare_base(0) + 1]
            odd = odd / odd[:, _spare_base(1):_spare_base(1) + 1]
            o_ref[:, p * LANES:(p + 1) * LANES] = jnp.where(lane < HEAD_DIM, even, odd).astype(o_ref.dtype)


def _fox_attention(q, k, v, nb, t_q, t_k, tq, tk):
    nq, nk = t_q // tq, t_k // tk
    off = t_k - t_q

    def last_k(i):
        return jnp.minimum((i * tq + tq - 1 + off) // tk, nk - 1)

    kv_spec = pl.BlockSpec((N_HEADS, tk, LANES), lambda b, i, j: (0, b * nk + jnp.minimum(j, last_k(i)), 0))
    return pl.pallas_call(
        functools.partial(_fox_kernel, tq, tk, off),
        grid=(nb, nq, nk),
        in_specs=[pl.BlockSpec((N_HEADS, tq, LANES), lambda b, i, j: (0, b * nq + i, 0)), kv_spec, kv_spec],
        out_specs=pl.BlockSpec((tq, W_HEADS), lambda b, i, j: (b * nq + i, 0)),
        out_shape=jax.ShapeDtypeStruct((nb * t_q, W_HEADS), bf16),
        scratch_shapes=[pltpu.VMEM((N_HEADS, tq, LANES), f32), pltpu.VMEM((N_HEADS, tq, LANES), f32)],
        compiler_params=_params(("arbitrary", "arbitrary", "arbitrary")),
        name="fox",
    )(q, k, v)


def _band_kernel(rows, gq, wk, has_prev, *refs):
    if has_prev:
        q_ref, kp_ref, kc_ref, vp_ref, vc_ref, bias_ref, o_ref, k_scr, v_scr = refs
        k_scr[0:WINDOW_A] = kp_ref[...]
        k_scr[WINDOW_A:WINDOW_A + rows] = kc_ref[...]
        v_scr[0:WINDOW_A] = vp_ref[...]
        v_scr[WINDOW_A:WINDOW_A + rows] = vc_ref[...]
        k_src, v_src = k_scr, v_scr
    else:
        q_ref, k_src, v_src, bias_ref, o_ref = refs
    i = pl.program_id(1)
    lane = lax.broadcasted_iota(jnp.int32, (gq, LANES), 1)
    low = lane < HEAD_DIM
    for g in range(rows // gq):
        r0 = g * gq
        if has_prev:
            key_pos = (i - 1) * WINDOW_A + r0 + lax.broadcasted_iota(jnp.int32, (gq, wk), 1)
            vis = key_pos >= 0
        for p in range(N_HEADS // 2):
            cols = slice(p * LANES, (p + 1) * LANES)
            q2 = q_ref[r0:r0 + gq, cols]
            kw = k_src[r0:r0 + wk, cols]
            vw = v_src[r0:r0 + wk, cols]
            outs = []
            for half in range(2):
                h = 2 * p + half
                qm = jnp.where(low if half == 0 else ~low, q2, jnp.zeros_like(q2))
                s = lax.dot_general(qm, kw, (((1,), (1,)), ((), ())), preferred_element_type=f32) * SCALE
                s = s + bias_ref[h]
                if has_prev:
                    s = jnp.where(vis, s, NEG_INF)
                m = jnp.max(s, axis=-1, keepdims=True)
                pr = jnp.exp(s - m)
                l = jnp.sum(pr, axis=-1, keepdims=True)
                pv = jnp.dot(pr.astype(bf16), vw, preferred_element_type=f32)
                outs.append(pv / l)
            o_ref[r0:r0 + gq, cols] = jnp.where(low, outs[0], outs[1]).astype(o_ref.dtype)


def _rel_bias_table(rel_bias, gq, wk, q_shift):
    period = wk + gq
    j = np.arange(period)
    k = np.where(j < wk, j, j - period)
    line = rel_bias.astype(f32)[:, np.clip(q_shift - k, -REL_CLIP, REL_CLIP) + REL_CLIP]
    tiled = jnp.tile(line, (1, gq))[:, :gq * (period - 1)]
    return tiled.reshape(-1, gq, period - 1)[:, :, :wk]


def _band_bias(rel_bias, gq, wk, q_shift):
    r = np.arange(gq)[:, None]
    s = np.arange(wk)[None, :]
    band0 = (r // CHUNK) * CHUNK + q_shift - WINDOW_A
    ok = (s >= band0) & (s < band0 + WINDOW_A + CHUNK)
    return jnp.where(ok[None], _rel_bias_table(rel_bias, gq, wk, q_shift), NEG_INF)


def _band_attention_prompt(q, k, v, rel_bias, nb, t, gq):
    rows = WINDOW_A
    wk = WINDOW_A + gq
    n_steps = t // rows
    bias = _band_bias(rel_bias, gq, wk, WINDOW_A)
    cur = pl.BlockSpec((rows, W_HEADS), lambda b, i: (b * n_steps + i, 0))
    prev = pl.BlockSpec((rows, W_HEADS), lambda b, i: (b * n_steps + jnp.maximum(i - 1, 0), 0))
    return pl.pallas_call(
        functools.partial(_band_kernel, rows, gq, wk, True),
        grid=(nb, n_steps),
        in_specs=[cur, prev, cur, prev, cur,
                  pl.BlockSpec((N_HEADS, gq, wk), lambda b, i: (0, 0, 0))],
        out_specs=cur,
        out_shape=jax.ShapeDtypeStruct((nb * t, W_HEADS), bf16),
        scratch_shapes=[pltpu.VMEM((2 * rows, W_HEADS), bf16), pltpu.VMEM((2 * rows, W_HEADS), bf16)],
        compiler_params=_params(("arbitrary", "arbitrary")),
        name="band_prompt",
    )(q, k, k, v, v, bias)


def _band_attention_sample(q, kk, vv, rel_bias, nb, s_new, l_cache, q_blk0):
    wk = l_cache + s_new
    bias = _rel_bias_table(rel_bias, s_new, wk, l_cache)
    return pl.pallas_call(
        functools.partial(_band_kernel, s_new, s_new, wk, False),
        grid=(nb, 1),
        in_specs=[pl.BlockSpec((s_new, W_HEADS), lambda b, i: (q_blk0 + b, 0)),
                  pl.BlockSpec((wk, W_HEADS), lambda b, i: (b, 0)),
                  pl.BlockSpec((wk, W_HEADS), lambda b, i: (b, 0)),
                  pl.BlockSpec((N_HEADS, s_new, wk), lambda b, i: (0, 0, 0))],
        out_specs=pl.BlockSpec((s_new, W_HEADS), lambda b, i: (b, 0)),
        out_shape=jax.ShapeDtypeStruct((nb * s_new, W_HEADS), bf16),
        compiler_params=_params(("arbitrary", "arbitrary")),
        name="band_sample",
    )(q, kk, vv, bias)


CONV_STRIP = 32


def _conv_kernel(tt, n_slab, init_ref, u_ref, w_ref, cb_ref, g_ref, b_ref, o_ref, ubuf, acc_scr):
    halo = CONV_HALO * n_slab

    @pl.when(pl.program_id(1) == 0)
    def _():
        ubuf[0:halo] = init_ref[0]

    ubuf[halo:halo + tt * n_slab] = u_ref[...]
    rs = min(CONV_STRIP, tt)
    first = CONV_HALO - (CONV_W - 1)

    def per_step(slab):
        return jnp.broadcast_to(slab[None], (rs, n_slab, LANES)).reshape(rs * n_slab, LANES)

    for s in range(tt // rs):
        acc = per_step(cb_ref[...])
        for j in range(CONV_W):
            r0 = (s * rs + first + j) * n_slab
            acc = acc + per_step(w_ref[j * n_slab:(j + 1) * n_slab, :]) * ubuf[r0:r0 + rs * n_slab, :]
        acc_scr[...] = acc
        rows = jnp.concatenate([acc_scr[pl.ds(q, rs, stride=n_slab), :] for q in range(n_slab)], axis=1)
        mu = jnp.mean(rows, axis=-1, keepdims=True)
        cen = rows - mu
        var = jnp.mean(cen * cen, axis=-1, keepdims=True)
        y = cen * lax.rsqrt(var + EPS) * g_ref[...] + b_ref[...]
        o_ref[s * rs:(s + 1) * rs, :] = (y * _sigmoid(y)).astype(o_ref.dtype)
    if tt >= CONV_HALO:
        ubuf[0:halo] = ubuf[tt * n_slab:tt * n_slab + halo]


def _conv_module(u_slab, init, conv_w, conv_b, ln_g, ln_b, nb, t, tt, blk0):
    c = conv_w.shape[1]
    n_slab = c // LANES
    n_t = t // tt
    vec = pl.BlockSpec((1, c), lambda b, i: (0, 0))
    return pl.pallas_call(
        functools.partial(_conv_kernel, tt, n_slab),
        grid=(nb, n_t),
        in_specs=[pl.BlockSpec((1, CONV_HALO * n_slab, LANES), lambda b, i: (b, 0, 0)),
                  pl.BlockSpec((tt * n_slab, LANES), lambda b, i: (blk0 + b * n_t + i, 0)),
                  pl.BlockSpec((CONV_W * n_slab, LANES), lambda b, i: (0, 0)),
                  pl.BlockSpec((n_slab, LANES), lambda b, i: (0, 0)), vec, vec],
        out_specs=pl.BlockSpec((tt, c), lambda b, i: (b * n_t + i, 0)),
        out_shape=jax.ShapeDtypeStruct((nb * t, c), bf16),
        scratch_shapes=[pltpu.VMEM(((CONV_HALO + tt) * n_slab, LANES), f32),
                        pltpu.VMEM((min(CONV_STRIP, tt) * n_slab, LANES), f32)],
        compiler_params=_params(("arbitrary", "arbitrary")),
        name="conv",
    )(init.reshape(nb, CONV_HALO * n_slab, LANES), u_slab, conv_w.reshape(CONV_W * n_slab, LANES),
      conv_b.reshape(n_slab, LANES), ln_g, ln_b)


def _mix_kernel(bounds, ya_p, ya_s, yb_p, yb_s, c_p, c_s, gates, pa, pb, pc, o_ref):
    d = o_ref.shape[1]

    def go(k):
        ya, yb, c = ((ya_p, yb_p, c_p), (ya_s, yb_s, c_s))[k]
        a = jnp.dot(ya[...], pa[...], preferred_element_type=f32)
        mixed = gates[:, 0:d].astype(f32) * a
        b = jnp.dot(yb[...], pb[...], preferred_element_type=f32)
        mixed = mixed + gates[:, d:2 * d].astype(f32) * b
        cc = jnp.dot(c[...], pc[...], preferred_element_type=f32)
        mixed = mixed + gates[:, 2 * d:3 * d].astype(f32) * cc
        o_ref[...] = mixed.astype(o_ref.dtype)

    _when_segment(pl.program_id(0), bounds, go)


def _mix(ya_p, ya_s, yb_p, yb_s, c_p, c_s, gates, pa, pb, pc, tm):
    m, d3 = gates.shape
    d = d3 // 3
    n_p, n_s = ya_p.shape[0] // tm, ya_s.shape[0] // tm
    bounds = (0, n_p, n_p + n_s)
    c_conv = c_p.shape[1]
    const = lambda shape: pl.BlockSpec(shape, lambda i: (0, 0))
    return pl.pallas_call(
        functools.partial(_mix_kernel, bounds),
        grid=(m // tm,),
        in_specs=[_seg_spec((tm, W_HEADS), 0, n_p), _seg_spec((tm, W_HEADS), n_p, n_s),
                  _seg_spec((tm, W_HEADS), 0, n_p), _seg_spec((tm, W_HEADS), n_p, n_s),
                  _seg_spec((tm, c_conv), 0, n_p), _seg_spec((tm, c_conv), n_p, n_s),
                  pl.BlockSpec((tm, d3), lambda i: (i, 0)),
                  const((W_HEADS, d)), const((W_HEADS, d)), const((c_conv, d))],
        out_specs=pl.BlockSpec((tm, d), lambda i: (i, 0)),
        out_shape=jax.ShapeDtypeStruct((m, d), bf16),
        compiler_params=_params(("arbitrary",)),
        name="mix",
    )(ya_p, ya_s, yb_p, yb_s, c_p, c_s, gates, pa, pb, pc)


def _route(logits):
    shape = logits.shape
    lane = lax.broadcasted_iota(jnp.int32, shape, 1)
    lane_f = lane.astype(f32)
    big = float(LANES)
    gl = jnp.where(lane < N_GROUPS, logits, -jnp.inf)
    g_max = jnp.max(gl, axis=-1, keepdims=True)
    g_idx = jnp.min(jnp.where(gl == g_max, lane_f, big), axis=-1, keepdims=True)
    g_sum = jnp.sum(jnp.exp(gl - g_max), axis=-1, keepdims=True)
    g_w = 1.0 / g_sum
    lo = N_GROUPS + g_idx * EXPERTS_PER_GROUP
    el = jnp.where((lane_f >= lo) & (lane_f < lo + EXPERTS_PER_GROUP), logits, -jnp.inf)
    m1 = jnp.max(el, axis=-1, keepdims=True)
    i1 = jnp.min(jnp.where(el == m1, lane_f, big), axis=-1, keepdims=True)
    el2 = jnp.where(lane_f == i1, -jnp.inf, el)
    m2 = jnp.max(el2, axis=-1, keepdims=True)
    i2 = jnp.min(jnp.where(el2 == m2, lane_f, big), axis=-1, keepdims=True)
    e21 = jnp.exp(m2 - m1)
    den = 1.0 + e21
    w1 = g_w * (1.0 / den)
    w2 = g_w * (e21 / den)
    eid = jnp.where(lane == 0, i1 - N_GROUPS, jnp.where(lane == 1, i2 - N_GROUPS, 0.0)).astype(jnp.int32)
    wgt = jnp.where(lane == 0, w1, jnp.where(lane == 1, w2, 0.0))
    return eid, wgt


def _outproj_kernel(n_slab, n_seg, bounds, *refs):
    xs = refs[:n_seg]
    mixed, wo, g2, wr, br, xo, h2o, eid_o, wgt_o = refs[n_seg:]
    tm = mixed.shape[0]
    i = pl.program_id(0)
    x = xs[0][...]
    for k in range(1, n_seg):
        x = jnp.where(i >= bounds[k], xs[k][...], x)
    xn = x + jnp.dot(mixed[...], wo[...], preferred_element_type=f32)
    xo[...] = xn
    h2 = _rms(xn, g2[...])
    for s in range(n_slab):
        h2o[pl.ds(s, tm, stride=n_slab), :] = h2[:, s * LANES:(s + 1) * LANES]
    h_hi = h2.astype(bf16)
    h_lo = (h2 - h_hi.astype(f32)).astype(bf16)
    hi = jnp.dot(h_hi, wr[...], preferred_element_type=f32)
    lo = jnp.dot(h_lo, wr[:, :LANES], preferred_element_type=f32)
    logits = hi[:, :LANES] + (hi[:, LANES:] + lo) + br[...]
    eid, wgt = _route(logits)
    eid_o[...] = eid
    wgt_o[...] = wgt


def _outproj(mixed, x_segs, wo, g2, wr, br, tm):
    m, d = mixed.shape
    n_slab = d // LANES
    counts = [a.shape[0] // tm for a in x_segs]
    bounds = [0]
    for cnt in counts:
        bounds.append(bounds[-1] + cnt)
    const = lambda shape: pl.BlockSpec(shape, lambda i: (0, 0))
    row = lambda w: pl.BlockSpec((tm, w), lambda i: (i, 0))
    sds = jax.ShapeDtypeStruct
    return pl.pallas_call(
        functools.partial(_outproj_kernel, n_slab, len(x_segs), tuple(bounds)),
        grid=(m // tm,),
        in_specs=[_seg_spec((tm, d), bounds[k], counts[k]) for k in range(len(x_segs))] +
                 [row(d), const((d, d)), const((1, d)), const((d, 2 * LANES)), const((1, LANES))],
        out_specs=[row(d), pl.BlockSpec((tm * n_slab, LANES), lambda i: (i, 0)), row(LANES), row(LANES)],
        out_shape=[sds((m, d), f32), sds((m * n_slab, LANES), f32), sds((m, LANES), jnp.int32),
                   sds((m, LANES), f32)],
        compiler_params=_params(("arbitrary",)),
        name="outproj",
    )(*x_segs, mixed, wo, g2, wr, br)


TB = 256
TD = 256


def _plan(eid):
    flat_e = eid.reshape(-1)
    n_assign = flat_e.shape[0]
    onehot = (flat_e[:, None] == jnp.arange(N_EXPERTS, dtype=jnp.int32)[None, :]).astype(jnp.int32)
    csum = jnp.cumsum(onehot, axis=0)
    counts = csum[-1]
    rank = jnp.sum(onehot * csum, axis=1) - 1
    n_blk_e = (counts + TB - 1) // TB
    blk_end = jnp.cumsum(n_blk_e)
    blk_start = blk_end - n_blk_e
    dest = blk_start[flat_e] * TB + rank
    n_blocks = -(-n_assign // TB) + N_EXPERTS
    blk_ids = jnp.arange(n_blocks, dtype=jnp.int32)
    blk_expert = jnp.minimum(jnp.sum((blk_end[None, :] <= blk_ids[:, None]).astype(jnp.int32), axis=1),
                             N_EXPERTS - 1)
    last_blk = jnp.where(n_blk_e > 0, blk_end - 1, -1).astype(jnp.int32)
    return dest.astype(jnp.int32), blk_expert, blk_end[-1:].astype(jnp.int32), last_blk, n_blocks


DMA_UNROLL = 8


def _issue_rows(n, copy):
    per_trip = DMA_UNROLL // TOP_K

    def trip(t, carry):
        for r in range(per_trip):
            for k in range(TOP_K):
                copy(t * per_trip + r, k).start()
        return carry

    lax.fori_loop(0, n // DMA_UNROLL, trip, 0)


def _dispatch_kernel(n_slab, n_blocks, dest_ref, last_ref, nu_ref, h_ref, xs_ref, zero_scr, sem, zero_sem):
    n = dest_ref.shape[2]
    blk_rows = TB * n_slab

    @pl.when(pl.program_id(0) == 0)
    def _():
        zero_scr[...] = jnp.zeros_like(zero_scr)

        def zero_block(b):
            rows = pl.ds(pl.multiple_of(b * blk_rows, blk_rows), blk_rows)
            return pltpu.make_async_copy(zero_scr, xs_ref.at[rows, :], zero_sem)

        def over_blocks(act):
            for e in range(N_EXPERTS):
                pl.when(last_ref[e] >= 0)(lambda e=e: act(zero_block(last_ref[e])))
            lax.fori_loop(nu_ref[0], n_blocks, lambda b, c: (act(zero_block(b)), c)[1], 0)

        over_blocks(lambda cp: cp.start())
        over_blocks(lambda cp: cp.wait())

    def copy(tok, k):
        src = h_ref.at[pl.ds(pl.multiple_of(tok * n_slab, n_slab), n_slab), :]
        dst = xs_ref.at[pl.ds(pl.multiple_of(dest_ref[0, 0, tok * TOP_K + k] * n_slab, n_slab), n_slab), :]
        return pltpu.make_async_copy(src, dst, sem)

    _issue_rows(n, copy)
    for half in range(TOP_K):
        rows = pl.ds(0, (n // TOP_K) * n_slab)
        pltpu.make_async_copy(h_ref, xs_ref.at[rows, :], sem).wait()


def _dispatch(h2_slab, dest, last_blk, n_used, n_blocks, n_slab, td):
    m = h2_slab.shape[0] // n_slab
    n_steps = m // td
    dest3 = dest.reshape(n_steps, 1, td * TOP_K)
    smem = pl.BlockSpec(memory_space=pltpu.SMEM)
    return pl.pallas_call(
        functools.partial(_dispatch_kernel, n_slab, n_blocks),
        grid=(n_steps,),
        in_specs=[pl.BlockSpec((1, 1, td * TOP_K), lambda i: (i, 0, 0), memory_space=pltpu.SMEM),
                  smem, smem,
                  pl.BlockSpec((td * n_slab, LANES), lambda i: (i, 0))],
        out_specs=pl.BlockSpec(memory_space=pl.ANY),
        out_shape=jax.ShapeDtypeStruct((n_blocks * TB * n_slab, LANES), f32),
        scratch_shapes=[pltpu.VMEM((TB * n_slab, LANES), f32), pltpu.SemaphoreType.DMA(()),
                        pltpu.SemaphoreType.DMA(())],
        compiler_params=_params(("arbitrary",)),
        name="dispatch",
    )(dest3, last_blk, n_used, h2_slab)


def _expert_kernel(n_slab, be_ref, nu_ref, xs_ref, wg_ref, wu_ref, wd_ref, o_ref, wgu_scr, wd_scr):
    i = pl.program_id(0)
    de = wd_ref.shape[1]

    @pl.when(i < nu_ref[0])
    def _():
        @pl.when((i == 0) | (be_ref[i] != be_ref[jnp.maximum(i - 1, 0)]))
        def _():
            wgu_scr[:, :de] = wg_ref[0].astype(bf16)
            wgu_scr[:, de:] = wu_ref[0].astype(bf16)
            wd_scr[...] = wd_ref[0].astype(bf16)

        x = jnp.concatenate([xs_ref[pl.ds(s, TB, stride=n_slab), :] for s in range(n_slab)], axis=1)
        gu = jnp.dot(x.astype(bf16), wgu_scr[...], preferred_element_type=f32)
        g = gu[:, :de]
        hmid = (g * _sigmoid(g)) * gu[:, de:]
        y = jnp.dot(hmid.astype(bf16), wd_scr[...], preferred_element_type=f32)
        for s in range(n_slab):
            o_ref[pl.ds(s, TB, stride=n_slab), :] = y[:, s * LANES:(s + 1) * LANES]


def _experts(xs, blk_expert, n_used, w_gate, w_up, w_down, layer, n_blocks, n_slab):
    d, de = w_gate.shape[2], w_gate.shape[3]
    blk = lambda i, be, nu: (jnp.minimum(i, nu[0] - 1), 0)
    by_expert = lambda i, be, nu: (layer, be[jnp.minimum(i, nu[0] - 1)], 0, 0)
    grid_spec = pltpu.PrefetchScalarGridSpec(
        num_scalar_prefetch=2,
        grid=(n_blocks,),
        in_specs=[pl.BlockSpec((TB * n_slab, LANES), blk),
                  pl.BlockSpec((None, 1, d, de), by_expert), pl.BlockSpec((None, 1, d, de), by_expert),
                  pl.BlockSpec((None, 1, de, d), by_expert)],
        out_specs=pl.BlockSpec((TB * n_slab, LANES), blk),
        scratch_shapes=[pltpu.VMEM((d, 2 * de), bf16), pltpu.VMEM((de, d), bf16)],
    )
    return pl.pallas_call(
        functools.partial(_expert_kernel, n_slab),
        grid_spec=grid_spec,
        out_shape=jax.ShapeDtypeStruct(xs.shape, f32),
        input_output_aliases={2: 0},
        compiler_params=_params(("arbitrary",)),
        name="experts",
    )(blk_expert, n_used, xs, w_gate, w_up, w_down)


def _combine_kernel(n_slab, final, bounds, dest_ref, next_ref, x_ref, wgt_ref, g_ref, ys_ref, *rest):
    outs, (gbuf, sems) = rest[:-2], rest[-2:]
    i = pl.program_id(0)
    n = dest_ref.shape[2]
    tm = x_ref.shape[0]
    slot = i % 2
    pitch = n_slab + SUBLANES

    def gather(idx_ref, s):
        def copy(tok, k):
            src = ys_ref.at[pl.ds(pl.multiple_of(idx_ref[0, 0, tok * TOP_K + k] * n_slab, n_slab), n_slab), :]
            row = k * tm * pitch + tok * pitch
            dst = gbuf.at[s, pl.ds(pl.multiple_of(row, SUBLANES), n_slab), :]
            return pltpu.make_async_copy(src, dst, sems.at[s])
        _issue_rows(n, copy)

    pl.when(i == 0)(lambda: gather(dest_ref, slot))
    pl.when(i + 1 < pl.num_programs(0))(lambda: gather(next_ref, 1 - slot))
    rows = pl.ds(0, n * n_slab)
    pltpu.make_async_copy(ys_ref.at[rows, :], gbuf.at[slot, rows, :], sems.at[slot]).wait()

    ys = []
    for k in range(TOP_K):
        y = jnp.concatenate([gbuf[slot, pl.ds(k * tm * pitch + s, tm, stride=pitch), :]
                             for s in range(n_slab)], axis=1)
        ys.append(y * wgt_ref[:, k:k + 1])
    x = x_ref[...] + (ys[0] + ys[1])
    if final:
        x = _rms(x, g_ref[...])

    def store(k):
        outs[k][...] = x

    _when_segment(i, bounds, store)


def _combine(x, ys, dest, wgt, g, n_slab, final, seg_rows):
    m, d = x.shape
    n_steps = m // TD
    dest3 = dest.reshape(n_steps, 1, TD * TOP_K)
    counts = [r // TD for r in seg_rows]
    bounds = [0]
    for cnt in counts:
        bounds.append(bounds[-1] + cnt)
    out_specs = [_seg_spec((TD, d), bounds[k], counts[k]) for k in range(len(seg_rows))]
    out_shape = [jax.ShapeDtypeStruct((r, d), f32) for r in seg_rows]
    idx_block = (1, 1, TD * TOP_K)
    return pl.pallas_call(
        functools.partial(_combine_kernel, n_slab, final, tuple(bounds)),
        grid=(n_steps,),
        in_specs=[pl.BlockSpec(idx_block, lambda i: (i, 0, 0), memory_space=pltpu.SMEM),
                  pl.BlockSpec(idx_block, lambda i: (jnp.minimum(i + 1, n_steps - 1), 0, 0),
                               memory_space=pltpu.SMEM),
                  pl.BlockSpec((TD, d), lambda i: (i, 0)),
                  pl.BlockSpec((TD, LANES), lambda i: (i, 0)),
                  pl.BlockSpec((1, d), lambda i: (0, 0)),
                  pl.BlockSpec(memory_space=pl.ANY)],
        out_specs=out_specs, out_shape=out_shape,
        scratch_shapes=[pltpu.VMEM((2, TD * TOP_K * (n_slab + SUBLANES), LANES), f32),
                        pltpu.SemaphoreType.DMA((2,))],
        compiler_params=_params(("arbitrary",)),
        name="combine",
    )(dest3, dest3, x, wgt, g, ys)


def _pack_w_in(w_in, b_in, d):
    c_conv = d // 2
    half = TN // 2
    n_qkv = 6 * W_HEADS
    f0 = n_qkv
    c0 = f0 + N_HEADS
    g0 = c0 + 2 * c_conv
    def pack(a):
        parts = [a[..., :n_qkv]]
        for c in range(c_conv // half):
            parts.append(a[..., c0 + c * half:c0 + (c + 1) * half])
            parts.append(a[..., c0 + c_conv + c * half:c0 + c_conv + (c + 1) * half])
        parts.append(a[..., g0:])
        return jnp.concatenate(parts, axis=-1)

    w_main = pack(w_in.astype(bf16))
    b_main = pack(b_in)[None, :].astype(f32)
    w_f = jnp.pad(w_in[:, f0:f0 + N_HEADS], ((0, 0), (0, LANES - N_HEADS))).astype(bf16)
    b_f = jnp.pad(b_in[f0:f0 + N_HEADS], (0, LANES - N_HEADS))[None, :].astype(f32)
    return w_main, b_main, w_f, b_f


def kernel(x_prompt, x_sample, cache_a_k, cache_a_v, cache_b_k, cache_b_v, cache_b_logf, state_conv, norm_mix_g, w_in, b_in, rel_bias, conv_w, conv_b, conv_ln_g, conv_ln_b, w_proj_a, w_proj_b, w_proj_c, w_out, norm_ffn_g, w_router_group, b_router_group, w_router_expert, b_router_expert, w_e_gate, w_e_up, w_e_down, norm_final_g):
    nb_p, t_p, d = x_prompt.shape
    nb_s, t_s, _ = x_sample.shape
    depth = w_in.shape[0]
    past = cache_b_k.shape[2]
    a_rows = cache_a_k.shape[2]
    m_p, m_s = nb_p * t_p, nb_s * t_s
    m = m_p + m_s
    c_conv = d // 2
    n_slab = d // LANES
    tm = _row_tile(np.gcd(m_p, m_s), 512)
    tm_mix = _row_tile(np.gcd(m_p, m_s), 256)
    assert m_p % TD == 0 and m_s % TD == 0 and t_s % 16 == 0 and m_p % t_s == 0

    a_keep = min(WINDOW_A, t_p)
    x_segs = [x_prompt.reshape(m_p, d), x_sample.reshape(m_s, d)]
    kv_states = [jnp.zeros((depth, rows, W_HEADS), f32)
                 for rows in (nb_p * a_keep, nb_p * a_keep, m_p, m_p, m_s, m_s, m_s, m_s)]
    p_states, s_states = [], []
    for l in range(depth):
        w_main, b_main, w_f, b_f = _pack_w_in(w_in[l], b_in[l], d)
        (qa, ka16, va16, qb, kb16, vb16, u, gates, logf), kv_states = _inproj(
            x_segs, norm_mix_g[l][None, :], w_main, b_main, w_f, b_f, kv_states, l, m_p, t_p, a_keep, tm)

        ya_p = _band_attention_prompt(qa, ka16, va16, rel_bias[l], nb_p, t_p, 4 * CHUNK)
        kk = jnp.concatenate([cache_a_k[l].reshape(nb_s, a_rows, W_HEADS).astype(bf16),
                              ka16[m_p:].reshape(nb_s, t_s, W_HEADS)], axis=1).reshape(-1, W_HEADS)
        vv = jnp.concatenate([cache_a_v[l].reshape(nb_s, a_rows, W_HEADS).astype(bf16),
                              va16[m_p:].reshape(nb_s, t_s, W_HEADS)], axis=1).reshape(-1, W_HEADS)
        ya_s = _band_attention_sample(qa, kk, vv, rel_bias[l], nb_s, t_s, a_rows, m_p // t_s)

        logf_p = logf[:m_p].reshape(nb_p, t_p, N_HEADS)
        logf_s = logf[m_p:].reshape(nb_s, t_s, N_HEADS)
        cum_p = _cumsum_time(logf_p.transpose(0, 2, 1))
        f_p = cum_p.transpose(0, 2, 1).reshape(m_p, N_HEADS)
        yb_p = _fox_attention(_fox_expand("q", qb, f_p, m_p, tm), _fox_expand("k", kb16, f_p, m_p, tm),
                              _fox_expand("v", vb16, None, m_p, tm),
                              nb_p, t_p, t_p, _row_tile(t_p, 2048), _row_tile(t_p, 512))
        cum_s = _cumsum_time(jnp.concatenate([cache_b_logf[l].astype(f32), logf_s], axis=1).transpose(0, 2, 1))
        t_ks = past + t_s
        kk = jnp.concatenate([cache_b_k[l].reshape(nb_s, past, W_HEADS).astype(bf16),
                              kb16[m_p:].reshape(nb_s, t_s, W_HEADS)], axis=1).reshape(-1, W_HEADS)
        vv = jnp.concatenate([cache_b_v[l].reshape(nb_s, past, W_HEADS).astype(bf16),
                              vb16[m_p:].reshape(nb_s, t_s, W_HEADS)], axis=1).reshape(-1, W_HEADS)
        f_ks = cum_s.transpose(0, 2, 1)
        yb_s = _fox_attention(
            _fox_expand("q", qb, f_ks[:, past:].reshape(m_s, N_HEADS), m_s, t_s, blk0=m_p // t_s),
            _fox_expand("k", kk, f_ks.reshape(nb_s * t_ks, N_HEADS), nb_s * t_ks, t_ks),
            _fox_expand("v", vv, None, nb_s * t_ks, t_ks),
            nb_s, t_s, t_ks, t_s, t_ks)

        conv_args = (conv_w[l], conv_b[l][None, :], conv_ln_g[l][None, :], conv_ln_b[l][None, :])
        c_p = _conv_module(u, jnp.zeros((nb_p, CONV_HALO, c_conv), f32), *conv_args,
                           nb_p, t_p, _row_tile(t_p, 256), 0)
        init_s = jnp.pad(state_conv[l], ((0, 0), (CONV_HALO - (CONV_W - 1), 0), (0, 0)))
        c_s = _conv_module(u, init_s, *conv_args, nb_s, t_s, t_s, m_p // t_s)

        mixed = _mix(ya_p, ya_s, yb_p, yb_s, c_p, c_s, gates, w_proj_a[l].astype(bf16),
                     w_proj_b[l].astype(bf16), w_proj_c[l].astype(bf16), tm_mix)
        wr = jnp.pad(jnp.concatenate([w_router_group[l], w_router_expert[l]], axis=1),
                     ((0, 0), (0, LANES - N_GROUPS - N_EXPERTS)))
        wr_hi = wr.astype(bf16)
        wr_parts = jnp.concatenate([wr_hi, (wr - wr_hi.astype(f32)).astype(bf16)], axis=1)
        br = jnp.pad(jnp.concatenate([b_router_group[l], b_router_expert[l]]),
                     (0, LANES - N_GROUPS - N_EXPERTS))[None, :]
        x_mid, h2_slab, eid, wgt = _outproj(mixed, x_segs, w_out[l].astype(bf16), norm_ffn_g[l][None, :],
                                            wr_parts, br, tm_mix)

        dest, blk_expert, n_used, last_blk, n_blocks = _plan(eid[:, :TOP_K])
        xs = _dispatch(h2_slab, dest, last_blk, n_used, n_blocks, n_slab, tm)
        ys = _experts(xs, blk_expert, n_used, w_e_gate, w_e_up, w_e_down, l, n_blocks, n_slab)
        final = l == depth - 1
        x_segs = _combine(x_mid, ys, dest, wgt, norm_final_g[None, :], n_slab, final,
                          (m_p, m_s) if final else (m,))

        n_cs = c_conv // LANES
        u_p = jnp.stack([u[((b + 1) * t_p - (CONV_W - 1)) * n_cs:(b + 1) * t_p * n_cs] for b in range(nb_p)])
        u_p = u_p.reshape(nb_p, CONV_W - 1, c_conv)
        u_s = u[m_p * n_cs:].reshape(nb_s, t_s, c_conv)
        p_states.append((logf_p, u_p))
        s_states.append((logf_s, jnp.concatenate([state_conv[l], u_s], axis=1)[:, -(CONV_W - 1):]))

    y_prompt = x_segs[0].reshape(nb_p, t_p, d)
    y_sample = x_segs[1].reshape(nb_s, t_s, d)
    stack = lambda states, k: jnp.stack([st[k] for st in states], axis=0)
    heads = lambda a, nb, t: a.reshape(depth, nb, t, N_HEADS, HEAD_DIM)
    ka_p, va_p, kb_p, vb_p, ka_s, va_s, kb_s, vb_s = kv_states
    return (y_prompt, y_sample,
            heads(ka_p, nb_p, a_keep), heads(va_p, nb_p, a_keep), heads(kb_p, nb_p, t_p), heads(vb_p, nb_p, t_p),
            stack(p_states, 0), stack(p_states, 1),
            heads(ka_s, nb_s, t_s), heads(va_s, nb_s, t_s), heads(kb_s, nb_s, t_s), heads(vb_s, nb_s, t_s),
            stack(s_states, 0), stack(s_states, 1))
```

```python
import functools

import jax
import jax.numpy as jnp
import numpy as np
from jax import lax
from jax.experimental import pallas as pl
from jax.experimental.pallas import tpu as pltpu

f32 = jnp.float32
bf16 = jnp.bfloat16

HEAD_DIM = 64
N_HEADS = 8
W_HEADS = N_HEADS * HEAD_DIM
CHUNK = 64
WINDOW_A = 8 * CHUNK
REL_CLIP = 128
CONV_W = 31
CONV_HALO = 32
N_GROUPS = 4
EXPERTS_PER_GROUP = 8
N_EXPERTS = N_GROUPS * EXPERTS_PER_GROUP
TOP_K = 2
SCALE = HEAD_DIM ** -0.5
EPS = 1e-6
NEG_INF = -1e30
LANES = 128
SUBLANES = 8
MIB = 1024 * 1024


def _params(sem, vmem_mib=48):
    return pltpu.CompilerParams(dimension_semantics=sem, vmem_limit_bytes=vmem_mib * MIB)


def _row_tile(m, cap):
    t = cap
    while m % t:
        t //= 2
    return t


def _sigmoid(z):
    return 0.5 * jnp.tanh(0.5 * z) + 0.5


def _rms(x, g):
    return x * lax.rsqrt(jnp.mean(x * x, axis=-1, keepdims=True) + EPS) * g


def _pack_bf16_pair(lo, hi):
    def rounded(x):
        bits = lax.bitcast_convert_type(x, jnp.uint32)
        return bits + jnp.uint32(0x7FFF) + ((bits >> 16) & jnp.uint32(1))
    return (rounded(hi) & jnp.uint32(0xFFFF0000)) | (rounded(lo) >> 16)


def _unpack_bf16_pair(word):
    lo = lax.bitcast_convert_type(word << 16, f32)
    hi = lax.bitcast_convert_type(word & jnp.uint32(0xFFFF0000), f32)
    return lo, hi


def _when_segment(i, bounds, fn):
    for k in range(len(bounds) - 1):
        pl.when((i >= bounds[k]) & (i < bounds[k + 1]))(functools.partial(fn, k))


def _seg_spec(block, start, count, width_axes=1):
    zeros = (0,) * width_axes
    return pl.BlockSpec(block, lambda i, *_: (jnp.clip(i - start, 0, count - 1),) + zeros)


TN = 1024
SEG_PER_TILE = TN // W_HEADS
N_QKV_TILES = 6 // SEG_PER_TILE


def _inproj_kernel(n_seg, bounds, n_conv, n_prompt, *refs):
    xs = refs[:n_seg]
    (g_ref, w_ref, b_ref, wf_ref, bf_ref) = refs[n_seg:n_seg + 5]
    outs = refs[n_seg + 5 + 8:-1]
    h_scr = refs[-1]
    (qa, ka16, va16, qb, kb16, vb16, u, gates, logf, ka_p, va_p, kb_p, vb_p, ka_s, va_s, kb_s, vb_s) = outs
    i = pl.program_id(0)
    j = pl.program_id(1)

    def norm(k):
        h_scr[...] = _rms(xs[k][...], g_ref[...]).astype(bf16)

    @pl.when(j == 0)
    def _():
        _when_segment(i, bounds, norm)
        zf = jnp.dot(h_scr[...], wf_ref[...], preferred_element_type=f32) + bf_ref[...]
        lf = jnp.minimum(zf, 0.0) - jnp.log1p(jnp.exp(-jnp.abs(zf)))
        logf[...] = lf[:, :N_HEADS]

    def project():
        return jnp.dot(h_scr[...], w_ref[...], preferred_element_type=f32) + b_ref[...]

    segments = ((qa, None, None), (ka16, ka_p, ka_s), (va16, va_p, va_s),
                (qb, None, None), (kb16, kb_p, kb_s), (vb16, vb_p, vb_s))
    is_prompt = i < n_prompt
    for t in range(N_QKV_TILES):
        for prompt_rows in (True, False):
            @pl.when((j == t) & (is_prompt if prompt_rows else ~is_prompt))
            def _(t=t, prompt_rows=prompt_rows):
                z = project()
                for n in range(SEG_PER_TILE):
                    zn = z[:, n * W_HEADS:(n + 1) * W_HEADS]
                    copy, state_p, state_s = segments[t * SEG_PER_TILE + n]
                    copy[...] = zn.astype(bf16)
                    state = state_p if prompt_rows else state_s
                    if state is not None:
                        state[...] = zn

    half = TN // 2
    for c in range(n_conv):
        @pl.when(j == N_QKV_TILES + c)
        def _(c=c):
            z = project()
            glu = z[:, :half] * _sigmoid(z[:, half:])
            tm = glu.shape[0]
            n_slab = n_conv * half // LANES
            for q in range(half // LANES):
                u[pl.ds(c * (half // LANES) + q, tm, stride=n_slab), :] = glu[:, q * LANES:(q + 1) * LANES]

    @pl.when(j >= N_QKV_TILES + n_conv)
    def _():
        gates[...] = _sigmoid(project()).astype(bf16)


def _inproj(x_segs, g, w_main, b_main, w_f, b_f, states, layer, m_p, t_p, a_keep, tm):
    d = x_segs[0].shape[1]
    m = sum(a.shape[0] for a in x_segs)
    c_conv = d // 2
    n_conv = c_conv // (TN // 2)
    n_gate = 3 * d // TN
    n_col = N_QKV_TILES + n_conv + n_gate
    assert w_main.shape == (d, n_col * TN)
    counts = [a.shape[0] // tm for a in x_segs]
    bounds = [0]
    for cnt in counts:
        bounds.append(bounds[-1] + cnt)
    n_seg = len(x_segs)
    n_p, n_s = m_p // tm, (m - m_p) // tm
    per_seq, keep = t_p // tm, a_keep // tm

    in_specs = [pl.BlockSpec((tm, d), (lambda i, j, s=bounds[k], n=counts[k]: (jnp.clip(i - s, 0, n - 1), 0)))
                for k in range(n_seg)]
    in_specs += [
        pl.BlockSpec((1, d), lambda i, j: (0, 0)),
        pl.BlockSpec((d, TN), lambda i, j: (0, j)),
        pl.BlockSpec((1, TN), lambda i, j: (0, j)),
        pl.BlockSpec((d, LANES), lambda i, j: (0, 0)),
        pl.BlockSpec((1, LANES), lambda i, j: (0, 0)),
    ] + [pl.BlockSpec(memory_space=pl.ANY)] * 8
    row = lambda width: pl.BlockSpec((tm, width), lambda i, j: (i, 0))

    def tail_rows(i, j):
        ip = jnp.minimum(i, n_p - 1)
        return (layer, (ip // per_seq) * keep + jnp.maximum(ip % per_seq - (per_seq - keep), 0), 0)

    state_block = (None, tm, W_HEADS)
    tail_spec = pl.BlockSpec(state_block, tail_rows)
    prompt_spec = pl.BlockSpec(state_block, lambda i, j: (layer, jnp.minimum(i, n_p - 1), 0))
    sample_spec = pl.BlockSpec(state_block, lambda i, j: (layer, jnp.clip(i - n_p, 0, n_s - 1), 0))
    out_specs = [row(W_HEADS)] * 6 + [
        pl.BlockSpec((tm * (c_conv // LANES), LANES), lambda i, j: (i, 0)),
        pl.BlockSpec((tm, TN), lambda i, j: (i, jnp.clip(j - (N_QKV_TILES + n_conv), 0, n_gate - 1))),
        row(N_HEADS),
        tail_spec, tail_spec, prompt_spec, prompt_spec, sample_spec, sample_spec, sample_spec, sample_spec]
    sds = jax.ShapeDtypeStruct
    out_shape = [sds((m, W_HEADS), bf16)] * 6 + [sds((m * (c_conv // LANES), LANES), f32), sds((m, 3 * d), bf16),
                                                 sds((m, N_HEADS), f32)]
    out_shape += [sds(s.shape, s.dtype) for s in states]
    n_in = n_seg + 5
    outs = pl.pallas_call(
        functools.partial(_inproj_kernel, n_seg, tuple(bounds), n_conv, n_p),
        grid=(m // tm, n_col),
        in_specs=in_specs, out_specs=out_specs, out_shape=out_shape,
        scratch_shapes=[pltpu.VMEM((tm, d), bf16)],
        input_output_aliases={n_in + k: 9 + k for k in range(8)},
        compiler_params=_params(("arbitrary", "arbitrary"), 58),
        name="inproj",
    )(*x_segs, g, w_main, b_main, w_f, b_f, *states)
    return outs[:9], outs[9:]


TC = 512


def _cumsum_kernel(x_ref, o_ref, carry):
    @pl.when(pl.program_id(1) == 0)
    def _():
        carry[...] = jnp.zeros_like(carry)

    blk = x_ref[0]
    r = lax.broadcasted_iota(jnp.int32, (TC, TC), 0)
    c = lax.broadcasted_iota(jnp.int32, (TC, TC), 1)
    tri = jnp.where(r <= c, 1.0, 0.0).astype(f32)
    cs = jnp.dot(blk, tri, precision=lax.Precision.HIGHEST, preferred_element_type=f32) + carry[:, 0:1]
    o_ref[0] = cs
    carry[...] = jnp.broadcast_to(cs[:, TC - 1:TC], carry.shape)


def _cumsum_time(x):
    nb, h, t = x.shape
    tp = -(-t // TC) * TC
    xp = jnp.pad(x, ((0, 0), (0, 0), (0, tp - t)))
    out = pl.pallas_call(
        _cumsum_kernel,
        grid=(nb, tp // TC),
        in_specs=[pl.BlockSpec((1, h, TC), lambda b, k: (b, 0, k))],
        out_specs=pl.BlockSpec((1, h, TC), lambda b, k: (b, 0, k)),
        out_shape=jax.ShapeDtypeStruct((nb, h, tp), f32),
        scratch_shapes=[pltpu.VMEM((h, LANES), f32)],
        compiler_params=_params(("arbitrary", "arbitrary")),
        name="cumsum",
    )(xp)
    return out[:, :, :t]


F_PARTS = 3


def _spare_base(h):
    return HEAD_DIM * (1 - h % 2)


def _fox_expand_kernel(kind, x_ref, f_ref, place_ref, o_ref):
    tm = x_ref.shape[0]
    lane = lax.broadcasted_iota(jnp.int32, (tm, LANES), 1)
    if kind != "v":
        rest = f_ref[...]
        stack = jnp.where(lane < (F_PARTS + 1) * N_HEADS, 1.0, 0.0)
        for n in range(F_PARTS):
            part = rest.astype(bf16).astype(f32)
            rest = rest - part
            stack = jnp.where((lane >= n * N_HEADS) & (lane < (n + 1) * N_HEADS), part, stack)
        spare_all = jnp.dot(stack.astype(bf16), place_ref[...], preferred_element_type=f32)
    for h in range(N_HEADS):
        pair = x_ref[:, (h // 2) * LANES:(h // 2 + 1) * LANES].astype(f32)
        if kind == "q":
            pair = pair * SCALE
        if kind == "v":
            spare = jnp.where(lane == _spare_base(h), 1.0, 0.0)
        else:
            spare = spare_all[:, h * LANES:(h + 1) * LANES]
        own = (lane < HEAD_DIM) if h % 2 == 0 else (lane >= HEAD_DIM)
        o_ref[h] = jnp.where(own, pair, spare).astype(o_ref.dtype)


def _fox_placement(kind):
    place = np.zeros((LANES, N_HEADS * LANES), np.float32)
    for h in range(N_HEADS):
        base = h * LANES + _spare_base(h)
        for n in range(F_PARTS):
            if kind == "q":
                place[n * N_HEADS + h, base + n] = 1.0
                place[F_PARTS * N_HEADS + h, base + F_PARTS + n] = 1.0
            else:
                place[F_PARTS * N_HEADS + h, base + n] = 1.0
                place[n * N_HEADS + h, base + F_PARTS + n] = -1.0
    return jnp.asarray(place, bf16)


def _fox_expand(kind, x, f, rows, tm, blk0=0):
    f_lanes = jnp.zeros((rows, LANES), f32) if f is None else jnp.tile(f, (1, LANES // N_HEADS))
    return pl.pallas_call(
        functools.partial(_fox_expand_kernel, kind),
        grid=(rows // tm,),
        in_specs=[pl.BlockSpec((tm, W_HEADS), lambda i: (blk0 + i, 0)),
                  pl.BlockSpec((tm, LANES), lambda i: (i, 0)),
                  pl.BlockSpec((LANES, N_HEADS * LANES), lambda i: (0, 0))],
        out_specs=pl.BlockSpec((N_HEADS, tm, LANES), lambda i: (0, i, 0)),
        out_shape=jax.ShapeDtypeStruct((N_HEADS, rows, LANES), bf16),
        compiler_params=_params(("arbitrary",)),
        name="fox_expand_" + kind,
    )(x, f_lanes, _fox_placement(kind))


FOX_HEADS_PER_TRIP = 2


def _fox_kernel(tq, tk, off, q_ref, k_ref, v_ref, o_ref, m_scr, acc_scr):
    i = pl.program_id(1)
    j = pl.program_id(2)

    @pl.when(j == 0)
    def _():
        m_scr[...] = jnp.full_like(m_scr, NEG_INF)
        acc_scr[...] = jnp.zeros_like(acc_scr)

    q_first = i * tq + off
    q_last = q_first + tq - 1
    k_first = j * tk
    k_last = k_first + tk - 1

    def body(row0, masked):
        rows = slice(row0, tq)
        if masked:
            kpos = k_first + lax.broadcasted_iota(jnp.int32, (tq - row0, tk), 1)
            qpos = q_first + row0 + lax.broadcasted_iota(jnp.int32, (tq - row0, tk), 0)
            vis = kpos <= qpos

        def head(h):
            s = lax.dot_general(q_ref[h, rows], k_ref[h], (((1,), (1,)), ((), ())), preferred_element_type=f32)
            if masked:
                s = jnp.where(vis, s, NEG_INF)
            m_old = m_scr[h, rows]
            m_new = jnp.maximum(m_old, jnp.max(s, axis=-1, keepdims=True))
            pr = jnp.exp(s - m_new[:, 0:1])
            pv = jnp.dot(pr.astype(bf16), v_ref[h], preferred_element_type=f32)
            acc_scr[h, rows] = jnp.exp(m_old - m_new) * acc_scr[h, rows] + pv
            m_scr[h, rows] = m_new

        def trip(g, carry):
            for n in range(FOX_HEADS_PER_TRIP):
                head(g * FOX_HEADS_PER_TRIP + n)
            return carry

        lax.fori_loop(0, N_HEADS // FOX_HEADS_PER_TRIP, trip, 0)

    pl.when(k_last <= q_first)(functools.partial(body, 0, False))
    if tq % tk == 0 and off % tk == 0:
        for c in range(tq // tk):
            pl.when(k_first == q_first + c * tk)(functools.partial(body, c * tk, True))
    else:
        pl.when((k_first <= q_last) & (k_last > q_first))(functools.partial(body, 0, True))

    @pl.when(j == pl.num_programs(2) - 1)
    def _():
        lane = lax.broadcasted_iota(jnp.int32, (tq, LANES), 1)
        for p in range(N_HEADS // 2):
            even = acc_scr[2 * p]
            odd = acc_scr[2 * p + 1]
            even = even / even[:, _spare_base(0):_spare_base(0) + 1]
            odd = odd / odd[:, _spare_base(1):_spare_base(1) + 1]
            o_ref[:, p * LANES:(p + 1) * LANES] = jnp.where(lane < HEAD_DIM, even, odd).astype(o_ref.dtype)


def _fox_attention(q, k, v, nb, t_q, t_k, tq, tk):
    nq, nk = t_q // tq, t_k // tk
    off = t_k - t_q

    def last_k(i):
        return jnp.minimum((i * tq + tq - 1 + off) // tk, nk - 1)

    kv_spec = pl.BlockSpec((N_HEADS, tk, LANES), lambda b, i, j: (0, b * nk + jnp.minimum(j, last_k(i)), 0))
    return pl.pallas_call(
        functools.partial(_fox_kernel, tq, tk, off),
        grid=(nb, nq, nk),
        in_specs=[pl.BlockSpec((N_HEADS, tq, LANES), lambda b, i, j: (0, b * nq + i, 0)), kv_spec, kv_spec],
        out_specs=pl.BlockSpec((tq, W_HEADS), lambda b, i, j: (b * nq + i, 0)),
        out_shape=jax.ShapeDtypeStruct((nb * t_q, W_HEADS), bf16),
        scratch_shapes=[pltpu.VMEM((N_HEADS, tq, LANES), f32), pltpu.VMEM((N_HEADS, tq, LANES), f32)],
        compiler_params=_params(("arbitrary", "arbitrary", "arbitrary")),
        name="fox",
    )(q, k, v)


def _band_kernel(rows, gq, wk, has_prev, *refs):
    if has_prev:
        q_ref, kp_ref, kc_ref, vp_ref, vc_ref, bias_ref, o_ref, k_scr, v_scr = refs
        k_scr[0:WINDOW_A] = kp_ref[...]
        k_scr[WINDOW_A:WINDOW_A + rows] = kc_ref[...]
        v_scr[0:WINDOW_A] = vp_ref[...]
        v_scr[WINDOW_A:WINDOW_A + rows] = vc_ref[...]
        k_src, v_src = k_scr, v_scr
    else:
        q_ref, k_src, v_src, bias_ref, o_ref = refs
    i = pl.program_id(1)
    lane = lax.broadcasted_iota(jnp.int32, (gq, LANES), 1)
    low = lane < HEAD_DIM
    for g in range(rows // gq):
        r0 = g * gq
        if has_prev:
            key_pos = (i - 1) * WINDOW_A + r0 + lax.broadcasted_iota(jnp.int32, (gq, wk), 1)
            vis = key_pos >= 0
        for p in range(N_HEADS // 2):
            cols = slice(p * LANES, (p + 1) * LANES)
            q2 = q_ref[r0:r0 + gq, cols]
            kw = k_src[r0:r0 + wk, cols]
            vw = v_src[r0:r0 + wk, cols]
            outs = []
            for half in range(2):
                h = 2 * p + half
                qm = jnp.where(low if half == 0 else ~low, q2, jnp.zeros_like(q2))
                s = lax.dot_general(qm, kw, (((1,), (1,)), ((), ())), preferred_element_type=f32) * SCALE
                s = s + bias_ref[h]
                if has_prev:
                    s = jnp.where(vis, s, NEG_INF)
                m = jnp.max(s, axis=-1, keepdims=True)
                pr = jnp.exp(s - m)
                l = jnp.sum(pr, axis=-1, keepdims=True)
                pv = jnp.dot(pr.astype(bf16), vw, preferred_element_type=f32)
                outs.append(pv / l)
            o_ref[r0:r0 + gq, cols] = jnp.where(low, outs[0], outs[1]).astype(o_ref.dtype)


def _rel_bias_table(rel_bias, gq, wk, q_shift):
    period = wk + gq
    j = np.arange(period)
    k = np.where(j < wk, j, j - period)
    line = rel_bias.astype(f32)[:, np.clip(q_shift - k, -REL_CLIP, REL_CLIP) + REL_CLIP]
    tiled = jnp.tile(line, (1, gq))[:, :gq * (period - 1)]
    return tiled.reshape(-1, gq, period - 1)[:, :, :wk]


def _band_bias(rel_bias, gq, wk, q_shift):
    r = np.arange(gq)[:, None]
    s = np.arange(wk)[None, :]
    band0 = (r // CHUNK) * CHUNK + q_shift - WINDOW_A
    ok = (s >= band0) & (s < band0 + WINDOW_A + CHUNK)
    return jnp.where(ok[None], _rel_bias_table(rel_bias, gq, wk, q_shift), NEG_INF)


def _band_attention_prompt(q, k, v, rel_bias, nb, t, gq):
    rows = WINDOW_A
    wk = WINDOW_A + gq
    n_steps = t // rows
    bias = _band_bias(rel_bias, gq, wk, WINDOW_A)
    cur = pl.BlockSpec((rows, W_HEADS), lambda b, i: (b * n_steps + i, 0))
    prev = pl.BlockSpec((rows, W_HEADS), lambda b, i: (b * n_steps + jnp.maximum(i - 1, 0), 0))
    return pl.pallas_call(
        functools.partial(_band_kernel, rows, gq, wk, True),
        grid=(nb, n_steps),
        in_specs=[cur, prev, cur, prev, cur,
                  pl.BlockSpec((N_HEADS, gq, wk), lambda b, i: (0, 0, 0))],
        out_specs=cur,
        out_shape=jax.ShapeDtypeStruct((nb * t, W_HEADS), bf16),
        scratch_shapes=[pltpu.VMEM((2 * rows, W_HEADS), bf16), pltpu.VMEM((2 * rows, W_HEADS), bf16)],
        compiler_params=_params(("arbitrary", "arbitrary")),
        name="band_prompt",
    )(q, k, k, v, v, bias)


def _band_attention_sample(q, kk, vv, rel_bias, nb, s_new, l_cache, q_blk0):
    wk = l_cache + s_new
    bias = _rel_bias_table(rel_bias, s_new, wk, l_cache)
    return pl.pallas_call(
        functools.partial(_band_kernel, s_new, s_new, wk, False),
        grid=(nb, 1),
        in_specs=[pl.BlockSpec((s_new, W_HEADS), lambda b, i: (q_blk0 + b, 0)),
                  pl.BlockSpec((wk, W_HEADS), lambda b, i: (b, 0)),
                  pl.BlockSpec((wk, W_HEADS), lambda b, i: (b, 0)),
                  pl.BlockSpec((N_HEADS, s_new, wk), lambda b, i: (0, 0, 0))],
        out_specs=pl.BlockSpec((s_new, W_HEADS), lambda b, i: (b, 0)),
        out_shape=jax.ShapeDtypeStruct((nb * s_new, W_HEADS), bf16),
        compiler_params=_params(("arbitrary", "arbitrary")),
        name="band_sample",
    )(q, kk, vv, bias)


CONV_STRIP = 32


def _conv_kernel(tt, n_slab, init_ref, u_ref, w_ref, cb_ref, g_ref, b_ref, o_ref, ubuf, acc_scr):
    halo = CONV_HALO * n_slab

    @pl.when(pl.program_id(1) == 0)
    def _():
        ubuf[0:halo] = init_ref[0]

    ubuf[halo:halo + tt * n_slab] = u_ref[...]
    rs = min(CONV_STRIP, tt)
    first = CONV_HALO - (CONV_W - 1)

    def per_step(slab):
        return jnp.broadcast_to(slab[None], (rs, n_slab, LANES)).reshape(rs * n_slab, LANES)

    for s in range(tt // rs):
        acc = per_step(cb_ref[...])
        for j in range(CONV_W):
            r0 = (s * rs + first + j) * n_slab
            acc = acc + per_step(w_ref[j * n_slab:(j + 1) * n_slab, :]) * ubuf[r0:r0 + rs * n_slab, :]
        acc_scr[...] = acc
        rows = jnp.concatenate([acc_scr[pl.ds(q, rs, stride=n_slab), :] for q in range(n_slab)], axis=1)
        mu = jnp.mean(rows, axis=-1, keepdims=True)
        cen = rows - mu
        var = jnp.mean(cen * cen, axis=-1, keepdims=True)
        y = cen * lax.rsqrt(var + EPS) * g_ref[...] + b_ref[...]
        o_ref[s * rs:(s + 1) * rs, :] = (y * _sigmoid(y)).astype(o_ref.dtype)
    if tt >= CONV_HALO:
        ubuf[0:halo] = ubuf[tt * n_slab:tt * n_slab + halo]


def _conv_module(u_slab, init, conv_w, conv_b, ln_g, ln_b, nb, t, tt, blk0):
    c = conv_w.shape[1]
    n_slab = c // LANES
    n_t = t // tt
    vec = pl.BlockSpec((1, c), lambda b, i: (0, 0))
    return pl.pallas_call(
        functools.partial(_conv_kernel, tt, n_slab),
        grid=(nb, n_t),
        in_specs=[pl.BlockSpec((1, CONV_HALO * n_slab, LANES), lambda b, i: (b, 0, 0)),
                  pl.BlockSpec((tt * n_slab, LANES), lambda b, i: (blk0 + b * n_t + i, 0)),
                  pl.BlockSpec((CONV_W * n_slab, LANES), lambda b, i: (0, 0)),
                  pl.BlockSpec((n_slab, LANES), lambda b, i: (0, 0)), vec, vec],
        out_specs=pl.BlockSpec((tt, c), lambda b, i: (b * n_t + i, 0)),
        out_shape=jax.ShapeDtypeStruct((nb * t, c), bf16),
        scratch_shapes=[pltpu.VMEM(((CONV_HALO + tt) * n_slab, LANES), f32),
                        pltpu.VMEM((min(CONV_STRIP, tt) * n_slab, LANES), f32)],
        compiler_params=_params(("arbitrary", "arbitrary")),
        name="conv",
    )(init.reshape(nb, CONV_HALO * n_slab, LANES), u_slab, conv_w.reshape(CONV_W * n_slab, LANES),
      conv_b.reshape(n_slab, LANES), ln_g, ln_b)


def _mix_kernel(bounds, ya_p, ya_s, yb_p, yb_s, c_p, c_s, gates, pa, pb, pc, o_ref):
    d = o_ref.shape[1]

    def go(k):
        ya, yb, c = ((ya_p, yb_p, c_p), (ya_s, yb_s, c_s))[k]
        a = jnp.dot(ya[...], pa[...], preferred_element_type=f32)
        mixed = gates[:, 0:d].astype(f32) * a
        b = jnp.dot(yb[...], pb[...], preferred_element_type=f32)
        mixed = mixed + gates[:, d:2 * d].astype(f32) * b
        cc = jnp.dot(c[...], pc[...], preferred_element_type=f32)
        mixed = mixed + gates[:, 2 * d:3 * d].astype(f32) * cc
        o_ref[...] = mixed.astype(o_ref.dtype)

    _when_segment(pl.program_id(0), bounds, go)


def _mix(ya_p, ya_s, yb_p, yb_s, c_p, c_s, gates, pa, pb, pc, tm):
    m, d3 = gates.shape
    d = d3 // 3
    n_p, n_s = ya_p.shape[0] // tm, ya_s.shape[0] // tm
    bounds = (0, n_p, n_p + n_s)
    c_conv = c_p.shape[1]
    const = lambda shape: pl.BlockSpec(shape, lambda i: (0, 0))
    return pl.pallas_call(
        functools.partial(_mix_kernel, bounds),
        grid=(m // tm,),
        in_specs=[_seg_spec((tm, W_HEADS), 0, n_p), _seg_spec((tm, W_HEADS), n_p, n_s),
                  _seg_spec((tm, W_HEADS), 0, n_p), _seg_spec((tm, W_HEADS), n_p, n_s),
                  _seg_spec((tm, c_conv), 0, n_p), _seg_spec((tm, c_conv), n_p, n_s),
                  pl.BlockSpec((tm, d3), lambda i: (i, 0)),
                  const((W_HEADS, d)), const((W_HEADS, d)), const((c_conv, d))],
        out_specs=pl.BlockSpec((tm, d), lambda i: (i, 0)),
        out_shape=jax.ShapeDtypeStruct((m, d), bf16),
        compiler_params=_params(("arbitrary",)),
        name="mix",
    )(ya_p, ya_s, yb_p, yb_s, c_p, c_s, gates, pa, pb, pc)


def _route(logits):
    shape = logits.shape
    lane = lax.broadcasted_iota(jnp.int32, shape, 1)
    lane_f = lane.astype(f32)
    big = float(LANES)
    gl = jnp.where(lane < N_GROUPS, logits, -jnp.inf)
    g_max = jnp.max(gl, axis=-1, keepdims=True)
    g_idx = jnp.min(jnp.where(gl == g_max, lane_f, big), axis=-1, keepdims=True)
    g_sum = jnp.sum(jnp.exp(gl - g_max), axis=-1, keepdims=True)
    g_w = 1.0 / g_sum
    lo = N_GROUPS + g_idx * EXPERTS_PER_GROUP
    el = jnp.where((lane_f >= lo) & (lane_f < lo + EXPERTS_PER_GROUP), logits, -jnp.inf)
    m1 = jnp.max(el, axis=-1, keepdims=True)
    i1 = jnp.min(jnp.where(el == m1, lane_f, big), axis=-1, keepdims=True)
    el2 = jnp.where(lane_f == i1, -jnp.inf, el)
    m2 = jnp.max(el2, axis=-1, keepdims=True)
    i2 = jnp.min(jnp.where(el2 == m2, lane_f, big), axis=-1, keepdims=True)
    e21 = jnp.exp(m2 - m1)
    den = 1.0 + e21
    w1 = g_w * (1.0 / den)
    w2 = g_w * (e21 / den)
    eid = jnp.where(lane == 0, i1 - N_GROUPS, jnp.where(lane == 1, i2 - N_GROUPS, 0.0)).astype(jnp.int32)
    wgt = jnp.where(lane == 0, w1, jnp.where(lane == 1, w2, 0.0))
    return eid, wgt


def _outproj_kernel(n_slab, n_seg, bounds, *refs):
    xs = refs[:n_seg]
    mixed, wo, g2, wr, br, xo, h2o, eid_o, wgt_o = refs[n_seg:]
    tm = mixed.shape[0]
    i = pl.program_id(0)
    x = xs[0][...]
    for k in range(1, n_seg):
        x = jnp.where(i >= bounds[k], xs[k][...], x)
    xn = x + jnp.dot(mixed[...], wo[...], preferred_element_type=f32)
    xo[...] = xn
    h2 = _rms(xn, g2[...])
    n_word = n_slab // 2
    for s in range(n_word):
        h2o[pl.ds(s, tm, stride=n_word), :] = _pack_bf16_pair(h2[:, s * LANES:(s + 1) * LANES],
                                                              h2[:, (n_word + s) * LANES:(n_word + s + 1) * LANES])
    h_hi = h2.astype(bf16)
    h_lo = (h2 - h_hi.astype(f32)).astype(bf16)
    hi = jnp.dot(h_hi, wr[...], preferred_element_type=f32)
    lo = jnp.dot(h_lo, wr[:, :LANES], preferred_element_type=f32)
    logits = hi[:, :LANES] + (hi[:, LANES:] + lo) + br[...]
    eid, wgt = _route(logits)
    eid_o[...] = eid
    wgt_o[...] = wgt


def _outproj(mixed, x_segs, wo, g2, wr, br, tm):
    m, d = mixed.shape
    n_slab = d // LANES
    counts = [a.shape[0] // tm for a in x_segs]
    bounds = [0]
    for cnt in counts:
        bounds.append(bounds[-1] + cnt)
    const = lambda shape: pl.BlockSpec(shape, lambda i: (0, 0))
    row = lambda w: pl.BlockSpec((tm, w), lambda i: (i, 0))
    sds = jax.ShapeDtypeStruct
    return pl.pallas_call(
        functools.partial(_outproj_kernel, n_slab, len(x_segs), tuple(bounds)),
        grid=(m // tm,),
        in_specs=[_seg_spec((tm, d), bounds[k], counts[k]) for k in range(len(x_segs))] +
                 [row(d), const((d, d)), const((1, d)), const((d, 2 * LANES)), const((1, LANES))],
        out_specs=[row(d), pl.BlockSpec((tm * n_slab // 2, LANES), lambda i: (i, 0)), row(LANES), row(LANES)],
        out_shape=[sds((m, d), f32), sds((m * n_slab // 2, LANES), jnp.uint32), sds((m, LANES), jnp.int32),
                   sds((m, LANES), f32)],
        compiler_params=_params(("arbitrary",)),
        name="outproj",
    )(*x_segs, mixed, wo, g2, wr, br)


TB = 256
TD = 256


def _plan(eid):
    flat_e = eid.reshape(-1)
    n_assign = flat_e.shape[0]
    onehot = (flat_e[:, None] == jnp.arange(N_EXPERTS, dtype=jnp.int32)[None, :]).astype(jnp.int32)
    csum = jnp.cumsum(onehot, axis=0)
    counts = csum[-1]
    rank = jnp.sum(onehot * csum, axis=1) - 1
    n_blk_e = (counts + TB - 1) // TB
    blk_end = jnp.cumsum(n_blk_e)
    blk_start = blk_end - n_blk_e
    dest = blk_start[flat_e] * TB + rank
    n_blocks = -(-n_assign // TB) + N_EXPERTS
    blk_ids = jnp.arange(n_blocks, dtype=jnp.int32)
    blk_expert = jnp.minimum(jnp.sum((blk_end[None, :] <= blk_ids[:, None]).astype(jnp.int32), axis=1),
                             N_EXPERTS - 1)
    last_blk = jnp.where(n_blk_e > 0, blk_end - 1, -1).astype(jnp.int32)
    return dest.astype(jnp.int32), blk_expert, blk_end[-1:].astype(jnp.int32), last_blk, n_blocks


DMA_UNROLL = 8


def _issue_rows(n, copy):
    per_trip = DMA_UNROLL // TOP_K

    def trip(t, carry):
        for r in range(per_trip):
            for k in range(TOP_K):
                copy(t * per_trip + r, k).start()
        return carry

    lax.fori_loop(0, n // DMA_UNROLL, trip, 0)


def _dispatch_kernel(n_slab, n_blocks, dest_ref, last_ref, nu_ref, h_ref, xs_ref, zero_scr, sem, zero_sem):
    n = dest_ref.shape[2]
    blk_rows = TB * n_slab

    @pl.when(pl.program_id(0) == 0)
    def _():
        zero_scr[...] = jnp.zeros_like(zero_scr)

        def zero_block(b):
            rows = pl.ds(pl.multiple_of(b * blk_rows, blk_rows), blk_rows)
            return pltpu.make_async_copy(zero_scr, xs_ref.at[rows, :], zero_sem)

        def over_blocks(act):
            for e in range(N_EXPERTS):
                pl.when(last_ref[e] >= 0)(lambda e=e: act(zero_block(last_ref[e])))
            lax.fori_loop(nu_ref[0], n_blocks, lambda b, c: (act(zero_block(b)), c)[1], 0)

        over_blocks(lambda cp: cp.start())
        over_blocks(lambda cp: cp.wait())

    def copy(tok, k):
        src = h_ref.at[pl.ds(pl.multiple_of(tok * n_slab, n_slab), n_slab), :]
        dst = xs_ref.at[pl.ds(pl.multiple_of(dest_ref[0, 0, tok * TOP_K + k] * n_slab, n_slab), n_slab), :]
        return pltpu.make_async_copy(src, dst, sem)

    _issue_rows(n, copy)
    for half in range(TOP_K):
        rows = pl.ds(0, (n // TOP_K) * n_slab)
        pltpu.make_async_copy(h_ref, xs_ref.at[rows, :], sem).wait()


def _dispatch(h2_slab, dest, last_blk, n_used, n_blocks, n_slab, td):
    m = h2_slab.shape[0] // n_slab
    n_steps = m // td
    dest3 = dest.reshape(n_steps, 1, td * TOP_K)
    smem = pl.BlockSpec(memory_space=pltpu.SMEM)
    return pl.pallas_call(
        functools.partial(_dispatch_kernel, n_slab, n_blocks),
        grid=(n_steps,),
        in_specs=[pl.BlockSpec((1, 1, td * TOP_K), lambda i: (i, 0, 0), memory_space=pltpu.SMEM),
                  smem, smem,
                  pl.BlockSpec((td * n_slab, LANES), lambda i: (i, 0))],
        out_specs=pl.BlockSpec(memory_space=pl.ANY),
        out_shape=jax.ShapeDtypeStruct((n_blocks * TB * n_slab, LANES), h2_slab.dtype),
        scratch_shapes=[pltpu.VMEM((TB * n_slab, LANES), h2_slab.dtype), pltpu.SemaphoreType.DMA(()),
                        pltpu.SemaphoreType.DMA(())],
        compiler_params=_params(("arbitrary",)),
        name="dispatch",
    )(dest3, last_blk, n_used, h2_slab)


def _expert_kernel(n_slab, be_ref, nu_ref, xs_ref, wg_ref, wu_ref, wd_ref, o_ref, wgu_scr, wd_scr):
    i = pl.program_id(0)
    de = wd_ref.shape[1]

    @pl.when(i < nu_ref[0])
    def _():
        @pl.when((i == 0) | (be_ref[i] != be_ref[jnp.maximum(i - 1, 0)]))
        def _():
            wgu_scr[:, :de] = wg_ref[0].astype(bf16)
            wgu_scr[:, de:] = wu_ref[0].astype(bf16)
            wd_scr[...] = wd_ref[0].astype(bf16)

        n_word = n_slab // 2
        halves = [_unpack_bf16_pair(xs_ref[pl.ds(s, TB, stride=n_word), :]) for s in range(n_word)]
        x = jnp.concatenate([lo for lo, _ in halves] + [hi for _, hi in halves], axis=1)
        gu = jnp.dot(x.astype(bf16), wgu_scr[...], preferred_element_type=f32)
        g = gu[:, :de]
        hmid = (g * _sigmoid(g)) * gu[:, de:]
        y = jnp.dot(hmid.astype(bf16), wd_scr[...], preferred_element_type=f32)
        for s in range(n_slab):
            o_ref[pl.ds(s, TB, stride=n_slab), :] = y[:, s * LANES:(s + 1) * LANES]

    @pl.when(i >= nu_ref[0])
    def _():
        o_ref[...] = jnp.zeros_like(o_ref)


def _experts(xs, blk_expert, n_used, w_gate, w_up, w_down, layer, n_blocks, n_slab):
    d, de = w_gate.shape[2], w_gate.shape[3]
    by_expert = lambda i, be, nu: (layer, be[jnp.minimum(i, nu[0] - 1)], 0, 0)
    grid_spec = pltpu.PrefetchScalarGridSpec(
        num_scalar_prefetch=2,
        grid=(n_blocks,),
        in_specs=[pl.BlockSpec((TB * n_slab // 2, LANES), lambda i, be, nu: (jnp.minimum(i, nu[0] - 1), 0)),
                  pl.BlockSpec((None, 1, d, de), by_expert), pl.BlockSpec((None, 1, d, de), by_expert),
                  pl.BlockSpec((None, 1, de, d), by_expert)],
        out_specs=pl.BlockSpec((TB * n_slab, LANES), lambda i, be, nu: (i, 0)),
        scratch_shapes=[pltpu.VMEM((d, 2 * de), bf16), pltpu.VMEM((de, d), bf16)],
    )
    return pl.pallas_call(
        functools.partial(_expert_kernel, n_slab),
        grid_spec=grid_spec,
        out_shape=jax.ShapeDtypeStruct((n_blocks * TB * n_slab, LANES), f32),
        compiler_params=_params(("arbitrary",)),
        name="experts",
    )(blk_expert, n_used, xs, w_gate, w_up, w_down)


def _combine_kernel(n_slab, final, bounds, dest_ref, next_ref, x_ref, wgt_ref, g_ref, ys_ref, *rest):
    outs, (gbuf, sems) = rest[:-2], rest[-2:]
    i = pl.program_id(0)
    n = dest_ref.shape[2]
    tm = x_ref.shape[0]
    slot = i % 2
    pitch = n_slab + SUBLANES

    def gather(idx_ref, s):
        def copy(tok, k):
            src = ys_ref.at[pl.ds(pl.multiple_of(idx_ref[0, 0, tok * TOP_K + k] * n_slab, n_slab), n_slab), :]
            row = k * tm * pitch + tok * pitch
            dst = gbuf.at[s, pl.ds(pl.multiple_of(row, SUBLANES), n_slab), :]
            return pltpu.make_async_copy(src, dst, sems.at[s])
        _issue_rows(n, copy)

    pl.when(i == 0)(lambda: gather(dest_ref, slot))
    pl.when(i + 1 < pl.num_programs(0))(lambda: gather(next_ref, 1 - slot))
    rows = pl.ds(0, n * n_slab)
    pltpu.make_async_copy(ys_ref.at[rows, :], gbuf.at[slot, rows, :], sems.at[slot]).wait()

    ys = []
    for k in range(TOP_K):
        y = jnp.concatenate([gbuf[slot, pl.ds(k * tm * pitch + s, tm, stride=pitch), :]
                             for s in range(n_slab)], axis=1)
        ys.append(y * wgt_ref[:, k:k + 1])
    x = x_ref[...] + (ys[0] + ys[1])
    if final:
        x = _rms(x, g_ref[...])

    def store(k):
        outs[k][...] = x

    _when_segment(i, bounds, store)


def _combine(x, ys, dest, wgt, g, n_slab, final, seg_rows):
    m, d = x.shape
    n_steps = m // TD
    dest3 = dest.reshape(n_steps, 1, TD * TOP_K)
    counts = [r // TD for r in seg_rows]
    bounds = [0]
    for cnt in counts:
        bounds.append(bounds[-1] + cnt)
    out_specs = [_seg_spec((TD, d), bounds[k], counts[k]) for k in range(len(seg_rows))]
    out_shape = [jax.ShapeDtypeStruct((r, d), f32) for r in seg_rows]
    idx_block = (1, 1, TD * TOP_K)
    return pl.pallas_call(
        functools.partial(_combine_kernel, n_slab, final, tuple(bounds)),
        grid=(n_steps,),
        in_specs=[pl.BlockSpec(idx_block, lambda i: (i, 0, 0), memory_space=pltpu.SMEM),
                  pl.BlockSpec(idx_block, lambda i: (jnp.minimum(i + 1, n_steps - 1), 0, 0),
                               memory_space=pltpu.SMEM),
                  pl.BlockSpec((TD, d), lambda i: (i, 0)),
                  pl.BlockSpec((TD, LANES), lambda i: (i, 0)),
                  pl.BlockSpec((1, d), lambda i: (0, 0)),
                  pl.BlockSpec(memory_space=pl.ANY)],
        out_specs=out_specs, out_shape=out_shape,
        scratch_shapes=[pltpu.VMEM((2, TD * TOP_K * (n_slab + SUBLANES), LANES), f32),
                        pltpu.SemaphoreType.DMA((2,))],
        compiler_params=_params(("arbitrary",)),
        name="combine",
    )(dest3, dest3, x, wgt, g, ys)


def _pack_w_in(w_in, b_in, d):
    c_conv = d // 2
    half = TN // 2
    n_qkv = 6 * W_HEADS
    f0 = n_qkv
    c0 = f0 + N_HEADS
    g0 = c0 + 2 * c_conv
    def pack(a):
        parts = [a[..., :n_qkv]]
        for c in range(c_conv // half):
            parts.append(a[..., c0 + c * half:c0 + (c + 1) * half])
            parts.append(a[..., c0 + c_conv + c * half:c0 + c_conv + (c + 1) * half])
        parts.append(a[..., g0:])
        return jnp.concatenate(parts, axis=-1)

    w_main = pack(w_in.astype(bf16))
    b_main = pack(b_in)[None, :].astype(f32)
    w_f = jnp.pad(w_in[:, f0:f0 + N_HEADS], ((0, 0), (0, LANES - N_HEADS))).astype(bf16)
    b_f = jnp.pad(b_in[f0:f0 + N_HEADS], (0, LANES - N_HEADS))[None, :].astype(f32)
    return w_main, b_main, w_f, b_f


def kernel(x_prompt, x_sample, cache_a_k, cache_a_v, cache_b_k, cache_b_v, cache_b_logf, state_conv, norm_mix_g, w_in, b_in, rel_bias, conv_w, conv_b, conv_ln_g, conv_ln_b, w_proj_a, w_proj_b, w_proj_c, w_out, norm_ffn_g, w_router_group, b_router_group, w_router_expert, b_router_expert, w_e_gate, w_e_up, w_e_down, norm_final_g):
    nb_p, t_p, d = x_prompt.shape
    nb_s, t_s, _ = x_sample.shape
    depth = w_in.shape[0]
    past = cache_b_k.shape[2]
    a_rows = cache_a_k.shape[2]
    m_p, m_s = nb_p * t_p, nb_s * t_s
    m = m_p + m_s
    c_conv = d // 2
    n_slab = d // LANES
    tm = _row_tile(np.gcd(m_p, m_s), 512)
    tm_mix = _row_tile(np.gcd(m_p, m_s), 256)
    assert m_p % TD == 0 and m_s % TD == 0 and t_s % 16 == 0 and m_p % t_s == 0

    a_keep = min(WINDOW_A, t_p)
    x_segs = [x_prompt.reshape(m_p, d), x_sample.reshape(m_s, d)]
    kv_states = [jnp.zeros((depth, rows, W_HEADS), f32)
                 for rows in (nb_p * a_keep, nb_p * a_keep, m_p, m_p, m_s, m_s, m_s, m_s)]
    p_states, s_states = [], []
    for l in range(depth):
        w_main, b_main, w_f, b_f = _pack_w_in(w_in[l], b_in[l], d)
        (qa, ka16, va16, qb, kb16, vb16, u, gates, logf), kv_states = _inproj(
            x_segs, norm_mix_g[l][None, :], w_main, b_main, w_f, b_f, kv_states, l, m_p, t_p, a_keep, tm)

        ya_p = _band_attention_prompt(qa, ka16, va16, rel_bias[l], nb_p, t_p, 4 * CHUNK)
        kk = jnp.concatenate([cache_a_k[l].reshape(nb_s, a_rows, W_HEADS).astype(bf16),
                              ka16[m_p:].reshape(nb_s, t_s, W_HEADS)], axis=1).reshape(-1, W_HEADS)
        vv = jnp.concatenate([cache_a_v[l].reshape(nb_s, a_rows, W_HEADS).astype(bf16),
                              va16[m_p:].reshape(nb_s, t_s, W_HEADS)], axis=1).reshape(-1, W_HEADS)
        ya_s = _band_attention_sample(qa, kk, vv, rel_bias[l], nb_s, t_s, a_rows, m_p // t_s)

        logf_p = logf[:m_p].reshape(nb_p, t_p, N_HEADS)
        logf_s = logf[m_p:].reshape(nb_s, t_s, N_HEADS)
        cum_p = _cumsum_time(logf_p.transpose(0, 2, 1))
        f_p = cum_p.transpose(0, 2, 1).reshape(m_p, N_HEADS)
        yb_p = _fox_attention(_fox_expand("q", qb, f_p, m_p, tm), _fox_expand("k", kb16, f_p, m_p, tm),
                              _fox_expand("v", vb16, None, m_p, tm),
                              nb_p, t_p, t_p, _row_tile(t_p, 2048), _row_tile(t_p, 512))
        cum_s = _cumsum_time(jnp.concatenate([cache_b_logf[l].astype(f32), logf_s], axis=1).transpose(0, 2, 1))
        t_ks = past + t_s
        kk = jnp.concatenate([cache_b_k[l].reshape(nb_s, past, W_HEADS).astype(bf16),
                              kb16[m_p:].reshape(nb_s, t_s, W_HEADS)], axis=1).reshape(-1, W_HEADS)
        vv = jnp.concatenate([cache_b_v[l].reshape(nb_s, past, W_HEADS).astype(bf16),
                              vb16[m_p:].reshape(nb_s, t_s, W_HEADS)], axis=1).reshape(-1, W_HEADS)
        f_ks = cum_s.transpose(0, 2, 1)
        yb_s = _fox_attention(
            _fox_expand("q", qb, f_ks[:, past:].reshape(m_s, N_HEADS), m_s, t_s, blk0=m_p // t_s),
            _fox_expand("k", kk, f_ks.reshape(nb_s * t_ks, N_HEADS), nb_s * t_ks, t_ks),
            _fox_expand("v", vv, None, nb_s * t_ks, t_ks),
            nb_s, t_s, t_ks, t_s, t_ks)

        conv_args = (conv_w[l], conv_b[l][None, :], conv_ln_g[l][None, :], conv_ln_b[l][None, :])
        c_p = _conv_module(u, jnp.zeros((nb_p, CONV_HALO, c_conv), f32), *conv_args,
                           nb_p, t_p, _row_tile(t_p, 256), 0)
        init_s = jnp.pad(state_conv[l], ((0, 0), (CONV_HALO - (CONV_W - 1), 0), (0, 0)))
        c_s = _conv_module(u, init_s, *conv_args, nb_s, t_s, t_s, m_p // t_s)

        mixed = _mix(ya_p, ya_s, yb_p, yb_s, c_p, c_s, gates, w_proj_a[l].astype(bf16),
                     w_proj_b[l].astype(bf16), w_proj_c[l].astype(bf16), tm_mix)
        wr = jnp.pad(jnp.concatenate([w_router_group[l], w_router_expert[l]], axis=1),
                     ((0, 0), (0, LANES - N_GROUPS - N_EXPERTS)))
        wr_hi = wr.astype(bf16)
        wr_parts = jnp.concatenate([wr_hi, (wr - wr_hi.astype(f32)).astype(bf16)], axis=1)
        br = jnp.pad(jnp.concatenate([b_router_group[l], b_router_expert[l]]),
                     (0, LANES - N_GROUPS - N_EXPERTS))[None, :]
        x_mid, h2_slab, eid, wgt = _outproj(mixed, x_segs, w_out[l].astype(bf16), norm_ffn_g[l][None, :],
                                            wr_parts, br, tm_mix)

        dest, blk_expert, n_used, last_blk, n_blocks = _plan(eid[:, :TOP_K])
        xs = _dispatch(h2_slab, dest, last_blk, n_used, n_blocks, n_slab // 2, tm)
        ys = _experts(xs, blk_expert, n_used, w_e_gate, w_e_up, w_e_down, l, n_blocks, n_slab)
        final = l == depth - 1
        x_segs = _combine(x_mid, ys, dest, wgt, norm_final_g[None, :], n_slab, final,
                          (m_p, m_s) if final else (m,))

        n_cs = c_conv // LANES
        u_p = jnp.stack([u[((b + 1) * t_p - (CONV_W - 1)) * n_cs:(b + 1) * t_p * n_cs] for b in range(nb_p)])
        u_p = u_p.reshape(nb_p, CONV_W - 1, c_conv)
        u_s = u[m_p * n_cs:].reshape(nb_s, t_s, c_conv)
        p_states.append((logf_p, u_p))
        s_states.append((logf_s, jnp.concatenate([state_conv[l], u_s], axis=1)[:, -(CONV_W - 1):]))

    y_prompt = x_segs[0].reshape(nb_p, t_p, d)
    y_sample = x_segs[1].reshape(nb_s, t_s, d)
    stack = lambda states, k: jnp.stack([st[k] for st in states], axis=0)
    heads = lambda a, nb, t: a.reshape(depth, nb, t, N_HEADS, HEAD_DIM)
    ka_p, va_p, kb_p, vb_p, ka_s, va_s, kb_s, vb_s = kv_states
    return (y_prompt, y_sample,
            heads(ka_p, nb_p, a_keep), heads(va_p, nb_p, a_keep), heads(kb_p, nb_p, t_p), heads(vb_p, nb_p, t_p),
            stack(p_states, 0), stack(p_states, 1),
            heads(ka_s, nb_s, t_s), heads(va_s, nb_s, t_s), heads(kb_s, nb_s, t_s), heads(vb_s, nb_s, t_s),
            stack(s_states, 0), stack(s_states, 1))
```

```python
import functools

import jax
import jax.numpy as jnp
import numpy as np
from jax import lax
from jax.experimental import pallas as pl
from jax.experimental.pallas import tpu as pltpu

f32 = jnp.float32
bf16 = jnp.bfloat16

HEAD_DIM = 64
N_HEADS = 8
W_HEADS = N_HEADS * HEAD_DIM
CHUNK = 64
WINDOW_A = 8 * CHUNK
REL_CLIP = 128
CONV_W = 31
CONV_HALO = 32
N_GROUPS = 4
EXPERTS_PER_GROUP = 8
N_EXPERTS = N_GROUPS * EXPERTS_PER_GROUP
TOP_K = 2
SCALE = HEAD_DIM ** -0.5
EPS = 1e-6
NEG_INF = -1e30
LANES = 128
SUBLANES = 8
MIB = 1024 * 1024


def _params(sem, vmem_mib=48):
    return pltpu.CompilerParams(dimension_semantics=sem, vmem_limit_bytes=vmem_mib * MIB)


def _row_tile(m, cap):
    t = cap
    while m % t:
        t //= 2
    return t


def _sigmoid(z):
    return 0.5 * jnp.tanh(0.5 * z) + 0.5


def _rms(x, g):
    return x * lax.rsqrt(jnp.mean(x * x, axis=-1, keepdims=True) + EPS) * g


def _pack_bf16_pair(lo, hi):
    def rounded(x):
        bits = lax.bitcast_convert_type(x, jnp.uint32)
        return bits + jnp.uint32(0x7FFF) + ((bits >> 16) & jnp.uint32(1))
    return (rounded(hi) & jnp.uint32(0xFFFF0000)) | (rounded(lo) >> 16)


def _unpack_bf16_pair(word):
    lo = lax.bitcast_convert_type(word << 16, f32)
    hi = lax.bitcast_convert_type(word & jnp.uint32(0xFFFF0000), f32)
    return lo, hi


def _when_segment(i, bounds, fn):
    for k in range(len(bounds) - 1):
        pl.when((i >= bounds[k]) & (i < bounds[k + 1]))(functools.partial(fn, k))


def _seg_spec(block, start, count, width_axes=1):
    zeros = (0,) * width_axes
    return pl.BlockSpec(block, lambda i, *_: (jnp.clip(i - start, 0, count - 1),) + zeros)


TN = 1024
SEG_PER_TILE = TN // W_HEADS


def _norm_kernel(n_seg, bounds, *refs):
    xs = refs[:n_seg]
    g_ref, wf_ref, bf_ref, h_ref, logf = refs[n_seg:]

    def norm(k):
        h_ref[...] = _rms(xs[k][...], g_ref[...]).astype(bf16)

    _when_segment(pl.program_id(0), bounds, norm)
    zf = jnp.dot(h_ref[...], wf_ref[...], preferred_element_type=f32) + bf_ref[...]
    lf = jnp.minimum(zf, 0.0) - jnp.log1p(jnp.exp(-jnp.abs(zf)))
    logf[...] = lf[:, :N_HEADS]


def _norm(x_segs, g, w_f, b_f, tm):
    d = x_segs[0].shape[1]
    m = sum(a.shape[0] for a in x_segs)
    counts = [a.shape[0] // tm for a in x_segs]
    bounds = [0]
    for cnt in counts:
        bounds.append(bounds[-1] + cnt)
    const = lambda shape: pl.BlockSpec(shape, lambda i: (0, 0))
    return pl.pallas_call(
        functools.partial(_norm_kernel, len(x_segs), tuple(bounds)),
        grid=(m // tm,),
        in_specs=[_seg_spec((tm, d), bounds[k], counts[k]) for k in range(len(x_segs))] +
                 [const((1, d)), const((d, LANES)), const((1, LANES))],
        out_specs=[pl.BlockSpec((tm, d), lambda i: (i, 0)), pl.BlockSpec((tm, N_HEADS), lambda i: (i, 0))],
        out_shape=[jax.ShapeDtypeStruct((m, d), bf16), jax.ShapeDtypeStruct((m, N_HEADS), f32)],
        compiler_params=_params(("arbitrary",)),
        name="norm",
    )(*x_segs, g, w_f, b_f)


def _qkv_kernel(with_state, n_prompt, h_ref, w_ref, b_ref, *refs):
    outs = refs[2 * sum(with_state):]
    is_prompt = pl.program_id(0) < n_prompt
    for prompt_rows in (True, False):
        @pl.when(is_prompt if prompt_rows else ~is_prompt)
        def _(prompt_rows=prompt_rows):
            z = jnp.dot(h_ref[...], w_ref[...], preferred_element_type=f32) + b_ref[...]
            k = 0
            for n, has_state in enumerate(with_state):
                zn = z[:, n * W_HEADS:(n + 1) * W_HEADS]
                outs[k][...] = zn.astype(bf16)
                k += 1
                if has_state:
                    outs[k if prompt_rows else k + 1][...] = zn
                    k += 2


def _qkv_tile(h, w_qkv, b_qkv, tile, seg_states, layer, m_p, t_p, a_keep, tm):
    m, d = h.shape
    n_p, n_s = m_p // tm, (m - m_p) // tm
    per_seq, keep = t_p // tm, a_keep // tm

    def tail_rows(i):
        ip = jnp.minimum(i, n_p - 1)
        return (layer, (ip // per_seq) * keep + jnp.maximum(ip % per_seq - (per_seq - keep), 0), 0)

    state_block = (None, tm, W_HEADS)
    tail_spec = pl.BlockSpec(state_block, tail_rows)
    prompt_spec = pl.BlockSpec(state_block, lambda i: (layer, jnp.minimum(i, n_p - 1), 0))
    sample_spec = pl.BlockSpec(state_block, lambda i: (layer, jnp.clip(i - n_p, 0, n_s - 1), 0))
    states, out_specs, out_shape, state_out_pos = [], [], [], []
    for seg in seg_states:
        out_specs.append(pl.BlockSpec((tm, W_HEADS), lambda i: (i, 0)))
        out_shape.append(jax.ShapeDtypeStruct((m, W_HEADS), bf16))
        if seg is not None:
            buf_p, buf_s, keep_tail = seg
            for buf, spec in ((buf_p, tail_spec if keep_tail else prompt_spec), (buf_s, sample_spec)):
                state_out_pos.append(len(out_specs))
                states.append(buf)
                out_specs.append(spec)
                out_shape.append(jax.ShapeDtypeStruct(buf.shape, buf.dtype))
    outs = pl.pallas_call(
        functools.partial(_qkv_kernel, tuple(seg is not None for seg in seg_states), n_p),
        grid=(m // tm,),
        in_specs=[pl.BlockSpec((tm, d), lambda i: (i, 0)),
                  pl.BlockSpec((d, TN), lambda i: (0, tile)),
                  pl.BlockSpec((1, TN), lambda i: (0, tile))] + [pl.BlockSpec(memory_space=pl.ANY)] * len(states),
        out_specs=out_specs, out_shape=out_shape,
        input_output_aliases={3 + k: pos for k, pos in enumerate(state_out_pos)},
        compiler_params=_params(("arbitrary",)),
        name="qkv",
    )(h, w_qkv, b_qkv, *states)
    copies = [o for k, o in enumerate(outs) if k not in state_out_pos]
    return copies, [outs[pos] for pos in state_out_pos]


def _glu_kernel(n_slab, h_ref, w_ref, b_ref, u_ref):
    tm = h_ref.shape[0]
    c = n_slab * LANES
    z = jnp.dot(h_ref[...], w_ref[...], preferred_element_type=f32) + b_ref[...]
    glu = z[:, :c] * _sigmoid(z[:, c:])
    for s in range(n_slab):
        u_ref[pl.ds(s, tm, stride=n_slab), :] = glu[:, s * LANES:(s + 1) * LANES]


def _glu(h, w, b, tm):
    m, d = h.shape
    c = w.shape[1] // 2
    n_slab = c // LANES
    return pl.pallas_call(
        functools.partial(_glu_kernel, n_slab),
        grid=(m // tm,),
        in_specs=[pl.BlockSpec((tm, d), lambda i: (i, 0)), pl.BlockSpec((d, 2 * c), lambda i: (0, 0)),
                  pl.BlockSpec((1, 2 * c), lambda i: (0, 0))],
        out_specs=pl.BlockSpec((tm * n_slab, LANES), lambda i: (i, 0)),
        out_shape=jax.ShapeDtypeStruct((m * n_slab, LANES), f32),
        compiler_params=_params(("arbitrary",)),
        name="glu",
    )(h, w, b)


def _gates_kernel(h_ref, w_ref, b_ref, o_ref):
    z = jnp.dot(h_ref[...], w_ref[...], preferred_element_type=f32) + b_ref[...]
    o_ref[...] = _sigmoid(z).astype(o_ref.dtype)


def _gates(h, w, b, tm):
    m, d = h.shape
    n = w.shape[1]
    return pl.pallas_call(
        _gates_kernel,
        grid=(n // TN, m // tm),
        in_specs=[pl.BlockSpec((tm, d), lambda j, i: (i, 0)), pl.BlockSpec((d, TN), lambda j, i: (0, j)),
                  pl.BlockSpec((1, TN), lambda j, i: (0, j))],
        out_specs=pl.BlockSpec((tm, TN), lambda j, i: (i, j)),
        out_shape=jax.ShapeDtypeStruct((m, n), bf16),
        compiler_params=_params(("arbitrary", "arbitrary")),
        name="gates",
    )(h, w, b)


def _inproj(x_segs, g, w_in, b_in, states, layer, m_p, t_p, a_keep, tm):
    d = x_segs[0].shape[1]
    c_conv = d // 2
    n_qkv = 6 * W_HEADS
    f0 = n_qkv
    c0 = f0 + N_HEADS
    g0 = c0 + 2 * c_conv
    cast = lambda a: a.astype(bf16)
    row = lambda a: a[None, :].astype(f32)
    w_f = jnp.pad(w_in[:, f0:c0], ((0, 0), (0, LANES - N_HEADS))).astype(bf16)
    b_f = jnp.pad(b_in[f0:c0], (0, LANES - N_HEADS))[None, :].astype(f32)
    h, logf = _norm(x_segs, g, w_f, b_f, tm)

    ka_p, va_p, kb_p, vb_p, ka_s, va_s, kb_s, vb_s = states
    w_qkv, b_qkv = cast(w_in[:, :n_qkv]), row(b_in[:n_qkv])
    tile = functools.partial(_qkv_tile, h, w_qkv, b_qkv, layer=layer, m_p=m_p, t_p=t_p, a_keep=a_keep, tm=tm)
    assert SEG_PER_TILE == 2
    (qa, ka16), (ka_p, ka_s) = tile(0, [None, (ka_p, ka_s, True)])
    (va16, qb), (va_p, va_s) = tile(1, [(va_p, va_s, True), None])
    (kb16, vb16), (kb_p, kb_s, vb_p, vb_s) = tile(2, [(kb_p, kb_s, False), (vb_p, vb_s, False)])
    u = _glu(h, cast(w_in[:, c0:g0]), row(b_in[c0:g0]), tm)
    gates = _gates(h, cast(w_in[:, g0:]), row(b_in[g0:]), tm)
    return (qa, ka16, va16, qb, kb16, vb16, u, gates, logf), [ka_p, va_p, kb_p, vb_p, ka_s, va_s, kb_s, vb_s]


TC = 512


def _cumsum_kernel(x_ref, o_ref, carry):
    @pl.when(pl.program_id(1) == 0)
    def _():
        carry[...] = jnp.zeros_like(carry)

    blk = x_ref[0]
    r = lax.broadcasted_iota(jnp.int32, (TC, TC), 0)
    c = lax.broadcasted_iota(jnp.int32, (TC, TC), 1)
    tri = jnp.where(r <= c, 1.0, 0.0).astype(bf16)
    cs = carry[:, 0:1]
    rest = blk
    for _ in range(F_PARTS):
        part = rest.astype(bf16)
        rest = rest - part.astype(f32)
        cs = cs + jnp.dot(part, tri, preferred_element_type=f32)
    o_ref[0] = cs
    carry[...] = jnp.broadcast_to(cs[:, TC - 1:TC], carry.shape)


def _cumsum_time(x):
    nb, h, t = x.shape
    tp = -(-t // TC) * TC
    xp = jnp.pad(x, ((0, 0), (0, 0), (0, tp - t)))
    out = pl.pallas_call(
        _cumsum_kernel,
        grid=(nb, tp // TC),
        in_specs=[pl.BlockSpec((1, h, TC), lambda b, k: (b, 0, k))],
        out_specs=pl.BlockSpec((1, h, TC), lambda b, k: (b, 0, k)),
        out_shape=jax.ShapeDtypeStruct((nb, h, tp), f32),
        scratch_shapes=[pltpu.VMEM((h, LANES), f32)],
        compiler_params=_params(("arbitrary", "arbitrary")),
        name="cumsum",
    )(xp)
    return out[:, :, :t]


F_PARTS = 3


def _spare_base(h):
    return HEAD_DIM * (1 - h % 2)


def _fox_expand_kernel(kind, x_ref, f_ref, place_ref, o_ref):
    tm = x_ref.shape[0]
    lane = lax.broadcasted_iota(jnp.int32, (tm, LANES), 1)
    if kind != "v":
        rest = f_ref[...]
        stack = jnp.where(lane < (F_PARTS + 1) * N_HEADS, 1.0, 0.0)
        for n in range(F_PARTS):
            part = rest.astype(bf16).astype(f32)
            rest = rest - part
            stack = jnp.where((lane >= n * N_HEADS) & (lane < (n + 1) * N_HEADS), part, stack)
        spare_all = jnp.dot(stack.astype(bf16), place_ref[...], preferred_element_type=f32)
    for h in range(N_HEADS):
        pair = x_ref[:, (h // 2) * LANES:(h // 2 + 1) * LANES].astype(f32)
        if kind == "q":
            pair = pair * SCALE
        if kind == "v":
            spare = jnp.where(lane == _spare_base(h), 1.0, 0.0)
        else:
            spare = spare_all[:, h * LANES:(h + 1) * LANES]
        own = (lane < HEAD_DIM) if h % 2 == 0 else (lane >= HEAD_DIM)
        o_ref[h] = jnp.where(own, pair, spare).astype(o_ref.dtype)


def _fox_placement(kind):
    place = np.zeros((LANES, N_HEADS * LANES), np.float32)
    for h in range(N_HEADS):
        base = h * LANES + _spare_base(h)
        for n in range(F_PARTS):
            if kind == "q":
                place[n * N_HEADS + h, base + n] = 1.0
                place[F_PARTS * N_HEADS + h, base + F_PARTS + n] = 1.0
            else:
                place[F_PARTS * N_HEADS + h, base + n] = 1.0
                place[n * N_HEADS + h, base + F_PARTS + n] = -1.0
    return jnp.asarray(place, bf16)


def _fox_expand(kind, x, f, rows, tm, blk0=0):
    f_lanes = jnp.zeros((rows, LANES), f32) if f is None else jnp.tile(f, (1, LANES // N_HEADS))
    return pl.pallas_call(
        functools.partial(_fox_expand_kernel, kind),
        grid=(rows // tm,),
        in_specs=[pl.BlockSpec((tm, W_HEADS), lambda i: (blk0 + i, 0)),
                  pl.BlockSpec((tm, LANES), lambda i: (i, 0)),
                  pl.BlockSpec((LANES, N_HEADS * LANES), lambda i: (0, 0))],
        out_specs=pl.BlockSpec((N_HEADS, tm, LANES), lambda i: (0, i, 0)),
        out_shape=jax.ShapeDtypeStruct((N_HEADS, rows, LANES), bf16),
        compiler_params=_params(("arbitrary",)),
        name="fox_expand_" + kind,
    )(x, f_lanes, _fox_placement(kind))


FOX_HEADS_PER_TRIP = 2


def _fox_kernel(tq, tk, off, q_ref, k_ref, v_ref, o_ref, m_scr, acc_scr):
    i = pl.program_id(1)
    j = pl.program_id(2)

    @pl.when(j == 0)
    def _():
        m_scr[...] = jnp.full_like(m_scr, NEG_INF)
        acc_scr[...] = jnp.zeros_like(acc_scr)

    q_first = i * tq + off
    q_last = q_first + tq - 1
    k_first = j * tk
    k_last = k_first + tk - 1

    def body(row0, masked):
        rows = slice(row0, tq)
        if masked:
            kpos = k_first + lax.broadcasted_iota(jnp.int32, (tq - row0, tk), 1)
            qpos = q_first + row0 + lax.broadcasted_iota(jnp.int32, (tq - row0, tk), 0)
            vis = kpos <= qpos

        def head(h):
            s = lax.dot_general(q_ref[h, rows], k_ref[h], (((1,), (1,)), ((), ())), preferred_element_type=f32)
            if masked:
                s = jnp.where(vis, s, NEG_INF)
            m_old = m_scr[h, rows]
            m_new = jnp.maximum(m_old, jnp.max(s, axis=-1, keepdims=True))
            pr = jnp.exp(s - m_new[:, 0:1])
            pv = jnp.dot(pr.astype(bf16), v_ref[h], preferred_element_type=f32)
            acc_scr[h, rows] = jnp.exp(m_old - m_new) * acc_scr[h, rows] + pv
            m_scr[h, rows] = m_new

        def trip(g, carry):
            for n in range(FOX_HEADS_PER_TRIP):
                head(g * FOX_HEADS_PER_TRIP + n)
            return carry

        lax.fori_loop(0, N_HEADS // FOX_HEADS_PER_TRIP, trip, 0)

    pl.when(k_last <= q_first)(functools.partial(body, 0, False))
    if tq % tk == 0 and off % tk == 0:
        for c in range(tq // tk):
            pl.when(k_first == q_first + c * tk)(functools.partial(body, c * tk, True))
    else:
        pl.when((k_first <= q_last) & (k_last > q_first))(functools.partial(body, 0, True))

    @pl.when(j == pl.num_programs(2) - 1)
    def _():
        lane = lax.broadcasted_iota(jnp.int32, (tq, LANES), 1)
        for p in range(N_HEADS // 2):
            even = acc_scr[2 * p]
            odd = acc_scr[2 * p + 1]
            even = even / even[:, _spare_base(0):_spare_base(0) + 1]
            odd = odd / odd[:, _spare_base(1):_spare_base(1) + 1]
            o_ref[:, p * LANES:(p + 1) * LANES] = jnp.where(lane < HEAD_DIM, even, odd).astype(o_ref.dtype)


def _fox_attention(q, k, v, nb, t_q, t_k, tq, tk):
    nq, nk = t_q // tq, t_k // tk
    off = t_k - t_q

    def last_k(i):
        return jnp.minimum((i * tq + tq - 1 + off) // tk, nk - 1)

    kv_spec = pl.BlockSpec((N_HEADS, tk, LANES), lambda b, i, j: (0, b * nk + jnp.minimum(j, last_k(i)), 0))
    return pl.pallas_call(
        functools.partial(_fox_kernel, tq, tk, off),
        grid=(nb, nq, nk),
        in_specs=[pl.BlockSpec((N_HEADS, tq, LANES), lambda b, i, j: (0, b * nq + i, 0)), kv_spec, kv_spec],
        out_specs=pl.BlockSpec((tq, W_HEADS), lambda b, i, j: (b * nq + i, 0)),
        out_shape=jax.ShapeDtypeStruct((nb * t_q, W_HEADS), bf16),
        scratch_shapes=[pltpu.VMEM((N_HEADS, tq, LANES), f32), pltpu.VMEM((N_HEADS, tq, LANES), f32)],
        compiler_params=_params(("arbitrary", "arbitrary", "arbitrary")),
        name="fox",
    )(q, k, v)


def _band_kernel(rows, gq, wk, has_prev, *refs):
    if has_prev:
        q_ref, kp_ref, kc_ref, vp_ref, vc_ref, bias_ref, o_ref, k_scr, v_scr = refs
        k_scr[0:WINDOW_A] = kp_ref[...]
        k_scr[WINDOW_A:WINDOW_A + rows] = kc_ref[...]
        v_scr[0:WINDOW_A] = vp_ref[...]
        v_scr[WINDOW_A:WINDOW_A + rows] = vc_ref[...]
        k_src, v_src = k_scr, v_scr
    else:
        q_ref, k_src, v_src, bias_ref, o_ref = refs
    i = pl.program_id(1)
    lane = lax.broadcasted_iota(jnp.int32, (gq, LANES), 1)
    low = lane < HEAD_DIM

    def attend(before_start):
        for g in range(rows // gq):
            r0 = g * gq
            if before_start:
                key_pos = (i - 1) * WINDOW_A + r0 + lax.broadcasted_iota(jnp.int32, (gq, wk), 1)
                vis = key_pos >= 0
            for p in range(N_HEADS // 2):
                cols = slice(p * LANES, (p + 1) * LANES)
                q2 = q_ref[r0:r0 + gq, cols] * SCALE
                kw = k_src[r0:r0 + wk, cols]
                vw = v_src[r0:r0 + wk, cols]
                outs = []
                for half in range(2):
                    h = 2 * p + half
                    qm = jnp.where(low if half == 0 else ~low, q2, jnp.zeros_like(q2))
                    s = lax.dot_general(qm, kw, (((1,), (1,)), ((), ())), preferred_element_type=f32)
                    s = s + bias_ref[h]
                    if before_start:
                        s = jnp.where(vis, s, NEG_INF)
                    m = jnp.max(s, axis=-1, keepdims=True)
                    pr = jnp.exp(s - m)
                    l = jnp.sum(pr, axis=-1, keepdims=True)
                    pv = jnp.dot(pr.astype(bf16), vw, preferred_element_type=f32)
                    outs.append(pv / l)
                o_ref[r0:r0 + gq, cols] = jnp.where(low, outs[0], outs[1]).astype(o_ref.dtype)

    if has_prev:
        pl.when(i == 0)(functools.partial(attend, True))
        pl.when(i > 0)(functools.partial(attend, False))
    else:
        attend(False)


def _rel_bias_table(rel_bias, gq, wk, q_shift):
    period = wk + gq
    j = np.arange(period)
    k = np.where(j < wk, j, j - period)
    line = rel_bias.astype(f32)[:, np.clip(q_shift - k, -REL_CLIP, REL_CLIP) + REL_CLIP]
    tiled = jnp.tile(line, (1, gq))[:, :gq * (period - 1)]
    return tiled.reshape(-1, gq, period - 1)[:, :, :wk]


def _band_bias(rel_bias, gq, wk, q_shift):
    r = np.arange(gq)[:, None]
    s = np.arange(wk)[None, :]
    band0 = (r // CHUNK) * CHUNK + q_shift - WINDOW_A
    ok = (s >= band0) & (s < band0 + WINDOW_A + CHUNK)
    return jnp.where(ok[None], _rel_bias_table(rel_bias, gq, wk, q_shift), NEG_INF)


def _band_attention_prompt(q, k, v, rel_bias, nb, t, gq):
    rows = WINDOW_A
    wk = WINDOW_A + gq
    n_steps = t // rows
    bias = _band_bias(rel_bias, gq, wk, WINDOW_A)
    cur = pl.BlockSpec((rows, W_HEADS), lambda b, i: (b * n_steps + i, 0))
    prev = pl.BlockSpec((rows, W_HEADS), lambda b, i: (b * n_steps + jnp.maximum(i - 1, 0), 0))
    return pl.pallas_call(
        functools.partial(_band_kernel, rows, gq, wk, True),
        grid=(nb, n_steps),
        in_specs=[cur, prev, cur, prev, cur,
                  pl.BlockSpec((N_HEADS, gq, wk), lambda b, i: (0, 0, 0))],
        out_specs=cur,
        out_shape=jax.ShapeDtypeStruct((nb * t, W_HEADS), bf16),
        scratch_shapes=[pltpu.VMEM((2 * rows, W_HEADS), bf16), pltpu.VMEM((2 * rows, W_HEADS), bf16)],
        compiler_params=_params(("arbitrary", "arbitrary")),
        name="band_prompt",
    )(q, k, k, v, v, bias)


def _band_attention_sample(q, kk, vv, rel_bias, nb, s_new, l_cache, q_blk0):
    wk = l_cache + s_new
    bias = _rel_bias_table(rel_bias, s_new, wk, l_cache)
    return pl.pallas_call(
        functools.partial(_band_kernel, s_new, s_new, wk, False),
        grid=(nb, 1),
        in_specs=[pl.BlockSpec((s_new, W_HEADS), lambda b, i: (q_blk0 + b, 0)),
                  pl.BlockSpec((wk, W_HEADS), lambda b, i: (b, 0)),
                  pl.BlockSpec((wk, W_HEADS), lambda b, i: (b, 0)),
                  pl.BlockSpec((N_HEADS, s_new, wk), lambda b, i: (0, 0, 0))],
        out_specs=pl.BlockSpec((s_new, W_HEADS), lambda b, i: (b, 0)),
        out_shape=jax.ShapeDtypeStruct((nb * s_new, W_HEADS), bf16),
        compiler_params=_params(("arbitrary", "arbitrary")),
        name="band_sample",
    )(q, kk, vv, bias)


CONV_STRIP = 32


def _conv_kernel(tt, n_slab, init_ref, u_ref, w_ref, cb_ref, g_ref, b_ref, o_ref, ubuf, acc_scr):
    halo = CONV_HALO * n_slab

    @pl.when(pl.program_id(1) == 0)
    def _():
        ubuf[0:halo] = init_ref[0]

    ubuf[halo:halo + tt * n_slab] = u_ref[...]
    rs = min(CONV_STRIP, tt)
    first = CONV_HALO - (CONV_W - 1)

    def per_step(slab):
        return jnp.broadcast_to(slab[None], (rs, n_slab, LANES)).reshape(rs * n_slab, LANES)

    for s in range(tt // rs):
        acc = per_step(cb_ref[...])
        for j in range(CONV_W):
            r0 = (s * rs + first + j) * n_slab
            acc = acc + per_step(w_ref[j * n_slab:(j + 1) * n_slab, :]) * ubuf[r0:r0 + rs * n_slab, :]
        acc_scr[...] = acc
        rows = jnp.concatenate([acc_scr[pl.ds(q, rs, stride=n_slab), :] for q in range(n_slab)], axis=1)
        mu = jnp.mean(rows, axis=-1, keepdims=True)
        cen = rows - mu
        var = jnp.mean(cen * cen, axis=-1, keepdims=True)
        y = cen * lax.rsqrt(var + EPS) * g_ref[...] + b_ref[...]
        o_ref[s * rs:(s + 1) * rs, :] = (y * _sigmoid(y)).astype(o_ref.dtype)
    if tt >= CONV_HALO:
        ubuf[0:halo] = ubuf[tt * n_slab:tt * n_slab + halo]


def _conv_module(u_slab, init, conv_w, conv_b, ln_g, ln_b, nb, t, tt, blk0):
    c = conv_w.shape[1]
    n_slab = c // LANES
    n_t = t // tt
    vec = pl.BlockSpec((1, c), lambda b, i: (0, 0))
    return pl.pallas_call(
        functools.partial(_conv_kernel, tt, n_slab),
        grid=(nb, n_t),
        in_specs=[pl.BlockSpec((1, CONV_HALO * n_slab, LANES), lambda b, i: (b, 0, 0)),
                  pl.BlockSpec((tt * n_slab, LANES), lambda b, i: (blk0 + b * n_t + i, 0)),
                  pl.BlockSpec((CONV_W * n_slab, LANES), lambda b, i: (0, 0)),
                  pl.BlockSpec((n_slab, LANES), lambda b, i: (0, 0)), vec, vec],
        out_specs=pl.BlockSpec((tt, c), lambda b, i: (b * n_t + i, 0)),
        out_shape=jax.ShapeDtypeStruct((nb * t, c), bf16),
        scratch_shapes=[pltpu.VMEM(((CONV_HALO + tt) * n_slab, LANES), f32),
                        pltpu.VMEM((min(CONV_STRIP, tt) * n_slab, LANES), f32)],
        compiler_params=_params(("arbitrary", "arbitrary")),
        name="conv",
    )(init.reshape(nb, CONV_HALO * n_slab, LANES), u_slab, conv_w.reshape(CONV_W * n_slab, LANES),
      conv_b.reshape(n_slab, LANES), ln_g, ln_b)


def _mix_kernel(bounds, ya_p, ya_s, yb_p, yb_s, c_p, c_s, gates, pa, pb, pc, o_ref):
    d = o_ref.shape[1]

    def go(k):
        ya, yb, c = ((ya_p, yb_p, c_p), (ya_s, yb_s, c_s))[k]
        a = jnp.dot(ya[...], pa[...], preferred_element_type=f32)
        mixed = gates[:, 0:d].astype(f32) * a
        b = jnp.dot(yb[...], pb[...], preferred_element_type=f32)
        mixed = mixed + gates[:, d:2 * d].astype(f32) * b
        cc = jnp.dot(c[...], pc[...], preferred_element_type=f32)
        mixed = mixed + gates[:, 2 * d:3 * d].astype(f32) * cc
        o_ref[...] = mixed.astype(o_ref.dtype)

    _when_segment(pl.program_id(0), bounds, go)


def _mix(ya_p, ya_s, yb_p, yb_s, c_p, c_s, gates, pa, pb, pc, tm):
    m, d3 = gates.shape
    d = d3 // 3
    n_p, n_s = ya_p.shape[0] // tm, ya_s.shape[0] // tm
    bounds = (0, n_p, n_p + n_s)
    c_conv = c_p.shape[1]
    const = lambda shape: pl.BlockSpec(shape, lambda i: (0, 0))
    return pl.pallas_call(
        functools.partial(_mix_kernel, bounds),
        grid=(m // tm,),
        in_specs=[_seg_spec((tm, W_HEADS), 0, n_p), _seg_spec((tm, W_HEADS), n_p, n_s),
                  _seg_spec((tm, W_HEADS), 0, n_p), _seg_spec((tm, W_HEADS), n_p, n_s),
                  _seg_spec((tm, c_conv), 0, n_p), _seg_spec((tm, c_conv), n_p, n_s),
                  pl.BlockSpec((tm, d3), lambda i: (i, 0)),
                  const((W_HEADS, d)), const((W_HEADS, d)), const((c_conv, d))],
        out_specs=pl.BlockSpec((tm, d), lambda i: (i, 0)),
        out_shape=jax.ShapeDtypeStruct((m, d), bf16),
        compiler_params=_params(("arbitrary",)),
        name="mix",
    )(ya_p, ya_s, yb_p, yb_s, c_p, c_s, gates, pa, pb, pc)


def _route(logits):
    shape = logits.shape
    lane = lax.broadcasted_iota(jnp.int32, shape, 1)
    lane_f = lane.astype(f32)
    big = float(LANES)
    gl = jnp.where(lane < N_GROUPS, logits, -jnp.inf)
    g_max = jnp.max(gl, axis=-1, keepdims=True)
    g_idx = jnp.min(jnp.where(gl == g_max, lane_f, big), axis=-1, keepdims=True)
    g_sum = jnp.sum(jnp.exp(gl - g_max), axis=-1, keepdims=True)
    g_w = 1.0 / g_sum
    lo = N_GROUPS + g_idx * EXPERTS_PER_GROUP
    el = jnp.where((lane_f >= lo) & (lane_f < lo + EXPERTS_PER_GROUP), logits, -jnp.inf)
    m1 = jnp.max(el, axis=-1, keepdims=True)
    i1 = jnp.min(jnp.where(el == m1, lane_f, big), axis=-1, keepdims=True)
    el2 = jnp.where(lane_f == i1, -jnp.inf, el)
    m2 = jnp.max(el2, axis=-1, keepdims=True)
    i2 = jnp.min(jnp.where(el2 == m2, lane_f, big), axis=-1, keepdims=True)
    e21 = jnp.exp(m2 - m1)
    den = 1.0 + e21
    w1 = g_w * (1.0 / den)
    w2 = g_w * (e21 / den)
    eid = jnp.where(lane == 0, i1 - N_GROUPS, jnp.where(lane == 1, i2 - N_GROUPS, 0.0)).astype(jnp.int32)
    wgt = jnp.where(lane == 0, w1, jnp.where(lane == 1, w2, 0.0))
    return eid, wgt


def _outproj_kernel(n_slab, n_seg, bounds, *refs):
    xs = refs[:n_seg]
    mixed, wo, g2, wr, br, xo, h2o, eid_o, wgt_o = refs[n_seg:]
    tm = mixed.shape[0]
    i = pl.program_id(0)
    x = xs[0][...]
    for k in range(1, n_seg):
        x = jnp.where(i >= bounds[k], xs[k][...], x)
    xn = x + jnp.dot(mixed[...], wo[...], preferred_element_type=f32)
    xo[...] = xn
    h2 = _rms(xn, g2[...])
    n_word = n_slab // 2
    for s in range(n_word):
        h2o[pl.ds(s, tm, stride=n_word), :] = _pack_bf16_pair(h2[:, s * LANES:(s + 1) * LANES],
                                                              h2[:, (n_word + s) * LANES:(n_word + s + 1) * LANES])
    h_hi = h2.astype(bf16)
    h_lo = (h2 - h_hi.astype(f32)).astype(bf16)
    hi = jnp.dot(h_hi, wr[...], preferred_element_type=f32)
    lo = jnp.dot(h_lo, wr[:, :LANES], preferred_element_type=f32)
    logits = hi[:, :LANES] + (hi[:, LANES:] + lo) + br[...]
    eid, wgt = _route(logits)
    eid_o[...] = eid
    wgt_o[...] = wgt


def _outproj(mixed, x_segs, wo, g2, wr, br, tm):
    m, d = mixed.shape
    n_slab = d // LANES
    counts = [a.shape[0] // tm for a in x_segs]
    bounds = [0]
    for cnt in counts:
        bounds.append(bounds[-1] + cnt)
    const = lambda shape: pl.BlockSpec(shape, lambda i: (0, 0))
    row = lambda w: pl.BlockSpec((tm, w), lambda i: (i, 0))
    sds = jax.ShapeDtypeStruct
    return pl.pallas_call(
        functools.partial(_outproj_kernel, n_slab, len(x_segs), tuple(bounds)),
        grid=(m // tm,),
        in_specs=[_seg_spec((tm, d), bounds[k], counts[k]) for k in range(len(x_segs))] +
                 [row(d), const((d, d)), const((1, d)), const((d, 2 * LANES)), const((1, LANES))],
        out_specs=[row(d), pl.BlockSpec((tm * n_slab // 2, LANES), lambda i: (i, 0)), row(LANES), row(LANES)],
        out_shape=[sds((m, d), f32), sds((m * n_slab // 2, LANES), jnp.uint32), sds((m, LANES), jnp.int32),
                   sds((m, LANES), f32)],
        compiler_params=_params(("arbitrary",)),
        name="outproj",
    )(*x_segs, mixed, wo, g2, wr, br)


TB = 256
TD = 256


def _plan(eid):
    flat_e = eid.reshape(-1)
    n_assign = flat_e.shape[0]
    onehot = (flat_e[:, None] == jnp.arange(N_EXPERTS, dtype=jnp.int32)[None, :]).astype(jnp.int32)
    csum = jnp.cumsum(onehot, axis=0)
    counts = csum[-1]
    rank = jnp.sum(onehot * csum, axis=1) - 1
    n_blk_e = (counts + TB - 1) // TB
    blk_end = jnp.cumsum(n_blk_e)
    blk_start = blk_end - n_blk_e
    dest = blk_start[flat_e] * TB + rank
    n_blocks = -(-n_assign // TB) + N_EXPERTS
    blk_ids = jnp.arange(n_blocks, dtype=jnp.int32)
    blk_expert = jnp.minimum(jnp.sum((blk_end[None, :] <= blk_ids[:, None]).astype(jnp.int32), axis=1),
                             N_EXPERTS - 1)
    last_blk = jnp.where(n_blk_e > 0, blk_end - 1, -1).astype(jnp.int32)
    return dest.astype(jnp.int32), blk_expert, blk_end[-1:].astype(jnp.int32), last_blk, n_blocks


DMA_UNROLL = 8


def _issue_rows(n, copy):
    per_trip = DMA_UNROLL // TOP_K

    def trip(t, carry):
        for r in range(per_trip):
            for k in range(TOP_K):
                copy(t * per_trip + r, k).start()
        return carry

    lax.fori_loop(0, n // DMA_UNROLL, trip, 0)


def _dispatch_kernel(n_slab, n_blocks, dest_ref, last_ref, nu_ref, h_ref, xs_ref, zero_scr, sem, zero_sem):
    n = dest_ref.shape[2]
    blk_rows = TB * n_slab

    @pl.when(pl.program_id(0) == 0)
    def _():
        zero_scr[...] = jnp.zeros_like(zero_scr)

        def zero_block(b):
            rows = pl.ds(pl.multiple_of(b * blk_rows, blk_rows), blk_rows)
            return pltpu.make_async_copy(zero_scr, xs_ref.at[rows, :], zero_sem)

        def over_blocks(act):
            for e in range(N_EXPERTS):
                pl.when(last_ref[e] >= 0)(lambda e=e: act(zero_block(last_ref[e])))
            lax.fori_loop(nu_ref[0], n_blocks, lambda b, c: (act(zero_block(b)), c)[1], 0)

        over_blocks(lambda cp: cp.start())
        over_blocks(lambda cp: cp.wait())

    def copy(tok, k):
        src = h_ref.at[pl.ds(pl.multiple_of(tok * n_slab, n_slab), n_slab), :]
        dst = xs_ref.at[pl.ds(pl.multiple_of(dest_ref[0, 0, tok * TOP_K + k] * n_slab, n_slab), n_slab), :]
        return pltpu.make_async_copy(src, dst, sem)

    _issue_rows(n, copy)
    for half in range(TOP_K):
        rows = pl.ds(0, (n // TOP_K) * n_slab)
        pltpu.make_async_copy(h_ref, xs_ref.at[rows, :], sem).wait()


def _dispatch(h2_slab, dest, last_blk, n_used, n_blocks, n_slab, td):
    m = h2_slab.shape[0] // n_slab
    n_steps = m // td
    dest3 = dest.reshape(n_steps, 1, td * TOP_K)
    smem = pl.BlockSpec(memory_space=pltpu.SMEM)
    return pl.pallas_call(
        functools.partial(_dispatch_kernel, n_slab, n_blocks),
        grid=(n_steps,),
        in_specs=[pl.BlockSpec((1, 1, td * TOP_K), lambda i: (i, 0, 0), memory_space=pltpu.SMEM),
                  smem, smem,
                  pl.BlockSpec((td * n_slab, LANES), lambda i: (i, 0))],
        out_specs=pl.BlockSpec(memory_space=pl.ANY),
        out_shape=jax.ShapeDtypeStruct((n_blocks * TB * n_slab, LANES), h2_slab.dtype),
        scratch_shapes=[pltpu.VMEM((TB * n_slab, LANES), h2_slab.dtype), pltpu.SemaphoreType.DMA(()),
                        pltpu.SemaphoreType.DMA(())],
        compiler_params=_params(("arbitrary",)),
        name="dispatch",
    )(dest3, last_blk, n_used, h2_slab)


def _expert_kernel(n_slab, be_ref, nu_ref, xs_ref, wg_ref, wu_ref, wd_ref, o_ref, wgu_scr, wd_scr):
    i = pl.program_id(0)
    de = wd_ref.shape[1]

    @pl.when(i < nu_ref[0])
    def _():
        @pl.when((i == 0) | (be_ref[i] != be_ref[jnp.maximum(i - 1, 0)]))
        def _():
            wgu_scr[:, :de] = wg_ref[0].astype(bf16)
            wgu_scr[:, de:] = wu_ref[0].astype(bf16)
            wd_scr[...] = wd_ref[0].astype(bf16)

        n_word = n_slab // 2
        halves = [_unpack_bf16_pair(xs_ref[pl.ds(s, TB, stride=n_word), :]) for s in range(n_word)]
        x = jnp.concatenate([lo for lo, _ in halves] + [hi for _, hi in halves], axis=1)
        gu = jnp.dot(x.astype(bf16), wgu_scr[...], preferred_element_type=f32)
        g = gu[:, :de]
        hmid = (g * _sigmoid(g)) * gu[:, de:]
        y = jnp.dot(hmid.astype(bf16), wd_scr[...], preferred_element_type=f32)
        for s in range(n_slab):
            o_ref[pl.ds(s, TB, stride=n_slab), :] = y[:, s * LANES:(s + 1) * LANES]

    @pl.when(i >= nu_ref[0])
    def _():
        o_ref[...] = jnp.zeros_like(o_ref)


def _experts(xs, blk_expert, n_used, w_gate, w_up, w_down, layer, n_blocks, n_slab):
    d, de = w_gate.shape[2], w_gate.shape[3]
    by_expert = lambda i, be, nu: (layer, be[jnp.minimum(i, nu[0] - 1)], 0, 0)
    grid_spec = pltpu.PrefetchScalarGridSpec(
        num_scalar_prefetch=2,
        grid=(n_blocks,),
        in_specs=[pl.BlockSpec((TB * n_slab // 2, LANES), lambda i, be, nu: (jnp.minimum(i, nu[0] - 1), 0)),
                  pl.BlockSpec((None, 1, d, de), by_expert), pl.BlockSpec((None, 1, d, de), by_expert),
                  pl.BlockSpec((None, 1, de, d), by_expert)],
        out_specs=pl.BlockSpec((TB * n_slab, LANES), lambda i, be, nu: (i, 0)),
        scratch_shapes=[pltpu.VMEM((d, 2 * de), bf16), pltpu.VMEM((de, d), bf16)],
    )
    return pl.pallas_call(
        functools.partial(_expert_kernel, n_slab),
        grid_spec=grid_spec,
        out_shape=jax.ShapeDtypeStruct((n_blocks * TB * n_slab, LANES), f32),
        compiler_params=_params(("arbitrary",)),
        name="experts",
    )(blk_expert, n_used, xs, w_gate, w_up, w_down)


def _combine_kernel(n_slab, final, bounds, dest_ref, next_ref, x_ref, wgt_ref, g_ref, ys_ref, *rest):
    outs, (gbuf, sems) = rest[:-2], rest[-2:]
    i = pl.program_id(0)
    n = dest_ref.shape[2]
    tm = x_ref.shape[0]
    slot = i % 2
    pitch = n_slab + SUBLANES

    def gather(idx_ref, s):
        def copy(tok, k):
            src = ys_ref.at[pl.ds(pl.multiple_of(idx_ref[0, 0, tok * TOP_K + k] * n_slab, n_slab), n_slab), :]
            row = k * tm * pitch + tok * pitch
            dst = gbuf.at[s, pl.ds(pl.multiple_of(row, SUBLANES), n_slab), :]
            return pltpu.make_async_copy(src, dst, sems.at[s])
        _issue_rows(n, copy)

    pl.when(i == 0)(lambda: gather(dest_ref, slot))
    pl.when(i + 1 < pl.num_programs(0))(lambda: gather(next_ref, 1 - slot))
    rows = pl.ds(0, n * n_slab)
    pltpu.make_async_copy(ys_ref.at[rows, :], gbuf.at[slot, rows, :], sems.at[slot]).wait()

    ys = []
    for k in range(TOP_K):
        y = jnp.concatenate([gbuf[slot, pl.ds(k * tm * pitch + s, tm, stride=pitch), :]
                             for s in range(n_slab)], axis=1)
        ys.append(y * wgt_ref[:, k:k + 1])
    x = x_ref[...] + (ys[0] + ys[1])
    if final:
        x = _rms(x, g_ref[...])

    def store(k):
        outs[k][...] = x

    _when_segment(i, bounds, store)


def _combine(x, ys, dest, wgt, g, n_slab, final, seg_rows):
    m, d = x.shape
    n_steps = m // TD
    dest3 = dest.reshape(n_steps, 1, TD * TOP_K)
    counts = [r // TD for r in seg_rows]
    bounds = [0]
    for cnt in counts:
        bounds.append(bounds[-1] + cnt)
    out_specs = [_seg_spec((TD, d), bounds[k], counts[k]) for k in range(len(seg_rows))]
    out_shape = [jax.ShapeDtypeStruct((r, d), f32) for r in seg_rows]
    idx_block = (1, 1, TD * TOP_K)
    return pl.pallas_call(
        functools.partial(_combine_kernel, n_slab, final, tuple(bounds)),
        grid=(n_steps,),
        in_specs=[pl.BlockSpec(idx_block, lambda i: (i, 0, 0), memory_space=pltpu.SMEM),
                  pl.BlockSpec(idx_block, lambda i: (jnp.minimum(i + 1, n_steps - 1), 0, 0),
                               memory_space=pltpu.SMEM),
                  pl.BlockSpec((TD, d), lambda i: (i, 0)),
                  pl.BlockSpec((TD, LANES), lambda i: (i, 0)),
                  pl.BlockSpec((1, d), lambda i: (0, 0)),
                  pl.BlockSpec(memory_space=pl.ANY)],
        out_specs=out_specs, out_shape=out_shape,
        scratch_shapes=[pltpu.VMEM((2, TD * TOP_K * (n_slab + SUBLANES), LANES), f32),
                        pltpu.SemaphoreType.DMA((2,))],
        compiler_params=_params(("arbitrary",)),
        name="combine",
    )(dest3, dest3, x, wgt, g, ys)


def kernel(x_prompt, x_sample, cache_a_k, cache_a_v, cache_b_k, cache_b_v, cache_b_logf, state_conv, norm_mix_g, w_in, b_in, rel_bias, conv_w, conv_b, conv_ln_g, conv_ln_b, w_proj_a, w_proj_b, w_proj_c, w_out, norm_ffn_g, w_router_group, b_router_group, w_router_expert, b_router_expert, w_e_gate, w_e_up, w_e_down, norm_final_g):
    nb_p, t_p, d = x_prompt.shape
    nb_s, t_s, _ = x_sample.shape
    depth = w_in.shape[0]
    past = cache_b_k.shape[2]
    a_rows = cache_a_k.shape[2]
    m_p, m_s = nb_p * t_p, nb_s * t_s
    m = m_p + m_s
    c_conv = d // 2
    n_slab = d // LANES
    tm = _row_tile(np.gcd(m_p, m_s), 512)
    tm_mix = _row_tile(np.gcd(m_p, m_s), 256)
    assert m_p % TD == 0 and m_s % TD == 0 and t_s % 16 == 0 and m_p % t_s == 0

    a_keep = min(WINDOW_A, t_p)
    x_segs = [x_prompt.reshape(m_p, d), x_sample.reshape(m_s, d)]
    kv_states = [jnp.zeros((depth, rows, W_HEADS), f32)
                 for rows in (nb_p * a_keep, nb_p * a_keep, m_p, m_p, m_s, m_s, m_s, m_s)]
    p_states, s_states = [], []
    for l in range(depth):
        (qa, ka16, va16, qb, kb16, vb16, u, gates, logf), kv_states = _inproj(
            x_segs, norm_mix_g[l][None, :], w_in[l], b_in[l], kv_states, l, m_p, t_p, a_keep, tm)

        ya_p = _band_attention_prompt(qa, ka16, va16, rel_bias[l], nb_p, t_p, 4 * CHUNK)
        kk = jnp.concatenate([cache_a_k[l].reshape(nb_s, a_rows, W_HEADS).astype(bf16),
                              ka16[m_p:].reshape(nb_s, t_s, W_HEADS)], axis=1).reshape(-1, W_HEADS)
        vv = jnp.concatenate([cache_a_v[l].reshape(nb_s, a_rows, W_HEADS).astype(bf16),
                              va16[m_p:].reshape(nb_s, t_s, W_HEADS)], axis=1).reshape(-1, W_HEADS)
        ya_s = _band_attention_sample(qa, kk, vv, rel_bias[l], nb_s, t_s, a_rows, m_p // t_s)

        logf_p = logf[:m_p].reshape(nb_p, t_p, N_HEADS)
        logf_s = logf[m_p:].reshape(nb_s, t_s, N_HEADS)
        cum_p = _cumsum_time(logf_p.transpose(0, 2, 1))
        f_p = cum_p.transpose(0, 2, 1).reshape(m_p, N_HEADS)
        yb_p = _fox_attention(_fox_expand("q", qb, f_p, m_p, tm), _fox_expand("k", kb16, f_p, m_p, tm),
                              _fox_expand("v", vb16, None, m_p, tm),
                              nb_p, t_p, t_p, _row_tile(t_p, 2048), _row_tile(t_p, 512))
        cum_s = _cumsum_time(jnp.concatenate([cache_b_logf[l].astype(f32), logf_s], axis=1).transpose(0, 2, 1))
        t_ks = past + t_s
        kk = jnp.concatenate([cache_b_k[l].reshape(nb_s, past, W_HEADS).astype(bf16),
                              kb16[m_p:].reshape(nb_s, t_s, W_HEADS)], axis=1).reshape(-1, W_HEADS)
        vv = jnp.concatenate([cache_b_v[l].reshape(nb_s, past, W_HEADS).astype(bf16),
                              vb16[m_p:].reshape(nb_s, t_s, W_HEADS)], axis=1).reshape(-1, W_HEADS)
        f_ks = cum_s.transpose(0, 2, 1)
        yb_s = _fox_attention(
            _fox_expand("q", qb, f_ks[:, past:].reshape(m_s, N_HEADS), m_s, t_s, blk0=m_p // t_s),
            _fox_expand("k", kk, f_ks.reshape(nb_s * t_ks, N_HEADS), nb_s * t_ks, t_ks),
            _fox_expand("v", vv, None, nb_s * t_ks, t_ks),
            nb_s, t_s, t_ks, t_s, t_ks)

        conv_args = (conv_w[l], conv_b[l][None, :], conv_ln_g[l][None, :], conv_ln_b[l][None, :])
        c_p = _conv_module(u, jnp.zeros((nb_p, CONV_HALO, c_conv), f32), *conv_args,
                           nb_p, t_p, _row_tile(t_p, 256), 0)
        init_s = jnp.pad(state_conv[l], ((0, 0), (CONV_HALO - (CONV_W - 1), 0), (0, 0)))
        c_s = _conv_module(u, init_s, *conv_args, nb_s, t_s, t_s, m_p // t_s)

        mixed = _mix(ya_p, ya_s, yb_p, yb_s, c_p, c_s, gates, w_proj_a[l].astype(bf16),
                     w_proj_b[l].astype(bf16), w_proj_c[l].astype(bf16), tm_mix)
        wr = jnp.pad(jnp.concatenate([w_router_group[l], w_router_expert[l]], axis=1),
                     ((0, 0), (0, LANES - N_GROUPS - N_EXPERTS)))
        wr_hi = wr.astype(bf16)
        wr_parts = jnp.concatenate([wr_hi, (wr - wr_hi.astype(f32)).astype(bf16)], axis=1)
        br = jnp.pad(jnp.concatenate([b_router_group[l], b_router_expert[l]]),
                     (0, LANES - N_GROUPS - N_EXPERTS))[None, :]
        x_mid, h2_slab, eid, wgt = _outproj(mixed, x_segs, w_out[l].astype(bf16), norm_ffn_g[l][None, :],
                                            wr_parts, br, tm_mix)

        dest, blk_expert, n_used, last_blk, n_blocks = _plan(eid[:, :TOP_K])
        xs = _dispatch(h2_slab, dest, last_blk, n_used, n_blocks, n_slab // 2, tm)
        ys = _experts(xs, blk_expert, n_used, w_e_gate, w_e_up, w_e_down, l, n_blocks, n_slab)
        final = l == depth - 1
        x_segs = _combine(x_mid, ys, dest, wgt, norm_final_g[None, :], n_slab, final,
                          (m_p, m_s) if final else (m,))

        n_cs = c_conv // LANES
        u_p = jnp.stack([u[((b + 1) * t_p - (CONV_W - 1)) * n_cs:(b + 1) * t_p * n_cs] for b in range(nb_p)])
        u_p = u_p.reshape(nb_p, CONV_W - 1, c_conv)
        u_s = u[m_p * n_cs:].reshape(nb_s, t_s, c_conv)
        p_states.append((logf_p, u_p))
        s_states.append((logf_s, jnp.concatenate([state_conv[l], u_s], axis=1)[:, -(CONV_W - 1):]))

    y_prompt = x_segs[0].reshape(nb_p, t_p, d)
    y_sample = x_segs[1].reshape(nb_s, t_s, d)
    stack = lambda states, k: jnp.stack([st[k] for st in states], axis=0)
    heads = lambda a, nb, t: a.reshape(depth, nb, t, N_HEADS, HEAD_DIM)
    ka_p, va_p, kb_p, vb_p, ka_s, va_s, kb_s, vb_s = kv_states
    return (y_prompt, y_sample,
            heads(ka_p, nb_p, a_keep), heads(va_p, nb_p, a_keep), heads(kb_p, nb_p, t_p), heads(vb_p, nb_p, t_p),
            stack(p_states, 0), stack(p_states, 1),
            heads(ka_s, nb_s, t_s), heads(va_s, nb_s, t_s), heads(kb_s, nb_s, t_s), heads(vb_s, nb_s, t_s),
            stack(s_states, 0), stack(s_states, 1))
```

```python
import functools

import jax
import jax.numpy as jnp
import numpy as np
from jax import lax
from jax.experimental import pallas as pl
from jax.experimental.pallas import tpu as pltpu

f32 = jnp.float32
bf16 = jnp.bfloat16

HEAD_DIM = 64
N_HEADS = 8
W_HEADS = N_HEADS * HEAD_DIM
CHUNK = 64
WINDOW_A = 8 * CHUNK
REL_CLIP = 128
CONV_W = 31
CONV_HALO = 32
N_GROUPS = 4
EXPERTS_PER_GROUP = 8
N_EXPERTS = N_GROUPS * EXPERTS_PER_GROUP
TOP_K = 2
SCALE = HEAD_DIM ** -0.5
EPS = 1e-6
NEG_INF = -1e30
LANES = 128
SUBLANES = 8
MIB = 1024 * 1024


def _params(sem, vmem_mib=48):
    return pltpu.CompilerParams(dimension_semantics=sem, vmem_limit_bytes=vmem_mib * MIB)


def _row_tile(m, cap):
    t = cap
    while m % t:
        t //= 2
    return t


def _sigmoid(z):
    return 0.5 * jnp.tanh(0.5 * z) + 0.5


def _rms(x, g):
    return x * lax.rsqrt(jnp.mean(x * x, axis=-1, keepdims=True) + EPS) * g


def _pack_bf16_pair(lo, hi):
    def rounded(x):
        bits = lax.bitcast_convert_type(x, jnp.uint32)
        return bits + jnp.uint32(0x7FFF) + ((bits >> 16) & jnp.uint32(1))
    return (rounded(hi) & jnp.uint32(0xFFFF0000)) | (rounded(lo) >> 16)


def _unpack_bf16_pair(word):
    lo = lax.bitcast_convert_type(word << 16, f32)
    hi = lax.bitcast_convert_type(word & jnp.uint32(0xFFFF0000), f32)
    return lo, hi


def _when_segment(i, bounds, fn):
    for k in range(len(bounds) - 1):
        pl.when((i >= bounds[k]) & (i < bounds[k + 1]))(functools.partial(fn, k))


def _seg_spec(block, start, count, width_axes=1):
    zeros = (0,) * width_axes
    return pl.BlockSpec(block, lambda i, *_: (jnp.clip(i - start, 0, count - 1),) + zeros)


TN = 1024
SEG_PER_TILE = TN // W_HEADS


def _norm_kernel(n_seg, bounds, *refs):
    xs = refs[:n_seg]
    g_ref, wf_ref, bf_ref, h_ref, logf = refs[n_seg:]

    def norm(k):
        h_ref[...] = _rms(xs[k][...], g_ref[...]).astype(bf16)

    _when_segment(pl.program_id(0), bounds, norm)
    zf = jnp.dot(h_ref[...], wf_ref[...], preferred_element_type=f32) + bf_ref[...]
    lf = jnp.minimum(zf, 0.0) - jnp.log1p(jnp.exp(-jnp.abs(zf)))
    logf[...] = lf[:, :N_HEADS]


def _norm(x_segs, g, w_f, b_f, tm):
    d = x_segs[0].shape[1]
    m = sum(a.shape[0] for a in x_segs)
    counts = [a.shape[0] // tm for a in x_segs]
    bounds = [0]
    for cnt in counts:
        bounds.append(bounds[-1] + cnt)
    const = lambda shape: pl.BlockSpec(shape, lambda i: (0, 0))
    return pl.pallas_call(
        functools.partial(_norm_kernel, len(x_segs), tuple(bounds)),
        grid=(m // tm,),
        in_specs=[_seg_spec((tm, d), bounds[k], counts[k]) for k in range(len(x_segs))] +
                 [const((1, d)), const((d, LANES)), const((1, LANES))],
        out_specs=[pl.BlockSpec((tm, d), lambda i: (i, 0)), pl.BlockSpec((tm, N_HEADS), lambda i: (i, 0))],
        out_shape=[jax.ShapeDtypeStruct((m, d), bf16), jax.ShapeDtypeStruct((m, N_HEADS), f32)],
        compiler_params=_params(("arbitrary",)),
        name="norm",
    )(*x_segs, g, w_f, b_f)


def _qkv_kernel(with_state, n_prompt, h_ref, w_ref, b_ref, *refs):
    outs = refs[2 * sum(with_state):]
    is_prompt = pl.program_id(0) < n_prompt
    for prompt_rows in (True, False):
        @pl.when(is_prompt if prompt_rows else ~is_prompt)
        def _(prompt_rows=prompt_rows):
            z = jnp.dot(h_ref[...], w_ref[...], preferred_element_type=f32) + b_ref[...]
            k = 0
            for n, has_state in enumerate(with_state):
                zn = z[:, n * W_HEADS:(n + 1) * W_HEADS]
                outs[k][...] = zn.astype(bf16)
                k += 1
                if has_state:
                    outs[k if prompt_rows else k + 1][...] = zn
                    k += 2


def _qkv_tile(h, w_qkv, b_qkv, tile, seg_states, layer, m_p, t_p, a_keep, tm):
    m, d = h.shape
    n_p, n_s = m_p // tm, (m - m_p) // tm
    per_seq, keep = t_p // tm, a_keep // tm

    def tail_rows(i):
        ip = jnp.minimum(i, n_p - 1)
        return (layer, (ip // per_seq) * keep + jnp.maximum(ip % per_seq - (per_seq - keep), 0), 0)

    state_block = (None, tm, W_HEADS)
    tail_spec = pl.BlockSpec(state_block, tail_rows)
    prompt_spec = pl.BlockSpec(state_block, lambda i: (layer, jnp.minimum(i, n_p - 1), 0))
    sample_spec = pl.BlockSpec(state_block, lambda i: (layer, jnp.clip(i - n_p, 0, n_s - 1), 0))
    states, out_specs, out_shape, state_out_pos = [], [], [], []
    for seg in seg_states:
        out_specs.append(pl.BlockSpec((tm, W_HEADS), lambda i: (i, 0)))
        out_shape.append(jax.ShapeDtypeStruct((m, W_HEADS), bf16))
        if seg is not None:
            buf_p, buf_s, keep_tail = seg
            for buf, spec in ((buf_p, tail_spec if keep_tail else prompt_spec), (buf_s, sample_spec)):
                state_out_pos.append(len(out_specs))
                states.append(buf)
                out_specs.append(spec)
                out_shape.append(jax.ShapeDtypeStruct(buf.shape, buf.dtype))
    outs = pl.pallas_call(
        functools.partial(_qkv_kernel, tuple(seg is not None for seg in seg_states), n_p),
        grid=(m // tm,),
        in_specs=[pl.BlockSpec((tm, d), lambda i: (i, 0)),
                  pl.BlockSpec((d, TN), lambda i: (0, tile)),
                  pl.BlockSpec((1, TN), lambda i: (0, tile))] + [pl.BlockSpec(memory_space=pl.ANY)] * len(states),
        out_specs=out_specs, out_shape=out_shape,
        input_output_aliases={3 + k: pos for k, pos in enumerate(state_out_pos)},
        compiler_params=_params(("arbitrary",)),
        name="qkv",
    )(h, w_qkv, b_qkv, *states)
    copies = [o for k, o in enumerate(outs) if k not in state_out_pos]
    return copies, [outs[pos] for pos in state_out_pos]


def _glu_kernel(n_slab, h_ref, w_ref, b_ref, u_ref):
    tm = h_ref.shape[0]
    c = n_slab * LANES
    z = jnp.dot(h_ref[...], w_ref[...], preferred_element_type=f32) + b_ref[...]
    glu = z[:, :c] * _sigmoid(z[:, c:])
    for s in range(n_slab):
        u_ref[pl.ds(s, tm, stride=n_slab), :] = glu[:, s * LANES:(s + 1) * LANES]


def _glu(h, w, b, tm):
    m, d = h.shape
    c = w.shape[1] // 2
    n_slab = c // LANES
    return pl.pallas_call(
        functools.partial(_glu_kernel, n_slab),
        grid=(m // tm,),
        in_specs=[pl.BlockSpec((tm, d), lambda i: (i, 0)), pl.BlockSpec((d, 2 * c), lambda i: (0, 0)),
                  pl.BlockSpec((1, 2 * c), lambda i: (0, 0))],
        out_specs=pl.BlockSpec((tm * n_slab, LANES), lambda i: (i, 0)),
        out_shape=jax.ShapeDtypeStruct((m * n_slab, LANES), f32),
        compiler_params=_params(("arbitrary",)),
        name="glu",
    )(h, w, b)


def _gates_kernel(h_ref, w_ref, b_ref, o_ref):
    z = jnp.dot(h_ref[...], w_ref[...], preferred_element_type=f32) + b_ref[...]
    o_ref[...] = _sigmoid(z).astype(o_ref.dtype)


def _gates(h, w, b, tm):
    m, d = h.shape
    n = w.shape[1]
    return pl.pallas_call(
        _gates_kernel,
        grid=(n // TN, m // tm),
        in_specs=[pl.BlockSpec((tm, d), lambda j, i: (i, 0)), pl.BlockSpec((d, TN), lambda j, i: (0, j)),
                  pl.BlockSpec((1, TN), lambda j, i: (0, j))],
        out_specs=pl.BlockSpec((tm, TN), lambda j, i: (i, j)),
        out_shape=jax.ShapeDtypeStruct((m, n), bf16),
        compiler_params=_params(("arbitrary", "arbitrary")),
        name="gates",
    )(h, w, b)


def _inproj(x_segs, g, w_in, b_in, states, layer, m_p, t_p, a_keep, tm):
    d = x_segs[0].shape[1]
    c_conv = d // 2
    n_qkv = 6 * W_HEADS
    f0 = n_qkv
    c0 = f0 + N_HEADS
    g0 = c0 + 2 * c_conv
    cast = lambda a: a.astype(bf16)
    row = lambda a: a[None, :].astype(f32)
    w_f = jnp.pad(w_in[:, f0:c0], ((0, 0), (0, LANES - N_HEADS))).astype(bf16)
    b_f = jnp.pad(b_in[f0:c0], (0, LANES - N_HEADS))[None, :].astype(f32)
    h, logf = _norm(x_segs, g, w_f, b_f, tm)

    ka_p, va_p, kb_p, vb_p, ka_s, va_s, kb_s, vb_s = states
    w_qkv, b_qkv = cast(w_in[:, :n_qkv]), row(b_in[:n_qkv])
    tile = functools.partial(_qkv_tile, h, w_qkv, b_qkv, layer=layer, m_p=m_p, t_p=t_p, a_keep=a_keep, tm=tm)
    assert SEG_PER_TILE == 2
    (qa, ka16), (ka_p, ka_s) = tile(0, [None, (ka_p, ka_s, True)])
    (va16, qb), (va_p, va_s) = tile(1, [(va_p, va_s, True), None])
    (kb16, vb16), (kb_p, kb_s, vb_p, vb_s) = tile(2, [(kb_p, kb_s, False), (vb_p, vb_s, False)])
    u = _glu(h, cast(w_in[:, c0:g0]), row(b_in[c0:g0]), tm)
    gates = _gates(h, cast(w_in[:, g0:]), row(b_in[g0:]), tm)
    return (qa, ka16, va16, qb, kb16, vb16, u, gates, logf), [ka_p, va_p, kb_p, vb_p, ka_s, va_s, kb_s, vb_s]


TC = 512


def _cumsum_kernel(x_ref, o_ref, carry):
    @pl.when(pl.program_id(1) == 0)
    def _():
        carry[...] = jnp.zeros_like(carry)

    blk = x_ref[0]
    r = lax.broadcasted_iota(jnp.int32, (TC, TC), 0)
    c = lax.broadcasted_iota(jnp.int32, (TC, TC), 1)
    tri = jnp.where(r <= c, 1.0, 0.0).astype(bf16)
    cs = carry[:, 0:1]
    rest = blk
    for _ in range(F_PARTS):
        part = rest.astype(bf16)
        rest = rest - part.astype(f32)
        cs = cs + jnp.dot(part, tri, preferred_element_type=f32)
    o_ref[0] = cs
    carry[...] = jnp.broadcast_to(cs[:, TC - 1:TC], carry.shape)


def _cumsum_time(x):
    nb, h, t = x.shape
    tp = -(-t // TC) * TC
    xp = jnp.pad(x, ((0, 0), (0, 0), (0, tp - t)))
    out = pl.pallas_call(
        _cumsum_kernel,
        grid=(nb, tp // TC),
        in_specs=[pl.BlockSpec((1, h, TC), lambda b, k: (b, 0, k))],
        out_specs=pl.BlockSpec((1, h, TC), lambda b, k: (b, 0, k)),
        out_shape=jax.ShapeDtypeStruct((nb, h, tp), f32),
        scratch_shapes=[pltpu.VMEM((h, LANES), f32)],
        compiler_params=_params(("arbitrary", "arbitrary")),
        name="cumsum",
    )(xp)
    return out[:, :, :t]


F_PARTS = 3


def _spare_base(h):
    return HEAD_DIM * (1 - h % 2)


def _fox_expand_kernel(kind, x_ref, f_ref, place_ref, o_ref):
    tm = x_ref.shape[0]
    lane = lax.broadcasted_iota(jnp.int32, (tm, LANES), 1)
    if kind != "v":
        rest = f_ref[...]
        stack = jnp.where(lane < (F_PARTS + 1) * N_HEADS, 1.0, 0.0)
        for n in range(F_PARTS):
            part = rest.astype(bf16).astype(f32)
            rest = rest - part
            stack = jnp.where((lane >= n * N_HEADS) & (lane < (n + 1) * N_HEADS), part, stack)
        spare_all = jnp.dot(stack.astype(bf16), place_ref[...], preferred_element_type=f32)
    for h in range(N_HEADS):
        pair = x_ref[:, (h // 2) * LANES:(h // 2 + 1) * LANES].astype(f32)
        if kind == "q":
            pair = pair * SCALE
        if kind == "v":
            spare = jnp.where(lane == _spare_base(h), 1.0, 0.0)
        else:
            spare = spare_all[:, h * LANES:(h + 1) * LANES]
        own = (lane < HEAD_DIM) if h % 2 == 0 else (lane >= HEAD_DIM)
        o_ref[h] = jnp.where(own, pair, spare).astype(o_ref.dtype)


def _fox_placement(kind):
    place = np.zeros((LANES, N_HEADS * LANES), np.float32)
    for h in range(N_HEADS):
        base = h * LANES + _spare_base(h)
        for n in range(F_PARTS):
            if kind == "q":
                place[n * N_HEADS + h, base + n] = 1.0
                place[F_PARTS * N_HEADS + h, base + F_PARTS + n] = 1.0
            else:
                place[F_PARTS * N_HEADS + h, base + n] = 1.0
                place[n * N_HEADS + h, base + F_PARTS + n] = -1.0
    return jnp.asarray(place, bf16)


def _fox_expand(kind, x, f, rows, tm, blk0=0):
    f_lanes = jnp.zeros((rows, LANES), f32) if f is None else jnp.tile(f, (1, LANES // N_HEADS))
    return pl.pallas_call(
        functools.partial(_fox_expand_kernel, kind),
        grid=(rows // tm,),
        in_specs=[pl.BlockSpec((tm, W_HEADS), lambda i: (blk0 + i, 0)),
                  pl.BlockSpec((tm, LANES), lambda i: (i, 0)),
                  pl.BlockSpec((LANES, N_HEADS * LANES), lambda i: (0, 0))],
        out_specs=pl.BlockSpec((N_HEADS, tm, LANES), lambda i: (0, i, 0)),
        out_shape=jax.ShapeDtypeStruct((N_HEADS, rows, LANES), bf16),
        compiler_params=_params(("arbitrary",)),
        name="fox_expand_" + kind,
    )(x, f_lanes, _fox_placement(kind))


FOX_HEADS_PER_TRIP = 2


def _fox_kernel(tq, tk, off, q_ref, k_ref, v_ref, o_ref, m_scr, acc_scr):
    i = pl.program_id(1)
    j = pl.program_id(2)

    @pl.when(j == 0)
    def _():
        m_scr[...] = jnp.full_like(m_scr, NEG_INF)
        acc_scr[...] = jnp.zeros_like(acc_scr)

    q_first = i * tq + off
    q_last = q_first + tq - 1
    k_first = j * tk
    k_last = k_first + tk - 1

    def body(row0, masked):
        rows = slice(row0, tq)
        if masked:
            kpos = k_first + lax.broadcasted_iota(jnp.int32, (tq - row0, tk), 1)
            qpos = q_first + row0 + lax.broadcasted_iota(jnp.int32, (tq - row0, tk), 0)
            vis = kpos <= qpos

        def head(h):
            s = lax.dot_general(q_ref[h, rows], k_ref[h], (((1,), (1,)), ((), ())), preferred_element_type=f32)
            if masked:
                s = jnp.where(vis, s, NEG_INF)
            m_old = m_scr[h, rows]
            m_new = jnp.maximum(m_old, jnp.max(s, axis=-1, keepdims=True))
            pr = jnp.exp(s - m_new[:, 0:1])
            pv = jnp.dot(pr.astype(bf16), v_ref[h], preferred_element_type=f32)
            acc_scr[h, rows] = jnp.exp(m_old - m_new) * acc_scr[h, rows] + pv
            m_scr[h, rows] = m_new

        def trip(g, carry):
            for n in range(FOX_HEADS_PER_TRIP):
                head(g * FOX_HEADS_PER_TRIP + n)
            return carry

        lax.fori_loop(0, N_HEADS // FOX_HEADS_PER_TRIP, trip, 0)

    pl.when(k_last <= q_first)(functools.partial(body, 0, False))
    if tq % tk == 0 and off % tk == 0:
        for c in range(tq // tk):
            pl.when(k_first == q_first + c * tk)(functools.partial(body, c * tk, True))
    else:
        pl.when((k_first <= q_last) & (k_last > q_first))(functools.partial(body, 0, True))

    @pl.when(j == pl.num_programs(2) - 1)
    def _():
        lane = lax.broadcasted_iota(jnp.int32, (tq, LANES), 1)
        for p in range(N_HEADS // 2):
            even = acc_scr[2 * p]
            odd = acc_scr[2 * p + 1]
            even = even / even[:, _spare_base(0):_spare_base(0) + 1]
            odd = odd / odd[:, _spare_base(1):_spare_base(1) + 1]
            o_ref[:, p * LANES:(p + 1) * LANES] = jnp.where(lane < HEAD_DIM, even, odd).astype(o_ref.dtype)


def _fox_attention(q, k, v, nb, t_q, t_k, tq, tk):
    nq, nk = t_q // tq, t_k // tk
    off = t_k - t_q

    def last_k(i):
        return jnp.minimum((i * tq + tq - 1 + off) // tk, nk - 1)

    kv_spec = pl.BlockSpec((N_HEADS, tk, LANES), lambda b, i, j: (0, b * nk + jnp.minimum(j, last_k(i)), 0))
    return pl.pallas_call(
        functools.partial(_fox_kernel, tq, tk, off),
        grid=(nb, nq, nk),
        in_specs=[pl.BlockSpec((N_HEADS, tq, LANES), lambda b, i, j: (0, b * nq + i, 0)), kv_spec, kv_spec],
        out_specs=pl.BlockSpec((tq, W_HEADS), lambda b, i, j: (b * nq + i, 0)),
        out_shape=jax.ShapeDtypeStruct((nb * t_q, W_HEADS), bf16),
        scratch_shapes=[pltpu.VMEM((N_HEADS, tq, LANES), f32), pltpu.VMEM((N_HEADS, tq, LANES), f32)],
        compiler_params=_params(("arbitrary", "arbitrary", "arbitrary")),
        name="fox",
    )(q, k, v)


def _band_kernel(rows, gq, wk, has_prev, *refs):
    if has_prev:
        q_ref, kp_ref, kc_ref, vp_ref, vc_ref, bias_ref, o_ref, k_scr, v_scr = refs
        k_scr[0:WINDOW_A] = kp_ref[...]
        k_scr[WINDOW_A:WINDOW_A + rows] = kc_ref[...]
        v_scr[0:WINDOW_A] = vp_ref[...]
        v_scr[WINDOW_A:WINDOW_A + rows] = vc_ref[...]
        k_src, v_src = k_scr, v_scr
    else:
        q_ref, k_src, v_src, bias_ref, o_ref = refs
    i = pl.program_id(1)
    lane = lax.broadcasted_iota(jnp.int32, (gq, LANES), 1)
    low = lane < HEAD_DIM

    def attend(before_start):
        for g in range(rows // gq):
            r0 = g * gq
            if before_start:
                key_pos = (i - 1) * WINDOW_A + r0 + lax.broadcasted_iota(jnp.int32, (gq, wk), 1)
                vis = key_pos >= 0
            for p in range(N_HEADS // 2):
                cols = slice(p * LANES, (p + 1) * LANES)
                q2 = q_ref[r0:r0 + gq, cols] * SCALE
                kw = k_src[r0:r0 + wk, cols]
                vw = v_src[r0:r0 + wk, cols]
                outs = []
                for half in range(2):
                    h = 2 * p + half
                    qm = jnp.where(low if half == 0 else ~low, q2, jnp.zeros_like(q2))
                    s = lax.dot_general(qm, kw, (((1,), (1,)), ((), ())), preferred_element_type=f32)
                    s = s + bias_ref[h]
                    if before_start:
                        s = jnp.where(vis, s, NEG_INF)
                    m = jnp.max(s, axis=-1, keepdims=True)
                    pr = jnp.exp(s - m)
                    l = jnp.sum(pr, axis=-1, keepdims=True)
                    pv = jnp.dot(pr.astype(bf16), vw, preferred_element_type=f32)
                    outs.append(pv / l)
                o_ref[r0:r0 + gq, cols] = jnp.where(low, outs[0], outs[1]).astype(o_ref.dtype)

    if has_prev:
        pl.when(i == 0)(functools.partial(attend, True))
        pl.when(i > 0)(functools.partial(attend, False))
    else:
        attend(False)


def _rel_bias_table(rel_bias, gq, wk, q_shift):
    period = wk + gq
    j = np.arange(period)
    k = np.where(j < wk, j, j - period)
    line = rel_bias.astype(f32)[:, np.clip(q_shift - k, -REL_CLIP, REL_CLIP) + REL_CLIP]
    tiled = jnp.tile(line, (1, gq))[:, :gq * (period - 1)]
    return tiled.reshape(-1, gq, period - 1)[:, :, :wk]


def _band_bias(rel_bias, gq, wk, q_shift):
    r = np.arange(gq)[:, None]
    s = np.arange(wk)[None, :]
    band0 = (r // CHUNK) * CHUNK + q_shift - WINDOW_A
    ok = (s >= band0) & (s < band0 + WINDOW_A + CHUNK)
    return jnp.where(ok[None], _rel_bias_table(rel_bias, gq, wk, q_shift), NEG_INF)


def _band_attention_prompt(q, k, v, rel_bias, nb, t, gq):
    rows = WINDOW_A
    wk = WINDOW_A + gq
    n_steps = t // rows
    bias = _band_bias(rel_bias, gq, wk, WINDOW_A)
    cur = pl.BlockSpec((rows, W_HEADS), lambda b, i: (b * n_steps + i, 0))
    prev = pl.BlockSpec((rows, W_HEADS), lambda b, i: (b * n_steps + jnp.maximum(i - 1, 0), 0))
    return pl.pallas_call(
        functools.partial(_band_kernel, rows, gq, wk, True),
        grid=(nb, n_steps),
        in_specs=[cur, prev, cur, prev, cur,
                  pl.BlockSpec((N_HEADS, gq, wk), lambda b, i: (0, 0, 0))],
        out_specs=cur,
        out_shape=jax.ShapeDtypeStruct((nb * t, W_HEADS), bf16),
        scratch_shapes=[pltpu.VMEM((2 * rows, W_HEADS), bf16), pltpu.VMEM((2 * rows, W_HEADS), bf16)],
        compiler_params=_params(("arbitrary", "arbitrary")),
        name="band_prompt",
    )(q, k, k, v, v, bias)


def _band_attention_sample(q, kk, vv, rel_bias, nb, s_new, l_cache, q_blk0):
    wk = l_cache + s_new
    bias = _rel_bias_table(rel_bias, s_new, wk, l_cache)
    return pl.pallas_call(
        functools.partial(_band_kernel, s_new, s_new, wk, False),
        grid=(nb, 1),
        in_specs=[pl.BlockSpec((s_new, W_HEADS), lambda b, i: (q_blk0 + b, 0)),
                  pl.BlockSpec((wk, W_HEADS), lambda b, i: (b, 0)),
                  pl.BlockSpec((wk, W_HEADS), lambda b, i: (b, 0)),
                  pl.BlockSpec((N_HEADS, s_new, wk), lambda b, i: (0, 0, 0))],
        out_specs=pl.BlockSpec((s_new, W_HEADS), lambda b, i: (b, 0)),
        out_shape=jax.ShapeDtypeStruct((nb * s_new, W_HEADS), bf16),
        compiler_params=_params(("arbitrary", "arbitrary")),
        name="band_sample",
    )(q, kk, vv, bias)


CONV_STRIP = 32


def _conv_kernel(tt, n_slab, init_ref, u_ref, w_ref, cb_ref, g_ref, b_ref, o_ref, ubuf, acc_scr):
    halo = CONV_HALO * n_slab

    @pl.when(pl.program_id(1) == 0)
    def _():
        ubuf[0:halo] = init_ref[0]

    ubuf[halo:halo + tt * n_slab] = u_ref[...]
    rs = min(CONV_STRIP, tt)
    first = CONV_HALO - (CONV_W - 1)

    def per_step(slab):
        return jnp.broadcast_to(slab[None], (rs, n_slab, LANES)).reshape(rs * n_slab, LANES)

    for s in range(tt // rs):
        acc = per_step(cb_ref[...])
        for j in range(CONV_W):
            r0 = (s * rs + first + j) * n_slab
            acc = acc + per_step(w_ref[j * n_slab:(j + 1) * n_slab, :]) * ubuf[r0:r0 + rs * n_slab, :]
        acc_scr[...] = acc
        rows = jnp.concatenate([acc_scr[pl.ds(q, rs, stride=n_slab), :] for q in range(n_slab)], axis=1)
        mu = jnp.mean(rows, axis=-1, keepdims=True)
        cen = rows - mu
        var = jnp.mean(cen * cen, axis=-1, keepdims=True)
        y = cen * lax.rsqrt(var + EPS) * g_ref[...] + b_ref[...]
        o_ref[s * rs:(s + 1) * rs, :] = (y * _sigmoid(y)).astype(o_ref.dtype)
    if tt >= CONV_HALO:
        ubuf[0:halo] = ubuf[tt * n_slab:tt * n_slab + halo]


def _conv_module(u_slab, init, conv_w, conv_b, ln_g, ln_b, nb, t, tt, blk0):
    c = conv_w.shape[1]
    n_slab = c // LANES
    n_t = t // tt
    vec = pl.BlockSpec((1, c), lambda b, i: (0, 0))
    return pl.pallas_call(
        functools.partial(_conv_kernel, tt, n_slab),
        grid=(nb, n_t),
        in_specs=[pl.BlockSpec((1, CONV_HALO * n_slab, LANES), lambda b, i: (b, 0, 0)),
                  pl.BlockSpec((tt * n_slab, LANES), lambda b, i: (blk0 + b * n_t + i, 0)),
                  pl.BlockSpec((CONV_W * n_slab, LANES), lambda b, i: (0, 0)),
                  pl.BlockSpec((n_slab, LANES), lambda b, i: (0, 0)), vec, vec],
        out_specs=pl.BlockSpec((tt, c), lambda b, i: (b * n_t + i, 0)),
        out_shape=jax.ShapeDtypeStruct((nb * t, c), bf16),
        scratch_shapes=[pltpu.VMEM(((CONV_HALO + tt) * n_slab, LANES), f32),
                        pltpu.VMEM((min(CONV_STRIP, tt) * n_slab, LANES), f32)],
        compiler_params=_params(("arbitrary", "arbitrary")),
        name="conv",
    )(init.reshape(nb, CONV_HALO * n_slab, LANES), u_slab, conv_w.reshape(CONV_W * n_slab, LANES),
      conv_b.reshape(n_slab, LANES), ln_g, ln_b)


def _mix_kernel(bounds, ya_p, ya_s, yb_p, yb_s, c_p, c_s, gates, pa, pb, pc, o_ref):
    d = o_ref.shape[1]

    def go(k):
        ya, yb, c = ((ya_p, yb_p, c_p), (ya_s, yb_s, c_s))[k]
        a = jnp.dot(ya[...], pa[...], preferred_element_type=f32)
        mixed = gates[:, 0:d].astype(f32) * a
        b = jnp.dot(yb[...], pb[...], preferred_element_type=f32)
        mixed = mixed + gates[:, d:2 * d].astype(f32) * b
        cc = jnp.dot(c[...], pc[...], preferred_element_type=f32)
        mixed = mixed + gates[:, 2 * d:3 * d].astype(f32) * cc
        o_ref[...] = mixed.astype(o_ref.dtype)

    _when_segment(pl.program_id(0), bounds, go)


def _mix(ya_p, ya_s, yb_p, yb_s, c_p, c_s, gates, pa, pb, pc, tm):
    m, d3 = gates.shape
    d = d3 // 3
    n_p, n_s = ya_p.shape[0] // tm, ya_s.shape[0] // tm
    bounds = (0, n_p, n_p + n_s)
    c_conv = c_p.shape[1]
    const = lambda shape: pl.BlockSpec(shape, lambda i: (0, 0))
    return pl.pallas_call(
        functools.partial(_mix_kernel, bounds),
        grid=(m // tm,),
        in_specs=[_seg_spec((tm, W_HEADS), 0, n_p), _seg_spec((tm, W_HEADS), n_p, n_s),
                  _seg_spec((tm, W_HEADS), 0, n_p), _seg_spec((tm, W_HEADS), n_p, n_s),
                  _seg_spec((tm, c_conv), 0, n_p), _seg_spec((tm, c_conv), n_p, n_s),
                  pl.BlockSpec((tm, d3), lambda i: (i, 0)),
                  const((W_HEADS, d)), const((W_HEADS, d)), const((c_conv, d))],
        out_specs=pl.BlockSpec((tm, d), lambda i: (i, 0)),
        out_shape=jax.ShapeDtypeStruct((m, d), bf16),
        compiler_params=_params(("arbitrary",)),
        name="mix",
    )(ya_p, ya_s, yb_p, yb_s, c_p, c_s, gates, pa, pb, pc)


def _route(logits):
    shape = logits.shape
    lane = lax.broadcasted_iota(jnp.int32, shape, 1)
    lane_f = lane.astype(f32)
    big = float(LANES)
    gl = jnp.where(lane < N_GROUPS, logits, -jnp.inf)
    g_max = jnp.max(gl, axis=-1, keepdims=True)
    g_idx = jnp.min(jnp.where(gl == g_max, lane_f, big), axis=-1, keepdims=True)
    g_sum = jnp.sum(jnp.exp(gl - g_max), axis=-1, keepdims=True)
    g_w = 1.0 / g_sum
    lo = N_GROUPS + g_idx * EXPERTS_PER_GROUP
    el = jnp.where((lane_f >= lo) & (lane_f < lo + EXPERTS_PER_GROUP), logits, -jnp.inf)
    m1 = jnp.max(el, axis=-1, keepdims=True)
    i1 = jnp.min(jnp.where(el == m1, lane_f, big), axis=-1, keepdims=True)
    el2 = jnp.where(lane_f == i1, -jnp.inf, el)
    m2 = jnp.max(el2, axis=-1, keepdims=True)
    i2 = jnp.min(jnp.where(el2 == m2, lane_f, big), axis=-1, keepdims=True)
    e21 = jnp.exp(m2 - m1)
    den = 1.0 + e21
    w1 = g_w * (1.0 / den)
    w2 = g_w * (e21 / den)
    eid = jnp.where(lane == 0, i1 - N_GROUPS, jnp.where(lane == 1, i2 - N_GROUPS, 0.0)).astype(jnp.int32)
    wgt = jnp.where(lane == 0, w1, jnp.where(lane == 1, w2, 0.0))
    return eid, wgt


def _outproj_kernel(n_slab, n_seg, bounds, *refs):
    xs = refs[:n_seg]
    mixed, wo, g2, wr, br, xo, h2o, eid_o, wgt_o = refs[n_seg:]
    tm = mixed.shape[0]
    i = pl.program_id(0)
    x = xs[0][...]
    for k in range(1, n_seg):
        x = jnp.where(i >= bounds[k], xs[k][...], x)
    xn = x + jnp.dot(mixed[...], wo[...], preferred_element_type=f32)
    xo[...] = xn
    h2 = _rms(xn, g2[...])
    n_word = n_slab // 2
    for s in range(n_word):
        h2o[pl.ds(s, tm, stride=n_word), :] = _pack_bf16_pair(h2[:, s * LANES:(s + 1) * LANES],
                                                              h2[:, (n_word + s) * LANES:(n_word + s + 1) * LANES])
    h_hi = h2.astype(bf16)
    h_lo = (h2 - h_hi.astype(f32)).astype(bf16)
    hi = jnp.dot(h_hi, wr[...], preferred_element_type=f32)
    lo = jnp.dot(h_lo, wr[:, :LANES], preferred_element_type=f32)
    logits = hi[:, :LANES] + (hi[:, LANES:] + lo) + br[...]
    eid, wgt = _route(logits)
    eid_o[...] = eid
    wgt_o[...] = wgt


def _outproj(mixed, x_segs, wo, g2, wr, br, tm):
    m, d = mixed.shape
    n_slab = d // LANES
    counts = [a.shape[0] // tm for a in x_segs]
    bounds = [0]
    for cnt in counts:
        bounds.append(bounds[-1] + cnt)
    const = lambda shape: pl.BlockSpec(shape, lambda i: (0, 0))
    row = lambda w: pl.BlockSpec((tm, w), lambda i: (i, 0))
    sds = jax.ShapeDtypeStruct
    return pl.pallas_call(
        functools.partial(_outproj_kernel, n_slab, len(x_segs), tuple(bounds)),
        grid=(m // tm,),
        in_specs=[_seg_spec((tm, d), bounds[k], counts[k]) for k in range(len(x_segs))] +
                 [row(d), const((d, d)), const((1, d)), const((d, 2 * LANES)), const((1, LANES))],
        out_specs=[row(d), pl.BlockSpec((tm * n_slab // 2, LANES), lambda i: (i, 0)), row(LANES), row(LANES)],
        out_shape=[sds((m, d), f32), sds((m * n_slab // 2, LANES), jnp.uint32), sds((m, LANES), jnp.int32),
                   sds((m, LANES), f32)],
        compiler_params=_params(("arbitrary",)),
        name="outproj",
    )(*x_segs, mixed, wo, g2, wr, br)


TB = 256
TD = 256


def _plan(eid):
    flat_e = eid.reshape(-1)
    n_assign = flat_e.shape[0]
    onehot = (flat_e[:, None] == jnp.arange(N_EXPERTS, dtype=jnp.int32)[None, :]).astype(jnp.int32)
    csum = jnp.cumsum(onehot, axis=0)
    counts = csum[-1]
    rank = jnp.sum(onehot * csum, axis=1) - 1
    n_blk_e = (counts + TB - 1) // TB
    blk_end = jnp.cumsum(n_blk_e)
    blk_start = blk_end - n_blk_e
    dest = blk_start[flat_e] * TB + rank
    n_blocks = -(-n_assign // TB) + N_EXPERTS
    blk_ids = jnp.arange(n_blocks, dtype=jnp.int32)
    blk_expert = jnp.minimum(jnp.sum((blk_end[None, :] <= blk_ids[:, None]).astype(jnp.int32), axis=1),
                             N_EXPERTS - 1)
    last_blk = jnp.where(n_blk_e > 0, blk_end - 1, -1).astype(jnp.int32)
    return dest.astype(jnp.int32), blk_expert, blk_end[-1:].astype(jnp.int32), last_blk, n_blocks


DMA_UNROLL = 8


def _issue_rows(n, copy):
    per_trip = DMA_UNROLL // TOP_K

    def trip(t, carry):
        for r in range(per_trip):
            for k in range(TOP_K):
                copy(t * per_trip + r, k).start()
        return carry

    lax.fori_loop(0, n // DMA_UNROLL, trip, 0)


def _dispatch_kernel(n_slab, n_blocks, dest_ref, last_ref, nu_ref, h_ref, xs_ref, zero_scr, sem, zero_sem):
    n = dest_ref.shape[2]
    blk_rows = TB * n_slab

    @pl.when(pl.program_id(0) == 0)
    def _():
        zero_scr[...] = jnp.zeros_like(zero_scr)

        def zero_block(b):
            rows = pl.ds(pl.multiple_of(b * blk_rows, blk_rows), blk_rows)
            return pltpu.make_async_copy(zero_scr, xs_ref.at[rows, :], zero_sem)

        def over_blocks(act):
            for e in range(N_EXPERTS):
                pl.when(last_ref[e] >= 0)(lambda e=e: act(zero_block(last_ref[e])))
            lax.fori_loop(nu_ref[0], n_blocks, lambda b, c: (act(zero_block(b)), c)[1], 0)

        over_blocks(lambda cp: cp.start())
        over_blocks(lambda cp: cp.wait())

    def copy(tok, k):
        src = h_ref.at[pl.ds(pl.multiple_of(tok * n_slab, n_slab), n_slab), :]
        dst = xs_ref.at[pl.ds(pl.multiple_of(dest_ref[0, 0, tok * TOP_K + k] * n_slab, n_slab), n_slab), :]
        return pltpu.make_async_copy(src, dst, sem)

    _issue_rows(n, copy)
    for half in range(TOP_K):
        rows = pl.ds(0, (n // TOP_K) * n_slab)
        pltpu.make_async_copy(h_ref, xs_ref.at[rows, :], sem).wait()


def _dispatch(h2_slab, dest, last_blk, n_used, n_blocks, n_slab, td):
    m = h2_slab.shape[0] // n_slab
    n_steps = m // td
    dest3 = dest.reshape(n_steps, 1, td * TOP_K)
    smem = pl.BlockSpec(memory_space=pltpu.SMEM)
    return pl.pallas_call(
        functools.partial(_dispatch_kernel, n_slab, n_blocks),
        grid=(n_steps,),
        in_specs=[pl.BlockSpec((1, 1, td * TOP_K), lambda i: (i, 0, 0), memory_space=pltpu.SMEM),
                  smem, smem,
                  pl.BlockSpec((td * n_slab, LANES), lambda i: (i, 0))],
        out_specs=pl.BlockSpec(memory_space=pl.ANY),
        out_shape=jax.ShapeDtypeStruct((n_blocks * TB * n_slab, LANES), h2_slab.dtype),
        scratch_shapes=[pltpu.VMEM((TB * n_slab, LANES), h2_slab.dtype), pltpu.SemaphoreType.DMA(()),
                        pltpu.SemaphoreType.DMA(())],
        compiler_params=_params(("arbitrary",)),
        name="dispatch",
    )(dest3, last_blk, n_used, h2_slab)


def _expert_kernel(n_slab, be_ref, nu_ref, xs_ref, wg_ref, wu_ref, wd_ref, o_ref, wgu_scr, wd_scr):
    i = pl.program_id(0)
    de = wd_ref.shape[1]

    @pl.when(i < nu_ref[0])
    def _():
        @pl.when((i == 0) | (be_ref[i] != be_ref[jnp.maximum(i - 1, 0)]))
        def _():
            wgu_scr[:, :de] = wg_ref[0].astype(bf16)
            wgu_scr[:, de:] = wu_ref[0].astype(bf16)
            wd_scr[...] = wd_ref[0].astype(bf16)

        n_word = n_slab // 2
        halves = [_unpack_bf16_pair(xs_ref[pl.ds(s, TB, stride=n_word), :]) for s in range(n_word)]
        x = jnp.concatenate([lo for lo, _ in halves] + [hi for _, hi in halves], axis=1)
        gu = jnp.dot(x.astype(bf16), wgu_scr[...], preferred_element_type=f32)
        g = gu[:, :de]
        hmid = (g * _sigmoid(g)) * gu[:, de:]
        y = jnp.dot(hmid.astype(bf16), wd_scr[...], preferred_element_type=f32)
        for s in range(n_word):
            o_ref[pl.ds(s, TB, stride=n_word), :] = _pack_bf16_pair(y[:, s * LANES:(s + 1) * LANES],
                                                                   y[:, (n_word + s) * LANES:(n_word + s + 1) * LANES])


def _experts(xs, blk_expert, n_used, w_gate, w_up, w_down, layer, n_blocks, n_slab):
    d, de = w_gate.shape[2], w_gate.shape[3]
    by_expert = lambda i, be, nu: (layer, be[jnp.minimum(i, nu[0] - 1)], 0, 0)
    grid_spec = pltpu.PrefetchScalarGridSpec(
        num_scalar_prefetch=2,
        grid=(n_blocks,),
        in_specs=[pl.BlockSpec((TB * n_slab // 2, LANES), lambda i, be, nu: (jnp.minimum(i, nu[0] - 1), 0)),
                  pl.BlockSpec((None, 1, d, de), by_expert), pl.BlockSpec((None, 1, d, de), by_expert),
                  pl.BlockSpec((None, 1, de, d), by_expert)],
        out_specs=pl.BlockSpec((TB * n_slab // 2, LANES), lambda i, be, nu: (jnp.minimum(i, nu[0] - 1), 0)),
        scratch_shapes=[pltpu.VMEM((d, 2 * de), bf16), pltpu.VMEM((de, d), bf16)],
    )
    return pl.pallas_call(
        functools.partial(_expert_kernel, n_slab),
        grid_spec=grid_spec,
        out_shape=jax.ShapeDtypeStruct(xs.shape, xs.dtype),
        input_output_aliases={2: 0},
        compiler_params=_params(("arbitrary",)),
        name="experts",
    )(blk_expert, n_used, xs, w_gate, w_up, w_down)


def _combine_kernel(n_slab, final, bounds, dest_ref, next_ref, x_ref, wgt_ref, g_ref, ys_ref, *rest):
    outs, (gbuf, sems) = rest[:-2], rest[-2:]
    i = pl.program_id(0)
    n = dest_ref.shape[2]
    tm = x_ref.shape[0]
    slot = i % 2
    n_word = n_slab // 2

    def gather(idx_ref, s):
        def copy(tok, k):
            src = ys_ref.at[pl.ds(pl.multiple_of(idx_ref[0, 0, tok * TOP_K + k] * n_word, n_word), n_word), :]
            dst = gbuf.at[s, pl.ds(pl.multiple_of((k * tm + tok) * n_word, n_word), n_word), :]
            return pltpu.make_async_copy(src, dst, sems.at[s])
        _issue_rows(n, copy)

    pl.when(i == 0)(lambda: gather(dest_ref, slot))
    pl.when(i + 1 < pl.num_programs(0))(lambda: gather(next_ref, 1 - slot))
    pltpu.make_async_copy(ys_ref.at[pl.ds(0, n * n_word), :], gbuf.at[slot], sems.at[slot]).wait()

    ys = []
    for k in range(TOP_K):
        halves = [_unpack_bf16_pair(gbuf[slot, pl.ds(k * tm * n_word + s, tm, stride=n_word), :])
                  for s in range(n_word)]
        y = jnp.concatenate([lo for lo, _ in halves] + [hi for _, hi in halves], axis=1)
        ys.append(y * wgt_ref[:, k:k + 1])
    x = x_ref[...] + (ys[0] + ys[1])
    if final:
        x = _rms(x, g_ref[...])

    def store(k):
        outs[k][...] = x

    _when_segment(i, bounds, store)


def _combine(x, ys, dest, wgt, g, n_slab, final, seg_rows):
    m, d = x.shape
    n_steps = m // TD
    dest3 = dest.reshape(n_steps, 1, TD * TOP_K)
    counts = [r // TD for r in seg_rows]
    bounds = [0]
    for cnt in counts:
        bounds.append(bounds[-1] + cnt)
    out_specs = [_seg_spec((TD, d), bounds[k], counts[k]) for k in range(len(seg_rows))]
    out_shape = [jax.ShapeDtypeStruct((r, d), f32) for r in seg_rows]
    idx_block = (1, 1, TD * TOP_K)
    return pl.pallas_call(
        functools.partial(_combine_kernel, n_slab, final, tuple(bounds)),
        grid=(n_steps,),
        in_specs=[pl.BlockSpec(idx_block, lambda i: (i, 0, 0), memory_space=pltpu.SMEM),
                  pl.BlockSpec(idx_block, lambda i: (jnp.minimum(i + 1, n_steps - 1), 0, 0),
                               memory_space=pltpu.SMEM),
                  pl.BlockSpec((TD, d), lambda i: (i, 0)),
                  pl.BlockSpec((TD, LANES), lambda i: (i, 0)),
                  pl.BlockSpec((1, d), lambda i: (0, 0)),
                  pl.BlockSpec(memory_space=pl.ANY)],
        out_specs=out_specs, out_shape=out_shape,
        scratch_shapes=[pltpu.VMEM((2, TD * TOP_K * n_slab // 2, LANES), jnp.uint32),
                        pltpu.SemaphoreType.DMA((2,))],
        compiler_params=_params(("arbitrary",)),
        name="combine",
    )(dest3, dest3, x, wgt, g, ys)


def kernel(x_prompt, x_sample, cache_a_k, cache_a_v, cache_b_k, cache_b_v, cache_b_logf, state_conv, norm_mix_g, w_in, b_in, rel_bias, conv_w, conv_b, conv_ln_g, conv_ln_b, w_proj_a, w_proj_b, w_proj_c, w_out, norm_ffn_g, w_router_group, b_router_group, w_router_expert, b_router_expert, w_e_gate, w_e_up, w_e_down, norm_final_g):
    nb_p, t_p, d = x_prompt.shape
    nb_s, t_s, _ = x_sample.shape
    depth = w_in.shape[0]
    past = cache_b_k.shape[2]
    a_rows = cache_a_k.shape[2]
    m_p, m_s = nb_p * t_p, nb_s * t_s
    m = m_p + m_s
    c_conv = d // 2
    n_slab = d // LANES
    tm = _row_tile(np.gcd(m_p, m_s), 512)
    tm_mix = _row_tile(np.gcd(m_p, m_s), 256)
    assert m_p % TD == 0 and m_s % TD == 0 and t_s % 16 == 0 and m_p % t_s == 0

    a_keep = min(WINDOW_A, t_p)
    x_segs = [x_prompt.reshape(m_p, d), x_sample.reshape(m_s, d)]
    kv_states = [jnp.zeros((depth, rows, W_HEADS), f32)
                 for rows in (nb_p * a_keep, nb_p * a_keep, m_p, m_p, m_s, m_s, m_s, m_s)]
    p_states, s_states = [], []
    for l in range(depth):
        (qa, ka16, va16, qb, kb16, vb16, u, gates, logf), kv_states = _inproj(
            x_segs, norm_mix_g[l][None, :], w_in[l], b_in[l], kv_states, l, m_p, t_p, a_keep, tm)

        ya_p = _band_attention_prompt(qa, ka16, va16, rel_bias[l], nb_p, t_p, 4 * CHUNK)
        kk = jnp.concatenate([cache_a_k[l].reshape(nb_s, a_rows, W_HEADS).astype(bf16),
                              ka16[m_p:].reshape(nb_s, t_s, W_HEADS)], axis=1).reshape(-1, W_HEADS)
        vv = jnp.concatenate([cache_a_v[l].reshape(nb_s, a_rows, W_HEADS).astype(bf16),
                              va16[m_p:].reshape(nb_s, t_s, W_HEADS)], axis=1).reshape(-1, W_HEADS)
        ya_s = _band_attention_sample(qa, kk, vv, rel_bias[l], nb_s, t_s, a_rows, m_p // t_s)

        logf_p = logf[:m_p].reshape(nb_p, t_p, N_HEADS)
        logf_s = logf[m_p:].reshape(nb_s, t_s, N_HEADS)
        cum_p = _cumsum_time(logf_p.transpose(0, 2, 1))
        f_p = cum_p.transpose(0, 2, 1).reshape(m_p, N_HEADS)
        yb_p = _fox_attention(_fox_expand("q", qb, f_p, m_p, tm), _fox_expand("k", kb16, f_p, m_p, tm),
                              _fox_expand("v", vb16, None, m_p, tm),
                              nb_p, t_p, t_p, _row_tile(t_p, 2048), _row_tile(t_p, 512))
        cum_s = _cumsum_time(jnp.concatenate([cache_b_logf[l].astype(f32), logf_s], axis=1).transpose(0, 2, 1))
        t_ks = past + t_s
        kk = jnp.concatenate([cache_b_k[l].reshape(nb_s, past, W_HEADS).astype(bf16),
                              kb16[m_p:].reshape(nb_s, t_s, W_HEADS)], axis=1).reshape(-1, W_HEADS)
        vv = jnp.concatenate([cache_b_v[l].reshape(nb_s, past, W_HEADS).astype(bf16),
                              vb16[m_p:].reshape(nb_s, t_s, W_HEADS)], axis=1).reshape(-1, W_HEADS)
        f_ks = cum_s.transpose(0, 2, 1)
        yb_s = _fox_attention(
            _fox_expand("q", qb, f_ks[:, past:].reshape(m_s, N_HEADS), m_s, t_s, blk0=m_p // t_s),
            _fox_expand("k", kk, f_ks.reshape(nb_s * t_ks, N_HEADS), nb_s * t_ks, t_ks),
            _fox_expand("v", vv, None, nb_s * t_ks, t_ks),
            nb_s, t_s, t_ks, t_s, t_ks)

        conv_args = (conv_w[l], conv_b[l][None, :], conv_ln_g[l][None, :], conv_ln_b[l][None, :])
        c_p = _conv_module(u, jnp.zeros((nb_p, CONV_HALO, c_conv), f32), *conv_args,
                           nb_p, t_p, _row_tile(t_p, 256), 0)
        init_s = jnp.pad(state_conv[l], ((0, 0), (CONV_HALO - (CONV_W - 1), 0), (0, 0)))
        c_s = _conv_module(u, init_s, *conv_args, nb_s, t_s, t_s, m_p // t_s)

        mixed = _mix(ya_p, ya_s, yb_p, yb_s, c_p, c_s, gates, w_proj_a[l].astype(bf16),
                     w_proj_b[l].astype(bf16), w_proj_c[l].astype(bf16), tm_mix)
        wr = jnp.pad(jnp.concatenate([w_router_group[l], w_router_expert[l]], axis=1),
                     ((0, 0), (0, LANES - N_GROUPS - N_EXPERTS)))
        wr_hi = wr.astype(bf16)
        wr_parts = jnp.concatenate([wr_hi, (wr - wr_hi.astype(f32)).astype(bf16)], axis=1)
        br = jnp.pad(jnp.concatenate([b_router_group[l], b_router_expert[l]]),
                     (0, LANES - N_GROUPS - N_EXPERTS))[None, :]
        x_mid, h2_slab, eid, wgt = _outproj(mixed, x_segs, w_out[l].astype(bf16), norm_ffn_g[l][None, :],
                                            wr_parts, br, tm_mix)

        dest, blk_expert, n_used, last_blk, n_blocks = _plan(eid[:, :TOP_K])
        xs = _dispatch(h2_slab, dest, last_blk, n_used, n_blocks, n_slab // 2, tm)
        ys = _experts(xs, blk_expert, n_used, w_e_gate, w_e_up, w_e_down, l, n_blocks, n_slab)
        final = l == depth - 1
        x_segs = _combine(x_mid, ys, dest, wgt, norm_final_g[None, :], n_slab, final,
                          (m_p, m_s) if final else (m,))

        n_cs = c_conv // LANES
        u_p = jnp.stack([u[((b + 1) * t_p - (CONV_W - 1)) * n_cs:(b + 1) * t_p * n_cs] for b in range(nb_p)])
        u_p = u_p.reshape(nb_p, CONV_W - 1, c_conv)
        u_s = u[m_p * n_cs:].reshape(nb_s, t_s, c_conv)
        p_states.append((logf_p, u_p))
        s_states.append((logf_s, jnp.concatenate([state_conv[l], u_s], axis=1)[:, -(CONV_W - 1):]))

    y_prompt = x_segs[0].reshape(nb_p, t_p, d)
    y_sample = x_segs[1].reshape(nb_s, t_s, d)
    stack = lambda states, k: jnp.stack([st[k] for st in states], axis=0)
    heads = lambda a, nb, t: a.reshape(depth, nb, t, N_HEADS, HEAD_DIM)
    ka_p, va_p, kb_p, vb_p, ka_s, va_s, kb_s, vb_s = kv_states
    return (y_prompt, y_sample,
            heads(ka_p, nb_p, a_keep), heads(va_p, nb_p, a_keep), heads(kb_p, nb_p, t_p), heads(vb_p, nb_p, t_p),
            stack(p_states, 0), stack(p_states, 1),
            heads(ka_s, nb_s, t_s), heads(va_s, nb_s, t_s), heads(kb_s, nb_s, t_s), heads(vb_s, nb_s, t_s),
            stack(s_states, 0), stack(s_states, 1))
```

```python
import functools

import jax
import jax.numpy as jnp
import numpy as np
from jax import lax
from jax.experimental import pallas as pl
from jax.experimental.pallas import tpu as pltpu

f32 = jnp.float32
bf16 = jnp.bfloat16

HEAD_DIM = 64
N_HEADS = 8
W_HEADS = N_HEADS * HEAD_DIM
CHUNK = 64
WINDOW_A = 8 * CHUNK
REL_CLIP = 128
CONV_W = 31
CONV_HALO = 32
N_GROUPS = 4
EXPERTS_PER_GROUP = 8
N_EXPERTS = N_GROUPS * EXPERTS_PER_GROUP
TOP_K = 2
SCALE = HEAD_DIM ** -0.5
EPS = 1e-6
NEG_INF = -1e30
LANES = 128
SUBLANES = 8
MIB = 1024 * 1024


def _params(sem, vmem_mib=48):
    return pltpu.CompilerParams(dimension_semantics=sem, vmem_limit_bytes=vmem_mib * MIB)


def _row_tile(m, cap):
    t = cap
    while m % t:
        t //= 2
    return t


def _sigmoid(z):
    return 0.5 * jnp.tanh(0.5 * z) + 0.5


def _rms(x, g):
    return x * lax.rsqrt(jnp.mean(x * x, axis=-1, keepdims=True) + EPS) * g


def _pack_bf16_pair(lo, hi):
    def rounded(x):
        bits = lax.bitcast_convert_type(x, jnp.uint32)
        return bits + jnp.uint32(0x7FFF) + ((bits >> 16) & jnp.uint32(1))
    return (rounded(hi) & jnp.uint32(0xFFFF0000)) | (rounded(lo) >> 16)


def _unpack_bf16_pair(word):
    lo = lax.bitcast_convert_type(word << 16, f32)
    hi = lax.bitcast_convert_type(word & jnp.uint32(0xFFFF0000), f32)
    return lo, hi


def _when_segment(i, bounds, fn):
    for k in range(len(bounds) - 1):
        pl.when((i >= bounds[k]) & (i < bounds[k + 1]))(functools.partial(fn, k))


def _seg_spec(block, start, count, width_axes=1):
    zeros = (0,) * width_axes
    return pl.BlockSpec(block, lambda i, *_: (jnp.clip(i - start, 0, count - 1),) + zeros)


TN = 1024
SEG_PER_TILE = TN // W_HEADS


def _norm_kernel(n_seg, bounds, *refs):
    xs = refs[:n_seg]
    g_ref, wf_ref, bf_ref, h_ref, logf = refs[n_seg:]

    def norm(k):
        h_ref[...] = _rms(xs[k][...], g_ref[...]).astype(bf16)

    _when_segment(pl.program_id(0), bounds, norm)
    zf = jnp.dot(h_ref[...], wf_ref[...], preferred_element_type=f32) + bf_ref[...]
    lf = jnp.minimum(zf, 0.0) - jnp.log1p(jnp.exp(-jnp.abs(zf)))
    logf[...] = lf[:, :N_HEADS]


def _norm(x_segs, g, w_f, b_f, tm):
    d = x_segs[0].shape[1]
    m = sum(a.shape[0] for a in x_segs)
    counts = [a.shape[0] // tm for a in x_segs]
    bounds = [0]
    for cnt in counts:
        bounds.append(bounds[-1] + cnt)
    const = lambda shape: pl.BlockSpec(shape, lambda i: (0, 0))
    return pl.pallas_call(
        functools.partial(_norm_kernel, len(x_segs), tuple(bounds)),
        grid=(m // tm,),
        in_specs=[_seg_spec((tm, d), bounds[k], counts[k]) for k in range(len(x_segs))] +
                 [const((1, d)), const((d, LANES)), const((1, LANES))],
        out_specs=[pl.BlockSpec((tm, d), lambda i: (i, 0)), pl.BlockSpec((tm, N_HEADS), lambda i: (i, 0))],
        out_shape=[jax.ShapeDtypeStruct((m, d), bf16), jax.ShapeDtypeStruct((m, N_HEADS), f32)],
        compiler_params=_params(("arbitrary",)),
        name="norm",
    )(*x_segs, g, w_f, b_f)


def _qkv_kernel(with_state, n_prompt, h_ref, w_ref, b_ref, *refs):
    outs = refs[2 * sum(with_state):]
    is_prompt = pl.program_id(0) < n_prompt
    for prompt_rows in (True, False):
        @pl.when(is_prompt if prompt_rows else ~is_prompt)
        def _(prompt_rows=prompt_rows):
            z = jnp.dot(h_ref[...], w_ref[...], preferred_element_type=f32) + b_ref[...]
            k = 0
            for n, has_state in enumerate(with_state):
                zn = z[:, n * W_HEADS:(n + 1) * W_HEADS]
                outs[k][...] = zn.astype(bf16)
                k += 1
                if has_state:
                    state = outs[k if prompt_rows else k + 1]
                    for hh in range(N_HEADS):
                        state[pl.ds(hh, zn.shape[0], stride=N_HEADS), :] = zn[:, hh * HEAD_DIM:(hh + 1) * HEAD_DIM]
                    k += 2


def _qkv_tile(h, w_qkv, b_qkv, tile, seg_states, layer, m_p, t_p, a_keep, tm):
    m, d = h.shape
    n_p, n_s = m_p // tm, (m - m_p) // tm
    per_seq, keep = t_p // tm, a_keep // tm

    def tail_rows(i):
        ip = jnp.minimum(i, n_p - 1)
        return (layer, (ip // per_seq) * keep + jnp.maximum(ip % per_seq - (per_seq - keep), 0), 0)

    state_block = (None, tm * N_HEADS, HEAD_DIM)
    tail_spec = pl.BlockSpec(state_block, tail_rows)
    prompt_spec = pl.BlockSpec(state_block, lambda i: (layer, jnp.minimum(i, n_p - 1), 0))
    sample_spec = pl.BlockSpec(state_block, lambda i: (layer, jnp.clip(i - n_p, 0, n_s - 1), 0))
    states, out_specs, out_shape, state_out_pos = [], [], [], []
    for seg in seg_states:
        out_specs.append(pl.BlockSpec((tm, W_HEADS), lambda i: (i, 0)))
        out_shape.append(jax.ShapeDtypeStruct((m, W_HEADS), bf16))
        if seg is not None:
            buf_p, buf_s, keep_tail = seg
            for buf, spec in ((buf_p, tail_spec if keep_tail else prompt_spec), (buf_s, sample_spec)):
                state_out_pos.append(len(out_specs))
                states.append(buf)
                out_specs.append(spec)
                out_shape.append(jax.ShapeDtypeStruct(buf.shape, buf.dtype))
    outs = pl.pallas_call(
        functools.partial(_qkv_kernel, tuple(seg is not None for seg in seg_states), n_p),
        grid=(m // tm,),
        in_specs=[pl.BlockSpec((tm, d), lambda i: (i, 0)),
                  pl.BlockSpec((d, TN), lambda i: (0, tile)),
                  pl.BlockSpec((1, TN), lambda i: (0, tile))] + [pl.BlockSpec(memory_space=pl.ANY)] * len(states),
        out_specs=out_specs, out_shape=out_shape,
        input_output_aliases={3 + k: pos for k, pos in enumerate(state_out_pos)},
        compiler_params=_params(("arbitrary",)),
        name="qkv",
    )(h, w_qkv, b_qkv, *states)
    copies = [o for k, o in enumerate(outs) if k not in state_out_pos]
    return copies, [outs[pos] for pos in state_out_pos]


def _glu_kernel(n_slab, h_ref, w_ref, b_ref, u_ref):
    tm = h_ref.shape[0]
    c = n_slab * LANES
    z = jnp.dot(h_ref[...], w_ref[...], preferred_element_type=f32) + b_ref[...]
    glu = z[:, :c] * _sigmoid(z[:, c:])
    for s in range(n_slab):
        u_ref[pl.ds(s, tm, stride=n_slab), :] = glu[:, s * LANES:(s + 1) * LANES]


def _glu(h, w, b, tm):
    m, d = h.shape
    c = w.shape[1] // 2
    n_slab = c // LANES
    return pl.pallas_call(
        functools.partial(_glu_kernel, n_slab),
        grid=(m // tm,),
        in_specs=[pl.BlockSpec((tm, d), lambda i: (i, 0)), pl.BlockSpec((d, 2 * c), lambda i: (0, 0)),
                  pl.BlockSpec((1, 2 * c), lambda i: (0, 0))],
        out_specs=pl.BlockSpec((tm * n_slab, LANES), lambda i: (i, 0)),
        out_shape=jax.ShapeDtypeStruct((m * n_slab, LANES), f32),
        compiler_params=_params(("arbitrary",)),
        name="glu",
    )(h, w, b)


def _gates_kernel(h_ref, w_ref, b_ref, o_ref):
    z = jnp.dot(h_ref[...], w_ref[...], preferred_element_type=f32) + b_ref[...]
    o_ref[...] = _sigmoid(z).astype(o_ref.dtype)


def _gates(h, w, b, tm):
    m, d = h.shape
    n = w.shape[1]
    return pl.pallas_call(
        _gates_kernel,
        grid=(n // TN, m // tm),
        in_specs=[pl.BlockSpec((tm, d), lambda j, i: (i, 0)), pl.BlockSpec((d, TN), lambda j, i: (0, j)),
                  pl.BlockSpec((1, TN), lambda j, i: (0, j))],
        out_specs=pl.BlockSpec((tm, TN), lambda j, i: (i, j)),
        out_shape=jax.ShapeDtypeStruct((m, n), bf16),
        compiler_params=_params(("arbitrary", "arbitrary")),
        name="gates",
    )(h, w, b)


def _inproj(x_segs, g, w_in, b_in, states, layer, m_p, t_p, a_keep, tm):
    d = x_segs[0].shape[1]
    c_conv = d // 2
    n_qkv = 6 * W_HEADS
    f0 = n_qkv
    c0 = f0 + N_HEADS
    g0 = c0 + 2 * c_conv
    cast = lambda a: a.astype(bf16)
    row = lambda a: a[None, :].astype(f32)
    w_f = jnp.pad(w_in[:, f0:c0], ((0, 0), (0, LANES - N_HEADS))).astype(bf16)
    b_f = jnp.pad(b_in[f0:c0], (0, LANES - N_HEADS))[None, :].astype(f32)
    h, logf = _norm(x_segs, g, w_f, b_f, tm)

    ka_p, va_p, kb_p, vb_p, ka_s, va_s, kb_s, vb_s = states
    w_qkv, b_qkv = cast(w_in[:, :n_qkv]), row(b_in[:n_qkv])
    tile = functools.partial(_qkv_tile, h, w_qkv, b_qkv, layer=layer, m_p=m_p, t_p=t_p, a_keep=a_keep, tm=tm)
    assert SEG_PER_TILE == 2
    (qa, ka16), (ka_p, ka_s) = tile(0, [None, (ka_p, ka_s, True)])
    (va16, qb), (va_p, va_s) = tile(1, [(va_p, va_s, True), None])
    (kb16, vb16), (kb_p, kb_s, vb_p, vb_s) = tile(2, [(kb_p, kb_s, False), (vb_p, vb_s, False)])
    u = _glu(h, cast(w_in[:, c0:g0]), row(b_in[c0:g0]), tm)
    gates = _gates(h, cast(w_in[:, g0:]), row(b_in[g0:]), tm)
    return (qa, ka16, va16, qb, kb16, vb16, u, gates, logf), [ka_p, va_p, kb_p, vb_p, ka_s, va_s, kb_s, vb_s]


TC = 512


def _cumsum_kernel(x_ref, o_ref, carry):
    @pl.when(pl.program_id(1) == 0)
    def _():
        carry[...] = jnp.zeros_like(carry)

    blk = x_ref[0]
    r = lax.broadcasted_iota(jnp.int32, (TC, TC), 0)
    c = lax.broadcasted_iota(jnp.int32, (TC, TC), 1)
    tri = jnp.where(r <= c, 1.0, 0.0).astype(bf16)
    cs = carry[:, 0:1]
    rest = blk
    for _ in range(F_PARTS):
        part = rest.astype(bf16)
        rest = rest - part.astype(f32)
        cs = cs + jnp.dot(part, tri, preferred_element_type=f32)
    o_ref[0] = cs
    carry[...] = jnp.broadcast_to(cs[:, TC - 1:TC], carry.shape)


def _cumsum_time(x):
    nb, h, t = x.shape
    tp = -(-t // TC) * TC
    xp = jnp.pad(x, ((0, 0), (0, 0), (0, tp - t)))
    out = pl.pallas_call(
        _cumsum_kernel,
        grid=(nb, tp // TC),
        in_specs=[pl.BlockSpec((1, h, TC), lambda b, k: (b, 0, k))],
        out_specs=pl.BlockSpec((1, h, TC), lambda b, k: (b, 0, k)),
        out_shape=jax.ShapeDtypeStruct((nb, h, tp), f32),
        scratch_shapes=[pltpu.VMEM((h, LANES), f32)],
        compiler_params=_params(("arbitrary", "arbitrary")),
        name="cumsum",
    )(xp)
    return out[:, :, :t]


F_PARTS = 3


def _spare_base(h):
    return HEAD_DIM * (1 - h % 2)


def _fox_expand_kernel(kind, x_ref, f_ref, place_ref, o_ref):
    tm = x_ref.shape[0]
    lane = lax.broadcasted_iota(jnp.int32, (tm, LANES), 1)
    if kind != "v":
        rest = f_ref[...]
        stack = jnp.where(lane < (F_PARTS + 1) * N_HEADS, 1.0, 0.0)
        for n in range(F_PARTS):
            part = rest.astype(bf16).astype(f32)
            rest = rest - part
            stack = jnp.where((lane >= n * N_HEADS) & (lane < (n + 1) * N_HEADS), part, stack)
        spare_all = jnp.dot(stack.astype(bf16), place_ref[...], preferred_element_type=f32)
    for h in range(N_HEADS):
        pair = x_ref[:, (h // 2) * LANES:(h // 2 + 1) * LANES].astype(f32)
        if kind == "q":
            pair = pair * SCALE
        if kind == "v":
            spare = jnp.where(lane == _spare_base(h), 1.0, 0.0)
        else:
            spare = spare_all[:, h * LANES:(h + 1) * LANES]
        own = (lane < HEAD_DIM) if h % 2 == 0 else (lane >= HEAD_DIM)
        o_ref[h] = jnp.where(own, pair, spare).astype(o_ref.dtype)


def _fox_placement(kind):
    place = np.zeros((LANES, N_HEADS * LANES), np.float32)
    for h in range(N_HEADS):
        base = h * LANES + _spare_base(h)
        for n in range(F_PARTS):
            if kind == "q":
                place[n * N_HEADS + h, base + n] = 1.0
                place[F_PARTS * N_HEADS + h, base + F_PARTS + n] = 1.0
            else:
                place[F_PARTS * N_HEADS + h, base + n] = 1.0
                place[n * N_HEADS + h, base + F_PARTS + n] = -1.0
    return jnp.asarray(place, bf16)


def _fox_expand(kind, x, f, rows, tm, blk0=0):
    f_lanes = jnp.zeros((rows, LANES), f32) if f is None else jnp.tile(f, (1, LANES // N_HEADS))
    return pl.pallas_call(
        functools.partial(_fox_expand_kernel, kind),
        grid=(rows // tm,),
        in_specs=[pl.BlockSpec((tm, W_HEADS), lambda i: (blk0 + i, 0)),
                  pl.BlockSpec((tm, LANES), lambda i: (i, 0)),
                  pl.BlockSpec((LANES, N_HEADS * LANES), lambda i: (0, 0))],
        out_specs=pl.BlockSpec((N_HEADS, tm, LANES), lambda i: (0, i, 0)),
        out_shape=jax.ShapeDtypeStruct((N_HEADS, rows, LANES), bf16),
        compiler_params=_params(("arbitrary",)),
        name="fox_expand_" + kind,
    )(x, f_lanes, _fox_placement(kind))


FOX_HEADS_PER_TRIP = 2


def _fox_kernel(tq, tk, off, q_ref, k_ref, v_ref, o_ref, m_scr, acc_scr):
    i = pl.program_id(1)
    j = pl.program_id(2)

    @pl.when(j == 0)
    def _():
        m_scr[...] = jnp.full_like(m_scr, NEG_INF)
        acc_scr[...] = jnp.zeros_like(acc_scr)

    q_first = i * tq + off
    q_last = q_first + tq - 1
    k_first = j * tk
    k_last = k_first + tk - 1

    def body(row0, masked):
        rows = slice(row0, tq)
        if masked:
            kpos = k_first + lax.broadcasted_iota(jnp.int32, (tq - row0, tk), 1)
            qpos = q_first + row0 + lax.broadcasted_iota(jnp.int32, (tq - row0, tk), 0)
            vis = kpos <= qpos

        def head(h):
            s = lax.dot_general(q_ref[h, rows], k_ref[h], (((1,), (1,)), ((), ())), preferred_element_type=f32)
            if masked:
                s = jnp.where(vis, s, NEG_INF)
            m_old = m_scr[h, rows]
            m_new = jnp.maximum(m_old, jnp.max(s, axis=-1, keepdims=True))
            pr = jnp.exp(s - m_new[:, 0:1])
            pv = jnp.dot(pr.astype(bf16), v_ref[h], preferred_element_type=f32)
            acc_scr[h, rows] = jnp.exp(m_old - m_new) * acc_scr[h, rows] + pv
            m_scr[h, rows] = m_new

        def trip(g, carry):
            for n in range(FOX_HEADS_PER_TRIP):
                head(g * FOX_HEADS_PER_TRIP + n)
            return carry

        lax.fori_loop(0, N_HEADS // FOX_HEADS_PER_TRIP, trip, 0)

    pl.when(k_last <= q_first)(functools.partial(body, 0, False))
    if tq % tk == 0 and off % tk == 0:
        for c in range(tq // tk):
            pl.when(k_first == q_first + c * tk)(functools.partial(body, c * tk, True))
    else:
        pl.when((k_first <= q_last) & (k_last > q_first))(functools.partial(body, 0, True))

    @pl.when(j == pl.num_programs(2) - 1)
    def _():
        lane = lax.broadcasted_iota(jnp.int32, (tq, LANES), 1)
        for p in range(N_HEADS // 2):
            even = acc_scr[2 * p]
            odd = acc_scr[2 * p + 1]
            even = even / even[:, _spare_base(0):_spare_base(0) + 1]
            odd = odd / odd[:, _spare_base(1):_spare_base(1) + 1]
            o_ref[:, p * LANES:(p + 1) * LANES] = jnp.where(lane < HEAD_DIM, even, odd).astype(o_ref.dtype)


def _fox_attention(q, k, v, nb, t_q, t_k, tq, tk):
    nq, nk = t_q // tq, t_k // tk
    off = t_k - t_q

    def last_k(i):
        return jnp.minimum((i * tq + tq - 1 + off) // tk, nk - 1)

    kv_spec = pl.BlockSpec((N_HEADS, tk, LANES), lambda b, i, j: (0, b * nk + jnp.minimum(j, last_k(i)), 0))
    return pl.pallas_call(
        functools.partial(_fox_kernel, tq, tk, off),
        grid=(nb, nq, nk),
        in_specs=[pl.BlockSpec((N_HEADS, tq, LANES), lambda b, i, j: (0, b * nq + i, 0)), kv_spec, kv_spec],
        out_specs=pl.BlockSpec((tq, W_HEADS), lambda b, i, j: (b * nq + i, 0)),
        out_shape=jax.ShapeDtypeStruct((nb * t_q, W_HEADS), bf16),
        scratch_shapes=[pltpu.VMEM((N_HEADS, tq, LANES), f32), pltpu.VMEM((N_HEADS, tq, LANES), f32)],
        compiler_params=_params(("arbitrary", "arbitrary", "arbitrary")),
        name="fox",
    )(q, k, v)


def _band_kernel(rows, gq, wk, has_prev, *refs):
    if has_prev:
        q_ref, kp_ref, kc_ref, vp_ref, vc_ref, bias_ref, o_ref, k_scr, v_scr = refs
        k_scr[0:WINDOW_A] = kp_ref[...]
        k_scr[WINDOW_A:WINDOW_A + rows] = kc_ref[...]
        v_scr[0:WINDOW_A] = vp_ref[...]
        v_scr[WINDOW_A:WINDOW_A + rows] = vc_ref[...]
        k_src, v_src = k_scr, v_scr
    else:
        q_ref, k_src, v_src, bias_ref, o_ref = refs
    i = pl.program_id(1)
    lane = lax.broadcasted_iota(jnp.int32, (gq, LANES), 1)
    low = lane < HEAD_DIM

    def attend(before_start):
        for g in range(rows // gq):
            r0 = g * gq
            if before_start:
                key_pos = (i - 1) * WINDOW_A + r0 + lax.broadcasted_iota(jnp.int32, (gq, wk), 1)
                vis = key_pos >= 0
            for p in range(N_HEADS // 2):
                cols = slice(p * LANES, (p + 1) * LANES)
                q2 = q_ref[r0:r0 + gq, cols] * SCALE
                kw = k_src[r0:r0 + wk, cols]
                vw = v_src[r0:r0 + wk, cols]
                outs = []
                for half in range(2):
                    h = 2 * p + half
                    qm = jnp.where(low if half == 0 else ~low, q2, jnp.zeros_like(q2))
                    s = lax.dot_general(qm, kw, (((1,), (1,)), ((), ())), preferred_element_type=f32)
                    s = s + bias_ref[h]
                    if before_start:
                        s = jnp.where(vis, s, NEG_INF)
                    m = jnp.max(s, axis=-1, keepdims=True)
                    pr = jnp.exp(s - m)
                    l = jnp.sum(pr, axis=-1, keepdims=True)
                    pv = jnp.dot(pr.astype(bf16), vw, preferred_element_type=f32)
                    outs.append(pv / l)
                o_ref[r0:r0 + gq, cols] = jnp.where(low, outs[0], outs[1]).astype(o_ref.dtype)

    if has_prev:
        pl.when(i == 0)(functools.partial(attend, True))
        pl.when(i > 0)(functools.partial(attend, False))
    else:
        attend(False)


def _rel_bias_table(rel_bias, gq, wk, q_shift):
    period = wk + gq
    j = np.arange(period)
    k = np.where(j < wk, j, j - period)
    line = rel_bias.astype(f32)[:, np.clip(q_shift - k, -REL_CLIP, REL_CLIP) + REL_CLIP]
    tiled = jnp.tile(line, (1, gq))[:, :gq * (period - 1)]
    return tiled.reshape(-1, gq, period - 1)[:, :, :wk]


def _band_bias(rel_bias, gq, wk, q_shift):
    r = np.arange(gq)[:, None]
    s = np.arange(wk)[None, :]
    band0 = (r // CHUNK) * CHUNK + q_shift - WINDOW_A
    ok = (s >= band0) & (s < band0 + WINDOW_A + CHUNK)
    return jnp.where(ok[None], _rel_bias_table(rel_bias, gq, wk, q_shift), NEG_INF)


def _band_attention_prompt(q, k, v, rel_bias, nb, t, gq):
    rows = WINDOW_A
    wk = WINDOW_A + gq
    n_steps = t // rows
    bias = _band_bias(rel_bias, gq, wk, WINDOW_A)
    cur = pl.BlockSpec((rows, W_HEADS), lambda b, i: (b * n_steps + i, 0))
    prev = pl.BlockSpec((rows, W_HEADS), lambda b, i: (b * n_steps + jnp.maximum(i - 1, 0), 0))
    return pl.pallas_call(
        functools.partial(_band_kernel, rows, gq, wk, True),
        grid=(nb, n_steps),
        in_specs=[cur, prev, cur, prev, cur,
                  pl.BlockSpec((N_HEADS, gq, wk), lambda b, i: (0, 0, 0))],
        out_specs=cur,
        out_shape=jax.ShapeDtypeStruct((nb * t, W_HEADS), bf16),
        scratch_shapes=[pltpu.VMEM((2 * rows, W_HEADS), bf16), pltpu.VMEM((2 * rows, W_HEADS), bf16)],
        compiler_params=_params(("arbitrary", "arbitrary")),
        name="band_prompt",
    )(q, k, k, v, v, bias)


def _band_attention_sample(q, kk, vv, rel_bias, nb, s_new, l_cache, q_blk0):
    wk = l_cache + s_new
    bias = _rel_bias_table(rel_bias, s_new, wk, l_cache)
    return pl.pallas_call(
        functools.partial(_band_kernel, s_new, s_new, wk, False),
        grid=(nb, 1),
        in_specs=[pl.BlockSpec((s_new, W_HEADS), lambda b, i: (q_blk0 + b, 0)),
                  pl.BlockSpec((wk, W_HEADS), lambda b, i: (b, 0)),
                  pl.BlockSpec((wk, W_HEADS), lambda b, i: (b, 0)),
                  pl.BlockSpec((N_HEADS, s_new, wk), lambda b, i: (0, 0, 0))],
        out_specs=pl.BlockSpec((s_new, W_HEADS), lambda b, i: (b, 0)),
        out_shape=jax.ShapeDtypeStruct((nb * s_new, W_HEADS), bf16),
        compiler_params=_params(("arbitrary", "arbitrary")),
        name="band_sample",
    )(q, kk, vv, bias)


CONV_STRIP = 32


def _conv_kernel(tt, n_slab, init_ref, u_ref, w_ref, cb_ref, g_ref, b_ref, o_ref, ubuf, acc_scr):
    halo = CONV_HALO * n_slab

    @pl.when(pl.program_id(1) == 0)
    def _():
        ubuf[0:halo] = init_ref[0]

    ubuf[halo:halo + tt * n_slab] = u_ref[...]
    rs = min(CONV_STRIP, tt)
    first = CONV_HALO - (CONV_W - 1)

    def per_step(slab):
        return jnp.broadcast_to(slab[None], (rs, n_slab, LANES)).reshape(rs * n_slab, LANES)

    for s in range(tt // rs):
        acc = per_step(cb_ref[...])
        for j in range(CONV_W):
            r0 = (s * rs + first + j) * n_slab
            acc = acc + per_step(w_ref[j * n_slab:(j + 1) * n_slab, :]) * ubuf[r0:r0 + rs * n_slab, :]
        acc_scr[...] = acc
        rows = jnp.concatenate([acc_scr[pl.ds(q, rs, stride=n_slab), :] for q in range(n_slab)], axis=1)
        mu = jnp.mean(rows, axis=-1, keepdims=True)
        cen = rows - mu
        var = jnp.mean(cen * cen, axis=-1, keepdims=True)
        y = cen * lax.rsqrt(var + EPS) * g_ref[...] + b_ref[...]
        o_ref[s * rs:(s + 1) * rs, :] = (y * _sigmoid(y)).astype(o_ref.dtype)
    if tt >= CONV_HALO:
        ubuf[0:halo] = ubuf[tt * n_slab:tt * n_slab + halo]


def _conv_module(u_slab, init, conv_w, conv_b, ln_g, ln_b, nb, t, tt, blk0):
    c = conv_w.shape[1]
    n_slab = c // LANES
    n_t = t // tt
    vec = pl.BlockSpec((1, c), lambda b, i: (0, 0))
    return pl.pallas_call(
        functools.partial(_conv_kernel, tt, n_slab),
        grid=(nb, n_t),
        in_specs=[pl.BlockSpec((1, CONV_HALO * n_slab, LANES), lambda b, i: (b, 0, 0)),
                  pl.BlockSpec((tt * n_slab, LANES), lambda b, i: (blk0 + b * n_t + i, 0)),
                  pl.BlockSpec((CONV_W * n_slab, LANES), lambda b, i: (0, 0)),
                  pl.BlockSpec((n_slab, LANES), lambda b, i: (0, 0)), vec, vec],
        out_specs=pl.BlockSpec((tt, c), lambda b, i: (b * n_t + i, 0)),
        out_shape=jax.ShapeDtypeStruct((nb * t, c), bf16),
        scratch_shapes=[pltpu.VMEM(((CONV_HALO + tt) * n_slab, LANES), f32),
                        pltpu.VMEM((min(CONV_STRIP, tt) * n_slab, LANES), f32)],
        compiler_params=_params(("arbitrary", "arbitrary")),
        name="conv",
    )(init.reshape(nb, CONV_HALO * n_slab, LANES), u_slab, conv_w.reshape(CONV_W * n_slab, LANES),
      conv_b.reshape(n_slab, LANES), ln_g, ln_b)


def _mix_kernel(bounds, ya_p, ya_s, yb_p, yb_s, c_p, c_s, gates, pa, pb, pc, o_ref):
    d = o_ref.shape[1]

    def go(k):
        ya, yb, c = ((ya_p, yb_p, c_p), (ya_s, yb_s, c_s))[k]
        a = jnp.dot(ya[...], pa[...], preferred_element_type=f32)
        mixed = gates[:, 0:d].astype(f32) * a
        b = jnp.dot(yb[...], pb[...], preferred_element_type=f32)
        mixed = mixed + gates[:, d:2 * d].astype(f32) * b
        cc = jnp.dot(c[...], pc[...], preferred_element_type=f32)
        mixed = mixed + gates[:, 2 * d:3 * d].astype(f32) * cc
        o_ref[...] = mixed.astype(o_ref.dtype)

    _when_segment(pl.program_id(0), bounds, go)


def _mix(ya_p, ya_s, yb_p, yb_s, c_p, c_s, gates, pa, pb, pc, tm):
    m, d3 = gates.shape
    d = d3 // 3
    n_p, n_s = ya_p.shape[0] // tm, ya_s.shape[0] // tm
    bounds = (0, n_p, n_p + n_s)
    c_conv = c_p.shape[1]
    const = lambda shape: pl.BlockSpec(shape, lambda i: (0, 0))
    return pl.pallas_call(
        functools.partial(_mix_kernel, bounds),
        grid=(m // tm,),
        in_specs=[_seg_spec((tm, W_HEADS), 0, n_p), _seg_spec((tm, W_HEADS), n_p, n_s),
                  _seg_spec((tm, W_HEADS), 0, n_p), _seg_spec((tm, W_HEADS), n_p, n_s),
                  _seg_spec((tm, c_conv), 0, n_p), _seg_spec((tm, c_conv), n_p, n_s),
                  pl.BlockSpec((tm, d3), lambda i: (i, 0)),
                  const((W_HEADS, d)), const((W_HEADS, d)), const((c_conv, d))],
        out_specs=pl.BlockSpec((tm, d), lambda i: (i, 0)),
        out_shape=jax.ShapeDtypeStruct((m, d), bf16),
        compiler_params=_params(("arbitrary",)),
        name="mix",
    )(ya_p, ya_s, yb_p, yb_s, c_p, c_s, gates, pa, pb, pc)


def _route(logits):
    shape = logits.shape
    lane = lax.broadcasted_iota(jnp.int32, shape, 1)
    lane_f = lane.astype(f32)
    big = float(LANES)
    gl = jnp.where(lane < N_GROUPS, logits, -jnp.inf)
    g_max = jnp.max(gl, axis=-1, keepdims=True)
    g_idx = jnp.min(jnp.where(gl == g_max, lane_f, big), axis=-1, keepdims=True)
    g_sum = jnp.sum(jnp.exp(gl - g_max), axis=-1, keepdims=True)
    g_w = 1.0 / g_sum
    lo = N_GROUPS + g_idx * EXPERTS_PER_GROUP
    el = jnp.where((lane_f >= lo) & (lane_f < lo + EXPERTS_PER_GROUP), logits, -jnp.inf)
    m1 = jnp.max(el, axis=-1, keepdims=True)
    i1 = jnp.min(jnp.where(el == m1, lane_f, big), axis=-1, keepdims=True)
    el2 = jnp.where(lane_f == i1, -jnp.inf, el)
    m2 = jnp.max(el2, axis=-1, keepdims=True)
    i2 = jnp.min(jnp.where(el2 == m2, lane_f, big), axis=-1, keepdims=True)
    e21 = jnp.exp(m2 - m1)
    den = 1.0 + e21
    w1 = g_w * (1.0 / den)
    w2 = g_w * (e21 / den)
    eid = jnp.where(lane == 0, i1 - N_GROUPS, jnp.where(lane == 1, i2 - N_GROUPS, 0.0)).astype(jnp.int32)
    wgt = jnp.where(lane == 0, w1, jnp.where(lane == 1, w2, 0.0))
    return eid, wgt


def _outproj_kernel(n_slab, n_seg, bounds, *refs):
    xs = refs[:n_seg]
    mixed, wo, g2, wr, br, xo, h2o, eid_o, wgt_o = refs[n_seg:]
    tm = mixed.shape[0]
    i = pl.program_id(0)
    x = xs[0][...]
    for k in range(1, n_seg):
        x = jnp.where(i >= bounds[k], xs[k][...], x)
    xn = x + jnp.dot(mixed[...], wo[...], preferred_element_type=f32)
    xo[...] = xn
    h2 = _rms(xn, g2[...])
    n_word = n_slab // 2
    for s in range(n_word):
        h2o[pl.ds(s, tm, stride=n_word), :] = _pack_bf16_pair(h2[:, s * LANES:(s + 1) * LANES],
                                                              h2[:, (n_word + s) * LANES:(n_word + s + 1) * LANES])
    h_hi = h2.astype(bf16)
    h_lo = (h2 - h_hi.astype(f32)).astype(bf16)
    hi = jnp.dot(h_hi, wr[...], preferred_element_type=f32)
    lo = jnp.dot(h_lo, wr[:, :LANES], preferred_element_type=f32)
    logits = hi[:, :LANES] + (hi[:, LANES:] + lo) + br[...]
    eid, wgt = _route(logits)
    eid_o[...] = eid
    wgt_o[...] = wgt


def _outproj(mixed, x_segs, wo, g2, wr, br, tm):
    m, d = mixed.shape
    n_slab = d // LANES
    counts = [a.shape[0] // tm for a in x_segs]
    bounds = [0]
    for cnt in counts:
        bounds.append(bounds[-1] + cnt)
    const = lambda shape: pl.BlockSpec(shape, lambda i: (0, 0))
    row = lambda w: pl.BlockSpec((tm, w), lambda i: (i, 0))
    sds = jax.ShapeDtypeStruct
    return pl.pallas_call(
        functools.partial(_outproj_kernel, n_slab, len(x_segs), tuple(bounds)),
        grid=(m // tm,),
        in_specs=[_seg_spec((tm, d), bounds[k], counts[k]) for k in range(len(x_segs))] +
                 [row(d), const((d, d)), const((1, d)), const((d, 2 * LANES)), const((1, LANES))],
        out_specs=[row(d), pl.BlockSpec((tm * n_slab // 2, LANES), lambda i: (i, 0)), row(LANES), row(LANES)],
        out_shape=[sds((m, d), f32), sds((m * n_slab // 2, LANES), jnp.uint32), sds((m, LANES), jnp.int32),
                   sds((m, LANES), f32)],
        compiler_params=_params(("arbitrary",)),
        name="outproj",
    )(*x_segs, mixed, wo, g2, wr, br)


TB = 256
TD = 256


def _plan(eid):
    flat_e = eid.reshape(-1)
    n_assign = flat_e.shape[0]
    onehot = (flat_e[:, None] == jnp.arange(N_EXPERTS, dtype=jnp.int32)[None, :]).astype(jnp.int32)
    csum = jnp.cumsum(onehot, axis=0)
    counts = csum[-1]
    rank = jnp.sum(onehot * csum, axis=1) - 1
    n_blk_e = (counts + TB - 1) // TB
    blk_end = jnp.cumsum(n_blk_e)
    blk_start = blk_end - n_blk_e
    dest = blk_start[flat_e] * TB + rank
    n_blocks = -(-n_assign // TB) + N_EXPERTS
    blk_ids = jnp.arange(n_blocks, dtype=jnp.int32)
    blk_expert = jnp.minimum(jnp.sum((blk_end[None, :] <= blk_ids[:, None]).astype(jnp.int32), axis=1),
                             N_EXPERTS - 1)
    last_blk = jnp.where(n_blk_e > 0, blk_end - 1, -1).astype(jnp.int32)
    return dest.astype(jnp.int32), blk_expert, blk_end[-1:].astype(jnp.int32), last_blk, n_blocks


DMA_UNROLL = 8


def _issue_rows(n, copy):
    per_trip = DMA_UNROLL // TOP_K

    def trip(t, carry):
        for r in range(per_trip):
            for k in range(TOP_K):
                copy(t * per_trip + r, k).start()
        return carry

    lax.fori_loop(0, n // DMA_UNROLL, trip, 0)


def _dispatch_kernel(n_slab, n_blocks, dest_ref, last_ref, nu_ref, h_ref, xs_ref, zero_scr, stage, sems, zero_sem):
    i = pl.program_id(0)
    last = pl.num_programs(0) - 1
    n = dest_ref.shape[2]
    blk_rows = TB * n_slab
    slot = i % 2

    def wait_slot(s):
        for _ in range(TOP_K):
            pltpu.make_async_copy(stage.at[s], xs_ref.at[pl.ds(0, (n // TOP_K) * n_slab), :], sems.at[s]).wait()

    @pl.when(i == 0)
    def _():
        zero_scr[...] = jnp.zeros_like(zero_scr)

        def zero_block(b):
            rows = pl.ds(pl.multiple_of(b * blk_rows, blk_rows), blk_rows)
            return pltpu.make_async_copy(zero_scr, xs_ref.at[rows, :], zero_sem)

        def over_blocks(act):
            for e in range(N_EXPERTS):
                pl.when(last_ref[e] >= 0)(lambda e=e: act(zero_block(last_ref[e])))
            lax.fori_loop(nu_ref[0], n_blocks, lambda b, c: (act(zero_block(b)), c)[1], 0)

        over_blocks(lambda cp: cp.start())
        over_blocks(lambda cp: cp.wait())

    pl.when(i >= 2)(lambda: wait_slot(slot))
    stage[slot] = h_ref[...]

    def copy(tok, k):
        src = stage.at[slot, pl.ds(pl.multiple_of(tok * n_slab, n_slab), n_slab), :]
        dst = xs_ref.at[pl.ds(pl.multiple_of(dest_ref[0, 0, tok * TOP_K + k] * n_slab, n_slab), n_slab), :]
        return pltpu.make_async_copy(src, dst, sems.at[slot])

    _issue_rows(n, copy)

    @pl.when(i == last)
    def _():
        pl.when(i >= 1)(lambda: wait_slot(1 - slot))
        wait_slot(slot)


def _dispatch(h2_slab, dest, last_blk, n_used, n_blocks, n_slab, td):
    m = h2_slab.shape[0] // n_slab
    n_steps = m // td
    dest3 = dest.reshape(n_steps, 1, td * TOP_K)
    smem = pl.BlockSpec(memory_space=pltpu.SMEM)
    return pl.pallas_call(
        functools.partial(_dispatch_kernel, n_slab, n_blocks),
        grid=(n_steps,),
        in_specs=[pl.BlockSpec((1, 1, td * TOP_K), lambda i: (i, 0, 0), memory_space=pltpu.SMEM),
                  smem, smem,
                  pl.BlockSpec((td * n_slab, LANES), lambda i: (i, 0))],
        out_specs=pl.BlockSpec(memory_space=pl.ANY),
        out_shape=jax.ShapeDtypeStruct((n_blocks * TB * n_slab, LANES), h2_slab.dtype),
        scratch_shapes=[pltpu.VMEM((TB * n_slab, LANES), h2_slab.dtype),
                        pltpu.VMEM((2, td * n_slab, LANES), h2_slab.dtype),
                        pltpu.SemaphoreType.DMA((2,)), pltpu.SemaphoreType.DMA(())],
        compiler_params=_params(("arbitrary",)),
        name="dispatch",
    )(dest3, last_blk, n_used, h2_slab)


def _expert_kernel(n_slab, be_ref, nu_ref, xs_ref, wg_ref, wu_ref, wd_ref, o_ref, wgu_scr, wd_scr):
    i = pl.program_id(0)
    de = wd_ref.shape[1]

    @pl.when(i < nu_ref[0])
    def _():
        @pl.when((i == 0) | (be_ref[i] != be_ref[jnp.maximum(i - 1, 0)]))
        def _():
            wgu_scr[:, :de] = wg_ref[0].astype(bf16)
            wgu_scr[:, de:] = wu_ref[0].astype(bf16)
            wd_scr[...] = wd_ref[0].astype(bf16)

        n_word = n_slab // 2
        halves = [_unpack_bf16_pair(xs_ref[pl.ds(s, TB, stride=n_word), :]) for s in range(n_word)]
        x = jnp.concatenate([lo for lo, _ in halves] + [hi for _, hi in halves], axis=1)
        gu = jnp.dot(x.astype(bf16), wgu_scr[...], preferred_element_type=f32)
        g = gu[:, :de]
        hmid = (g * _sigmoid(g)) * gu[:, de:]
        y = jnp.dot(hmid.astype(bf16), wd_scr[...], preferred_element_type=f32)
        for s in range(n_word):
            o_ref[pl.ds(s, TB, stride=n_word), :] = _pack_bf16_pair(y[:, s * LANES:(s + 1) * LANES],
                                                                   y[:, (n_word + s) * LANES:(n_word + s + 1) * LANES])


def _experts(xs, blk_expert, n_used, w_gate, w_up, w_down, layer, n_blocks, n_slab):
    d, de = w_gate.shape[2], w_gate.shape[3]
    by_expert = lambda i, be, nu: (layer, be[jnp.minimum(i, nu[0] - 1)], 0, 0)
    grid_spec = pltpu.PrefetchScalarGridSpec(
        num_scalar_prefetch=2,
        grid=(n_blocks,),
        in_specs=[pl.BlockSpec((TB * n_slab // 2, LANES), lambda i, be, nu: (jnp.minimum(i, nu[0] - 1), 0)),
                  pl.BlockSpec((None, 1, d, de), by_expert), pl.BlockSpec((None, 1, d, de), by_expert),
                  pl.BlockSpec((None, 1, de, d), by_expert)],
        out_specs=pl.BlockSpec((TB * n_slab // 2, LANES), lambda i, be, nu: (jnp.minimum(i, nu[0] - 1), 0)),
        scratch_shapes=[pltpu.VMEM((d, 2 * de), bf16), pltpu.VMEM((de, d), bf16)],
    )
    return pl.pallas_call(
        functools.partial(_expert_kernel, n_slab),
        grid_spec=grid_spec,
        out_shape=jax.ShapeDtypeStruct(xs.shape, xs.dtype),
        input_output_aliases={2: 0},
        compiler_params=_params(("arbitrary",)),
        name="experts",
    )(blk_expert, n_used, xs, w_gate, w_up, w_down)


def _combine_kernel(n_slab, final, bounds, dest_ref, next_ref, x_ref, wgt_ref, g_ref, ys_ref, *rest):
    outs, (gbuf, sems) = rest[:-2], rest[-2:]
    i = pl.program_id(0)
    n = dest_ref.shape[2]
    tm = x_ref.shape[0]
    slot = i % 2
    n_word = n_slab // 2

    def gather(idx_ref, s):
        def copy(tok, k):
            src = ys_ref.at[pl.ds(pl.multiple_of(idx_ref[0, 0, tok * TOP_K + k] * n_word, n_word), n_word), :]
            dst = gbuf.at[s, pl.ds(pl.multiple_of((k * tm + tok) * n_word, n_word), n_word), :]
            return pltpu.make_async_copy(src, dst, sems.at[s])
        _issue_rows(n, copy)

    pl.when(i == 0)(lambda: gather(dest_ref, slot))
    pl.when(i + 1 < pl.num_programs(0))(lambda: gather(next_ref, 1 - slot))
    pltpu.make_async_copy(ys_ref.at[pl.ds(0, n * n_word), :], gbuf.at[slot], sems.at[slot]).wait()

    ys = []
    for k in range(TOP_K):
        halves = [_unpack_bf16_pair(gbuf[slot, pl.ds(k * tm * n_word + s, tm, stride=n_word), :])
                  for s in range(n_word)]
        y = jnp.concatenate([lo for lo, _ in halves] + [hi for _, hi in halves], axis=1)
        ys.append(y * wgt_ref[:, k:k + 1])
    x = x_ref[...] + (ys[0] + ys[1])
    if final:
        x = _rms(x, g_ref[...])

    def store(k):
        outs[k][...] = x

    _when_segment(i, bounds, store)


def _combine(x, ys, dest, wgt, g, n_slab, final, seg_rows):
    m, d = x.shape
    n_steps = m // TD
    dest3 = dest.reshape(n_steps, 1, TD * TOP_K)
    counts = [r // TD for r in seg_rows]
    bounds = [0]
    for cnt in counts:
        bounds.append(bounds[-1] + cnt)
    out_specs = [_seg_spec((TD, d), bounds[k], counts[k]) for k in range(len(seg_rows))]
    out_shape = [jax.ShapeDtypeStruct((r, d), f32) for r in seg_rows]
    idx_block = (1, 1, TD * TOP_K)
    return pl.pallas_call(
        functools.partial(_combine_kernel, n_slab, final, tuple(bounds)),
        grid=(n_steps,),
        in_specs=[pl.BlockSpec(idx_block, lambda i: (i, 0, 0), memory_space=pltpu.SMEM),
                  pl.BlockSpec(idx_block, lambda i: (jnp.minimum(i + 1, n_steps - 1), 0, 0),
                               memory_space=pltpu.SMEM),
                  pl.BlockSpec((TD, d), lambda i: (i, 0)),
                  pl.BlockSpec((TD, LANES), lambda i: (i, 0)),
                  pl.BlockSpec((1, d), lambda i: (0, 0)),
                  pl.BlockSpec(memory_space=pl.ANY)],
        out_specs=out_specs, out_shape=out_shape,
        scratch_shapes=[pltpu.VMEM((2, TD * TOP_K * n_slab // 2, LANES), jnp.uint32),
                        pltpu.SemaphoreType.DMA((2,))],
        compiler_params=_params(("arbitrary",)),
        name="combine",
    )(dest3, dest3, x, wgt, g, ys)


def kernel(x_prompt, x_sample, cache_a_k, cache_a_v, cache_b_k, cache_b_v, cache_b_logf, state_conv, norm_mix_g, w_in, b_in, rel_bias, conv_w, conv_b, conv_ln_g, conv_ln_b, w_proj_a, w_proj_b, w_proj_c, w_out, norm_ffn_g, w_router_group, b_router_group, w_router_expert, b_router_expert, w_e_gate, w_e_up, w_e_down, norm_final_g):
    nb_p, t_p, d = x_prompt.shape
    nb_s, t_s, _ = x_sample.shape
    depth = w_in.shape[0]
    past = cache_b_k.shape[2]
    a_rows = cache_a_k.shape[2]
    m_p, m_s = nb_p * t_p, nb_s * t_s
    m = m_p + m_s
    c_conv = d // 2
    n_slab = d // LANES
    tm = _row_tile(np.gcd(m_p, m_s), 512)
    tm_mix = _row_tile(np.gcd(m_p, m_s), 256)
    assert m_p % TD == 0 and m_s % TD == 0 and t_s % 16 == 0 and m_p % t_s == 0

    a_keep = min(WINDOW_A, t_p)
    x_segs = [x_prompt.reshape(m_p, d), x_sample.reshape(m_s, d)]
    kv_states = [jnp.zeros((depth, rows * N_HEADS, HEAD_DIM), f32)
                 for rows in (nb_p * a_keep, nb_p * a_keep, m_p, m_p, m_s, m_s, m_s, m_s)]
    p_states, s_states = [], []
    for l in range(depth):
        (qa, ka16, va16, qb, kb16, vb16, u, gates, logf), kv_states = _inproj(
            x_segs, norm_mix_g[l][None, :], w_in[l], b_in[l], kv_states, l, m_p, t_p, a_keep, tm)

        ya_p = _band_attention_prompt(qa, ka16, va16, rel_bias[l], nb_p, t_p, 4 * CHUNK)
        kk = jnp.concatenate([cache_a_k[l].reshape(nb_s, a_rows, W_HEADS).astype(bf16),
                              ka16[m_p:].reshape(nb_s, t_s, W_HEADS)], axis=1).reshape(-1, W_HEADS)
        vv = jnp.concatenate([cache_a_v[l].reshape(nb_s, a_rows, W_HEADS).astype(bf16),
                              va16[m_p:].reshape(nb_s, t_s, W_HEADS)], axis=1).reshape(-1, W_HEADS)
        ya_s = _band_attention_sample(qa, kk, vv, rel_bias[l], nb_s, t_s, a_rows, m_p // t_s)

        logf_p = logf[:m_p].reshape(nb_p, t_p, N_HEADS)
        logf_s = logf[m_p:].reshape(nb_s, t_s, N_HEADS)
        cum_p = _cumsum_time(logf_p.transpose(0, 2, 1))
        f_p = cum_p.transpose(0, 2, 1).reshape(m_p, N_HEADS)
        yb_p = _fox_attention(_fox_expand("q", qb, f_p, m_p, tm), _fox_expand("k", kb16, f_p, m_p, tm),
                              _fox_expand("v", vb16, None, m_p, tm),
                              nb_p, t_p, t_p, _row_tile(t_p, 2048), _row_tile(t_p, 512))
        cum_s = _cumsum_time(jnp.concatenate([cache_b_logf[l].astype(f32), logf_s], axis=1).transpose(0, 2, 1))
        t_ks = past + t_s
        kk = jnp.concatenate([cache_b_k[l].reshape(nb_s, past, W_HEADS).astype(bf16),
                              kb16[m_p:].reshape(nb_s, t_s, W_HEADS)], axis=1).reshape(-1, W_HEADS)
        vv = jnp.concatenate([cache_b_v[l].reshape(nb_s, past, W_HEADS).astype(bf16),
                              vb16[m_p:].reshape(nb_s, t_s, W_HEADS)], axis=1).reshape(-1, W_HEADS)
        f_ks = cum_s.transpose(0, 2, 1)
        yb_s = _fox_attention(
            _fox_expand("q", qb, f_ks[:, past:].reshape(m_s, N_HEADS), m_s, t_s, blk0=m_p // t_s),
            _fox_expand("k", kk, f_ks.reshape(nb_s * t_ks, N_HEADS), nb_s * t_ks, t_ks),
            _fox_expand("v", vv, None, nb_s * t_ks, t_ks),
            nb_s, t_s, t_ks, t_s, t_ks)

        conv_args = (conv_w[l], conv_b[l][None, :], conv_ln_g[l][None, :], conv_ln_b[l][None, :])
        c_p = _conv_module(u, jnp.zeros((nb_p, CONV_HALO, c_conv), f32), *conv_args,
                           nb_p, t_p, _row_tile(t_p, 256), 0)
        init_s = jnp.pad(state_conv[l], ((0, 0), (CONV_HALO - (CONV_W - 1), 0), (0, 0)))
        c_s = _conv_module(u, init_s, *conv_args, nb_s, t_s, t_s, m_p // t_s)

        mixed = _mix(ya_p, ya_s, yb_p, yb_s, c_p, c_s, gates, w_proj_a[l].astype(bf16),
                     w_proj_b[l].astype(bf16), w_proj_c[l].astype(bf16), tm_mix)
        wr = jnp.pad(jnp.concatenate([w_router_group[l], w_router_expert[l]], axis=1),
                     ((0, 0), (0, LANES - N_GROUPS - N_EXPERTS)))
        wr_hi = wr.astype(bf16)
        wr_parts = jnp.concatenate([wr_hi, (wr - wr_hi.astype(f32)).astype(bf16)], axis=1)
        br = jnp.pad(jnp.concatenate([b_router_group[l], b_router_expert[l]]),
                     (0, LANES - N_GROUPS - N_EXPERTS))[None, :]
        x_mid, h2_slab, eid, wgt = _outproj(mixed, x_segs, w_out[l].astype(bf16), norm_ffn_g[l][None, :],
                                            wr_parts, br, tm_mix)

        dest, blk_expert, n_used, last_blk, n_blocks = _plan(eid[:, :TOP_K])
        xs = _dispatch(h2_slab, dest, last_blk, n_used, n_blocks, n_slab // 2, tm)
        ys = _experts(xs, blk_expert, n_used, w_e_gate, w_e_up, w_e_down, l, n_blocks, n_slab)
        final = l == depth - 1
        x_segs = _combine(x_mid, ys, dest, wgt, norm_final_g[None, :], n_slab, final,
                          (m_p, m_s) if final else (m,))

        n_cs = c_conv // LANES
        u_p = jnp.stack([u[((b + 1) * t_p - (CONV_W - 1)) * n_cs:(b + 1) * t_p * n_cs] for b in range(nb_p)])
        u_p = u_p.reshape(nb_p, CONV_W - 1, c_conv)
        u_s = u[m_p * n_cs:].reshape(nb_s, t_s, c_conv)
        p_states.append((logf_p, u_p))
        s_states.append((logf_s, jnp.concatenate([state_conv[l], u_s], axis=1)[:, -(CONV_W - 1):]))

    y_prompt = x_segs[0].reshape(nb_p, t_p, d)
    y_sample = x_segs[1].reshape(nb_s, t_s, d)
    stack = lambda states, k: jnp.stack([st[k] for st in states], axis=0)
    heads = lambda a, nb, t: a.reshape(depth, nb, t, N_HEADS, HEAD_DIM)
    ka_p, va_p, kb_p, vb_p, ka_s, va_s, kb_s, vb_s = kv_states
    return (y_prompt, y_sample,
            heads(ka_p, nb_p, a_keep), heads(va_p, nb_p, a_keep), heads(kb_p, nb_p, t_p), heads(vb_p, nb_p, t_p),
            stack(p_states, 0), stack(p_states, 1),
            heads(ka_s, nb_s, t_s), heads(va_s, nb_s, t_s), heads(kb_s, nb_s, t_s), heads(vb_s, nb_s, t_s),
            stack(s_states, 0), stack(s_states, 1))
```

```python
import functools

import jax
import jax.numpy as jnp
import numpy as np
from jax import lax
from jax.experimental import pallas as pl
from jax.experimental.pallas import tpu as pltpu

f32 = jnp.float32
bf16 = jnp.bfloat16

HEAD_DIM = 64
N_HEADS = 8
W_HEADS = N_HEADS * HEAD_DIM
CHUNK = 64
WINDOW_A = 8 * CHUNK
REL_CLIP = 128
CONV_W = 31
CONV_HALO = 32
N_GROUPS = 4
EXPERTS_PER_GROUP = 8
N_EXPERTS = N_GROUPS * EXPERTS_PER_GROUP
TOP_K = 2
SCALE = HEAD_DIM ** -0.5
EPS = 1e-6
NEG_INF = -1e30
LANES = 128
SUBLANES = 8
MIB = 1024 * 1024


def _params(sem, vmem_mib=48):
    return pltpu.CompilerParams(dimension_semantics=sem, vmem_limit_bytes=vmem_mib * MIB)


def _row_tile(m, cap):
    t = cap
    while m % t:
        t //= 2
    return t


def _sigmoid(z):
    return 0.5 * jnp.tanh(0.5 * z) + 0.5


def _rms(x, g):
    return x * lax.rsqrt(jnp.mean(x * x, axis=-1, keepdims=True) + EPS) * g


def _pack_bf16_pair(lo, hi):
    def rounded(x):
        bits = lax.bitcast_convert_type(x, jnp.uint32)
        return bits + jnp.uint32(0x7FFF) + ((bits >> 16) & jnp.uint32(1))
    return (rounded(hi) & jnp.uint32(0xFFFF0000)) | (rounded(lo) >> 16)


def _unpack_bf16_pair(word):
    lo = lax.bitcast_convert_type(word << 16, f32)
    hi = lax.bitcast_convert_type(word & jnp.uint32(0xFFFF0000), f32)
    return lo, hi


def _when_segment(i, bounds, fn):
    for k in range(len(bounds) - 1):
        pl.when((i >= bounds[k]) & (i < bounds[k + 1]))(functools.partial(fn, k))


def _seg_spec(block, start, count, width_axes=1):
    zeros = (0,) * width_axes
    return pl.BlockSpec(block, lambda i, *_: (jnp.clip(i - start, 0, count - 1),) + zeros)


TN = 1024
SEG_PER_TILE = TN // W_HEADS


def _log_forget(h, w_f, b_f):
    zf = jnp.dot(h, w_f, preferred_element_type=f32) + b_f
    return (jnp.minimum(zf, 0.0) - jnp.log1p(jnp.exp(-jnp.abs(zf))))[:, :N_HEADS]


def _forget_weights(w_in, b_in):
    f0 = 6 * W_HEADS
    w_f = jnp.pad(w_in[:, f0:f0 + N_HEADS], ((0, 0), (0, LANES - N_HEADS))).astype(bf16)
    b_f = jnp.pad(b_in[f0:f0 + N_HEADS], (0, LANES - N_HEADS))[None, :].astype(f32)
    return w_f, b_f


def _norm_kernel(n_seg, bounds, *refs):
    xs = refs[:n_seg]
    g_ref, wf_ref, bf_ref, h_ref, logf = refs[n_seg:]

    def norm(k):
        h_ref[...] = _rms(xs[k][...], g_ref[...]).astype(bf16)

    _when_segment(pl.program_id(0), bounds, norm)
    logf[...] = _log_forget(h_ref[...], wf_ref[...], bf_ref[...])


def _norm(x_segs, g, w_f, b_f, tm):
    d = x_segs[0].shape[1]
    m = sum(a.shape[0] for a in x_segs)
    counts = [a.shape[0] // tm for a in x_segs]
    bounds = [0]
    for cnt in counts:
        bounds.append(bounds[-1] + cnt)
    const = lambda shape: pl.BlockSpec(shape, lambda i: (0, 0))
    return pl.pallas_call(
        functools.partial(_norm_kernel, len(x_segs), tuple(bounds)),
        grid=(m // tm,),
        in_specs=[_seg_spec((tm, d), bounds[k], counts[k]) for k in range(len(x_segs))] +
                 [const((1, d)), const((d, LANES)), const((1, LANES))],
        out_specs=[pl.BlockSpec((tm, d), lambda i: (i, 0)), pl.BlockSpec((tm, N_HEADS), lambda i: (i, 0))],
        out_shape=[jax.ShapeDtypeStruct((m, d), bf16), jax.ShapeDtypeStruct((m, N_HEADS), f32)],
        compiler_params=_params(("arbitrary",)),
        name="norm",
    )(*x_segs, g, w_f, b_f)


def _qkv_kernel(with_state, n_prompt, h_ref, w_ref, b_ref, *refs):
    outs = refs[2 * sum(with_state):]
    is_prompt = pl.program_id(0) < n_prompt
    for prompt_rows in (True, False):
        @pl.when(is_prompt if prompt_rows else ~is_prompt)
        def _(prompt_rows=prompt_rows):
            z = jnp.dot(h_ref[...], w_ref[...], preferred_element_type=f32) + b_ref[...]
            k = 0
            for n, has_state in enumerate(with_state):
                zn = z[:, n * W_HEADS:(n + 1) * W_HEADS]
                outs[k][...] = zn.astype(bf16)
                k += 1
                if has_state:
                    state = outs[k if prompt_rows else k + 1]
                    for hh in range(N_HEADS):
                        state[pl.ds(hh, zn.shape[0], stride=N_HEADS), :] = zn[:, hh * HEAD_DIM:(hh + 1) * HEAD_DIM]
                    k += 2


def _qkv_tile(h, w_qkv, b_qkv, tile, seg_states, layer, m_p, t_p, a_keep, tm):
    m, d = h.shape
    n_p, n_s = m_p // tm, (m - m_p) // tm
    per_seq, keep = t_p // tm, a_keep // tm

    def tail_rows(i):
        ip = jnp.minimum(i, n_p - 1)
        return (layer, (ip // per_seq) * keep + jnp.maximum(ip % per_seq - (per_seq - keep), 0), 0)

    state_block = (None, tm * N_HEADS, HEAD_DIM)
    tail_spec = pl.BlockSpec(state_block, tail_rows)
    prompt_spec = pl.BlockSpec(state_block, lambda i: (layer, jnp.minimum(i, n_p - 1), 0))
    sample_spec = pl.BlockSpec(state_block, lambda i: (layer, jnp.clip(i - n_p, 0, n_s - 1), 0))
    states, out_specs, out_shape, state_out_pos = [], [], [], []
    for seg in seg_states:
        out_specs.append(pl.BlockSpec((tm, W_HEADS), lambda i: (i, 0)))
        out_shape.append(jax.ShapeDtypeStruct((m, W_HEADS), bf16))
        if seg is not None:
            buf_p, buf_s, keep_tail = seg
            for buf, spec in ((buf_p, tail_spec if keep_tail else prompt_spec), (buf_s, sample_spec)):
                state_out_pos.append(len(out_specs))
                states.append(buf)
                out_specs.append(spec)
                out_shape.append(jax.ShapeDtypeStruct(buf.shape, buf.dtype))
    outs = pl.pallas_call(
        functools.partial(_qkv_kernel, tuple(seg is not None for seg in seg_states), n_p),
        grid=(m // tm,),
        in_specs=[pl.BlockSpec((tm, d), lambda i: (i, 0)),
                  pl.BlockSpec((d, TN), lambda i: (0, tile)),
                  pl.BlockSpec((1, TN), lambda i: (0, tile))] + [pl.BlockSpec(memory_space=pl.ANY)] * len(states),
        out_specs=out_specs, out_shape=out_shape,
        input_output_aliases={3 + k: pos for k, pos in enumerate(state_out_pos)},
        compiler_params=_params(("arbitrary",)),
        name="qkv",
    )(h, w_qkv, b_qkv, *states)
    copies = [o for k, o in enumerate(outs) if k not in state_out_pos]
    return copies, [outs[pos] for pos in state_out_pos]


def _glu_kernel(n_slab, h_ref, w_ref, b_ref, u_ref):
    tm = h_ref.shape[0]
    c = n_slab * LANES
    z = jnp.dot(h_ref[...], w_ref[...], preferred_element_type=f32) + b_ref[...]
    glu = z[:, :c] * _sigmoid(z[:, c:])
    for s in range(n_slab):
        u_ref[pl.ds(s, tm, stride=n_slab), :] = glu[:, s * LANES:(s + 1) * LANES]


def _glu(h, w, b, tm):
    m, d = h.shape
    c = w.shape[1] // 2
    n_slab = c // LANES
    return pl.pallas_call(
        functools.partial(_glu_kernel, n_slab),
        grid=(m // tm,),
        in_specs=[pl.BlockSpec((tm, d), lambda i: (i, 0)), pl.BlockSpec((d, 2 * c), lambda i: (0, 0)),
                  pl.BlockSpec((1, 2 * c), lambda i: (0, 0))],
        out_specs=pl.BlockSpec((tm * n_slab, LANES), lambda i: (i, 0)),
        out_shape=jax.ShapeDtypeStruct((m * n_slab, LANES), f32),
        compiler_params=_params(("arbitrary",)),
        name="glu",
    )(h, w, b)


def _gates_kernel(h_ref, w_ref, b_ref, o_ref):
    z = jnp.dot(h_ref[...], w_ref[...], preferred_element_type=f32) + b_ref[...]
    o_ref[...] = _sigmoid(z).astype(o_ref.dtype)


def _gates(h, w, b, tm):
    m, d = h.shape
    n = w.shape[1]
    tn = 2 * TN if n % (2 * TN) == 0 else TN
    return pl.pallas_call(
        _gates_kernel,
        grid=(n // tn, m // tm),
        in_specs=[pl.BlockSpec((tm, d), lambda j, i: (i, 0)), pl.BlockSpec((d, tn), lambda j, i: (0, j)),
                  pl.BlockSpec((1, tn), lambda j, i: (0, j))],
        out_specs=pl.BlockSpec((tm, tn), lambda j, i: (i, j)),
        out_shape=jax.ShapeDtypeStruct((m, n), bf16),
        compiler_params=_params(("arbitrary", "arbitrary")),
        name="gates",
    )(h, w, b)


def _inproj(x_segs, g, w_in, b_in, states, layer, m_p, t_p, a_keep, tm, normed=None):
    d = x_segs[0].shape[1]
    c_conv = d // 2
    n_qkv = 6 * W_HEADS
    f0 = n_qkv
    c0 = f0 + N_HEADS
    g0 = c0 + 2 * c_conv
    cast = lambda a: a.astype(bf16)
    row = lambda a: a[None, :].astype(f32)
    h, logf = normed if normed is not None else _norm(x_segs, g, *_forget_weights(w_in, b_in), tm)

    ka_p, va_p, kb_p, vb_p, ka_s, va_s, kb_s, vb_s = states
    w_qkv, b_qkv = cast(w_in[:, :n_qkv]), row(b_in[:n_qkv])
    tile = functools.partial(_qkv_tile, h, w_qkv, b_qkv, layer=layer, m_p=m_p, t_p=t_p, a_keep=a_keep, tm=tm)
    assert SEG_PER_TILE == 2
    (qa, ka16), (ka_p, ka_s) = tile(0, [None, (ka_p, ka_s, True)])
    (va16, qb), (va_p, va_s) = tile(1, [(va_p, va_s, True), None])
    (kb16, vb16), (kb_p, kb_s, vb_p, vb_s) = tile(2, [(kb_p, kb_s, False), (vb_p, vb_s, False)])
    u = _glu(h, cast(w_in[:, c0:g0]), row(b_in[c0:g0]), tm)
    gates = _gates(h, cast(w_in[:, g0:]), row(b_in[g0:]), tm)
    return (qa, ka16, va16, qb, kb16, vb16, u, gates, logf), [ka_p, va_p, kb_p, vb_p, ka_s, va_s, kb_s, vb_s]


TC = 512


def _cumsum_kernel(x_ref, o_ref, carry):
    @pl.when(pl.program_id(1) == 0)
    def _():
        carry[...] = jnp.zeros_like(carry)

    blk = x_ref[0]
    r = lax.broadcasted_iota(jnp.int32, (TC, TC), 0)
    c = lax.broadcasted_iota(jnp.int32, (TC, TC), 1)
    tri = jnp.where(r <= c, 1.0, 0.0).astype(bf16)
    cs = carry[:, 0:1]
    rest = blk
    for _ in range(F_PARTS):
        part = rest.astype(bf16)
        rest = rest - part.astype(f32)
        cs = cs + jnp.dot(part, tri, preferred_element_type=f32)
    o_ref[0] = cs
    carry[...] = jnp.broadcast_to(cs[:, TC - 1:TC], carry.shape)


def _cumsum_time(x):
    nb, h, t = x.shape
    tp = -(-t // TC) * TC
    xp = jnp.pad(x, ((0, 0), (0, 0), (0, tp - t)))
    out = pl.pallas_call(
        _cumsum_kernel,
        grid=(nb, tp // TC),
        in_specs=[pl.BlockSpec((1, h, TC), lambda b, k: (b, 0, k))],
        out_specs=pl.BlockSpec((1, h, TC), lambda b, k: (b, 0, k)),
        out_shape=jax.ShapeDtypeStruct((nb, h, tp), f32),
        scratch_shapes=[pltpu.VMEM((h, LANES), f32)],
        compiler_params=_params(("arbitrary", "arbitrary")),
        name="cumsum",
    )(xp)
    return out[:, :, :t]


F_PARTS = 3


def _spare_base(h):
    return HEAD_DIM * (1 - h % 2)


def _fox_expand_kernel(kind, x_ref, f_ref, place_ref, o_ref):
    tm = x_ref.shape[0]
    lane = lax.broadcasted_iota(jnp.int32, (tm, LANES), 1)
    if kind != "v":
        rest = f_ref[...]
        stack = jnp.where(lane < (F_PARTS + 1) * N_HEADS, 1.0, 0.0)
        for n in range(F_PARTS):
            part = rest.astype(bf16).astype(f32)
            rest = rest - part
            stack = jnp.where((lane >= n * N_HEADS) & (lane < (n + 1) * N_HEADS), part, stack)
        spare_all = jnp.dot(stack.astype(bf16), place_ref[...], preferred_element_type=f32)
    for h in range(N_HEADS):
        pair = x_ref[:, (h // 2) * LANES:(h // 2 + 1) * LANES].astype(f32)
        if kind == "q":
            pair = pair * SCALE
        if kind == "v":
            spare = jnp.where(lane == _spare_base(h), 1.0, 0.0)
        else:
            spare = spare_all[:, h * LANES:(h + 1) * LANES]
        own = (lane < HEAD_DIM) if h % 2 == 0 else (lane >= HEAD_DIM)
        o_ref[h] = jnp.where(own, pair, spare).astype(o_ref.dtype)


def _fox_placement(kind):
    place = np.zeros((LANES, N_HEADS * LANES), np.float32)
    for h in range(N_HEADS):
        base = h * LANES + _spare_base(h)
        for n in range(F_PARTS):
            if kind == "q":
                place[n * N_HEADS + h, base + n] = 1.0
                place[F_PARTS * N_HEADS + h, base + F_PARTS + n] = 1.0
            else:
                place[F_PARTS * N_HEADS + h, base + n] = 1.0
                place[n * N_HEADS + h, base + F_PARTS + n] = -1.0
    return jnp.asarray(place, bf16)


def _fox_expand(kind, x, f, rows, tm, blk0=0):
    f_lanes = jnp.zeros((rows, LANES), f32) if f is None else jnp.tile(f, (1, LANES // N_HEADS))
    return pl.pallas_call(
        functools.partial(_fox_expand_kernel, kind),
        grid=(rows // tm,),
        in_specs=[pl.BlockSpec((tm, W_HEADS), lambda i: (blk0 + i, 0)),
                  pl.BlockSpec((tm, LANES), lambda i: (i, 0)),
                  pl.BlockSpec((LANES, N_HEADS * LANES), lambda i: (0, 0))],
        out_specs=pl.BlockSpec((N_HEADS, tm, LANES), lambda i: (0, i, 0)),
        out_shape=jax.ShapeDtypeStruct((N_HEADS, rows, LANES), bf16),
        compiler_params=_params(("arbitrary",)),
        name="fox_expand_" + kind,
    )(x, f_lanes, _fox_placement(kind))


FOX_HEADS_PER_TRIP = 2


def _fox_kernel(tq, tk, off, q_ref, k_ref, v_ref, o_ref, m_scr, acc_scr):
    i = pl.program_id(1)
    j = pl.program_id(2)

    @pl.when(j == 0)
    def _():
        m_scr[...] = jnp.full_like(m_scr, NEG_INF)
        acc_scr[...] = jnp.zeros_like(acc_scr)

    q_first = i * tq + off
    q_last = q_first + tq - 1
    k_first = j * tk
    k_last = k_first + tk - 1

    def body(row0, masked):
        rows = slice(row0, tq)
        if masked:
            kpos = k_first + lax.broadcasted_iota(jnp.int32, (tq - row0, tk), 1)
            qpos = q_first + row0 + lax.broadcasted_iota(jnp.int32, (tq - row0, tk), 0)
            vis = kpos <= qpos

        def head(h):
            s = lax.dot_general(q_ref[h, rows], k_ref[h], (((1,), (1,)), ((), ())), preferred_element_type=f32)
            if masked:
                s = jnp.where(vis, s, NEG_INF)
            m_old = m_scr[h, rows]
            m_new = jnp.maximum(m_old, jnp.max(s, axis=-1, keepdims=True))
            pr = jnp.exp(s - m_new[:, 0:1])
            pv = jnp.dot(pr.astype(bf16), v_ref[h], preferred_element_type=f32)
            acc_scr[h, rows] = jnp.exp(m_old - m_new) * acc_scr[h, rows] + pv
            m_scr[h, rows] = m_new

        def trip(g, carry):
            for n in range(FOX_HEADS_PER_TRIP):
                head(g * FOX_HEADS_PER_TRIP + n)
            return carry

        lax.fori_loop(0, N_HEADS // FOX_HEADS_PER_TRIP, trip, 0)

    pl.when(k_last <= q_first)(functools.partial(body, 0, False))
    if tq % tk == 0 and off % tk == 0:
        for c in range(tq // tk):
            pl.when(k_first == q_first + c * tk)(functools.partial(body, c * tk, True))
    else:
        pl.when((k_first <= q_last) & (k_last > q_first))(functools.partial(body, 0, True))

    @pl.when(j == pl.num_programs(2) - 1)
    def _():
        lane = lax.broadcasted_iota(jnp.int32, (tq, LANES), 1)
        for p in range(N_HEADS // 2):
            even = acc_scr[2 * p]
            odd = acc_scr[2 * p + 1]
            even = even / even[:, _spare_base(0):_spare_base(0) + 1]
            odd = odd / odd[:, _spare_base(1):_spare_base(1) + 1]
            o_ref[:, p * LANES:(p + 1) * LANES] = jnp.where(lane < HEAD_DIM, even, odd).astype(o_ref.dtype)


def _fox_attention(q, k, v, nb, t_q, t_k, tq, tk):
    nq, nk = t_q // tq, t_k // tk
    off = t_k - t_q

    def last_k(i):
        return jnp.minimum((i * tq + tq - 1 + off) // tk, nk - 1)

    kv_spec = pl.BlockSpec((N_HEADS, tk, LANES), lambda b, i, j: (0, b * nk + jnp.minimum(j, last_k(i)), 0))
    return pl.pallas_call(
        functools.partial(_fox_kernel, tq, tk, off),
        grid=(nb, nq, nk),
        in_specs=[pl.BlockSpec((N_HEADS, tq, LANES), lambda b, i, j: (0, b * nq + i, 0)), kv_spec, kv_spec],
        out_specs=pl.BlockSpec((tq, W_HEADS), lambda b, i, j: (b * nq + i, 0)),
        out_shape=jax.ShapeDtypeStruct((nb * t_q, W_HEADS), bf16),
        scratch_shapes=[pltpu.VMEM((N_HEADS, tq, LANES), f32), pltpu.VMEM((N_HEADS, tq, LANES), f32)],
        compiler_params=_params(("arbitrary", "arbitrary", "arbitrary")),
        name="fox",
    )(q, k, v)


def _band_kernel(rows, gq, wk, has_prev, *refs):
    if has_prev:
        q_ref, kp_ref, kc_ref, vp_ref, vc_ref, bias_ref, o_ref, k_scr, v_scr = refs
        k_scr[0:WINDOW_A] = kp_ref[...]
        k_scr[WINDOW_A:WINDOW_A + rows] = kc_ref[...]
        v_scr[0:WINDOW_A] = vp_ref[...]
        v_scr[WINDOW_A:WINDOW_A + rows] = vc_ref[...]
        k_src, v_src = k_scr, v_scr
    else:
        q_ref, k_src, v_src, bias_ref, o_ref = refs
    i = pl.program_id(1)
    lane = lax.broadcasted_iota(jnp.int32, (gq, LANES), 1)
    low = lane < HEAD_DIM

    def attend(before_start):
        for g in range(rows // gq):
            r0 = g * gq
            if before_start:
                key_pos = (i - 1) * WINDOW_A + r0 + lax.broadcasted_iota(jnp.int32, (gq, wk), 1)
                vis = key_pos >= 0
            for p in range(N_HEADS // 2):
                cols = slice(p * LANES, (p + 1) * LANES)
                q2 = q_ref[r0:r0 + gq, cols] * SCALE
                kw = k_src[r0:r0 + wk, cols]
                vw = v_src[r0:r0 + wk, cols]
                outs = []
                for half in range(2):
                    h = 2 * p + half
                    qm = jnp.where(low if half == 0 else ~low, q2, jnp.zeros_like(q2))
                    s = lax.dot_general(qm, kw, (((1,), (1,)), ((), ())), preferred_element_type=f32)
                    s = s + bias_ref[h]
                    if before_start:
                        s = jnp.where(vis, s, NEG_INF)
                    m = jnp.max(s, axis=-1, keepdims=True)
                    pr = jnp.exp(s - m)
                    l = jnp.sum(pr, axis=-1, keepdims=True)
                    pv = jnp.dot(pr.astype(bf16), vw, preferred_element_type=f32)
                    outs.append(pv / l)
                o_ref[r0:r0 + gq, cols] = jnp.where(low, outs[0], outs[1]).astype(o_ref.dtype)

    if has_prev:
        pl.when(i == 0)(functools.partial(attend, True))
        pl.when(i > 0)(functools.partial(attend, False))
    else:
        attend(False)


def _rel_bias_table(rel_bias, gq, wk, q_shift):
    period = wk + gq
    j = np.arange(period)
    k = np.where(j < wk, j, j - period)
    line = rel_bias.astype(f32)[:, np.clip(q_shift - k, -REL_CLIP, REL_CLIP) + REL_CLIP]
    tiled = jnp.tile(line, (1, gq))[:, :gq * (period - 1)]
    return tiled.reshape(-1, gq, period - 1)[:, :, :wk]


def _band_bias(rel_bias, gq, wk, q_shift):
    r = np.arange(gq)[:, None]
    s = np.arange(wk)[None, :]
    band0 = (r // CHUNK) * CHUNK + q_shift - WINDOW_A
    ok = (s >= band0) & (s < band0 + WINDOW_A + CHUNK)
    return jnp.where(ok[None], _rel_bias_table(rel_bias, gq, wk, q_shift), NEG_INF)


def _band_attention_prompt(q, k, v, rel_bias, nb, t, gq):
    rows = WINDOW_A
    wk = WINDOW_A + gq
    n_steps = t // rows
    bias = _band_bias(rel_bias, gq, wk, WINDOW_A)
    cur = pl.BlockSpec((rows, W_HEADS), lambda b, i: (b * n_steps + i, 0))
    prev = pl.BlockSpec((rows, W_HEADS), lambda b, i: (b * n_steps + jnp.maximum(i - 1, 0), 0))
    return pl.pallas_call(
        functools.partial(_band_kernel, rows, gq, wk, True),
        grid=(nb, n_steps),
        in_specs=[cur, prev, cur, prev, cur,
                  pl.BlockSpec((N_HEADS, gq, wk), lambda b, i: (0, 0, 0))],
        out_specs=cur,
        out_shape=jax.ShapeDtypeStruct((nb * t, W_HEADS), bf16),
        scratch_shapes=[pltpu.VMEM((2 * rows, W_HEADS), bf16), pltpu.VMEM((2 * rows, W_HEADS), bf16)],
        compiler_params=_params(("arbitrary", "arbitrary")),
        name="band_prompt",
    )(q, k, k, v, v, bias)


def _band_attention_sample(q, kk, vv, rel_bias, nb, s_new, l_cache, q_blk0):
    wk = l_cache + s_new
    bias = _rel_bias_table(rel_bias, s_new, wk, l_cache)
    return pl.pallas_call(
        functools.partial(_band_kernel, s_new, s_new, wk, False),
        grid=(nb, 1),
        in_specs=[pl.BlockSpec((s_new, W_HEADS), lambda b, i: (q_blk0 + b, 0)),
                  pl.BlockSpec((wk, W_HEADS), lambda b, i: (b, 0)),
                  pl.BlockSpec((wk, W_HEADS), lambda b, i: (b, 0)),
                  pl.BlockSpec((N_HEADS, s_new, wk), lambda b, i: (0, 0, 0))],
        out_specs=pl.BlockSpec((s_new, W_HEADS), lambda b, i: (b, 0)),
        out_shape=jax.ShapeDtypeStruct((nb * s_new, W_HEADS), bf16),
        compiler_params=_params(("arbitrary", "arbitrary")),
        name="band_sample",
    )(q, kk, vv, bias)


CONV_STRIP = 32


def _conv_kernel(tt, n_slab, init_ref, u_ref, w_ref, cb_ref, g_ref, b_ref, o_ref, ubuf, acc_scr):
    halo = CONV_HALO * n_slab

    @pl.when(pl.program_id(1) == 0)
    def _():
        ubuf[0:halo] = init_ref[0]

    ubuf[halo:halo + tt * n_slab] = u_ref[...]
    rs = min(CONV_STRIP, tt)
    first = CONV_HALO - (CONV_W - 1)

    def per_step(slab):
        return jnp.broadcast_to(slab[None], (rs, n_slab, LANES)).reshape(rs * n_slab, LANES)

    for s in range(tt // rs):
        acc = per_step(cb_ref[...])
        for j in range(CONV_W):
            r0 = (s * rs + first + j) * n_slab
            acc = acc + per_step(w_ref[j * n_slab:(j + 1) * n_slab, :]) * ubuf[r0:r0 + rs * n_slab, :]
        acc_scr[...] = acc
        rows = jnp.concatenate([acc_scr[pl.ds(q, rs, stride=n_slab), :] for q in range(n_slab)], axis=1)
        mu = jnp.mean(rows, axis=-1, keepdims=True)
        cen = rows - mu
        var = jnp.mean(cen * cen, axis=-1, keepdims=True)
        y = cen * lax.rsqrt(var + EPS) * g_ref[...] + b_ref[...]
        o_ref[s * rs:(s + 1) * rs, :] = (y * _sigmoid(y)).astype(o_ref.dtype)
    if tt >= CONV_HALO:
        ubuf[0:halo] = ubuf[tt * n_slab:tt * n_slab + halo]


def _conv_module(u_slab, init, conv_w, conv_b, ln_g, ln_b, nb, t, tt, blk0):
    c = conv_w.shape[1]
    n_slab = c // LANES
    n_t = t // tt
    vec = pl.BlockSpec((1, c), lambda b, i: (0, 0))
    return pl.pallas_call(
        functools.partial(_conv_kernel, tt, n_slab),
        grid=(nb, n_t),
        in_specs=[pl.BlockSpec((1, CONV_HALO * n_slab, LANES), lambda b, i: (b, 0, 0)),
                  pl.BlockSpec((tt * n_slab, LANES), lambda b, i: (blk0 + b * n_t + i, 0)),
                  pl.BlockSpec((CONV_W * n_slab, LANES), lambda b, i: (0, 0)),
                  pl.BlockSpec((n_slab, LANES), lambda b, i: (0, 0)), vec, vec],
        out_specs=pl.BlockSpec((tt, c), lambda b, i: (b * n_t + i, 0)),
        out_shape=jax.ShapeDtypeStruct((nb * t, c), bf16),
        scratch_shapes=[pltpu.VMEM(((CONV_HALO + tt) * n_slab, LANES), f32),
                        pltpu.VMEM((min(CONV_STRIP, tt) * n_slab, LANES), f32)],
        compiler_params=_params(("arbitrary", "arbitrary")),
        name="conv",
    )(init.reshape(nb, CONV_HALO * n_slab, LANES), u_slab, conv_w.reshape(CONV_W * n_slab, LANES),
      conv_b.reshape(n_slab, LANES), ln_g, ln_b)


def _mix_kernel(bounds, ya_p, ya_s, yb_p, yb_s, c_p, c_s, gates, pa, pb, pc, o_ref):
    d = o_ref.shape[1]

    def go(k):
        ya, yb, c = ((ya_p, yb_p, c_p), (ya_s, yb_s, c_s))[k]
        a = jnp.dot(ya[...], pa[...], preferred_element_type=f32)
        mixed = gates[:, 0:d].astype(f32) * a
        b = jnp.dot(yb[...], pb[...], preferred_element_type=f32)
        mixed = mixed + gates[:, d:2 * d].astype(f32) * b
        cc = jnp.dot(c[...], pc[...], preferred_element_type=f32)
        mixed = mixed + gates[:, 2 * d:3 * d].astype(f32) * cc
        o_ref[...] = mixed.astype(o_ref.dtype)

    _when_segment(pl.program_id(0), bounds, go)


def _mix(ya_p, ya_s, yb_p, yb_s, c_p, c_s, gates, pa, pb, pc, tm):
    m, d3 = gates.shape
    d = d3 // 3
    n_p, n_s = ya_p.shape[0] // tm, ya_s.shape[0] // tm
    bounds = (0, n_p, n_p + n_s)
    c_conv = c_p.shape[1]
    const = lambda shape: pl.BlockSpec(shape, lambda i: (0, 0))
    return pl.pallas_call(
        functools.partial(_mix_kernel, bounds),
        grid=(m // tm,),
        in_specs=[_seg_spec((tm, W_HEADS), 0, n_p), _seg_spec((tm, W_HEADS), n_p, n_s),
                  _seg_spec((tm, W_HEADS), 0, n_p), _seg_spec((tm, W_HEADS), n_p, n_s),
                  _seg_spec((tm, c_conv), 0, n_p), _seg_spec((tm, c_conv), n_p, n_s),
                  pl.BlockSpec((tm, d3), lambda i: (i, 0)),
                  const((W_HEADS, d)), const((W_HEADS, d)), const((c_conv, d))],
        out_specs=pl.BlockSpec((tm, d), lambda i: (i, 0)),
        out_shape=jax.ShapeDtypeStruct((m, d), bf16),
        compiler_params=_params(("arbitrary",)),
        name="mix",
    )(ya_p, ya_s, yb_p, yb_s, c_p, c_s, gates, pa, pb, pc)


def _route(logits):
    shape = logits.shape
    lane = lax.broadcasted_iota(jnp.int32, shape, 1)
    lane_f = lane.astype(f32)
    big = float(LANES)
    gl = jnp.where(lane < N_GROUPS, logits, -jnp.inf)
    g_max = jnp.max(gl, axis=-1, keepdims=True)
    g_idx = jnp.min(jnp.where(gl == g_max, lane_f, big), axis=-1, keepdims=True)
    g_sum = jnp.sum(jnp.exp(gl - g_max), axis=-1, keepdims=True)
    g_w = 1.0 / g_sum
    lo = N_GROUPS + g_idx * EXPERTS_PER_GROUP
    el = jnp.where((lane_f >= lo) & (lane_f < lo + EXPERTS_PER_GROUP), logits, -jnp.inf)
    m1 = jnp.max(el, axis=-1, keepdims=True)
    i1 = jnp.min(jnp.where(el == m1, lane_f, big), axis=-1, keepdims=True)
    el2 = jnp.where(lane_f == i1, -jnp.inf, el)
    m2 = jnp.max(el2, axis=-1, keepdims=True)
    i2 = jnp.min(jnp.where(el2 == m2, lane_f, big), axis=-1, keepdims=True)
    e21 = jnp.exp(m2 - m1)
    den = 1.0 + e21
    w1 = g_w * (1.0 / den)
    w2 = g_w * (e21 / den)
    eid = jnp.where(lane == 0, i1 - N_GROUPS, jnp.where(lane == 1, i2 - N_GROUPS, 0.0)).astype(jnp.int32)
    wgt = jnp.where(lane == 0, w1, jnp.where(lane == 1, w2, 0.0))
    return eid, wgt


def _outproj_kernel(n_slab, n_seg, bounds, *refs):
    xs = refs[:n_seg]
    mixed, wo, g2, wr, br, xo, h2o, eid_o, wgt_o = refs[n_seg:]
    tm = mixed.shape[0]
    i = pl.program_id(0)
    x = xs[0][...]
    for k in range(1, n_seg):
        x = jnp.where(i >= bounds[k], xs[k][...], x)
    xn = x + jnp.dot(mixed[...], wo[...], preferred_element_type=f32)
    xo[...] = xn
    h2 = _rms(xn, g2[...])
    n_word = n_slab // 2
    for s in range(n_word):
        h2o[pl.ds(s, tm, stride=n_word), :] = _pack_bf16_pair(h2[:, s * LANES:(s + 1) * LANES],
                                                              h2[:, (n_word + s) * LANES:(n_word + s + 1) * LANES])
    h_hi = h2.astype(bf16)
    h_lo = (h2 - h_hi.astype(f32)).astype(bf16)
    hi = jnp.dot(h_hi, wr[...], preferred_element_type=f32)
    lo = jnp.dot(h_lo, wr[:, :LANES], preferred_element_type=f32)
    logits = hi[:, :LANES] + (hi[:, LANES:] + lo) + br[...]
    eid, wgt = _route(logits)
    eid_o[...] = eid
    wgt_o[...] = wgt


def _outproj(mixed, x_segs, wo, g2, wr, br, tm):
    m, d = mixed.shape
    n_slab = d // LANES
    counts = [a.shape[0] // tm for a in x_segs]
    bounds = [0]
    for cnt in counts:
        bounds.append(bounds[-1] + cnt)
    const = lambda shape: pl.BlockSpec(shape, lambda i: (0, 0))
    row = lambda w: pl.BlockSpec((tm, w), lambda i: (i, 0))
    sds = jax.ShapeDtypeStruct
    return pl.pallas_call(
        functools.partial(_outproj_kernel, n_slab, len(x_segs), tuple(bounds)),
        grid=(m // tm,),
        in_specs=[_seg_spec((tm, d), bounds[k], counts[k]) for k in range(len(x_segs))] +
                 [row(d), const((d, d)), const((1, d)), const((d, 2 * LANES)), const((1, LANES))],
        out_specs=[row(d), pl.BlockSpec((tm * n_slab // 2, LANES), lambda i: (i, 0)), row(LANES), row(LANES)],
        out_shape=[sds((m, d), f32), sds((m * n_slab // 2, LANES), jnp.uint32), sds((m, LANES), jnp.int32),
                   sds((m, LANES), f32)],
        compiler_params=_params(("arbitrary",)),
        name="outproj",
    )(*x_segs, mixed, wo, g2, wr, br)


TB = 256
TD = 256


def _plan(eid):
    flat_e = eid.reshape(-1)
    n_assign = flat_e.shape[0]
    onehot = (flat_e[:, None] == jnp.arange(N_EXPERTS, dtype=jnp.int32)[None, :]).astype(jnp.int32)
    csum = jnp.cumsum(onehot, axis=0)
    counts = csum[-1]
    rank = jnp.sum(onehot * csum, axis=1) - 1
    n_blk_e = (counts + TB - 1) // TB
    blk_end = jnp.cumsum(n_blk_e)
    blk_start = blk_end - n_blk_e
    dest = blk_start[flat_e] * TB + rank
    n_blocks = -(-n_assign // TB) + N_EXPERTS
    blk_ids = jnp.arange(n_blocks, dtype=jnp.int32)
    blk_expert = jnp.minimum(jnp.sum((blk_end[None, :] <= blk_ids[:, None]).astype(jnp.int32), axis=1),
                             N_EXPERTS - 1)
    last_blk = jnp.where(n_blk_e > 0, blk_end - 1, -1).astype(jnp.int32)
    return dest.astype(jnp.int32), blk_expert, blk_end[-1:].astype(jnp.int32), last_blk, n_blocks


DMA_UNROLL = 8


def _issue_rows(n, copy):
    per_trip = DMA_UNROLL // TOP_K

    def trip(t, carry):
        for r in range(per_trip):
            for k in range(TOP_K):
                copy(t * per_trip + r, k).start()
        return carry

    lax.fori_loop(0, n // DMA_UNROLL, trip, 0)


def _dispatch_kernel(n_slab, n_blocks, dest_ref, last_ref, nu_ref, h_ref, xs_ref, zero_scr, stage, sems, zero_sem):
    i = pl.program_id(0)
    last = pl.num_programs(0) - 1
    n = dest_ref.shape[2]
    blk_rows = TB * n_slab
    slot = i % 2

    def wait_slot(s):
        for _ in range(TOP_K):
            pltpu.make_async_copy(stage.at[s], xs_ref.at[pl.ds(0, (n // TOP_K) * n_slab), :], sems.at[s]).wait()

    @pl.when(i == 0)
    def _():
        zero_scr[...] = jnp.zeros_like(zero_scr)

        def zero_block(b):
            rows = pl.ds(pl.multiple_of(b * blk_rows, blk_rows), blk_rows)
            return pltpu.make_async_copy(zero_scr, xs_ref.at[rows, :], zero_sem)

        def over_blocks(act):
            for e in range(N_EXPERTS):
                pl.when(last_ref[e] >= 0)(lambda e=e: act(zero_block(last_ref[e])))
            lax.fori_loop(nu_ref[0], n_blocks, lambda b, c: (act(zero_block(b)), c)[1], 0)

        over_blocks(lambda cp: cp.start())
        over_blocks(lambda cp: cp.wait())

    pl.when(i >= 2)(lambda: wait_slot(slot))
    stage[slot] = h_ref[...]

    def copy(tok, k):
        src = stage.at[slot, pl.ds(pl.multiple_of(tok * n_slab, n_slab), n_slab), :]
        dst = xs_ref.at[pl.ds(pl.multiple_of(dest_ref[0, 0, tok * TOP_K + k] * n_slab, n_slab), n_slab), :]
        return pltpu.make_async_copy(src, dst, sems.at[slot])

    _issue_rows(n, copy)

    @pl.when(i == last)
    def _():
        pl.when(i >= 1)(lambda: wait_slot(1 - slot))
        wait_slot(slot)


def _dispatch(h2_slab, dest, last_blk, n_used, n_blocks, n_slab, td):
    m = h2_slab.shape[0] // n_slab
    n_steps = m // td
    dest3 = dest.reshape(n_steps, 1, td * TOP_K)
    smem = pl.BlockSpec(memory_space=pltpu.SMEM)
    return pl.pallas_call(
        functools.partial(_dispatch_kernel, n_slab, n_blocks),
        grid=(n_steps,),
        in_specs=[pl.BlockSpec((1, 1, td * TOP_K), lambda i: (i, 0, 0), memory_space=pltpu.SMEM),
                  smem, smem,
                  pl.BlockSpec((td * n_slab, LANES), lambda i: (i, 0))],
        out_specs=pl.BlockSpec(memory_space=pl.ANY),
        out_shape=jax.ShapeDtypeStruct((n_blocks * TB * n_slab, LANES), h2_slab.dtype),
        scratch_shapes=[pltpu.VMEM((TB * n_slab, LANES), h2_slab.dtype),
                        pltpu.VMEM((2, td * n_slab, LANES), h2_slab.dtype),
                        pltpu.SemaphoreType.DMA((2,)), pltpu.SemaphoreType.DMA(())],
        compiler_params=_params(("arbitrary",)),
        name="dispatch",
    )(dest3, last_blk, n_used, h2_slab)


def _expert_kernel(n_slab, be_ref, nu_ref, xs_ref, wg_ref, wu_ref, wd_ref, o_ref, wgu_scr, wd_scr):
    i = pl.program_id(0)
    de = wd_ref.shape[1]

    @pl.when(i < nu_ref[0])
    def _():
        @pl.when((i == 0) | (be_ref[i] != be_ref[jnp.maximum(i - 1, 0)]))
        def _():
            wgu_scr[:, :de] = wg_ref[0].astype(bf16)
            wgu_scr[:, de:] = wu_ref[0].astype(bf16)
            wd_scr[...] = wd_ref[0].astype(bf16)

        n_word = n_slab // 2
        halves = [_unpack_bf16_pair(xs_ref[pl.ds(s, TB, stride=n_word), :]) for s in range(n_word)]
        x = jnp.concatenate([lo for lo, _ in halves] + [hi for _, hi in halves], axis=1)
        gu = jnp.dot(x.astype(bf16), wgu_scr[...], preferred_element_type=f32)
        g = gu[:, :de]
        hmid = (g * _sigmoid(g)) * gu[:, de:]
        y = jnp.dot(hmid.astype(bf16), wd_scr[...], preferred_element_type=f32)
        for s in range(n_word):
            o_ref[pl.ds(s, TB, stride=n_word), :] = _pack_bf16_pair(y[:, s * LANES:(s + 1) * LANES],
                                                                   y[:, (n_word + s) * LANES:(n_word + s + 1) * LANES])


def _experts(xs, blk_expert, n_used, w_gate, w_up, w_down, layer, n_blocks, n_slab):
    d, de = w_gate.shape[2], w_gate.shape[3]
    by_expert = lambda i, be, nu: (layer, be[jnp.minimum(i, nu[0] - 1)], 0, 0)
    grid_spec = pltpu.PrefetchScalarGridSpec(
        num_scalar_prefetch=2,
        grid=(n_blocks,),
        in_specs=[pl.BlockSpec((TB * n_slab // 2, LANES), lambda i, be, nu: (jnp.minimum(i, nu[0] - 1), 0)),
                  pl.BlockSpec((None, 1, d, de), by_expert), pl.BlockSpec((None, 1, d, de), by_expert),
                  pl.BlockSpec((None, 1, de, d), by_expert)],
        out_specs=pl.BlockSpec((TB * n_slab // 2, LANES), lambda i, be, nu: (jnp.minimum(i, nu[0] - 1), 0)),
        scratch_shapes=[pltpu.VMEM((d, 2 * de), bf16), pltpu.VMEM((de, d), bf16)],
    )
    return pl.pallas_call(
        functools.partial(_expert_kernel, n_slab),
        grid_spec=grid_spec,
        out_shape=jax.ShapeDtypeStruct(xs.shape, xs.dtype),
        input_output_aliases={2: 0},
        compiler_params=_params(("arbitrary",)),
        name="experts",
    )(blk_expert, n_used, xs, w_gate, w_up, w_down)


def _combine_kernel(n_slab, final, norm_next, bounds, dest_ref, next_ref, x_ref, wgt_ref, g_ref, ys_ref, *rest):
    if norm_next:
        (gn_ref, wf_ref, bf_ref), rest = rest[:3], rest[3:]
        outs, (hn_ref, logf_ref, gbuf, sems) = rest[:-4], rest[-4:]
    else:
        outs, (gbuf, sems) = rest[:-2], rest[-2:]
    i = pl.program_id(0)
    n = dest_ref.shape[2]
    tm = x_ref.shape[0]
    slot = i % 2
    n_word = n_slab // 2

    def gather(idx_ref, s):
        def copy(tok, k):
            src = ys_ref.at[pl.ds(pl.multiple_of(idx_ref[0, 0, tok * TOP_K + k] * n_word, n_word), n_word), :]
            dst = gbuf.at[s, pl.ds(pl.multiple_of((k * tm + tok) * n_word, n_word), n_word), :]
            return pltpu.make_async_copy(src, dst, sems.at[s])
        _issue_rows(n, copy)

    pl.when(i == 0)(lambda: gather(dest_ref, slot))
    pl.when(i + 1 < pl.num_programs(0))(lambda: gather(next_ref, 1 - slot))
    pltpu.make_async_copy(ys_ref.at[pl.ds(0, n * n_word), :], gbuf.at[slot], sems.at[slot]).wait()

    ys = []
    for k in range(TOP_K):
        halves = [_unpack_bf16_pair(gbuf[slot, pl.ds(k * tm * n_word + s, tm, stride=n_word), :])
                  for s in range(n_word)]
        y = jnp.concatenate([lo for lo, _ in halves] + [hi for _, hi in halves], axis=1)
        ys.append(y * wgt_ref[:, k:k + 1])
    x = x_ref[...] + (ys[0] + ys[1])
    if final:
        x = _rms(x, g_ref[...])
    if norm_next:
        hn_ref[...] = _rms(x, gn_ref[...]).astype(bf16)
        logf_ref[...] = _log_forget(hn_ref[...], wf_ref[...], bf_ref[...])

    def store(k):
        outs[k][...] = x

    _when_segment(i, bounds, store)


def _combine(x, ys, dest, wgt, g, n_slab, final, seg_rows, norm_next=None):
    m, d = x.shape
    n_steps = m // TD
    dest3 = dest.reshape(n_steps, 1, TD * TOP_K)
    counts = [r // TD for r in seg_rows]
    bounds = [0]
    for cnt in counts:
        bounds.append(bounds[-1] + cnt)
    out_specs = [_seg_spec((TD, d), bounds[k], counts[k]) for k in range(len(seg_rows))]
    out_shape = [jax.ShapeDtypeStruct((r, d), f32) for r in seg_rows]
    idx_block = (1, 1, TD * TOP_K)
    extra_in, extra_specs = [], []
    if norm_next is not None:
        extra_in = list(norm_next)
        extra_specs = [pl.BlockSpec(a.shape, lambda i: (0, 0)) for a in norm_next]
        out_specs += [pl.BlockSpec((TD, d), lambda i: (i, 0)), pl.BlockSpec((TD, N_HEADS), lambda i: (i, 0))]
        out_shape += [jax.ShapeDtypeStruct((m, d), bf16), jax.ShapeDtypeStruct((m, N_HEADS), f32)]
    return pl.pallas_call(
        functools.partial(_combine_kernel, n_slab, final, norm_next is not None, tuple(bounds)),
        grid=(n_steps,),
        in_specs=[pl.BlockSpec(idx_block, lambda i: (i, 0, 0), memory_space=pltpu.SMEM),
                  pl.BlockSpec(idx_block, lambda i: (jnp.minimum(i + 1, n_steps - 1), 0, 0),
                               memory_space=pltpu.SMEM),
                  pl.BlockSpec((TD, d), lambda i: (i, 0)),
                  pl.BlockSpec((TD, LANES), lambda i: (i, 0)),
                  pl.BlockSpec((1, d), lambda i: (0, 0)),
                  pl.BlockSpec(memory_space=pl.ANY)] + extra_specs,
        out_specs=out_specs, out_shape=out_shape,
        scratch_shapes=[pltpu.VMEM((2, TD * TOP_K * n_slab // 2, LANES), jnp.uint32),
                        pltpu.SemaphoreType.DMA((2,))],
        compiler_params=_params(("arbitrary",)),
        name="combine",
    )(dest3, dest3, x, wgt, g, ys, *extra_in)


def kernel(x_prompt, x_sample, cache_a_k, cache_a_v, cache_b_k, cache_b_v, cache_b_logf, state_conv, norm_mix_g, w_in, b_in, rel_bias, conv_w, conv_b, conv_ln_g, conv_ln_b, w_proj_a, w_proj_b, w_proj_c, w_out, norm_ffn_g, w_router_group, b_router_group, w_router_expert, b_router_expert, w_e_gate, w_e_up, w_e_down, norm_final_g):
    nb_p, t_p, d = x_prompt.shape
    nb_s, t_s, _ = x_sample.shape
    depth = w_in.shape[0]
    past = cache_b_k.shape[2]
    a_rows = cache_a_k.shape[2]
    m_p, m_s = nb_p * t_p, nb_s * t_s
    m = m_p + m_s
    c_conv = d // 2
    n_slab = d // LANES
    tm = _row_tile(np.gcd(m_p, m_s), 512)
    tm_mix = _row_tile(np.gcd(m_p, m_s), 256)
    assert m_p % TD == 0 and m_s % TD == 0 and t_s % 16 == 0 and m_p % t_s == 0

    a_keep = min(WINDOW_A, t_p)
    x_segs = [x_prompt.reshape(m_p, d), x_sample.reshape(m_s, d)]
    kv_states = [jnp.zeros((depth, rows * N_HEADS, HEAD_DIM), f32)
                 for rows in (nb_p * a_keep, nb_p * a_keep, m_p, m_p, m_s, m_s, m_s, m_s)]
    p_states, s_states = [], []
    normed = None
    for l in range(depth):
        (qa, ka16, va16, qb, kb16, vb16, u, gates, logf), kv_states = _inproj(
            x_segs, norm_mix_g[l][None, :], w_in[l], b_in[l], kv_states, l, m_p, t_p, a_keep, tm, normed)

        ya_p = _band_attention_prompt(qa, ka16, va16, rel_bias[l], nb_p, t_p, 4 * CHUNK)
        kk = jnp.concatenate([cache_a_k[l].reshape(nb_s, a_rows, W_HEADS).astype(bf16),
                              ka16[m_p:].reshape(nb_s, t_s, W_HEADS)], axis=1).reshape(-1, W_HEADS)
        vv = jnp.concatenate([cache_a_v[l].reshape(nb_s, a_rows, W_HEADS).astype(bf16),
                              va16[m_p:].reshape(nb_s, t_s, W_HEADS)], axis=1).reshape(-1, W_HEADS)
        ya_s = _band_attention_sample(qa, kk, vv, rel_bias[l], nb_s, t_s, a_rows, m_p // t_s)

        logf_p = logf[:m_p].reshape(nb_p, t_p, N_HEADS)
        logf_s = logf[m_p:].reshape(nb_s, t_s, N_HEADS)
        cum_p = _cumsum_time(logf_p.transpose(0, 2, 1))
        f_p = cum_p.transpose(0, 2, 1).reshape(m_p, N_HEADS)
        yb_p = _fox_attention(_fox_expand("q", qb, f_p, m_p, tm), _fox_expand("k", kb16, f_p, m_p, tm),
                              _fox_expand("v", vb16, None, m_p, tm),
                              nb_p, t_p, t_p, _row_tile(t_p, 2048), _row_tile(t_p, 512))
        cum_s = _cumsum_time(jnp.concatenate([cache_b_logf[l].astype(f32), logf_s], axis=1).transpose(0, 2, 1))
        t_ks = past + t_s
        kk = jnp.concatenate([cache_b_k[l].reshape(nb_s, past, W_HEADS).astype(bf16),
                              kb16[m_p:].reshape(nb_s, t_s, W_HEADS)], axis=1).reshape(-1, W_HEADS)
        vv = jnp.concatenate([cache_b_v[l].reshape(nb_s, past, W_HEADS).astype(bf16),
                              vb16[m_p:].reshape(nb_s, t_s, W_HEADS)], axis=1).reshape(-1, W_HEADS)
        f_ks = cum_s.transpose(0, 2, 1)
        yb_s = _fox_attention(
            _fox_expand("q", qb, f_ks[:, past:].reshape(m_s, N_HEADS), m_s, t_s, blk0=m_p // t_s),
            _fox_expand("k", kk, f_ks.reshape(nb_s * t_ks, N_HEADS), nb_s * t_ks, t_ks),
            _fox_expand("v", vv, None, nb_s * t_ks, t_ks),
            nb_s, t_s, t_ks, t_s, t_ks)

        conv_args = (conv_w[l], conv_b[l][None, :], conv_ln_g[l][None, :], conv_ln_b[l][None, :])
        c_p = _conv_module(u, jnp.zeros((nb_p, CONV_HALO, c_conv), f32), *conv_args,
                           nb_p, t_p, _row_tile(t_p, 256), 0)
        init_s = jnp.pad(state_conv[l], ((0, 0), (CONV_HALO - (CONV_W - 1), 0), (0, 0)))
        c_s = _conv_module(u, init_s, *conv_args, nb_s, t_s, t_s, m_p // t_s)

        mixed = _mix(ya_p, ya_s, yb_p, yb_s, c_p, c_s, gates, w_proj_a[l].astype(bf16),
                     w_proj_b[l].astype(bf16), w_proj_c[l].astype(bf16), tm_mix)
        wr = jnp.pad(jnp.concatenate([w_router_group[l], w_router_expert[l]], axis=1),
                     ((0, 0), (0, LANES - N_GROUPS - N_EXPERTS)))
        wr_hi = wr.astype(bf16)
        wr_parts = jnp.concatenate([wr_hi, (wr - wr_hi.astype(f32)).astype(bf16)], axis=1)
        br = jnp.pad(jnp.concatenate([b_router_group[l], b_router_expert[l]]),
                     (0, LANES - N_GROUPS - N_EXPERTS))[None, :]
        x_mid, h2_slab, eid, wgt = _outproj(mixed, x_segs, w_out[l].astype(bf16), norm_ffn_g[l][None, :],
                                            wr_parts, br, tm_mix)

        dest, blk_expert, n_used, last_blk, n_blocks = _plan(eid[:, :TOP_K])
        xs = _dispatch(h2_slab, dest, last_blk, n_used, n_blocks, n_slab // 2, tm)
        ys = _experts(xs, blk_expert, n_used, w_e_gate, w_e_up, w_e_down, l, n_blocks, n_slab)
        final = l == depth - 1
        if final:
            x_segs = _combine(x_mid, ys, dest, wgt, norm_final_g[None, :], n_slab, True, (m_p, m_s))
        else:
            *x_segs, h_next, logf_next = _combine(
                x_mid, ys, dest, wgt, norm_final_g[None, :], n_slab, False, (m,),
                norm_next=(norm_mix_g[l + 1][None, :],) + _forget_weights(w_in[l + 1], b_in[l + 1]))
            normed = (h_next, logf_next)

        n_cs = c_conv // LANES
        u_p = jnp.stack([u[((b + 1) * t_p - (CONV_W - 1)) * n_cs:(b + 1) * t_p * n_cs] for b in range(nb_p)])
        u_p = u_p.reshape(nb_p, CONV_W - 1, c_conv)
        u_s = u[m_p * n_cs:].reshape(nb_s, t_s, c_conv)
        p_states.append((logf_p, u_p))
        s_states.append((logf_s, jnp.concatenate([state_conv[l], u_s], axis=1)[:, -(CONV_W - 1):]))

    y_prompt = x_segs[0].reshape(nb_p, t_p, d)
    y_sample = x_segs[1].reshape(nb_s, t_s, d)
    stack = lambda states, k: jnp.stack([st[k] for st in states], axis=0)
    heads = lambda a, nb, t: a.reshape(depth, nb, t, N_HEADS, HEAD_DIM)
    ka_p, va_p, kb_p, vb_p, ka_s, va_s, kb_s, vb_s = kv_states
    return (y_prompt, y_sample,
            heads(ka_p, nb_p, a_keep), heads(va_p, nb_p, a_keep), heads(kb_p, nb_p, t_p), heads(vb_p, nb_p, t_p),
            stack(p_states, 0), stack(p_states, 1),
            heads(ka_s, nb_s, t_s), heads(va_s, nb_s, t_s), heads(kb_s, nb_s, t_s), heads(vb_s, nb_s, t_s),
            stack(s_states, 0), stack(s_states, 1))
```

```python
import functools

import jax
import jax.numpy as jnp
import numpy as np
from jax import lax
from jax.experimental import pallas as pl
from jax.experimental.pallas import tpu as pltpu

f32 = jnp.float32
bf16 = jnp.bfloat16

HEAD_DIM = 64
N_HEADS = 8
W_HEADS = N_HEADS * HEAD_DIM
CHUNK = 64
WINDOW_A = 8 * CHUNK
REL_CLIP = 128
CONV_W = 31
CONV_HALO = 32
N_GROUPS = 4
EXPERTS_PER_GROUP = 8
N_EXPERTS = N_GROUPS * EXPERTS_PER_GROUP
TOP_K = 2
SCALE = HEAD_DIM ** -0.5
EPS = 1e-6
NEG_INF = -1e30
LANES = 128
MIB = 1024 * 1024


def _params(sem, vmem_mib=48):
    return pltpu.CompilerParams(dimension_semantics=sem, vmem_limit_bytes=vmem_mib * MIB)


def _row_tile(m, cap):
    t = cap
    while m % t:
        t //= 2
    return t


def _sigmoid(z):
    return 0.5 * jnp.tanh(0.5 * z) + 0.5


def _rms(x, g):
    return x * lax.rsqrt(jnp.mean(x * x, axis=-1, keepdims=True) + EPS) * g


def _pack_bf16_pair(lo, hi):
    def rounded(x):
        bits = lax.bitcast_convert_type(x, jnp.uint32)
        return bits + jnp.uint32(0x7FFF) + ((bits >> 16) & jnp.uint32(1))
    return (rounded(hi) & jnp.uint32(0xFFFF0000)) | (rounded(lo) >> 16)


def _unpack_bf16_pair(word):
    lo = lax.bitcast_convert_type(word << 16, f32)
    hi = lax.bitcast_convert_type(word & jnp.uint32(0xFFFF0000), f32)
    return lo, hi


def _when_segment(i, bounds, fn):
    for k in range(len(bounds) - 1):
        pl.when((i >= bounds[k]) & (i < bounds[k + 1]))(functools.partial(fn, k))


def _seg_spec(block, start, count, width_axes=1):
    zeros = (0,) * width_axes
    return pl.BlockSpec(block, lambda i, *_: (jnp.clip(i - start, 0, count - 1),) + zeros)


TN = 1024
SEG_PER_TILE = TN // W_HEADS


def _log_forget(h, w_f, b_f):
    zf = jnp.dot(h, w_f, preferred_element_type=f32) + b_f
    return (jnp.minimum(zf, 0.0) - jnp.log1p(jnp.exp(-jnp.abs(zf))))[:, :N_HEADS]


def _forget_weights(w_in, b_in):
    f0 = 6 * W_HEADS
    w_f = jnp.pad(w_in[:, f0:f0 + N_HEADS], ((0, 0), (0, LANES - N_HEADS))).astype(bf16)
    b_f = jnp.pad(b_in[f0:f0 + N_HEADS], (0, LANES - N_HEADS))[None, :].astype(f32)
    return w_f, b_f


def _norm_kernel(n_seg, bounds, *refs):
    xs = refs[:n_seg]
    g_ref, wf_ref, bf_ref, h_ref, logf = refs[n_seg:]

    def norm(k):
        h_ref[...] = _rms(xs[k][...], g_ref[...]).astype(bf16)

    _when_segment(pl.program_id(0), bounds, norm)
    logf[...] = _log_forget(h_ref[...], wf_ref[...], bf_ref[...])


def _norm(x_segs, g, w_f, b_f, tm):
    d = x_segs[0].shape[1]
    m = sum(a.shape[0] for a in x_segs)
    counts = [a.shape[0] // tm for a in x_segs]
    bounds = [0]
    for cnt in counts:
        bounds.append(bounds[-1] + cnt)
    const = lambda shape: pl.BlockSpec(shape, lambda i: (0, 0))
    return pl.pallas_call(
        functools.partial(_norm_kernel, len(x_segs), tuple(bounds)),
        grid=(m // tm,),
        in_specs=[_seg_spec((tm, d), bounds[k], counts[k]) for k in range(len(x_segs))] +
                 [const((1, d)), const((d, LANES)), const((1, LANES))],
        out_specs=[pl.BlockSpec((tm, d), lambda i: (i, 0)), pl.BlockSpec((tm, N_HEADS), lambda i: (i, 0))],
        out_shape=[jax.ShapeDtypeStruct((m, d), bf16), jax.ShapeDtypeStruct((m, N_HEADS), f32)],
        compiler_params=_params(("arbitrary",)),
        name="norm",
    )(*x_segs, g, w_f, b_f)


def _qkv_kernel(with_state, n_prompt, h_ref, w_ref, b_ref, *refs):
    outs = refs[2 * sum(with_state):]
    is_prompt = pl.program_id(0) < n_prompt
    for prompt_rows in (True, False):
        @pl.when(is_prompt if prompt_rows else ~is_prompt)
        def _(prompt_rows=prompt_rows):
            z = jnp.dot(h_ref[...], w_ref[...], preferred_element_type=f32) + b_ref[...]
            k = 0
            for n, has_state in enumerate(with_state):
                zn = z[:, n * W_HEADS:(n + 1) * W_HEADS]
                outs[k][...] = zn.astype(bf16)
                k += 1
                if has_state:
                    state = outs[k if prompt_rows else k + 1]
                    for hh in range(N_HEADS):
                        state[pl.ds(hh, zn.shape[0], stride=N_HEADS), :] = zn[:, hh * HEAD_DIM:(hh + 1) * HEAD_DIM]
                    k += 2


def _qkv_tile(h, w_qkv, b_qkv, tile, seg_states, layer, m_p, t_p, a_keep, tm):
    m, d = h.shape
    n_p, n_s = m_p // tm, (m - m_p) // tm
    per_seq, keep = t_p // tm, a_keep // tm

    def tail_rows(i):
        ip = jnp.minimum(i, n_p - 1)
        return (layer, (ip // per_seq) * keep + jnp.maximum(ip % per_seq - (per_seq - keep), 0), 0)

    state_block = (None, tm * N_HEADS, HEAD_DIM)
    tail_spec = pl.BlockSpec(state_block, tail_rows)
    prompt_spec = pl.BlockSpec(state_block, lambda i: (layer, jnp.minimum(i, n_p - 1), 0))
    sample_spec = pl.BlockSpec(state_block, lambda i: (layer, jnp.clip(i - n_p, 0, n_s - 1), 0))
    states, out_specs, out_shape, state_out_pos = [], [], [], []
    for seg in seg_states:
        out_specs.append(pl.BlockSpec((tm, W_HEADS), lambda i: (i, 0)))
        out_shape.append(jax.ShapeDtypeStruct((m, W_HEADS), bf16))
        if seg is not None:
            buf_p, buf_s, keep_tail = seg
            for buf, spec in ((buf_p, tail_spec if keep_tail else prompt_spec), (buf_s, sample_spec)):
                state_out_pos.append(len(out_specs))
                states.append(buf)
                out_specs.append(spec)
                out_shape.append(jax.ShapeDtypeStruct(buf.shape, buf.dtype))
    outs = pl.pallas_call(
        functools.partial(_qkv_kernel, tuple(seg is not None for seg in seg_states), n_p),
        grid=(m // tm,),
        in_specs=[pl.BlockSpec((tm, d), lambda i: (i, 0)),
                  pl.BlockSpec((d, TN), lambda i: (0, tile)),
                  pl.BlockSpec((1, TN), lambda i: (0, tile))] + [pl.BlockSpec(memory_space=pl.ANY)] * len(states),
        out_specs=out_specs, out_shape=out_shape,
        input_output_aliases={3 + k: pos for k, pos in enumerate(state_out_pos)},
        compiler_params=_params(("arbitrary",)),
        name="qkv",
    )(h, w_qkv, b_qkv, *states)
    copies = [o for k, o in enumerate(outs) if k not in state_out_pos]
    return copies, [outs[pos] for pos in state_out_pos]


def _glu_kernel(n_slab, h_ref, w_ref, b_ref, u_ref):
    tm = h_ref.shape[0]
    c = n_slab * LANES
    z = jnp.dot(h_ref[...], w_ref[...], preferred_element_type=f32) + b_ref[...]
    glu = z[:, :c] * _sigmoid(z[:, c:])
    for s in range(n_slab):
        u_ref[pl.ds(s, tm, stride=n_slab), :] = glu[:, s * LANES:(s + 1) * LANES]


def _glu(h, w, b, tm):
    m, d = h.shape
    c = w.shape[1] // 2
    n_slab = c // LANES
    return pl.pallas_call(
        functools.partial(_glu_kernel, n_slab),
        grid=(m // tm,),
        in_specs=[pl.BlockSpec((tm, d), lambda i: (i, 0)), pl.BlockSpec((d, 2 * c), lambda i: (0, 0)),
                  pl.BlockSpec((1, 2 * c), lambda i: (0, 0))],
        out_specs=pl.BlockSpec((tm * n_slab, LANES), lambda i: (i, 0)),
        out_shape=jax.ShapeDtypeStruct((m * n_slab, LANES), f32),
        compiler_params=_params(("arbitrary",)),
        name="glu",
    )(h, w, b)


def _gates_kernel(h_ref, w_ref, b_ref, o_ref):
    z = jnp.dot(h_ref[...], w_ref[...], preferred_element_type=f32) + b_ref[...]
    o_ref[...] = _sigmoid(z).astype(o_ref.dtype)


def _gates(h, w, b, tm):
    m, d = h.shape
    n = w.shape[1]
    tn = 2 * TN if n % (2 * TN) == 0 else TN
    return pl.pallas_call(
        _gates_kernel,
        grid=(n // tn, m // tm),
        in_specs=[pl.BlockSpec((tm, d), lambda j, i: (i, 0)), pl.BlockSpec((d, tn), lambda j, i: (0, j)),
                  pl.BlockSpec((1, tn), lambda j, i: (0, j))],
        out_specs=pl.BlockSpec((tm, tn), lambda j, i: (i, j)),
        out_shape=jax.ShapeDtypeStruct((m, n), bf16),
        compiler_params=_params(("arbitrary", "arbitrary")),
        name="gates",
    )(h, w, b)


def _inproj(x_segs, g, w_in, b_in, states, layer, m_p, t_p, a_keep, tm, normed=None):
    d = x_segs[0].shape[1]
    c_conv = d // 2
    n_qkv = 6 * W_HEADS
    f0 = n_qkv
    c0 = f0 + N_HEADS
    g0 = c0 + 2 * c_conv
    cast = lambda a: a.astype(bf16)
    row = lambda a: a[None, :].astype(f32)
    h, logf = normed if normed is not None else _norm(x_segs, g, *_forget_weights(w_in, b_in), tm)

    ka_p, va_p, kb_p, vb_p, ka_s, va_s, kb_s, vb_s = states
    w_qkv, b_qkv = cast(w_in[:, :n_qkv]), row(b_in[:n_qkv])
    tile = functools.partial(_qkv_tile, h, w_qkv, b_qkv, layer=layer, m_p=m_p, t_p=t_p, a_keep=a_keep, tm=tm)
    assert SEG_PER_TILE == 2
    (qa, ka16), (ka_p, ka_s) = tile(0, [None, (ka_p, ka_s, True)])
    (va16, qb), (va_p, va_s) = tile(1, [(va_p, va_s, True), None])
    (kb16, vb16), (kb_p, kb_s, vb_p, vb_s) = tile(2, [(kb_p, kb_s, False), (vb_p, vb_s, False)])
    u = _glu(h, cast(w_in[:, c0:g0]), row(b_in[c0:g0]), tm)
    gates = _gates(h, cast(w_in[:, g0:]), row(b_in[g0:]), tm)
    return (qa, ka16, va16, qb, kb16, vb16, u, gates, logf), [ka_p, va_p, kb_p, vb_p, ka_s, va_s, kb_s, vb_s]


TC = 512


def _cumsum_kernel(x_ref, o_ref, carry):
    @pl.when(pl.program_id(1) == 0)
    def _():
        carry[...] = jnp.zeros_like(carry)

    blk = x_ref[0]
    r = lax.broadcasted_iota(jnp.int32, (TC, TC), 0)
    c = lax.broadcasted_iota(jnp.int32, (TC, TC), 1)
    tri = jnp.where(r <= c, 1.0, 0.0).astype(bf16)
    cs = carry[:, 0:1]
    rest = blk
    for _ in range(F_PARTS):
        part = rest.astype(bf16)
        rest = rest - part.astype(f32)
        cs = cs + jnp.dot(part, tri, preferred_element_type=f32)
    o_ref[0] = cs
    carry[...] = jnp.broadcast_to(cs[:, TC - 1:TC], carry.shape)


def _cumsum_time(x):
    nb, h, t = x.shape
    tp = -(-t // TC) * TC
    xp = jnp.pad(x, ((0, 0), (0, 0), (0, tp - t)))
    out = pl.pallas_call(
        _cumsum_kernel,
        grid=(nb, tp // TC),
        in_specs=[pl.BlockSpec((1, h, TC), lambda b, k: (b, 0, k))],
        out_specs=pl.BlockSpec((1, h, TC), lambda b, k: (b, 0, k)),
        out_shape=jax.ShapeDtypeStruct((nb, h, tp), f32),
        scratch_shapes=[pltpu.VMEM((h, LANES), f32)],
        compiler_params=_params(("arbitrary", "arbitrary")),
        name="cumsum",
    )(xp)
    return out[:, :, :t]


F_PARTS = 3


def _spare_base(h):
    return HEAD_DIM * (1 - h % 2)


def _fox_expand_kernel(kinds, *refs):
    n_kind = len(kinds)
    n_place = sum(kind != "v" for kind in kinds)
    x_refs, f_ref = refs[:n_kind], refs[n_kind]
    place_refs = list(refs[n_kind + 1:n_kind + 1 + n_place])
    o_refs = refs[n_kind + 1 + n_place:]
    tm = x_refs[0].shape[0]
    lane = lax.broadcasted_iota(jnp.int32, (tm, LANES), 1)
    if n_place:
        rest = f_ref[...]
        stack = jnp.where(lane < (F_PARTS + 1) * N_HEADS, 1.0, 0.0)
        for n in range(F_PARTS):
            part = rest.astype(bf16).astype(f32)
            rest = rest - part
            stack = jnp.where((lane >= n * N_HEADS) & (lane < (n + 1) * N_HEADS), part, stack)
        stack = stack.astype(bf16)
    for kind, x_ref, o_ref in zip(kinds, x_refs, o_refs):
        if kind != "v":
            spare_all = jnp.dot(stack, place_refs.pop(0)[...], preferred_element_type=f32)
        for h in range(N_HEADS):
            pair = x_ref[:, (h // 2) * LANES:(h // 2 + 1) * LANES].astype(f32)
            if kind == "q":
                pair = pair * SCALE
            if kind == "v":
                spare = jnp.where(lane == _spare_base(h), 1.0, 0.0)
            else:
                spare = spare_all[:, h * LANES:(h + 1) * LANES]
            own = (lane < HEAD_DIM) if h % 2 == 0 else (lane >= HEAD_DIM)
            o_ref[h] = jnp.where(own, pair, spare).astype(o_ref.dtype)


def _fox_placement(kind):
    place = np.zeros((LANES, N_HEADS * LANES), np.float32)
    for h in range(N_HEADS):
        base = h * LANES + _spare_base(h)
        for n in range(F_PARTS):
            if kind == "q":
                place[n * N_HEADS + h, base + n] = 1.0
                place[F_PARTS * N_HEADS + h, base + F_PARTS + n] = 1.0
            else:
                place[F_PARTS * N_HEADS + h, base + n] = 1.0
                place[n * N_HEADS + h, base + F_PARTS + n] = -1.0
    return jnp.asarray(place, bf16)


def _fox_expand(kinds, xs, f, rows, tm, blk0=0):
    places = [_fox_placement(kind) for kind in kinds if kind != "v"]
    slab = jax.ShapeDtypeStruct((N_HEADS, rows, LANES), bf16)
    return pl.pallas_call(
        functools.partial(_fox_expand_kernel, tuple(kinds)),
        grid=(rows // tm,),
        in_specs=[pl.BlockSpec((tm, W_HEADS), lambda i: (blk0 + i, 0))] * len(kinds) +
                 [pl.BlockSpec((tm, LANES), lambda i: (i, 0))] +
                 [pl.BlockSpec((LANES, N_HEADS * LANES), lambda i: (0, 0))] * len(places),
        out_specs=[pl.BlockSpec((N_HEADS, tm, LANES), lambda i: (0, i, 0))] * len(kinds),
        out_shape=[slab] * len(kinds),
        compiler_params=_params(("arbitrary",)),
        name="fox_expand_" + "".join(kinds),
    )(*xs, jnp.tile(f, (1, LANES // N_HEADS)), *places)


FOX_HEADS_PER_TRIP = 2


def _fox_kernel(tq, tk, off, q_ref, k_ref, v_ref, o_ref, m_scr, acc_scr):
    i = pl.program_id(1)
    j = pl.program_id(2)

    @pl.when(j == 0)
    def _():
        m_scr[...] = jnp.full_like(m_scr, NEG_INF)
        acc_scr[...] = jnp.zeros_like(acc_scr)

    q_first = i * tq + off
    q_last = q_first + tq - 1
    k_first = j * tk
    k_last = k_first + tk - 1

    def body(row0, masked):
        rows = slice(row0, tq)
        if masked:
            kpos = k_first + lax.broadcasted_iota(jnp.int32, (tq - row0, tk), 1)
            qpos = q_first + row0 + lax.broadcasted_iota(jnp.int32, (tq - row0, tk), 0)
            vis = kpos <= qpos

        def head(h):
            s = lax.dot_general(q_ref[h, rows], k_ref[h], (((1,), (1,)), ((), ())), preferred_element_type=f32)
            if masked:
                s = jnp.where(vis, s, NEG_INF)
            m_old = m_scr[h, rows]
            m_new = jnp.maximum(m_old, jnp.max(s, axis=-1, keepdims=True))
            pr = jnp.exp(s - m_new[:, 0:1])
            pv = jnp.dot(pr.astype(bf16), v_ref[h], preferred_element_type=f32)
            acc_scr[h, rows] = jnp.exp(m_old - m_new) * acc_scr[h, rows] + pv
            m_scr[h, rows] = m_new

        def trip(g, carry):
            for n in range(FOX_HEADS_PER_TRIP):
                head(g * FOX_HEADS_PER_TRIP + n)
            return carry

        lax.fori_loop(0, N_HEADS // FOX_HEADS_PER_TRIP, trip, 0)

    pl.when(k_last <= q_first)(functools.partial(body, 0, False))
    if tq % tk == 0 and off % tk == 0:
        for c in range(tq // tk):
            pl.when(k_first == q_first + c * tk)(functools.partial(body, c * tk, True))
    else:
        pl.when((k_first <= q_last) & (k_last > q_first))(functools.partial(body, 0, True))

    @pl.when(j == pl.num_programs(2) - 1)
    def _():
        lane = lax.broadcasted_iota(jnp.int32, (tq, LANES), 1)
        for p in range(N_HEADS // 2):
            even = acc_scr[2 * p]
            odd = acc_scr[2 * p + 1]
            even = even / even[:, _spare_base(0):_spare_base(0) + 1]
            odd = odd / odd[:, _spare_base(1):_spare_base(1) + 1]
            o_ref[:, p * LANES:(p + 1) * LANES] = jnp.where(lane < HEAD_DIM, even, odd).astype(o_ref.dtype)


def _fox_attention(q, k, v, nb, t_q, t_k, tq, tk):
    nq, nk = t_q // tq, t_k // tk
    off = t_k - t_q

    def last_k(i):
        return jnp.minimum((i * tq + tq - 1 + off) // tk, nk - 1)

    kv_spec = pl.BlockSpec((N_HEADS, tk, LANES), lambda b, i, j: (0, b * nk + jnp.minimum(j, last_k(i)), 0))
    return pl.pallas_call(
        functools.partial(_fox_kernel, tq, tk, off),
        grid=(nb, nq, nk),
        in_specs=[pl.BlockSpec((N_HEADS, tq, LANES), lambda b, i, j: (0, b * nq + i, 0)), kv_spec, kv_spec],
        out_specs=pl.BlockSpec((tq, W_HEADS), lambda b, i, j: (b * nq + i, 0)),
        out_shape=jax.ShapeDtypeStruct((nb * t_q, W_HEADS), bf16),
        scratch_shapes=[pltpu.VMEM((N_HEADS, tq, LANES), f32), pltpu.VMEM((N_HEADS, tq, LANES), f32)],
        compiler_params=_params(("arbitrary", "arbitrary", "arbitrary")),
        name="fox",
    )(q, k, v)


def _band_kernel(rows, gq, wk, has_prev, *refs):
    if has_prev:
        q_ref, kp_ref, kc_ref, vp_ref, vc_ref, bias_ref, o_ref, k_scr, v_scr = refs
        k_scr[0:WINDOW_A] = kp_ref[...]
        k_scr[WINDOW_A:WINDOW_A + rows] = kc_ref[...]
        v_scr[0:WINDOW_A] = vp_ref[...]
        v_scr[WINDOW_A:WINDOW_A + rows] = vc_ref[...]
        k_src, v_src = k_scr, v_scr
    else:
        q_ref, k_src, v_src, bias_ref, o_ref = refs
    i = pl.program_id(1)
    lane = lax.broadcasted_iota(jnp.int32, (gq, LANES), 1)
    low = lane < HEAD_DIM

    def attend(before_start):
        for g in range(rows // gq):
            r0 = g * gq
            if before_start:
                key_pos = (i - 1) * WINDOW_A + r0 + lax.broadcasted_iota(jnp.int32, (gq, wk), 1)
                vis = key_pos >= 0
            for p in range(N_HEADS // 2):
                cols = slice(p * LANES, (p + 1) * LANES)
                q2 = q_ref[r0:r0 + gq, cols] * SCALE
                kw = k_src[r0:r0 + wk, cols]
                vw = v_src[r0:r0 + wk, cols]
                outs = []
                for half in range(2):
                    h = 2 * p + half
                    qm = jnp.where(low if half == 0 else ~low, q2, jnp.zeros_like(q2))
                    s = lax.dot_general(qm, kw, (((1,), (1,)), ((), ())), preferred_element_type=f32)
                    s = s + bias_ref[h]
                    if before_start:
                        s = jnp.where(vis, s, NEG_INF)
                    m = jnp.max(s, axis=-1, keepdims=True)
                    pr = jnp.exp(s - m)
                    l = jnp.sum(pr, axis=-1, keepdims=True)
                    pv = jnp.dot(pr.astype(bf16), vw, preferred_element_type=f32)
                    outs.append(pv / l)
                o_ref[r0:r0 + gq, cols] = jnp.where(low, outs[0], outs[1]).astype(o_ref.dtype)

    if has_prev:
        pl.when(i == 0)(functools.partial(attend, True))
        pl.when(i > 0)(functools.partial(attend, False))
    else:
        attend(False)


def _rel_bias_table(rel_bias, gq, wk, q_shift):
    period = wk + gq
    j = np.arange(period)
    k = np.where(j < wk, j, j - period)
    line = rel_bias.astype(f32)[:, np.clip(q_shift - k, -REL_CLIP, REL_CLIP) + REL_CLIP]
    tiled = jnp.tile(line, (1, gq))[:, :gq * (period - 1)]
    return tiled.reshape(-1, gq, period - 1)[:, :, :wk]


def _band_bias(rel_bias, gq, wk, q_shift):
    r = np.arange(gq)[:, None]
    s = np.arange(wk)[None, :]
    band0 = (r // CHUNK) * CHUNK + q_shift - WINDOW_A
    ok = (s >= band0) & (s < band0 + WINDOW_A + CHUNK)
    return jnp.where(ok[None], _rel_bias_table(rel_bias, gq, wk, q_shift), NEG_INF)


def _band_attention_prompt(q, k, v, rel_bias, nb, t, gq):
    rows = WINDOW_A
    wk = WINDOW_A + gq
    n_steps = t // rows
    bias = _band_bias(rel_bias, gq, wk, WINDOW_A)
    cur = pl.BlockSpec((rows, W_HEADS), lambda b, i: (b * n_steps + i, 0))
    prev = pl.BlockSpec((rows, W_HEADS), lambda b, i: (b * n_steps + jnp.maximum(i - 1, 0), 0))
    return pl.pallas_call(
        functools.partial(_band_kernel, rows, gq, wk, True),
        grid=(nb, n_steps),
        in_specs=[cur, prev, cur, prev, cur,
                  pl.BlockSpec((N_HEADS, gq, wk), lambda b, i: (0, 0, 0))],
        out_specs=cur,
        out_shape=jax.ShapeDtypeStruct((nb * t, W_HEADS), bf16),
        scratch_shapes=[pltpu.VMEM((2 * rows, W_HEADS), bf16), pltpu.VMEM((2 * rows, W_HEADS), bf16)],
        compiler_params=_params(("arbitrary", "arbitrary")),
        name="band_prompt",
    )(q, k, k, v, v, bias)


def _band_attention_sample(q, kk, vv, rel_bias, nb, s_new, l_cache, q_blk0):
    wk = l_cache + s_new
    bias = _rel_bias_table(rel_bias, s_new, wk, l_cache)
    return pl.pallas_call(
        functools.partial(_band_kernel, s_new, s_new, wk, False),
        grid=(nb, 1),
        in_specs=[pl.BlockSpec((s_new, W_HEADS), lambda b, i: (q_blk0 + b, 0)),
                  pl.BlockSpec((wk, W_HEADS), lambda b, i: (b, 0)),
                  pl.BlockSpec((wk, W_HEADS), lambda b, i: (b, 0)),
                  pl.BlockSpec((N_HEADS, s_new, wk), lambda b, i: (0, 0, 0))],
        out_specs=pl.BlockSpec((s_new, W_HEADS), lambda b, i: (b, 0)),
        out_shape=jax.ShapeDtypeStruct((nb * s_new, W_HEADS), bf16),
        compiler_params=_params(("arbitrary", "arbitrary")),
        name="band_sample",
    )(q, kk, vv, bias)


CONV_STRIP = 32


def _conv_kernel(tt, n_slab, init_ref, u_ref, w_ref, cb_ref, g_ref, b_ref, o_ref, ubuf, acc_scr):
    halo = CONV_HALO * n_slab

    @pl.when(pl.program_id(1) == 0)
    def _():
        ubuf[0:halo] = init_ref[0]

    ubuf[halo:halo + tt * n_slab] = u_ref[...]
    rs = min(CONV_STRIP, tt)
    first = CONV_HALO - (CONV_W - 1)

    def per_step(slab):
        return jnp.broadcast_to(slab[None], (rs, n_slab, LANES)).reshape(rs * n_slab, LANES)

    for s in range(tt // rs):
        acc = per_step(cb_ref[...])
        for j in range(CONV_W):
            r0 = (s * rs + first + j) * n_slab
            acc = acc + per_step(w_ref[j * n_slab:(j + 1) * n_slab, :]) * ubuf[r0:r0 + rs * n_slab, :]
        acc_scr[...] = acc
        rows = jnp.concatenate([acc_scr[pl.ds(q, rs, stride=n_slab), :] for q in range(n_slab)], axis=1)
        mu = jnp.mean(rows, axis=-1, keepdims=True)
        cen = rows - mu
        var = jnp.mean(cen * cen, axis=-1, keepdims=True)
        y = cen * lax.rsqrt(var + EPS) * g_ref[...] + b_ref[...]
        o_ref[s * rs:(s + 1) * rs, :] = (y * _sigmoid(y)).astype(o_ref.dtype)
    if tt >= CONV_HALO:
        ubuf[0:halo] = ubuf[tt * n_slab:tt * n_slab + halo]


def _conv_module(u_slab, init, conv_w, conv_b, ln_g, ln_b, nb, t, tt, blk0):
    c = conv_w.shape[1]
    n_slab = c // LANES
    n_t = t // tt
    vec = pl.BlockSpec((1, c), lambda b, i: (0, 0))
    return pl.pallas_call(
        functools.partial(_conv_kernel, tt, n_slab),
        grid=(nb, n_t),
        in_specs=[pl.BlockSpec((1, CONV_HALO * n_slab, LANES), lambda b, i: (b, 0, 0)),
                  pl.BlockSpec((tt * n_slab, LANES), lambda b, i: (blk0 + b * n_t + i, 0)),
                  pl.BlockSpec((CONV_W * n_slab, LANES), lambda b, i: (0, 0)),
                  pl.BlockSpec((n_slab, LANES), lambda b, i: (0, 0)), vec, vec],
        out_specs=pl.BlockSpec((tt, c), lambda b, i: (b * n_t + i, 0)),
        out_shape=jax.ShapeDtypeStruct((nb * t, c), bf16),
        scratch_shapes=[pltpu.VMEM(((CONV_HALO + tt) * n_slab, LANES), f32),
                        pltpu.VMEM((min(CONV_STRIP, tt) * n_slab, LANES), f32)],
        compiler_params=_params(("arbitrary", "arbitrary")),
        name="conv",
    )(init.reshape(nb, CONV_HALO * n_slab, LANES), u_slab, conv_w.reshape(CONV_W * n_slab, LANES),
      conv_b.reshape(n_slab, LANES), ln_g, ln_b)


def _mix_kernel(bounds, ya_p, ya_s, yb_p, yb_s, c_p, c_s, gates, pa, pb, pc, o_ref):
    d = o_ref.shape[1]

    def go(k):
        ya, yb, c = ((ya_p, yb_p, c_p), (ya_s, yb_s, c_s))[k]
        a = jnp.dot(ya[...], pa[...], preferred_element_type=f32)
        mixed = gates[:, 0:d].astype(f32) * a
        b = jnp.dot(yb[...], pb[...], preferred_element_type=f32)
        mixed = mixed + gates[:, d:2 * d].astype(f32) * b
        cc = jnp.dot(c[...], pc[...], preferred_element_type=f32)
        mixed = mixed + gates[:, 2 * d:3 * d].astype(f32) * cc
        o_ref[...] = mixed.astype(o_ref.dtype)

    _when_segment(pl.program_id(0), bounds, go)


def _mix(ya_p, ya_s, yb_p, yb_s, c_p, c_s, gates, pa, pb, pc, tm):
    m, d3 = gates.shape
    d = d3 // 3
    n_p, n_s = ya_p.shape[0] // tm, ya_s.shape[0] // tm
    bounds = (0, n_p, n_p + n_s)
    c_conv = c_p.shape[1]
    const = lambda shape: pl.BlockSpec(shape, lambda i: (0, 0))
    return pl.pallas_call(
        functools.partial(_mix_kernel, bounds),
        grid=(m // tm,),
        in_specs=[_seg_spec((tm, W_HEADS), 0, n_p), _seg_spec((tm, W_HEADS), n_p, n_s),
                  _seg_spec((tm, W_HEADS), 0, n_p), _seg_spec((tm, W_HEADS), n_p, n_s),
                  _seg_spec((tm, c_conv), 0, n_p), _seg_spec((tm, c_conv), n_p, n_s),
                  pl.BlockSpec((tm, d3), lambda i: (i, 0)),
                  const((W_HEADS, d)), const((W_HEADS, d)), const((c_conv, d))],
        out_specs=pl.BlockSpec((tm, d), lambda i: (i, 0)),
        out_shape=jax.ShapeDtypeStruct((m, d), bf16),
        compiler_params=_params(("arbitrary",)),
        name="mix",
    )(ya_p, ya_s, yb_p, yb_s, c_p, c_s, gates, pa, pb, pc)


def _route(logits):
    shape = logits.shape
    lane = lax.broadcasted_iota(jnp.int32, shape, 1)
    lane_f = lane.astype(f32)
    big = float(LANES)
    gl = jnp.where(lane < N_GROUPS, logits, -jnp.inf)
    g_max = jnp.max(gl, axis=-1, keepdims=True)
    g_idx = jnp.min(jnp.where(gl == g_max, lane_f, big), axis=-1, keepdims=True)
    g_sum = jnp.sum(jnp.exp(gl - g_max), axis=-1, keepdims=True)
    g_w = 1.0 / g_sum
    lo = N_GROUPS + g_idx * EXPERTS_PER_GROUP
    el = jnp.where((lane_f >= lo) & (lane_f < lo + EXPERTS_PER_GROUP), logits, -jnp.inf)
    m1 = jnp.max(el, axis=-1, keepdims=True)
    i1 = jnp.min(jnp.where(el == m1, lane_f, big), axis=-1, keepdims=True)
    el2 = jnp.where(lane_f == i1, -jnp.inf, el)
    m2 = jnp.max(el2, axis=-1, keepdims=True)
    i2 = jnp.min(jnp.where(el2 == m2, lane_f, big), axis=-1, keepdims=True)
    e21 = jnp.exp(m2 - m1)
    den = 1.0 + e21
    w1 = g_w * (1.0 / den)
    w2 = g_w * (e21 / den)
    eid = jnp.where(lane == 0, i1 - N_GROUPS, jnp.where(lane == 1, i2 - N_GROUPS, 0.0)).astype(jnp.int32)
    wgt = jnp.where(lane == 0, w1, jnp.where(lane == 1, w2, 0.0))
    return eid, wgt


def _outproj_kernel(n_slab, n_seg, bounds, *refs):
    xs = refs[:n_seg]
    mixed, wo, g2, wr, br, xo, h2o, eid_o, wgt_o = refs[n_seg:]
    tm = mixed.shape[0]
    i = pl.program_id(0)
    x = xs[0][...]
    for k in range(1, n_seg):
        x = jnp.where(i >= bounds[k], xs[k][...], x)
    xn = x + jnp.dot(mixed[...], wo[...], preferred_element_type=f32)
    xo[...] = xn
    h2 = _rms(xn, g2[...])
    n_word = n_slab // 2
    for s in range(n_word):
        h2o[pl.ds(s, tm, stride=n_word), :] = _pack_bf16_pair(h2[:, s * LANES:(s + 1) * LANES],
                                                              h2[:, (n_word + s) * LANES:(n_word + s + 1) * LANES])
    h_hi = h2.astype(bf16)
    h_lo = (h2 - h_hi.astype(f32)).astype(bf16)
    hi = jnp.dot(h_hi, wr[...], preferred_element_type=f32)
    lo = jnp.dot(h_lo, wr[:, :LANES], preferred_element_type=f32)
    logits = hi[:, :LANES] + (hi[:, LANES:] + lo) + br[...]
    eid, wgt = _route(logits)
    eid_o[...] = eid
    wgt_o[...] = wgt


def _outproj(mixed, x_segs, wo, g2, wr, br, tm):
    m, d = mixed.shape
    n_slab = d // LANES
    counts = [a.shape[0] // tm for a in x_segs]
    bounds = [0]
    for cnt in counts:
        bounds.append(bounds[-1] + cnt)
    const = lambda shape: pl.BlockSpec(shape, lambda i: (0, 0))
    row = lambda w: pl.BlockSpec((tm, w), lambda i: (i, 0))
    sds = jax.ShapeDtypeStruct
    return pl.pallas_call(
        functools.partial(_outproj_kernel, n_slab, len(x_segs), tuple(bounds)),
        grid=(m // tm,),
        in_specs=[_seg_spec((tm, d), bounds[k], counts[k]) for k in range(len(x_segs))] +
                 [row(d), const((d, d)), const((1, d)), const((d, 2 * LANES)), const((1, LANES))],
        out_specs=[row(d), pl.BlockSpec((tm * n_slab // 2, LANES), lambda i: (i, 0)), row(LANES), row(LANES)],
        out_shape=[sds((m, d), f32), sds((m * n_slab // 2, LANES), jnp.uint32), sds((m, LANES), jnp.int32),
                   sds((m, LANES), f32)],
        compiler_params=_params(("arbitrary",)),
        name="outproj",
    )(*x_segs, mixed, wo, g2, wr, br)


TB = 256
TD = 256


def _plan(eid):
    flat_e = eid.reshape(-1)
    n_assign = flat_e.shape[0]
    onehot = (flat_e[:, None] == jnp.arange(N_EXPERTS, dtype=jnp.int32)[None, :]).astype(jnp.int32)
    csum = jnp.cumsum(onehot, axis=0)
    counts = csum[-1]
    rank = jnp.sum(onehot * csum, axis=1) - 1
    n_blk_e = (counts + TB - 1) // TB
    blk_end = jnp.cumsum(n_blk_e)
    blk_start = blk_end - n_blk_e
    dest = blk_start[flat_e] * TB + rank
    n_blocks = -(-n_assign // TB) + N_EXPERTS
    blk_ids = jnp.arange(n_blocks, dtype=jnp.int32)
    blk_expert = jnp.minimum(jnp.sum((blk_end[None, :] <= blk_ids[:, None]).astype(jnp.int32), axis=1),
                             N_EXPERTS - 1)
    last_blk = jnp.where(n_blk_e > 0, blk_end - 1, -1).astype(jnp.int32)
    return dest.astype(jnp.int32), blk_expert, blk_end[-1:].astype(jnp.int32), last_blk, n_blocks


DMA_UNROLL = 8


def _issue_rows(n, copy):
    per_trip = DMA_UNROLL // TOP_K

    def trip(t, carry):
        for r in range(per_trip):
            for k in range(TOP_K):
                copy(t * per_trip + r, k).start()
        return carry

    lax.fori_loop(0, n // DMA_UNROLL, trip, 0)


def _dispatch_kernel(n_slab, n_blocks, dest_ref, last_ref, nu_ref, h_ref, xs_ref, zero_scr, stage, sems, zero_sem):
    i = pl.program_id(0)
    last = pl.num_programs(0) - 1
    n = dest_ref.shape[2]
    blk_rows = TB * n_slab
    slot = i % 2

    def wait_slot(s):
        for _ in range(TOP_K):
            pltpu.make_async_copy(stage.at[s], xs_ref.at[pl.ds(0, (n // TOP_K) * n_slab), :], sems.at[s]).wait()

    @pl.when(i == 0)
    def _():
        zero_scr[...] = jnp.zeros_like(zero_scr)

        def zero_block(b):
            rows = pl.ds(pl.multiple_of(b * blk_rows, blk_rows), blk_rows)
            return pltpu.make_async_copy(zero_scr, xs_ref.at[rows, :], zero_sem)

        def over_blocks(act):
            for e in range(N_EXPERTS):
                pl.when(last_ref[e] >= 0)(lambda e=e: act(zero_block(last_ref[e])))
            lax.fori_loop(nu_ref[0], n_blocks, lambda b, c: (act(zero_block(b)), c)[1], 0)

        over_blocks(lambda cp: cp.start())
        over_blocks(lambda cp: cp.wait())

    pl.when(i >= 2)(lambda: wait_slot(slot))
    stage[slot] = h_ref[...]

    def copy(tok, k):
        src = stage.at[slot, pl.ds(pl.multiple_of(tok * n_slab, n_slab), n_slab), :]
        dst = xs_ref.at[pl.ds(pl.multiple_of(dest_ref[0, 0, tok * TOP_K + k] * n_slab, n_slab), n_slab), :]
        return pltpu.make_async_copy(src, dst, sems.at[slot])

    _issue_rows(n, copy)

    @pl.when(i == last)
    def _():
        pl.when(i >= 1)(lambda: wait_slot(1 - slot))
        wait_slot(slot)


def _dispatch(h2_slab, dest, last_blk, n_used, n_blocks, n_slab, td):
    m = h2_slab.shape[0] // n_slab
    n_steps = m // td
    dest3 = dest.reshape(n_steps, 1, td * TOP_K)
    smem = pl.BlockSpec(memory_space=pltpu.SMEM)
    return pl.pallas_call(
        functools.partial(_dispatch_kernel, n_slab, n_blocks),
        grid=(n_steps,),
        in_specs=[pl.BlockSpec((1, 1, td * TOP_K), lambda i: (i, 0, 0), memory_space=pltpu.SMEM),
                  smem, smem,
                  pl.BlockSpec((td * n_slab, LANES), lambda i: (i, 0))],
        out_specs=pl.BlockSpec(memory_space=pl.ANY),
        out_shape=jax.ShapeDtypeStruct((n_blocks * TB * n_slab, LANES), h2_slab.dtype),
        scratch_shapes=[pltpu.VMEM((TB * n_slab, LANES), h2_slab.dtype),
                        pltpu.VMEM((2, td * n_slab, LANES), h2_slab.dtype),
                        pltpu.SemaphoreType.DMA((2,)), pltpu.SemaphoreType.DMA(())],
        compiler_params=_params(("arbitrary",)),
        name="dispatch",
    )(dest3, last_blk, n_used, h2_slab)


def _expert_kernel(n_slab, be_ref, nu_ref, xs_ref, wg_ref, wu_ref, wd_ref, o_ref, wgu_scr, wd_scr):
    i = pl.program_id(0)
    de = wd_ref.shape[1]

    @pl.when(i < nu_ref[0])
    def _():
        @pl.when((i == 0) | (be_ref[i] != be_ref[jnp.maximum(i - 1, 0)]))
        def _():
            wgu_scr[:, :de] = wg_ref[0].astype(bf16)
            wgu_scr[:, de:] = wu_ref[0].astype(bf16)
            wd_scr[...] = wd_ref[0].astype(bf16)

        n_word = n_slab // 2
        halves = [_unpack_bf16_pair(xs_ref[pl.ds(s, TB, stride=n_word), :]) for s in range(n_word)]
        x = jnp.concatenate([lo for lo, _ in halves] + [hi for _, hi in halves], axis=1)
        gu = jnp.dot(x.astype(bf16), wgu_scr[...], preferred_element_type=f32)
        g = gu[:, :de]
        hmid = (g * _sigmoid(g)) * gu[:, de:]
        y = jnp.dot(hmid.astype(bf16), wd_scr[...], preferred_element_type=f32)
        for s in range(n_word):
            o_ref[pl.ds(s, TB, stride=n_word), :] = _pack_bf16_pair(y[:, s * LANES:(s + 1) * LANES],
                                                                   y[:, (n_word + s) * LANES:(n_word + s + 1) * LANES])


def _experts(xs, blk_expert, n_used, w_gate, w_up, w_down, layer, n_blocks, n_slab):
    d, de = w_gate.shape[2], w_gate.shape[3]
    by_expert = lambda i, be, nu: (layer, be[jnp.minimum(i, nu[0] - 1)], 0, 0)
    grid_spec = pltpu.PrefetchScalarGridSpec(
        num_scalar_prefetch=2,
        grid=(n_blocks,),
        in_specs=[pl.BlockSpec((TB * n_slab // 2, LANES), lambda i, be, nu: (jnp.minimum(i, nu[0] - 1), 0)),
                  pl.BlockSpec((None, 1, d, de), by_expert), pl.BlockSpec((None, 1, d, de), by_expert),
                  pl.BlockSpec((None, 1, de, d), by_expert)],
        out_specs=pl.BlockSpec((TB * n_slab // 2, LANES), lambda i, be, nu: (jnp.minimum(i, nu[0] - 1), 0)),
        scratch_shapes=[pltpu.VMEM((d, 2 * de), bf16), pltpu.VMEM((de, d), bf16)],
    )
    return pl.pallas_call(
        functools.partial(_expert_kernel, n_slab),
        grid_spec=grid_spec,
        out_shape=jax.ShapeDtypeStruct(xs.shape, xs.dtype),
        input_output_aliases={2: 0},
        compiler_params=_params(("arbitrary",)),
        name="experts",
    )(blk_expert, n_used, xs, w_gate, w_up, w_down)


def _combine_kernel(n_slab, final, norm_next, bounds, dest_ref, next_ref, x_ref, wgt_ref, g_ref, ys_ref, *rest):
    if norm_next:
        (gn_ref, wf_ref, bf_ref), rest = rest[:3], rest[3:]
        outs, (hn_ref, logf_ref, gbuf, sems) = rest[:-4], rest[-4:]
    else:
        outs, (gbuf, sems) = rest[:-2], rest[-2:]
    i = pl.program_id(0)
    n = dest_ref.shape[2]
    tm = x_ref.shape[0]
    slot = i % 2
    n_word = n_slab // 2

    def gather(idx_ref, s):
        def copy(tok, k):
            src = ys_ref.at[pl.ds(pl.multiple_of(idx_ref[0, 0, tok * TOP_K + k] * n_word, n_word), n_word), :]
            dst = gbuf.at[s, pl.ds(pl.multiple_of((k * tm + tok) * n_word, n_word), n_word), :]
            return pltpu.make_async_copy(src, dst, sems.at[s])
        _issue_rows(n, copy)

    pl.when(i == 0)(lambda: gather(dest_ref, slot))
    pl.when(i + 1 < pl.num_programs(0))(lambda: gather(next_ref, 1 - slot))
    pltpu.make_async_copy(ys_ref.at[pl.ds(0, n * n_word), :], gbuf.at[slot], sems.at[slot]).wait()

    ys = []
    for k in range(TOP_K):
        halves = [_unpack_bf16_pair(gbuf[slot, pl.ds(k * tm * n_word + s, tm, stride=n_word), :])
                  for s in range(n_word)]
        y = jnp.concatenate([lo for lo, _ in halves] + [hi for _, hi in halves], axis=1)
        ys.append(y * wgt_ref[:, k:k + 1])
    x = x_ref[...] + (ys[0] + ys[1])
    if final:
        x = _rms(x, g_ref[...])
    if norm_next:
        hn_ref[...] = _rms(x, gn_ref[...]).astype(bf16)
        logf_ref[...] = _log_forget(hn_ref[...], wf_ref[...], bf_ref[...])

    def store(k):
        outs[k][...] = x

    _when_segment(i, bounds, store)


def _combine(x, ys, dest, wgt, g, n_slab, final, seg_rows, norm_next=None):
    m, d = x.shape
    n_steps = m // TD
    dest3 = dest.reshape(n_steps, 1, TD * TOP_K)
    counts = [r // TD for r in seg_rows]
    bounds = [0]
    for cnt in counts:
        bounds.append(bounds[-1] + cnt)
    out_specs = [_seg_spec((TD, d), bounds[k], counts[k]) for k in range(len(seg_rows))]
    out_shape = [jax.ShapeDtypeStruct((r, d), f32) for r in seg_rows]
    idx_block = (1, 1, TD * TOP_K)
    extra_in, extra_specs = [], []
    if norm_next is not None:
        extra_in = list(norm_next)
        extra_specs = [pl.BlockSpec(a.shape, lambda i: (0, 0)) for a in norm_next]
        out_specs += [pl.BlockSpec((TD, d), lambda i: (i, 0)), pl.BlockSpec((TD, N_HEADS), lambda i: (i, 0))]
        out_shape += [jax.ShapeDtypeStruct((m, d), bf16), jax.ShapeDtypeStruct((m, N_HEADS), f32)]
    return pl.pallas_call(
        functools.partial(_combine_kernel, n_slab, final, norm_next is not None, tuple(bounds)),
        grid=(n_steps,),
        in_specs=[pl.BlockSpec(idx_block, lambda i: (i, 0, 0), memory_space=pltpu.SMEM),
                  pl.BlockSpec(idx_block, lambda i: (jnp.minimum(i + 1, n_steps - 1), 0, 0),
                               memory_space=pltpu.SMEM),
                  pl.BlockSpec((TD, d), lambda i: (i, 0)),
                  pl.BlockSpec((TD, LANES), lambda i: (i, 0)),
                  pl.BlockSpec((1, d), lambda i: (0, 0)),
                  pl.BlockSpec(memory_space=pl.ANY)] + extra_specs,
        out_specs=out_specs, out_shape=out_shape,
        scratch_shapes=[pltpu.VMEM((2, TD * TOP_K * n_slab // 2, LANES), jnp.uint32),
                        pltpu.SemaphoreType.DMA((2,))],
        compiler_params=_params(("arbitrary",)),
        name="combine",
    )(dest3, dest3, x, wgt, g, ys, *extra_in)


def kernel(x_prompt, x_sample, cache_a_k, cache_a_v, cache_b_k, cache_b_v, cache_b_logf, state_conv, norm_mix_g, w_in, b_in, rel_bias, conv_w, conv_b, conv_ln_g, conv_ln_b, w_proj_a, w_proj_b, w_proj_c, w_out, norm_ffn_g, w_router_group, b_router_group, w_router_expert, b_router_expert, w_e_gate, w_e_up, w_e_down, norm_final_g):
    nb_p, t_p, d = x_prompt.shape
    nb_s, t_s, _ = x_sample.shape
    depth = w_in.shape[0]
    past = cache_b_k.shape[2]
    a_rows = cache_a_k.shape[2]
    m_p, m_s = nb_p * t_p, nb_s * t_s
    m = m_p + m_s
    c_conv = d // 2
    n_slab = d // LANES
    tm = _row_tile(np.gcd(m_p, m_s), 512)
    tm_mix = _row_tile(np.gcd(m_p, m_s), 256)
    assert m_p % TD == 0 and m_s % TD == 0 and t_s % 16 == 0 and m_p % t_s == 0

    a_keep = min(WINDOW_A, t_p)
    x_segs = [x_prompt.reshape(m_p, d), x_sample.reshape(m_s, d)]
    kv_states = [jnp.zeros((depth, rows * N_HEADS, HEAD_DIM), f32)
                 for rows in (nb_p * a_keep, nb_p * a_keep, m_p, m_p, m_s, m_s, m_s, m_s)]
    p_states, s_states = [], []
    normed = None
    for l in range(depth):
        (qa, ka16, va16, qb, kb16, vb16, u, gates, logf), kv_states = _inproj(
            x_segs, norm_mix_g[l][None, :], w_in[l], b_in[l], kv_states, l, m_p, t_p, a_keep, tm, normed)

        ya_p = _band_attention_prompt(qa, ka16, va16, rel_bias[l], nb_p, t_p, 4 * CHUNK)
        kk = jnp.concatenate([cache_a_k[l].reshape(nb_s, a_rows, W_HEADS).astype(bf16),
                              ka16[m_p:].reshape(nb_s, t_s, W_HEADS)], axis=1).reshape(-1, W_HEADS)
        vv = jnp.concatenate([cache_a_v[l].reshape(nb_s, a_rows, W_HEADS).astype(bf16),
                              va16[m_p:].reshape(nb_s, t_s, W_HEADS)], axis=1).reshape(-1, W_HEADS)
        ya_s = _band_attention_sample(qa, kk, vv, rel_bias[l], nb_s, t_s, a_rows, m_p // t_s)

        logf_p = logf[:m_p].reshape(nb_p, t_p, N_HEADS)
        logf_s = logf[m_p:].reshape(nb_s, t_s, N_HEADS)
        cum_p = _cumsum_time(logf_p.transpose(0, 2, 1))
        f_p = cum_p.transpose(0, 2, 1).reshape(m_p, N_HEADS)
        yb_p = _fox_attention(*_fox_expand("qkv", (qb, kb16, vb16), f_p, m_p, tm),
                              nb_p, t_p, t_p, _row_tile(t_p, 2048), _row_tile(t_p, 512))
        cum_s = _cumsum_time(jnp.concatenate([cache_b_logf[l].astype(f32), logf_s], axis=1).transpose(0, 2, 1))
        t_ks = past + t_s
        kk = jnp.concatenate([cache_b_k[l].reshape(nb_s, past, W_HEADS).astype(bf16),
                              kb16[m_p:].reshape(nb_s, t_s, W_HEADS)], axis=1).reshape(-1, W_HEADS)
        vv = jnp.concatenate([cache_b_v[l].reshape(nb_s, past, W_HEADS).astype(bf16),
                              vb16[m_p:].reshape(nb_s, t_s, W_HEADS)], axis=1).reshape(-1, W_HEADS)
        f_ks = cum_s.transpose(0, 2, 1)
        yb_s = _fox_attention(
            *_fox_expand("q", (qb,), f_ks[:, past:].reshape(m_s, N_HEADS), m_s, t_s, blk0=m_p // t_s),
            *_fox_expand("kv", (kk, vv), f_ks.reshape(nb_s * t_ks, N_HEADS), nb_s * t_ks, t_ks),
            nb_s, t_s, t_ks, t_s, t_ks)

        conv_args = (conv_w[l], conv_b[l][None, :], conv_ln_g[l][None, :], conv_ln_b[l][None, :])
        c_p = _conv_module(u, jnp.zeros((nb_p, CONV_HALO, c_conv), f32), *conv_args,
                           nb_p, t_p, _row_tile(t_p, 256), 0)
        init_s = jnp.pad(state_conv[l], ((0, 0), (CONV_HALO - (CONV_W - 1), 0), (0, 0)))
        c_s = _conv_module(u, init_s, *conv_args, nb_s, t_s, t_s, m_p // t_s)

        mixed = _mix(ya_p, ya_s, yb_p, yb_s, c_p, c_s, gates, w_proj_a[l].astype(bf16),
                     w_proj_b[l].astype(bf16), w_proj_c[l].astype(bf16), tm_mix)
        wr = jnp.pad(jnp.concatenate([w_router_group[l], w_router_expert[l]], axis=1),
                     ((0, 0), (0, LANES - N_GROUPS - N_EXPERTS)))
        wr_hi = wr.astype(bf16)
        wr_parts = jnp.concatenate([wr_hi, (wr - wr_hi.astype(f32)).astype(bf16)], axis=1)
        br = jnp.pad(jnp.concatenate([b_router_group[l], b_router_expert[l]]),
                     (0, LANES - N_GROUPS - N_EXPERTS))[None, :]
        x_mid, h2_slab, eid, wgt = _outproj(mixed, x_segs, w_out[l].astype(bf16), norm_ffn_g[l][None, :],
                                            wr_parts, br, tm_mix)

        dest, blk_expert, n_used, last_blk, n_blocks = _plan(eid[:, :TOP_K])
        xs = _dispatch(h2_slab, dest, last_blk, n_used, n_blocks, n_slab // 2, tm)
        ys = _experts(xs, blk_expert, n_used, w_e_gate, w_e_up, w_e_down, l, n_blocks, n_slab)
        final = l == depth - 1
        if final:
            x_segs = _combine(x_mid, ys, dest, wgt, norm_final_g[None, :], n_slab, True, (m_p, m_s))
        else:
            *x_segs, h_next, logf_next = _combine(
                x_mid, ys, dest, wgt, norm_final_g[None, :], n_slab, False, (m,),
                norm_next=(norm_mix_g[l + 1][None, :],) + _forget_weights(w_in[l + 1], b_in[l + 1]))
            normed = (h_next, logf_next)

        n_cs = c_conv // LANES
        u_p = jnp.stack([u[((b + 1) * t_p - (CONV_W - 1)) * n_cs:(b + 1) * t_p * n_cs] for b in range(nb_p)])
        u_p = u_p.reshape(nb_p, CONV_W - 1, c_conv)
        u_s = u[m_p * n_cs:].reshape(nb_s, t_s, c_conv)
        p_states.append((logf_p, u_p))
        s_states.append((logf_s, jnp.concatenate([state_conv[l], u_s], axis=1)[:, -(CONV_W - 1):]))

    y_prompt = x_segs[0].reshape(nb_p, t_p, d)
    y_sample = x_segs[1].reshape(nb_s, t_s, d)
    stack = lambda states, k: jnp.stack([st[k] for st in states], axis=0)
    heads = lambda a, nb, t: a.reshape(depth, nb, t, N_HEADS, HEAD_DIM)
    ka_p, va_p, kb_p, vb_p, ka_s, va_s, kb_s, vb_s = kv_states
    return (y_prompt, y_sample,
            heads(ka_p, nb_p, a_keep), heads(va_p, nb_p, a_keep), heads(kb_p, nb_p, t_p), heads(vb_p, nb_p, t_p),
            stack(p_states, 0), stack(p_states, 1),
            heads(ka_s, nb_s, t_s), heads(va_s, nb_s, t_s), heads(kb_s, nb_s, t_s), heads(vb_s, nb_s, t_s),
            stack(s_states, 0), stack(s_states, 1))
```

```python
import functools

import jax
import jax.numpy as jnp
import numpy as np
from jax import lax
from jax.experimental import pallas as pl
from jax.experimental.pallas import tpu as pltpu

f32 = jnp.float32
bf16 = jnp.bfloat16

HEAD_DIM = 64
N_HEADS = 8
W_HEADS = N_HEADS * HEAD_DIM
CHUNK = 64
WINDOW_A = 8 * CHUNK
REL_CLIP = 128
CONV_W = 31
CONV_HALO = 32
N_GROUPS = 4
EXPERTS_PER_GROUP = 8
N_EXPERTS = N_GROUPS * EXPERTS_PER_GROUP
TOP_K = 2
SCALE = HEAD_DIM ** -0.5
EPS = 1e-6
NEG_INF = -1e30
LANES = 128
MIB = 1024 * 1024


def _params(sem, vmem_mib=48):
    return pltpu.CompilerParams(dimension_semantics=sem, vmem_limit_bytes=vmem_mib * MIB)


def _row_tile(m, cap):
    t = cap
    while m % t:
        t //= 2
    return t


def _sigmoid(z):
    return 0.5 * jnp.tanh(0.5 * z) + 0.5


def _rms(x, g):
    return x * lax.rsqrt(jnp.mean(x * x, axis=-1, keepdims=True) + EPS) * g


def _pack_bf16_pair(lo, hi):
    def rounded(x):
        bits = lax.bitcast_convert_type(x, jnp.uint32)
        return bits + jnp.uint32(0x7FFF) + ((bits >> 16) & jnp.uint32(1))
    return (rounded(hi) & jnp.uint32(0xFFFF0000)) | (rounded(lo) >> 16)


def _unpack_bf16_pair(word):
    lo = lax.bitcast_convert_type(word << 16, f32)
    hi = lax.bitcast_convert_type(word & jnp.uint32(0xFFFF0000), f32)
    return lo, hi


def _when_segment(i, bounds, fn):
    for k in range(len(bounds) - 1):
        pl.when((i >= bounds[k]) & (i < bounds[k + 1]))(functools.partial(fn, k))


def _seg_spec(block, start, count, width_axes=1):
    zeros = (0,) * width_axes
    return pl.BlockSpec(block, lambda i, *_: (jnp.clip(i - start, 0, count - 1),) + zeros)


TN = 1024
SEG_PER_TILE = TN // W_HEADS


def _log_forget(h, w_f, b_f):
    zf = jnp.dot(h, w_f, preferred_element_type=f32) + b_f
    return (jnp.minimum(zf, 0.0) - jnp.log1p(jnp.exp(-jnp.abs(zf))))[:, :N_HEADS]


def _forget_weights(w_in, b_in):
    f0 = 6 * W_HEADS
    w_f = jnp.pad(w_in[:, f0:f0 + N_HEADS], ((0, 0), (0, LANES - N_HEADS))).astype(bf16)
    b_f = jnp.pad(b_in[f0:f0 + N_HEADS], (0, LANES - N_HEADS))[None, :].astype(f32)
    return w_f, b_f


def _norm_kernel(n_seg, bounds, *refs):
    xs = refs[:n_seg]
    g_ref, wf_ref, bf_ref, h_ref, logf = refs[n_seg:]

    def norm(k):
        h_ref[...] = _rms(xs[k][...], g_ref[...]).astype(bf16)

    _when_segment(pl.program_id(0), bounds, norm)
    logf[...] = _log_forget(h_ref[...], wf_ref[...], bf_ref[...])


def _norm(x_segs, g, w_f, b_f, tm):
    d = x_segs[0].shape[1]
    m = sum(a.shape[0] for a in x_segs)
    counts = [a.shape[0] // tm for a in x_segs]
    bounds = [0]
    for cnt in counts:
        bounds.append(bounds[-1] + cnt)
    const = lambda shape: pl.BlockSpec(shape, lambda i: (0, 0))
    return pl.pallas_call(
        functools.partial(_norm_kernel, len(x_segs), tuple(bounds)),
        grid=(m // tm,),
        in_specs=[_seg_spec((tm, d), bounds[k], counts[k]) for k in range(len(x_segs))] +
                 [const((1, d)), const((d, LANES)), const((1, LANES))],
        out_specs=[pl.BlockSpec((tm, d), lambda i: (i, 0)), pl.BlockSpec((tm, N_HEADS), lambda i: (i, 0))],
        out_shape=[jax.ShapeDtypeStruct((m, d), bf16), jax.ShapeDtypeStruct((m, N_HEADS), f32)],
        compiler_params=_params(("arbitrary",)),
        name="norm",
    )(*x_segs, g, w_f, b_f)


def _qkv_kernel(with_state, n_prompt, h_ref, w_ref, b_ref, *refs):
    outs = refs[2 * sum(with_state):]
    is_prompt = pl.program_id(0) < n_prompt
    for prompt_rows in (True, False):
        @pl.when(is_prompt if prompt_rows else ~is_prompt)
        def _(prompt_rows=prompt_rows):
            z = jnp.dot(h_ref[...], w_ref[...], preferred_element_type=f32) + b_ref[...]
            k = 0
            for n, has_state in enumerate(with_state):
                zn = z[:, n * W_HEADS:(n + 1) * W_HEADS]
                outs[k][...] = zn.astype(bf16)
                k += 1
                if has_state:
                    state = outs[k if prompt_rows else k + 1]
                    for hh in range(N_HEADS):
                        state[pl.ds(hh, zn.shape[0], stride=N_HEADS), :] = zn[:, hh * HEAD_DIM:(hh + 1) * HEAD_DIM]
                    k += 2


def _qkv_tile(h, w_qkv, b_qkv, tile, seg_states, layer, m_p, t_p, a_keep, tm):
    m, d = h.shape
    n_p, n_s = m_p // tm, (m - m_p) // tm
    per_seq, keep = t_p // tm, a_keep // tm

    def tail_rows(i):
        ip = jnp.minimum(i, n_p - 1)
        return (layer, (ip // per_seq) * keep + jnp.maximum(ip % per_seq - (per_seq - keep), 0), 0)

    state_block = (None, tm * N_HEADS, HEAD_DIM)
    tail_spec = pl.BlockSpec(state_block, tail_rows)
    prompt_spec = pl.BlockSpec(state_block, lambda i: (layer, jnp.minimum(i, n_p - 1), 0))
    sample_spec = pl.BlockSpec(state_block, lambda i: (layer, jnp.clip(i - n_p, 0, n_s - 1), 0))
    states, out_specs, out_shape, state_out_pos = [], [], [], []
    for seg in seg_states:
        out_specs.append(pl.BlockSpec((tm, W_HEADS), lambda i: (i, 0)))
        out_shape.append(jax.ShapeDtypeStruct((m, W_HEADS), bf16))
        if seg is not None:
            buf_p, buf_s, keep_tail = seg
            for buf, spec in ((buf_p, tail_spec if keep_tail else prompt_spec), (buf_s, sample_spec)):
                state_out_pos.append(len(out_specs))
                states.append(buf)
                out_specs.append(spec)
                out_shape.append(jax.ShapeDtypeStruct(buf.shape, buf.dtype))
    outs = pl.pallas_call(
        functools.partial(_qkv_kernel, tuple(seg is not None for seg in seg_states), n_p),
        grid=(m // tm,),
        in_specs=[pl.BlockSpec((tm, d), lambda i: (i, 0)),
                  pl.BlockSpec((d, TN), lambda i: (0, tile)),
                  pl.BlockSpec((1, TN), lambda i: (0, tile))] + [pl.BlockSpec(memory_space=pl.ANY)] * len(states),
        out_specs=out_specs, out_shape=out_shape,
        input_output_aliases={3 + k: pos for k, pos in enumerate(state_out_pos)},
        compiler_params=_params(("arbitrary",)),
        name="qkv",
    )(h, w_qkv, b_qkv, *states)
    copies = [o for k, o in enumerate(outs) if k not in state_out_pos]
    return copies, [outs[pos] for pos in state_out_pos]


def _glu_kernel(n_slab, h_ref, w_ref, b_ref, u_ref):
    tm = h_ref.shape[0]
    c = n_slab * LANES
    z = jnp.dot(h_ref[...], w_ref[...], preferred_element_type=f32) + b_ref[...]
    glu = z[:, :c] * _sigmoid(z[:, c:])
    for s in range(n_slab):
        u_ref[pl.ds(s, tm, stride=n_slab), :] = glu[:, s * LANES:(s + 1) * LANES]


def _glu(h, w, b, tm):
    m, d = h.shape
    c = w.shape[1] // 2
    n_slab = c // LANES
    return pl.pallas_call(
        functools.partial(_glu_kernel, n_slab),
        grid=(m // tm,),
        in_specs=[pl.BlockSpec((tm, d), lambda i: (i, 0)), pl.BlockSpec((d, 2 * c), lambda i: (0, 0)),
                  pl.BlockSpec((1, 2 * c), lambda i: (0, 0))],
        out_specs=pl.BlockSpec((tm * n_slab, LANES), lambda i: (i, 0)),
        out_shape=jax.ShapeDtypeStruct((m * n_slab, LANES), f32),
        compiler_params=_params(("arbitrary",)),
        name="glu",
    )(h, w, b)


def _gates_kernel(h_ref, w_ref, b_ref, o_ref):
    z = jnp.dot(h_ref[...], w_ref[...], preferred_element_type=f32) + b_ref[...]
    o_ref[...] = _sigmoid(z).astype(o_ref.dtype)


def _gates(h, w, b, tm):
    m, d = h.shape
    n = w.shape[1]
    tn = 2 * TN if n % (2 * TN) == 0 else TN
    return pl.pallas_call(
        _gates_kernel,
        grid=(n // tn, m // tm),
        in_specs=[pl.BlockSpec((tm, d), lambda j, i: (i, 0)), pl.BlockSpec((d, tn), lambda j, i: (0, j)),
                  pl.BlockSpec((1, tn), lambda j, i: (0, j))],
        out_specs=pl.BlockSpec((tm, tn), lambda j, i: (i, j)),
        out_shape=jax.ShapeDtypeStruct((m, n), bf16),
        compiler_params=_params(("arbitrary", "arbitrary")),
        name="gates",
    )(h, w, b)


def _inproj(x_segs, g, w_in, b_in, states, layer, m_p, t_p, a_keep, tm, normed=None):
    d = x_segs[0].shape[1]
    c_conv = d // 2
    n_qkv = 6 * W_HEADS
    f0 = n_qkv
    c0 = f0 + N_HEADS
    g0 = c0 + 2 * c_conv
    cast = lambda a: a.astype(bf16)
    row = lambda a: a[None, :].astype(f32)
    h, logf = normed if normed is not None else _norm(x_segs, g, *_forget_weights(w_in, b_in), tm)

    ka_p, va_p, kb_p, vb_p, ka_s, va_s, kb_s, vb_s = states
    w_qkv, b_qkv = cast(w_in[:, :n_qkv]), row(b_in[:n_qkv])
    tile = functools.partial(_qkv_tile, h, w_qkv, b_qkv, layer=layer, m_p=m_p, t_p=t_p, a_keep=a_keep, tm=tm)
    assert SEG_PER_TILE == 2
    (qa, ka16), (ka_p, ka_s) = tile(0, [None, (ka_p, ka_s, True)])
    (va16, qb), (va_p, va_s) = tile(1, [(va_p, va_s, True), None])
    (kb16, vb16), (kb_p, kb_s, vb_p, vb_s) = tile(2, [(kb_p, kb_s, False), (vb_p, vb_s, False)])
    u = _glu(h, cast(w_in[:, c0:g0]), row(b_in[c0:g0]), tm)
    gates = _gates(h, cast(w_in[:, g0:]), row(b_in[g0:]), tm)
    return (qa, ka16, va16, qb, kb16, vb16, u, gates, logf), [ka_p, va_p, kb_p, vb_p, ka_s, va_s, kb_s, vb_s]


TC = 512


def _cumsum_kernel(x_ref, o_ref, carry):
    @pl.when(pl.program_id(1) == 0)
    def _():
        carry[...] = jnp.zeros_like(carry)

    blk = x_ref[0]
    r = lax.broadcasted_iota(jnp.int32, (TC, TC), 0)
    c = lax.broadcasted_iota(jnp.int32, (TC, TC), 1)
    tri = jnp.where(r <= c, 1.0, 0.0).astype(bf16)
    cs = carry[:, 0:1]
    rest = blk
    for _ in range(F_PARTS):
        part = rest.astype(bf16)
        rest = rest - part.astype(f32)
        cs = cs + jnp.dot(part, tri, preferred_element_type=f32)
    o_ref[0] = cs
    carry[...] = jnp.broadcast_to(cs[:, TC - 1:TC], carry.shape)


def _cumsum_time(x):
    nb, h, t = x.shape
    tp = -(-t // TC) * TC
    xp = jnp.pad(x, ((0, 0), (0, 0), (0, tp - t)))
    out = pl.pallas_call(
        _cumsum_kernel,
        grid=(nb, tp // TC),
        in_specs=[pl.BlockSpec((1, h, TC), lambda b, k: (b, 0, k))],
        out_specs=pl.BlockSpec((1, h, TC), lambda b, k: (b, 0, k)),
        out_shape=jax.ShapeDtypeStruct((nb, h, tp), f32),
        scratch_shapes=[pltpu.VMEM((h, LANES), f32)],
        compiler_params=_params(("arbitrary", "arbitrary")),
        name="cumsum",
    )(xp)
    return out[:, :, :t]


F_PARTS = 3


def _spare_base(h):
    return HEAD_DIM * (1 - h % 2)


def _fox_expand_kernel(kinds, *refs):
    n_kind = len(kinds)
    n_place = sum(kind != "v" for kind in kinds)
    x_refs, f_ref = refs[:n_kind], refs[n_kind]
    place_refs = list(refs[n_kind + 1:n_kind + 1 + n_place])
    o_refs = refs[n_kind + 1 + n_place:]
    tm = x_refs[0].shape[0]
    lane = lax.broadcasted_iota(jnp.int32, (tm, LANES), 1)
    if n_place:
        rest = f_ref[...]
        stack = jnp.where(lane < (F_PARTS + 1) * N_HEADS, 1.0, 0.0)
        for n in range(F_PARTS):
            part = rest.astype(bf16).astype(f32)
            rest = rest - part
            stack = jnp.where((lane >= n * N_HEADS) & (lane < (n + 1) * N_HEADS), part, stack)
        stack = stack.astype(bf16)
    for kind, x_ref, o_ref in zip(kinds, x_refs, o_refs):
        if kind != "v":
            spare_all = jnp.dot(stack, place_refs.pop(0)[...], preferred_element_type=f32)
        for h in range(N_HEADS):
            pair = x_ref[:, (h // 2) * LANES:(h // 2 + 1) * LANES].astype(f32)
            if kind == "q":
                pair = pair * SCALE
            if kind == "v":
                spare = jnp.where(lane == _spare_base(h), 1.0, 0.0)
            else:
                spare = spare_all[:, h * LANES:(h + 1) * LANES]
            own = (lane < HEAD_DIM) if h % 2 == 0 else (lane >= HEAD_DIM)
            o_ref[h] = jnp.where(own, pair, spare).astype(o_ref.dtype)


def _fox_placement(kind):
    place = np.zeros((LANES, N_HEADS * LANES), np.float32)
    for h in range(N_HEADS):
        base = h * LANES + _spare_base(h)
        for n in range(F_PARTS):
            if kind == "q":
                place[n * N_HEADS + h, base + n] = 1.0
                place[F_PARTS * N_HEADS + h, base + F_PARTS + n] = 1.0
            else:
                place[F_PARTS * N_HEADS + h, base + n] = 1.0
                place[n * N_HEADS + h, base + F_PARTS + n] = -1.0
    return jnp.asarray(place, bf16)


def _fox_expand(kinds, xs, f, rows, tm, blk0=0):
    places = [_fox_placement(kind) for kind in kinds if kind != "v"]
    slab = jax.ShapeDtypeStruct((N_HEADS, rows, LANES), bf16)
    return pl.pallas_call(
        functools.partial(_fox_expand_kernel, tuple(kinds)),
        grid=(rows // tm,),
        in_specs=[pl.BlockSpec((tm, W_HEADS), lambda i: (blk0 + i, 0))] * len(kinds) +
                 [pl.BlockSpec((tm, LANES), lambda i: (i, 0))] +
                 [pl.BlockSpec((LANES, N_HEADS * LANES), lambda i: (0, 0))] * len(places),
        out_specs=[pl.BlockSpec((N_HEADS, tm, LANES), lambda i: (0, i, 0))] * len(kinds),
        out_shape=[slab] * len(kinds),
        compiler_params=_params(("arbitrary",)),
        name="fox_expand_" + "".join(kinds),
    )(*xs, jnp.tile(f, (1, LANES // N_HEADS)), *places)


FOX_HEADS_PER_TRIP = 2


def _fox_kernel(tq, tk, off, q_ref, k_ref, v_ref, o_ref, m_scr, acc_scr):
    i = pl.program_id(1)
    j = pl.program_id(2)

    @pl.when(j == 0)
    def _():
        m_scr[...] = jnp.full_like(m_scr, NEG_INF)
        acc_scr[...] = jnp.zeros_like(acc_scr)

    q_first = i * tq + off
    q_last = q_first + tq - 1
    k_first = j * tk
    k_last = k_first + tk - 1

    def body(row0, masked):
        rows = slice(row0, tq)
        if masked:
            kpos = k_first + lax.broadcasted_iota(jnp.int32, (tq - row0, tk), 1)
            qpos = q_first + row0 + lax.broadcasted_iota(jnp.int32, (tq - row0, tk), 0)
            vis = kpos <= qpos

        def head(h):
            s = lax.dot_general(q_ref[h, rows], k_ref[h], (((1,), (1,)), ((), ())), preferred_element_type=f32)
            if masked:
                s = jnp.where(vis, s, NEG_INF)
            m_old = m_scr[h, rows]
            m_new = jnp.maximum(m_old, jnp.max(s, axis=-1, keepdims=True))
            pr = jnp.exp(s - m_new[:, 0:1])
            pv = jnp.dot(pr.astype(bf16), v_ref[h], preferred_element_type=f32)
            acc_scr[h, rows] = jnp.exp(m_old - m_new) * acc_scr[h, rows] + pv
            m_scr[h, rows] = m_new

        def trip(g, carry):
            for n in range(FOX_HEADS_PER_TRIP):
                head(g * FOX_HEADS_PER_TRIP + n)
            return carry

        lax.fori_loop(0, N_HEADS // FOX_HEADS_PER_TRIP, trip, 0)

    pl.when(k_last <= q_first)(functools.partial(body, 0, False))
    if tq % tk == 0 and off % tk == 0:
        for c in range(tq // tk):
            pl.when(k_first == q_first + c * tk)(functools.partial(body, c * tk, True))
    else:
        pl.when((k_first <= q_last) & (k_last > q_first))(functools.partial(body, 0, True))

    @pl.when(j == pl.num_programs(2) - 1)
    def _():
        lane = lax.broadcasted_iota(jnp.int32, (tq, LANES), 1)
        for p in range(N_HEADS // 2):
            even = acc_scr[2 * p]
            odd = acc_scr[2 * p + 1]
            even = even / even[:, _spare_base(0):_spare_base(0) + 1]
            odd = odd / odd[:, _spare_base(1):_spare_base(1) + 1]
            o_ref[:, p * LANES:(p + 1) * LANES] = jnp.where(lane < HEAD_DIM, even, odd).astype(o_ref.dtype)


def _fox_attention(q, k, v, nb, t_q, t_k, tq, tk):
    nq, nk = t_q // tq, t_k // tk
    off = t_k - t_q

    def last_k(i):
        return jnp.minimum((i * tq + tq - 1 + off) // tk, nk - 1)

    kv_spec = pl.BlockSpec((N_HEADS, tk, LANES), lambda b, i, j: (0, b * nk + jnp.minimum(j, last_k(i)), 0))
    return pl.pallas_call(
        functools.partial(_fox_kernel, tq, tk, off),
        grid=(nb, nq, nk),
        in_specs=[pl.BlockSpec((N_HEADS, tq, LANES), lambda b, i, j: (0, b * nq + i, 0)), kv_spec, kv_spec],
        out_specs=pl.BlockSpec((tq, W_HEADS), lambda b, i, j: (b * nq + i, 0)),
        out_shape=jax.ShapeDtypeStruct((nb * t_q, W_HEADS), bf16),
        scratch_shapes=[pltpu.VMEM((N_HEADS, tq, LANES), f32), pltpu.VMEM((N_HEADS, tq, LANES), f32)],
        compiler_params=_params(("arbitrary", "arbitrary", "arbitrary")),
        name="fox",
    )(q, k, v)


def _band_kernel(rows, gq, wk, has_prev, *refs):
    if has_prev:
        q_ref, kp_ref, kc_ref, vp_ref, vc_ref, bias_ref, o_ref, k_scr, v_scr = refs
        k_scr[0:WINDOW_A] = kp_ref[...]
        k_scr[WINDOW_A:WINDOW_A + rows] = kc_ref[...]
        v_scr[0:WINDOW_A] = vp_ref[...]
        v_scr[WINDOW_A:WINDOW_A + rows] = vc_ref[...]
        k_src, v_src = k_scr, v_scr
    else:
        q_ref, k_src, v_src, bias_ref, o_ref = refs
    i = pl.program_id(1)
    lane = lax.broadcasted_iota(jnp.int32, (gq, LANES), 1)
    low = lane < HEAD_DIM

    def attend(before_start):
        for g in range(rows // gq):
            r0 = g * gq
            if before_start:
                key_pos = (i - 1) * WINDOW_A + r0 + lax.broadcasted_iota(jnp.int32, (gq, wk), 1)
                vis = key_pos >= 0
            for p in range(N_HEADS // 2):
                cols = slice(p * LANES, (p + 1) * LANES)
                q2 = q_ref[r0:r0 + gq, cols] * SCALE
                kw = k_src[r0:r0 + wk, cols]
                vw = v_src[r0:r0 + wk, cols]
                outs = []
                for half in range(2):
                    h = 2 * p + half
                    qm = jnp.where(low if half == 0 else ~low, q2, jnp.zeros_like(q2))
                    s = lax.dot_general(qm, kw, (((1,), (1,)), ((), ())), preferred_element_type=f32)
                    s = s + bias_ref[h]
                    if before_start:
                        s = jnp.where(vis, s, NEG_INF)
                    m = jnp.max(s, axis=-1, keepdims=True)
                    pr = jnp.exp(s - m)
                    l = jnp.sum(pr, axis=-1, keepdims=True)
                    pv = jnp.dot(pr.astype(bf16), vw, preferred_element_type=f32)
                    outs.append(pv / l)
                o_ref[r0:r0 + gq, cols] = jnp.where(low, outs[0], outs[1]).astype(o_ref.dtype)

    if has_prev:
        pl.when(i == 0)(functools.partial(attend, True))
        pl.when(i > 0)(functools.partial(attend, False))
    else:
        attend(False)


def _rel_bias_table(rel_bias, gq, wk, q_shift):
    period = wk + gq
    j = np.arange(period)
    k = np.where(j < wk, j, j - period)
    line = rel_bias.astype(f32)[:, np.clip(q_shift - k, -REL_CLIP, REL_CLIP) + REL_CLIP]
    tiled = jnp.tile(line, (1, gq))[:, :gq * (period - 1)]
    return tiled.reshape(-1, gq, period - 1)[:, :, :wk]


def _band_bias(rel_bias, gq, wk, q_shift):
    r = np.arange(gq)[:, None]
    s = np.arange(wk)[None, :]
    band0 = (r // CHUNK) * CHUNK + q_shift - WINDOW_A
    ok = (s >= band0) & (s < band0 + WINDOW_A + CHUNK)
    return jnp.where(ok[None], _rel_bias_table(rel_bias, gq, wk, q_shift), NEG_INF)


def _band_attention_prompt(q, k, v, rel_bias, nb, t, gq):
    rows = WINDOW_A
    wk = WINDOW_A + gq
    n_steps = t // rows
    bias = _band_bias(rel_bias, gq, wk, WINDOW_A)
    cur = pl.BlockSpec((rows, W_HEADS), lambda b, i: (b * n_steps + i, 0))
    prev = pl.BlockSpec((rows, W_HEADS), lambda b, i: (b * n_steps + jnp.maximum(i - 1, 0), 0))
    return pl.pallas_call(
        functools.partial(_band_kernel, rows, gq, wk, True),
        grid=(nb, n_steps),
        in_specs=[cur, prev, cur, prev, cur,
                  pl.BlockSpec((N_HEADS, gq, wk), lambda b, i: (0, 0, 0))],
        out_specs=cur,
        out_shape=jax.ShapeDtypeStruct((nb * t, W_HEADS), bf16),
        scratch_shapes=[pltpu.VMEM((2 * rows, W_HEADS), bf16), pltpu.VMEM((2 * rows, W_HEADS), bf16)],
        compiler_params=_params(("arbitrary", "arbitrary")),
        name="band_prompt",
    )(q, k, k, v, v, bias)


def _band_attention_sample(q, kk, vv, rel_bias, nb, s_new, l_cache, q_blk0):
    wk = l_cache + s_new
    bias = _rel_bias_table(rel_bias, s_new, wk, l_cache)
    return pl.pallas_call(
        functools.partial(_band_kernel, s_new, s_new, wk, False),
        grid=(nb, 1),
        in_specs=[pl.BlockSpec((s_new, W_HEADS), lambda b, i: (q_blk0 + b, 0)),
                  pl.BlockSpec((wk, W_HEADS), lambda b, i: (b, 0)),
                  pl.BlockSpec((wk, W_HEADS), lambda b, i: (b, 0)),
                  pl.BlockSpec((N_HEADS, s_new, wk), lambda b, i: (0, 0, 0))],
        out_specs=pl.BlockSpec((s_new, W_HEADS), lambda b, i: (b, 0)),
        out_shape=jax.ShapeDtypeStruct((nb * s_new, W_HEADS), bf16),
        compiler_params=_params(("arbitrary", "arbitrary")),
        name="band_sample",
    )(q, kk, vv, bias)


CONV_STRIP = 32


def _conv_kernel(tt, n_slab, init_ref, u_ref, w_ref, cb_ref, g_ref, b_ref, o_ref, ubuf, acc_scr):
    halo = CONV_HALO * n_slab

    @pl.when(pl.program_id(1) == 0)
    def _():
        ubuf[0:halo] = init_ref[0]

    ubuf[halo:halo + tt * n_slab] = u_ref[...]
    rs = min(CONV_STRIP, tt)
    first = CONV_HALO - (CONV_W - 1)

    def per_step(slab):
        return jnp.broadcast_to(slab[None], (rs, n_slab, LANES)).reshape(rs * n_slab, LANES)

    for s in range(tt // rs):
        acc = per_step(cb_ref[...])
        for j in range(CONV_W):
            r0 = (s * rs + first + j) * n_slab
            acc = acc + per_step(w_ref[j * n_slab:(j + 1) * n_slab, :]) * ubuf[r0:r0 + rs * n_slab, :]
        acc_scr[...] = acc
        rows = jnp.concatenate([acc_scr[pl.ds(q, rs, stride=n_slab), :] for q in range(n_slab)], axis=1)
        mu = jnp.mean(rows, axis=-1, keepdims=True)
        cen = rows - mu
        var = jnp.mean(cen * cen, axis=-1, keepdims=True)
        y = cen * lax.rsqrt(var + EPS) * g_ref[...] + b_ref[...]
        o_ref[s * rs:(s + 1) * rs, :] = (y * _sigmoid(y)).astype(o_ref.dtype)
    if tt >= CONV_HALO:
        ubuf[0:halo] = ubuf[tt * n_slab:tt * n_slab + halo]


def _conv_module(u_slab, init, conv_w, conv_b, ln_g, ln_b, nb, t, tt, blk0):
    c = conv_w.shape[1]
    n_slab = c // LANES
    n_t = t // tt
    vec = pl.BlockSpec((1, c), lambda b, i: (0, 0))
    return pl.pallas_call(
        functools.partial(_conv_kernel, tt, n_slab),
        grid=(nb, n_t),
        in_specs=[pl.BlockSpec((1, CONV_HALO * n_slab, LANES), lambda b, i: (b, 0, 0)),
                  pl.BlockSpec((tt * n_slab, LANES), lambda b, i: (blk0 + b * n_t + i, 0)),
                  pl.BlockSpec((CONV_W * n_slab, LANES), lambda b, i: (0, 0)),
                  pl.BlockSpec((n_slab, LANES), lambda b, i: (0, 0)), vec, vec],
        out_specs=pl.BlockSpec((tt, c), lambda b, i: (b * n_t + i, 0)),
        out_shape=jax.ShapeDtypeStruct((nb * t, c), bf16),
        scratch_shapes=[pltpu.VMEM(((CONV_HALO + tt) * n_slab, LANES), f32),
                        pltpu.VMEM((min(CONV_STRIP, tt) * n_slab, LANES), f32)],
        compiler_params=_params(("arbitrary", "arbitrary")),
        name="conv",
    )(init.reshape(nb, CONV_HALO * n_slab, LANES), u_slab, conv_w.reshape(CONV_W * n_slab, LANES),
      conv_b.reshape(n_slab, LANES), ln_g, ln_b)


def _mix_kernel(bounds, ya_p, ya_s, yb_p, yb_s, c_p, c_s, gates, pa, pb, pc, o_ref):
    d = o_ref.shape[1]

    def go(k):
        ya, yb, c = ((ya_p, yb_p, c_p), (ya_s, yb_s, c_s))[k]
        a = jnp.dot(ya[...], pa[...], preferred_element_type=f32)
        mixed = gates[:, 0:d].astype(f32) * a
        b = jnp.dot(yb[...], pb[...], preferred_element_type=f32)
        mixed = mixed + gates[:, d:2 * d].astype(f32) * b
        cc = jnp.dot(c[...], pc[...], preferred_element_type=f32)
        mixed = mixed + gates[:, 2 * d:3 * d].astype(f32) * cc
        o_ref[...] = mixed.astype(o_ref.dtype)

    _when_segment(pl.program_id(0), bounds, go)


def _mix(ya_p, ya_s, yb_p, yb_s, c_p, c_s, gates, pa, pb, pc, tm):
    m, d3 = gates.shape
    d = d3 // 3
    n_p, n_s = ya_p.shape[0] // tm, ya_s.shape[0] // tm
    bounds = (0, n_p, n_p + n_s)
    c_conv = c_p.shape[1]
    const = lambda shape: pl.BlockSpec(shape, lambda i: (0, 0))
    return pl.pallas_call(
        functools.partial(_mix_kernel, bounds),
        grid=(m // tm,),
        in_specs=[_seg_spec((tm, W_HEADS), 0, n_p), _seg_spec((tm, W_HEADS), n_p, n_s),
                  _seg_spec((tm, W_HEADS), 0, n_p), _seg_spec((tm, W_HEADS), n_p, n_s),
                  _seg_spec((tm, c_conv), 0, n_p), _seg_spec((tm, c_conv), n_p, n_s),
                  pl.BlockSpec((tm, d3), lambda i: (i, 0)),
                  const((W_HEADS, d)), const((W_HEADS, d)), const((c_conv, d))],
        out_specs=pl.BlockSpec((tm, d), lambda i: (i, 0)),
        out_shape=jax.ShapeDtypeStruct((m, d), bf16),
        compiler_params=_params(("arbitrary",)),
        name="mix",
    )(ya_p, ya_s, yb_p, yb_s, c_p, c_s, gates, pa, pb, pc)


def _route(logits):
    shape = logits.shape
    lane = lax.broadcasted_iota(jnp.int32, shape, 1)
    lane_f = lane.astype(f32)
    big = float(LANES)
    gl = jnp.where(lane < N_GROUPS, logits, -jnp.inf)
    g_max = jnp.max(gl, axis=-1, keepdims=True)
    g_idx = jnp.min(jnp.where(gl == g_max, lane_f, big), axis=-1, keepdims=True)
    g_sum = jnp.sum(jnp.exp(gl - g_max), axis=-1, keepdims=True)
    g_w = 1.0 / g_sum
    lo = N_GROUPS + g_idx * EXPERTS_PER_GROUP
    el = jnp.where((lane_f >= lo) & (lane_f < lo + EXPERTS_PER_GROUP), logits, -jnp.inf)
    m1 = jnp.max(el, axis=-1, keepdims=True)
    i1 = jnp.min(jnp.where(el == m1, lane_f, big), axis=-1, keepdims=True)
    el2 = jnp.where(lane_f == i1, -jnp.inf, el)
    m2 = jnp.max(el2, axis=-1, keepdims=True)
    i2 = jnp.min(jnp.where(el2 == m2, lane_f, big), axis=-1, keepdims=True)
    e21 = jnp.exp(m2 - m1)
    den = 1.0 + e21
    w1 = g_w * (1.0 / den)
    w2 = g_w * (e21 / den)
    eid = jnp.where(lane == 0, i1 - N_GROUPS, jnp.where(lane == 1, i2 - N_GROUPS, 0.0)).astype(jnp.int32)
    wgt = jnp.where(lane == 0, w1, jnp.where(lane == 1, w2, 0.0))
    return eid, wgt


def _outproj_kernel(n_slab, n_seg, bounds, *refs):
    xs = refs[:n_seg]
    mixed, wo, g2, wr, br, xo, h2o, eid_o, wgt_o = refs[n_seg:]
    tm = mixed.shape[0]
    i = pl.program_id(0)
    x = xs[0][...]
    for k in range(1, n_seg):
        x = jnp.where(i >= bounds[k], xs[k][...], x)
    xn = x + jnp.dot(mixed[...], wo[...], preferred_element_type=f32)
    xo[...] = xn
    h2 = _rms(xn, g2[...])
    n_word = n_slab // 2
    for s in range(n_word):
        h2o[pl.ds(s, tm, stride=n_word), :] = _pack_bf16_pair(h2[:, s * LANES:(s + 1) * LANES],
                                                              h2[:, (n_word + s) * LANES:(n_word + s + 1) * LANES])
    h_hi = h2.astype(bf16)
    h_lo = (h2 - h_hi.astype(f32)).astype(bf16)
    hi = jnp.dot(h_hi, wr[...], preferred_element_type=f32)
    lo = jnp.dot(h_lo, wr[:, :LANES], preferred_element_type=f32)
    logits = hi[:, :LANES] + (hi[:, LANES:] + lo) + br[...]
    eid, wgt = _route(logits)
    eid_o[...] = eid
    wgt_o[...] = wgt


def _outproj(mixed, x_segs, wo, g2, wr, br, tm):
    m, d = mixed.shape
    n_slab = d // LANES
    counts = [a.shape[0] // tm for a in x_segs]
    bounds = [0]
    for cnt in counts:
        bounds.append(bounds[-1] + cnt)
    const = lambda shape: pl.BlockSpec(shape, lambda i: (0, 0))
    row = lambda w: pl.BlockSpec((tm, w), lambda i: (i, 0))
    sds = jax.ShapeDtypeStruct
    return pl.pallas_call(
        functools.partial(_outproj_kernel, n_slab, len(x_segs), tuple(bounds)),
        grid=(m // tm,),
        in_specs=[_seg_spec((tm, d), bounds[k], counts[k]) for k in range(len(x_segs))] +
                 [row(d), const((d, d)), const((1, d)), const((d, 2 * LANES)), const((1, LANES))],
        out_specs=[row(d), pl.BlockSpec((tm * n_slab // 2, LANES), lambda i: (i, 0)), row(LANES), row(LANES)],
        out_shape=[sds((m, d), f32), sds((m * n_slab // 2, LANES), jnp.uint32), sds((m, LANES), jnp.int32),
                   sds((m, LANES), f32)],
        compiler_params=_params(("arbitrary",)),
        name="outproj",
    )(*x_segs, mixed, wo, g2, wr, br)


TB = 256
TD = 256


def _plan(eid):
    flat_e = eid.reshape(-1)
    n_assign = flat_e.shape[0]
    onehot = (flat_e[:, None] == jnp.arange(N_EXPERTS, dtype=jnp.int32)[None, :]).astype(jnp.int32)
    csum = jnp.cumsum(onehot, axis=0)
    counts = csum[-1]
    rank = jnp.sum(onehot * csum, axis=1) - 1
    n_blk_e = (counts + TB - 1) // TB
    blk_end = jnp.cumsum(n_blk_e)
    blk_start = blk_end - n_blk_e
    dest = blk_start[flat_e] * TB + rank
    n_blocks = -(-n_assign // TB) + N_EXPERTS
    blk_ids = jnp.arange(n_blocks, dtype=jnp.int32)
    blk_expert = jnp.minimum(jnp.sum((blk_end[None, :] <= blk_ids[:, None]).astype(jnp.int32), axis=1),
                             N_EXPERTS - 1)
    last_blk = jnp.where(n_blk_e > 0, blk_end - 1, -1).astype(jnp.int32)
    return dest.astype(jnp.int32), blk_expert, blk_end[-1:].astype(jnp.int32), last_blk, n_blocks


DMA_UNROLL = 8


def _issue_rows(n, copy):
    per_trip = DMA_UNROLL // TOP_K

    def trip(t, carry):
        for r in range(per_trip):
            for k in range(TOP_K):
                copy(t * per_trip + r, k).start(priority=k)
        return carry

    lax.fori_loop(0, n // DMA_UNROLL, trip, 0)


def _dispatch_kernel(n_slab, n_blocks, dest_ref, last_ref, nu_ref, h_ref, xs_ref, zero_scr, stage, sems, zero_sem):
    i = pl.program_id(0)
    last = pl.num_programs(0) - 1
    n = dest_ref.shape[2]
    blk_rows = TB * n_slab
    slot = i % 2

    def wait_slot(s):
        for _ in range(TOP_K):
            pltpu.make_async_copy(stage.at[s], xs_ref.at[pl.ds(0, (n // TOP_K) * n_slab), :], sems.at[s]).wait()

    @pl.when(i == 0)
    def _():
        zero_scr[...] = jnp.zeros_like(zero_scr)

        def zero_block(b):
            rows = pl.ds(pl.multiple_of(b * blk_rows, blk_rows), blk_rows)
            return pltpu.make_async_copy(zero_scr, xs_ref.at[rows, :], zero_sem)

        def over_blocks(act):
            for e in range(N_EXPERTS):
                pl.when(last_ref[e] >= 0)(lambda e=e: act(zero_block(last_ref[e])))
            lax.fori_loop(nu_ref[0], n_blocks, lambda b, c: (act(zero_block(b)), c)[1], 0)

        over_blocks(lambda cp: cp.start())
        over_blocks(lambda cp: cp.wait())

    pl.when(i >= 2)(lambda: wait_slot(slot))
    stage[slot] = h_ref[...]

    def copy(tok, k):
        src = stage.at[slot, pl.ds(pl.multiple_of(tok * n_slab, n_slab), n_slab), :]
        dst = xs_ref.at[pl.ds(pl.multiple_of(dest_ref[0, 0, tok * TOP_K + k] * n_slab, n_slab), n_slab), :]
        return pltpu.make_async_copy(src, dst, sems.at[slot])

    _issue_rows(n, copy)

    @pl.when(i == last)
    def _():
        pl.when(i >= 1)(lambda: wait_slot(1 - slot))
        wait_slot(slot)


def _dispatch(h2_slab, dest, last_blk, n_used, n_blocks, n_slab, td):
    m = h2_slab.shape[0] // n_slab
    n_steps = m // td
    dest3 = dest.reshape(n_steps, 1, td * TOP_K)
    smem = pl.BlockSpec(memory_space=pltpu.SMEM)
    return pl.pallas_call(
        functools.partial(_dispatch_kernel, n_slab, n_blocks),
        grid=(n_steps,),
        in_specs=[pl.BlockSpec((1, 1, td * TOP_K), lambda i: (i, 0, 0), memory_space=pltpu.SMEM),
                  smem, smem,
                  pl.BlockSpec((td * n_slab, LANES), lambda i: (i, 0))],
        out_specs=pl.BlockSpec(memory_space=pl.ANY),
        out_shape=jax.ShapeDtypeStruct((n_blocks * TB * n_slab, LANES), h2_slab.dtype),
        scratch_shapes=[pltpu.VMEM((TB * n_slab, LANES), h2_slab.dtype),
                        pltpu.VMEM((2, td * n_slab, LANES), h2_slab.dtype),
                        pltpu.SemaphoreType.DMA((2,)), pltpu.SemaphoreType.DMA(())],
        compiler_params=_params(("arbitrary",)),
        name="dispatch",
    )(dest3, last_blk, n_used, h2_slab)


def _expert_kernel(n_slab, be_ref, nu_ref, xs_ref, wg_ref, wu_ref, wd_ref, o_ref, wgu_scr, wd_scr):
    i = pl.program_id(0)
    de = wd_ref.shape[1]

    @pl.when(i < nu_ref[0])
    def _():
        @pl.when((i == 0) | (be_ref[i] != be_ref[jnp.maximum(i - 1, 0)]))
        def _():
            wgu_scr[:, :de] = wg_ref[0].astype(bf16)
            wgu_scr[:, de:] = wu_ref[0].astype(bf16)
            wd_scr[...] = wd_ref[0].astype(bf16)

        n_word = n_slab // 2
        halves = [_unpack_bf16_pair(xs_ref[pl.ds(s, TB, stride=n_word), :]) for s in range(n_word)]
        x = jnp.concatenate([lo for lo, _ in halves] + [hi for _, hi in halves], axis=1)
        gu = jnp.dot(x.astype(bf16), wgu_scr[...], preferred_element_type=f32)
        g = gu[:, :de]
        hmid = (g * _sigmoid(g)) * gu[:, de:]
        y = jnp.dot(hmid.astype(bf16), wd_scr[...], preferred_element_type=f32)
        for s in range(n_word):
            o_ref[pl.ds(s, TB, stride=n_word), :] = _pack_bf16_pair(y[:, s * LANES:(s + 1) * LANES],
                                                                   y[:, (n_word + s) * LANES:(n_word + s + 1) * LANES])


def _experts(xs, blk_expert, n_used, w_gate, w_up, w_down, layer, n_blocks, n_slab):
    d, de = w_gate.shape[2], w_gate.shape[3]
    by_expert = lambda i, be, nu: (layer, be[jnp.minimum(i, nu[0] - 1)], 0, 0)
    grid_spec = pltpu.PrefetchScalarGridSpec(
        num_scalar_prefetch=2,
        grid=(n_blocks,),
        in_specs=[pl.BlockSpec((TB * n_slab // 2, LANES), lambda i, be, nu: (jnp.minimum(i, nu[0] - 1), 0)),
                  pl.BlockSpec((None, 1, d, de), by_expert), pl.BlockSpec((None, 1, d, de), by_expert),
                  pl.BlockSpec((None, 1, de, d), by_expert)],
        out_specs=pl.BlockSpec((TB * n_slab // 2, LANES), lambda i, be, nu: (jnp.minimum(i, nu[0] - 1), 0)),
        scratch_shapes=[pltpu.VMEM((d, 2 * de), bf16), pltpu.VMEM((de, d), bf16)],
    )
    return pl.pallas_call(
        functools.partial(_expert_kernel, n_slab),
        grid_spec=grid_spec,
        out_shape=jax.ShapeDtypeStruct(xs.shape, xs.dtype),
        input_output_aliases={2: 0},
        compiler_params=_params(("arbitrary",)),
        name="experts",
    )(blk_expert, n_used, xs, w_gate, w_up, w_down)


def _combine_kernel(n_slab, final, norm_next, bounds, dest_ref, next_ref, x_ref, wgt_ref, g_ref, ys_ref, *rest):
    if norm_next:
        (gn_ref, wf_ref, bf_ref), rest = rest[:3], rest[3:]
        outs, (hn_ref, logf_ref, gbuf, sems) = rest[:-4], rest[-4:]
    else:
        outs, (gbuf, sems) = rest[:-2], rest[-2:]
    i = pl.program_id(0)
    n = dest_ref.shape[2]
    tm = x_ref.shape[0]
    slot = i % 2
    n_word = n_slab // 2

    def gather(idx_ref, s):
        def copy(tok, k):
            src = ys_ref.at[pl.ds(pl.multiple_of(idx_ref[0, 0, tok * TOP_K + k] * n_word, n_word), n_word), :]
            dst = gbuf.at[s, pl.ds(pl.multiple_of((k * tm + tok) * n_word, n_word), n_word), :]
            return pltpu.make_async_copy(src, dst, sems.at[s])
        _issue_rows(n, copy)

    pl.when(i == 0)(lambda: gather(dest_ref, slot))
    pl.when(i + 1 < pl.num_programs(0))(lambda: gather(next_ref, 1 - slot))
    pltpu.make_async_copy(ys_ref.at[pl.ds(0, n * n_word), :], gbuf.at[slot], sems.at[slot]).wait()

    ys = []
    for k in range(TOP_K):
        halves = [_unpack_bf16_pair(gbuf[slot, pl.ds(k * tm * n_word + s, tm, stride=n_word), :])
                  for s in range(n_word)]
        y = jnp.concatenate([lo for lo, _ in halves] + [hi for _, hi in halves], axis=1)
        ys.append(y * wgt_ref[:, k:k + 1])
    x = x_ref[...] + (ys[0] + ys[1])
    if final:
        x = _rms(x, g_ref[...])
    if norm_next:
        hn_ref[...] = _rms(x, gn_ref[...]).astype(bf16)
        logf_ref[...] = _log_forget(hn_ref[...], wf_ref[...], bf_ref[...])

    def store(k):
        outs[k][...] = x

    _when_segment(i, bounds, store)


def _combine(x, ys, dest, wgt, g, n_slab, final, seg_rows, norm_next=None):
    m, d = x.shape
    n_steps = m // TD
    dest3 = dest.reshape(n_steps, 1, TD * TOP_K)
    counts = [r // TD for r in seg_rows]
    bounds = [0]
    for cnt in counts:
        bounds.append(bounds[-1] + cnt)
    out_specs = [_seg_spec((TD, d), bounds[k], counts[k]) for k in range(len(seg_rows))]
    out_shape = [jax.ShapeDtypeStruct((r, d), f32) for r in seg_rows]
    idx_block = (1, 1, TD * TOP_K)
    extra_in, extra_specs = [], []
    if norm_next is not None:
        extra_in = list(norm_next)
        extra_specs = [pl.BlockSpec(a.shape, lambda i: (0, 0)) for a in norm_next]
        out_specs += [pl.BlockSpec((TD, d), lambda i: (i, 0)), pl.BlockSpec((TD, N_HEADS), lambda i: (i, 0))]
        out_shape += [jax.ShapeDtypeStruct((m, d), bf16), jax.ShapeDtypeStruct((m, N_HEADS), f32)]
    return pl.pallas_call(
        functools.partial(_combine_kernel, n_slab, final, norm_next is not None, tuple(bounds)),
        grid=(n_steps,),
        in_specs=[pl.BlockSpec(idx_block, lambda i: (i, 0, 0), memory_space=pltpu.SMEM),
                  pl.BlockSpec(idx_block, lambda i: (jnp.minimum(i + 1, n_steps - 1), 0, 0),
                               memory_space=pltpu.SMEM),
                  pl.BlockSpec((TD, d), lambda i: (i, 0)),
                  pl.BlockSpec((TD, LANES), lambda i: (i, 0)),
                  pl.BlockSpec((1, d), lambda i: (0, 0)),
                  pl.BlockSpec(memory_space=pl.ANY)] + extra_specs,
        out_specs=out_specs, out_shape=out_shape,
        scratch_shapes=[pltpu.VMEM((2, TD * TOP_K * n_slab // 2, LANES), jnp.uint32),
                        pltpu.SemaphoreType.DMA((2,))],
        compiler_params=_params(("arbitrary",)),
        name="combine",
    )(dest3, dest3, x, wgt, g, ys, *extra_in)


def kernel(x_prompt, x_sample, cache_a_k, cache_a_v, cache_b_k, cache_b_v, cache_b_logf, state_conv, norm_mix_g, w_in, b_in, rel_bias, conv_w, conv_b, conv_ln_g, conv_ln_b, w_proj_a, w_proj_b, w_proj_c, w_out, norm_ffn_g, w_router_group, b_router_group, w_router_expert, b_router_expert, w_e_gate, w_e_up, w_e_down, norm_final_g):
    nb_p, t_p, d = x_prompt.shape
    nb_s, t_s, _ = x_sample.shape
    depth = w_in.shape[0]
    past = cache_b_k.shape[2]
    a_rows = cache_a_k.shape[2]
    m_p, m_s = nb_p * t_p, nb_s * t_s
    m = m_p + m_s
    c_conv = d // 2
    n_slab = d // LANES
    tm = _row_tile(np.gcd(m_p, m_s), 512)
    tm_mix = _row_tile(np.gcd(m_p, m_s), 256)
    assert m_p % TD == 0 and m_s % TD == 0 and t_s % 16 == 0 and m_p % t_s == 0

    a_keep = min(WINDOW_A, t_p)
    x_segs = [x_prompt.reshape(m_p, d), x_sample.reshape(m_s, d)]
    kv_states = [jnp.zeros((depth, rows * N_HEADS, HEAD_DIM), f32)
                 for rows in (nb_p * a_keep, nb_p * a_keep, m_p, m_p, m_s, m_s, m_s, m_s)]
    p_states, s_states = [], []
    normed = None
    for l in range(depth):
        (qa, ka16, va16, qb, kb16, vb16, u, gates, logf), kv_states = _inproj(
            x_segs, norm_mix_g[l][None, :], w_in[l], b_in[l], kv_states, l, m_p, t_p, a_keep, tm, normed)

        ya_p = _band_attention_prompt(qa, ka16, va16, rel_bias[l], nb_p, t_p, 4 * CHUNK)
        kk = jnp.concatenate([cache_a_k[l].reshape(nb_s, a_rows, W_HEADS).astype(bf16),
                              ka16[m_p:].reshape(nb_s, t_s, W_HEADS)], axis=1).reshape(-1, W_HEADS)
        vv = jnp.concatenate([cache_a_v[l].reshape(nb_s, a_rows, W_HEADS).astype(bf16),
                              va16[m_p:].reshape(nb_s, t_s, W_HEADS)], axis=1).reshape(-1, W_HEADS)
        ya_s = _band_attention_sample(qa, kk, vv, rel_bias[l], nb_s, t_s, a_rows, m_p // t_s)

        logf_p = logf[:m_p].reshape(nb_p, t_p, N_HEADS)
        logf_s = logf[m_p:].reshape(nb_s, t_s, N_HEADS)
        cum_p = _cumsum_time(logf_p.transpose(0, 2, 1))
        f_p = cum_p.transpose(0, 2, 1).reshape(m_p, N_HEADS)
        yb_p = _fox_attention(*_fox_expand("qkv", (qb, kb16, vb16), f_p, m_p, tm),
                              nb_p, t_p, t_p, _row_tile(t_p, 2048), _row_tile(t_p, 512))
        cum_s = _cumsum_time(jnp.concatenate([cache_b_logf[l].astype(f32), logf_s], axis=1).transpose(0, 2, 1))
        t_ks = past + t_s
        kk = jnp.concatenate([cache_b_k[l].reshape(nb_s, past, W_HEADS).astype(bf16),
                              kb16[m_p:].reshape(nb_s, t_s, W_HEADS)], axis=1).reshape(-1, W_HEADS)
        vv = jnp.concatenate([cache_b_v[l].reshape(nb_s, past, W_HEADS).astype(bf16),
                              vb16[m_p:].reshape(nb_s, t_s, W_HEADS)], axis=1).reshape(-1, W_HEADS)
        f_ks = cum_s.transpose(0, 2, 1)
        yb_s = _fox_attention(
            *_fox_expand("q", (qb,), f_ks[:, past:].reshape(m_s, N_HEADS), m_s, t_s, blk0=m_p // t_s),
            *_fox_expand("kv", (kk, vv), f_ks.reshape(nb_s * t_ks, N_HEADS), nb_s * t_ks, t_ks),
            nb_s, t_s, t_ks, t_s, t_ks)

        conv_args = (conv_w[l], conv_b[l][None, :], conv_ln_g[l][None, :], conv_ln_b[l][None, :])
        c_p = _conv_module(u, jnp.zeros((nb_p, CONV_HALO, c_conv), f32), *conv_args,
                           nb_p, t_p, _row_tile(t_p, 256), 0)
        init_s = jnp.pad(state_conv[l], ((0, 0), (CONV_HALO - (CONV_W - 1), 0), (0, 0)))
        c_s = _conv_module(u, init_s, *conv_args, nb_s, t_s, t_s, m_p // t_s)

        mixed = _mix(ya_p, ya_s, yb_p, yb_s, c_p, c_s, gates, w_proj_a[l].astype(bf16),
                     w_proj_b[l].astype(bf16), w_proj_c[l].astype(bf16), tm_mix)
        wr = jnp.pad(jnp.concatenate([w_router_group[l], w_router_expert[l]], axis=1),
                     ((0, 0), (0, LANES - N_GROUPS - N_EXPERTS)))
        wr_hi = wr.astype(bf16)
        wr_parts = jnp.concatenate([wr_hi, (wr - wr_hi.astype(f32)).astype(bf16)], axis=1)
        br = jnp.pad(jnp.concatenate([b_router_group[l], b_router_expert[l]]),
                     (0, LANES - N_GROUPS - N_EXPERTS))[None, :]
        x_mid, h2_slab, eid, wgt = _outproj(mixed, x_segs, w_out[l].astype(bf16), norm_ffn_g[l][None, :],
                                            wr_parts, br, tm_mix)

        dest, blk_expert, n_used, last_blk, n_blocks = _plan(eid[:, :TOP_K])
        xs = _dispatch(h2_slab, dest, last_blk, n_used, n_blocks, n_slab // 2, tm)
        ys = _experts(xs, blk_expert, n_used, w_e_gate, w_e_up, w_e_down, l, n_blocks, n_slab)
        final = l == depth - 1
        if final:
            x_segs = _combine(x_mid, ys, dest, wgt, norm_final_g[None, :], n_slab, True, (m_p, m_s))
        else:
            *x_segs, h_next, logf_next = _combine(
                x_mid, ys, dest, wgt, norm_final_g[None, :], n_slab, False, (m,),
                norm_next=(norm_mix_g[l + 1][None, :],) + _forget_weights(w_in[l + 1], b_in[l + 1]))
            normed = (h_next, logf_next)

        n_cs = c_conv // LANES
        u_p = jnp.stack([u[((b + 1) * t_p - (CONV_W - 1)) * n_cs:(b + 1) * t_p * n_cs] for b in range(nb_p)])
        u_p = u_p.reshape(nb_p, CONV_W - 1, c_conv)
        u_s = u[m_p * n_cs:].reshape(nb_s, t_s, c_conv)
        p_states.append((logf_p, u_p))
        s_states.append((logf_s, jnp.concatenate([state_conv[l], u_s], axis=1)[:, -(CONV_W - 1):]))

    y_prompt = x_segs[0].reshape(nb_p, t_p, d)
    y_sample = x_segs[1].reshape(nb_s, t_s, d)
    stack = lambda states, k: jnp.stack([st[k] for st in states], axis=0)
    heads = lambda a, nb, t: a.reshape(depth, nb, t, N_HEADS, HEAD_DIM)
    ka_p, va_p, kb_p, vb_p, ka_s, va_s, kb_s, vb_s = kv_states
    return (y_prompt, y_sample,
            heads(ka_p, nb_p, a_keep), heads(va_p, nb_p, a_keep), heads(kb_p, nb_p, t_p), heads(vb_p, nb_p, t_p),
            stack(p_states, 0), stack(p_states, 1),
            heads(ka_s, nb_s, t_s), heads(va_s, nb_s, t_s), heads(kb_s, nb_s, t_s), heads(vb_s, nb_s, t_s),
            stack(s_states, 0), stack(s_states, 1))
```

```python
import functools

import jax
import jax.numpy as jnp
import numpy as np
from jax import lax
from jax.experimental import pallas as pl
from jax.experimental.pallas import tpu as pltpu

f32 = jnp.float32
bf16 = jnp.bfloat16

HEAD_DIM = 64
N_HEADS = 8
W_HEADS = N_HEADS * HEAD_DIM
CHUNK = 64
WINDOW_A = 8 * CHUNK
REL_CLIP = 128
CONV_W = 31
CONV_HALO = 32
N_GROUPS = 4
EXPERTS_PER_GROUP = 8
N_EXPERTS = N_GROUPS * EXPERTS_PER_GROUP
TOP_K = 2
SCALE = HEAD_DIM ** -0.5
EPS = 1e-6
NEG_INF = -1e30
LANES = 128
MIB = 1024 * 1024


def _params(sem, vmem_mib=48):
    return pltpu.CompilerParams(dimension_semantics=sem, vmem_limit_bytes=vmem_mib * MIB)


def _row_tile(m, cap):
    t = cap
    while m % t:
        t //= 2
    return t


def _sigmoid(z):
    return 0.5 * jnp.tanh(0.5 * z) + 0.5


def _rms(x, g):
    return x * lax.rsqrt(jnp.mean(x * x, axis=-1, keepdims=True) + EPS) * g


def _pack_bf16_pair(lo, hi):
    def rounded(x):
        bits = lax.bitcast_convert_type(x, jnp.uint32)
        return bits + jnp.uint32(0x7FFF) + ((bits >> 16) & jnp.uint32(1))
    return (rounded(hi) & jnp.uint32(0xFFFF0000)) | (rounded(lo) >> 16)


def _unpack_bf16_pair(word):
    lo = lax.bitcast_convert_type(word << 16, f32)
    hi = lax.bitcast_convert_type(word & jnp.uint32(0xFFFF0000), f32)
    return lo, hi


def _when_segment(i, bounds, fn):
    for k in range(len(bounds) - 1):
        pl.when((i >= bounds[k]) & (i < bounds[k + 1]))(functools.partial(fn, k))


def _seg_spec(block, start, count, width_axes=1):
    zeros = (0,) * width_axes
    return pl.BlockSpec(block, lambda i, *_: (jnp.clip(i - start, 0, count - 1),) + zeros)


TN = 1024
SEG_PER_TILE = TN // W_HEADS


def _log_forget(h, w_f, b_f):
    zf = jnp.dot(h, w_f, preferred_element_type=f32) + b_f
    return (jnp.minimum(zf, 0.0) - jnp.log1p(jnp.exp(-jnp.abs(zf))))[:, :N_HEADS]


def _forget_weights(w_in, b_in):
    f0 = 6 * W_HEADS
    w_f = jnp.pad(w_in[:, f0:f0 + N_HEADS], ((0, 0), (0, LANES - N_HEADS))).astype(bf16)
    b_f = jnp.pad(b_in[f0:f0 + N_HEADS], (0, LANES - N_HEADS))[None, :].astype(f32)
    return w_f, b_f


def _norm_kernel(n_seg, bounds, *refs):
    xs = refs[:n_seg]
    g_ref, wf_ref, bf_ref, h_ref, logf = refs[n_seg:]

    def norm(k):
        h_ref[...] = _rms(xs[k][...], g_ref[...]).astype(bf16)

    _when_segment(pl.program_id(0), bounds, norm)
    logf[...] = _log_forget(h_ref[...], wf_ref[...], bf_ref[...])


def _norm(x_segs, g, w_f, b_f, tm):
    d = x_segs[0].shape[1]
    m = sum(a.shape[0] for a in x_segs)
    counts = [a.shape[0] // tm for a in x_segs]
    bounds = [0]
    for cnt in counts:
        bounds.append(bounds[-1] + cnt)
    const = lambda shape: pl.BlockSpec(shape, lambda i: (0, 0))
    return pl.pallas_call(
        functools.partial(_norm_kernel, len(x_segs), tuple(bounds)),
        grid=(m // tm,),
        in_specs=[_seg_spec((tm, d), bounds[k], counts[k]) for k in range(len(x_segs))] +
                 [const((1, d)), const((d, LANES)), const((1, LANES))],
        out_specs=[pl.BlockSpec((tm, d), lambda i: (i, 0)), pl.BlockSpec((tm, N_HEADS), lambda i: (i, 0))],
        out_shape=[jax.ShapeDtypeStruct((m, d), bf16), jax.ShapeDtypeStruct((m, N_HEADS), f32)],
        compiler_params=_params(("arbitrary",)),
        name="norm",
    )(*x_segs, g, w_f, b_f)


def _qkv_kernel(with_state, n_prompt, h_ref, w_ref, b_ref, *refs):
    outs = refs[2 * sum(with_state):]
    is_prompt = pl.program_id(0) < n_prompt
    for prompt_rows in (True, False):
        @pl.when(is_prompt if prompt_rows else ~is_prompt)
        def _(prompt_rows=prompt_rows):
            z = jnp.dot(h_ref[...], w_ref[...], preferred_element_type=f32) + b_ref[...]
            k = 0
            for n, has_state in enumerate(with_state):
                zn = z[:, n * W_HEADS:(n + 1) * W_HEADS]
                outs[k][...] = zn.astype(bf16)
                k += 1
                if has_state:
                    state = outs[k if prompt_rows else k + 1]
                    for hh in range(N_HEADS):
                        state[pl.ds(hh, zn.shape[0], stride=N_HEADS), :] = zn[:, hh * HEAD_DIM:(hh + 1) * HEAD_DIM]
                    k += 2


def _qkv_tile(h, w_qkv, b_qkv, tile, seg_states, layer, m_p, t_p, a_keep, tm):
    m, d = h.shape
    n_p, n_s = m_p // tm, (m - m_p) // tm
    per_seq, keep = t_p // tm, a_keep // tm

    def tail_rows(i):
        ip = jnp.minimum(i, n_p - 1)
        return (layer, (ip // per_seq) * keep + jnp.maximum(ip % per_seq - (per_seq - keep), 0), 0)

    state_block = (None, tm * N_HEADS, HEAD_DIM)
    tail_spec = pl.BlockSpec(state_block, tail_rows)
    prompt_spec = pl.BlockSpec(state_block, lambda i: (layer, jnp.minimum(i, n_p - 1), 0))
    sample_spec = pl.BlockSpec(state_block, lambda i: (layer, jnp.clip(i - n_p, 0, n_s - 1), 0))
    states, out_specs, out_shape, state_out_pos = [], [], [], []
    for seg in seg_states:
        out_specs.append(pl.BlockSpec((tm, W_HEADS), lambda i: (i, 0)))
        out_shape.append(jax.ShapeDtypeStruct((m, W_HEADS), bf16))
        if seg is not None:
            buf_p, buf_s, keep_tail = seg
            for buf, spec in ((buf_p, tail_spec if keep_tail else prompt_spec), (buf_s, sample_spec)):
                state_out_pos.append(len(out_specs))
                states.append(buf)
                out_specs.append(spec)
                out_shape.append(jax.ShapeDtypeStruct(buf.shape, buf.dtype))
    outs = pl.pallas_call(
        functools.partial(_qkv_kernel, tuple(seg is not None for seg in seg_states), n_p),
        grid=(m // tm,),
        in_specs=[pl.BlockSpec((tm, d), lambda i: (i, 0)),
                  pl.BlockSpec((d, TN), lambda i: (0, tile)),
                  pl.BlockSpec((1, TN), lambda i: (0, tile))] + [pl.BlockSpec(memory_space=pl.ANY)] * len(states),
        out_specs=out_specs, out_shape=out_shape,
        input_output_aliases={3 + k: pos for k, pos in enumerate(state_out_pos)},
        compiler_params=_params(("arbitrary",)),
        name="qkv",
    )(h, w_qkv, b_qkv, *states)
    copies = [o for k, o in enumerate(outs) if k not in state_out_pos]
    return copies, [outs[pos] for pos in state_out_pos]


def _glu_kernel(n_slab, h_ref, w_ref, b_ref, u_ref):
    tm = h_ref.shape[0]
    c = n_slab * LANES
    z = jnp.dot(h_ref[...], w_ref[...], preferred_element_type=f32) + b_ref[...]
    glu = z[:, :c] * _sigmoid(z[:, c:])
    for s in range(n_slab):
        u_ref[pl.ds(s, tm, stride=n_slab), :] = glu[:, s * LANES:(s + 1) * LANES]


def _glu(h, w, b, tm):
    m, d = h.shape
    c = w.shape[1] // 2
    n_slab = c // LANES
    return pl.pallas_call(
        functools.partial(_glu_kernel, n_slab),
        grid=(m // tm,),
        in_specs=[pl.BlockSpec((tm, d), lambda i: (i, 0)), pl.BlockSpec((d, 2 * c), lambda i: (0, 0)),
                  pl.BlockSpec((1, 2 * c), lambda i: (0, 0))],
        out_specs=pl.BlockSpec((tm * n_slab, LANES), lambda i: (i, 0)),
        out_shape=jax.ShapeDtypeStruct((m * n_slab, LANES), f32),
        compiler_params=_params(("arbitrary",)),
        name="glu",
    )(h, w, b)


def _gates_kernel(h_ref, w_ref, b_ref, o_ref):
    z = jnp.dot(h_ref[...], w_ref[...], preferred_element_type=f32) + b_ref[...]
    o_ref[...] = _sigmoid(z).astype(o_ref.dtype)


def _gates(h, w, b, tm):
    m, d = h.shape
    n = w.shape[1]
    tn = 2 * TN if n % (2 * TN) == 0 else TN
    return pl.pallas_call(
        _gates_kernel,
        grid=(n // tn, m // tm),
        in_specs=[pl.BlockSpec((tm, d), lambda j, i: (i, 0)), pl.BlockSpec((d, tn), lambda j, i: (0, j)),
                  pl.BlockSpec((1, tn), lambda j, i: (0, j))],
        out_specs=pl.BlockSpec((tm, tn), lambda j, i: (i, j)),
        out_shape=jax.ShapeDtypeStruct((m, n), bf16),
        compiler_params=_params(("arbitrary", "arbitrary")),
        name="gates",
    )(h, w, b)


def _inproj(x_segs, g, w_in, b_in, states, layer, m_p, t_p, a_keep, tm, normed=None):
    d = x_segs[0].shape[1]
    c_conv = d // 2
    n_qkv = 6 * W_HEADS
    f0 = n_qkv
    c0 = f0 + N_HEADS
    g0 = c0 + 2 * c_conv
    cast = lambda a: a.astype(bf16)
    row = lambda a: a[None, :].astype(f32)
    h, logf = normed if normed is not None else _norm(x_segs, g, *_forget_weights(w_in, b_in), tm)

    ka_p, va_p, kb_p, vb_p, ka_s, va_s, kb_s, vb_s = states
    w_qkv, b_qkv = cast(w_in[:, :n_qkv]), row(b_in[:n_qkv])
    tile = functools.partial(_qkv_tile, h, w_qkv, b_qkv, layer=layer, m_p=m_p, t_p=t_p, a_keep=a_keep, tm=tm)
    assert SEG_PER_TILE == 2
    (qa, ka16), (ka_p, ka_s) = tile(0, [None, (ka_p, ka_s, True)])
    (va16, qb), (va_p, va_s) = tile(1, [(va_p, va_s, True), None])
    (kb16, vb16), (kb_p, kb_s, vb_p, vb_s) = tile(2, [(kb_p, kb_s, False), (vb_p, vb_s, False)])
    u = _glu(h, cast(w_in[:, c0:g0]), row(b_in[c0:g0]), tm)
    gates = _gates(h, cast(w_in[:, g0:]), row(b_in[g0:]), tm)
    return (qa, ka16, va16, qb, kb16, vb16, u, gates, logf), [ka_p, va_p, kb_p, vb_p, ka_s, va_s, kb_s, vb_s]


TC = 512


def _cumsum_kernel(x_ref, o_ref, carry):
    @pl.when(pl.program_id(1) == 0)
    def _():
        carry[...] = jnp.zeros_like(carry)

    blk = x_ref[0]
    r = lax.broadcasted_iota(jnp.int32, (TC, TC), 0)
    c = lax.broadcasted_iota(jnp.int32, (TC, TC), 1)
    tri = jnp.where(r <= c, 1.0, 0.0).astype(bf16)
    cs = carry[:, 0:1]
    rest = blk
    for _ in range(F_PARTS):
        part = rest.astype(bf16)
        rest = rest - part.astype(f32)
        cs = cs + jnp.dot(part, tri, preferred_element_type=f32)
    o_ref[0] = cs
    carry[...] = jnp.broadcast_to(cs[:, TC - 1:TC], carry.shape)


def _cumsum_time(x):
    nb, h, t = x.shape
    tp = -(-t // TC) * TC
    xp = jnp.pad(x, ((0, 0), (0, 0), (0, tp - t)))
    out = pl.pallas_call(
        _cumsum_kernel,
        grid=(nb, tp // TC),
        in_specs=[pl.BlockSpec((1, h, TC), lambda b, k: (b, 0, k))],
        out_specs=pl.BlockSpec((1, h, TC), lambda b, k: (b, 0, k)),
        out_shape=jax.ShapeDtypeStruct((nb, h, tp), f32),
        scratch_shapes=[pltpu.VMEM((h, LANES), f32)],
        compiler_params=_params(("arbitrary", "arbitrary")),
        name="cumsum",
    )(xp)
    return out[:, :, :t]


F_PARTS = 3


def _spare_base(h):
    return HEAD_DIM * (1 - h % 2)


def _fox_expand_kernel(kinds, *refs):
    n_kind = len(kinds)
    n_place = sum(kind != "v" for kind in kinds)
    x_refs, f_ref = refs[:n_kind], refs[n_kind]
    place_refs = list(refs[n_kind + 1:n_kind + 1 + n_place])
    o_refs = refs[n_kind + 1 + n_place:]
    tm = x_refs[0].shape[0]
    lane = lax.broadcasted_iota(jnp.int32, (tm, LANES), 1)
    if n_place:
        rest = f_ref[...]
        stack = jnp.where(lane < (F_PARTS + 1) * N_HEADS, 1.0, 0.0)
        for n in range(F_PARTS):
            part = rest.astype(bf16).astype(f32)
            rest = rest - part
            stack = jnp.where((lane >= n * N_HEADS) & (lane < (n + 1) * N_HEADS), part, stack)
        stack = stack.astype(bf16)
    for kind, x_ref, o_ref in zip(kinds, x_refs, o_refs):
        if kind != "v":
            spare_all = jnp.dot(stack, place_refs.pop(0)[...], preferred_element_type=f32)
        for h in range(N_HEADS):
            pair = x_ref[:, (h // 2) * LANES:(h // 2 + 1) * LANES].astype(f32)
            if kind == "q":
                pair = pair * SCALE
            if kind == "v":
                spare = jnp.where(lane == _spare_base(h), 1.0, 0.0)
            else:
                spare = spare_all[:, h * LANES:(h + 1) * LANES]
            own = (lane < HEAD_DIM) if h % 2 == 0 else (lane >= HEAD_DIM)
            o_ref[h] = jnp.where(own, pair, spare).astype(o_ref.dtype)


def _fox_placement(kind):
    place = np.zeros((LANES, N_HEADS * LANES), np.float32)
    for h in range(N_HEADS):
        base = h * LANES + _spare_base(h)
        for n in range(F_PARTS):
            if kind == "q":
                place[n * N_HEADS + h, base + n] = 1.0
                place[F_PARTS * N_HEADS + h, base + F_PARTS + n] = 1.0
            else:
                place[F_PARTS * N_HEADS + h, base + n] = 1.0
                place[n * N_HEADS + h, base + F_PARTS + n] = -1.0
    return jnp.asarray(place, bf16)


def _fox_expand(kinds, xs, f, rows, tm, blk0=0):
    places = [_fox_placement(kind) for kind in kinds if kind != "v"]
    slab = jax.ShapeDtypeStruct((N_HEADS, rows, LANES), bf16)
    return pl.pallas_call(
        functools.partial(_fox_expand_kernel, tuple(kinds)),
        grid=(rows // tm,),
        in_specs=[pl.BlockSpec((tm, W_HEADS), lambda i: (blk0 + i, 0))] * len(kinds) +
                 [pl.BlockSpec((tm, LANES), lambda i: (i, 0))] +
                 [pl.BlockSpec((LANES, N_HEADS * LANES), lambda i: (0, 0))] * len(places),
        out_specs=[pl.BlockSpec((N_HEADS, tm, LANES), lambda i: (0, i, 0))] * len(kinds),
        out_shape=[slab] * len(kinds),
        compiler_params=_params(("arbitrary",)),
        name="fox_expand_" + "".join(kinds),
    )(*xs, jnp.tile(f, (1, LANES // N_HEADS)), *places)


FOX_HEADS_PER_TRIP = 2


def _fox_kernel(tq, tk, off, q_ref, k_ref, v_ref, o_ref, m_scr, acc_scr):
    i = pl.program_id(1)
    j = pl.program_id(2)

    @pl.when(j == 0)
    def _():
        m_scr[...] = jnp.full_like(m_scr, NEG_INF)
        acc_scr[...] = jnp.zeros_like(acc_scr)

    q_first = i * tq + off
    q_last = q_first + tq - 1
    k_first = j * tk
    k_last = k_first + tk - 1

    def body(row0, masked):
        rows = slice(row0, tq)
        if masked:
            kpos = k_first + lax.broadcasted_iota(jnp.int32, (tq - row0, tk), 1)
            qpos = q_first + row0 + lax.broadcasted_iota(jnp.int32, (tq - row0, tk), 0)
            vis = kpos <= qpos

        def head(h):
            s = lax.dot_general(q_ref[h, rows], k_ref[h], (((1,), (1,)), ((), ())), preferred_element_type=f32)
            if masked:
                s = jnp.where(vis, s, NEG_INF)
            m_old = m_scr[h, rows]
            m_new = jnp.maximum(m_old, jnp.max(s, axis=-1, keepdims=True))
            pr = jnp.exp(s - m_new[:, 0:1])
            pv = jnp.dot(pr.astype(bf16), v_ref[h], preferred_element_type=f32)
            acc_scr[h, rows] = jnp.exp(m_old - m_new) * acc_scr[h, rows] + pv
            m_scr[h, rows] = m_new

        def trip(g, carry):
            for n in range(FOX_HEADS_PER_TRIP):
                head(g * FOX_HEADS_PER_TRIP + n)
            return carry

        lax.fori_loop(0, N_HEADS // FOX_HEADS_PER_TRIP, trip, 0)

    pl.when(k_last <= q_first)(functools.partial(body, 0, False))
    if tq % tk == 0 and off % tk == 0:
        for c in range(tq // tk):
            pl.when(k_first == q_first + c * tk)(functools.partial(body, c * tk, True))
    else:
        pl.when((k_first <= q_last) & (k_last > q_first))(functools.partial(body, 0, True))

    @pl.when(j == pl.num_programs(2) - 1)
    def _():
        lane = lax.broadcasted_iota(jnp.int32, (tq, LANES), 1)
        for p in range(N_HEADS // 2):
            even = acc_scr[2 * p]
            odd = acc_scr[2 * p + 1]
            even = even / even[:, _spare_base(0):_spare_base(0) + 1]
            odd = odd / odd[:, _spare_base(1):_spare_base(1) + 1]
            o_ref[:, p * LANES:(p + 1) * LANES] = jnp.where(lane < HEAD_DIM, even, odd).astype(o_ref.dtype)


def _fox_attention(q, k, v, nb, t_q, t_k, tq, tk):
    nq, nk = t_q // tq, t_k // tk
    off = t_k - t_q

    def last_k(i):
        return jnp.minimum((i * tq + tq - 1 + off) // tk, nk - 1)

    kv_spec = pl.BlockSpec((N_HEADS, tk, LANES), lambda b, i, j: (0, b * nk + jnp.minimum(j, last_k(i)), 0))
    return pl.pallas_call(
        functools.partial(_fox_kernel, tq, tk, off),
        grid=(nb, nq, nk),
        in_specs=[pl.BlockSpec((N_HEADS, tq, LANES), lambda b, i, j: (0, b * nq + i, 0)), kv_spec, kv_spec],
        out_specs=pl.BlockSpec((tq, W_HEADS), lambda b, i, j: (b * nq + i, 0)),
        out_shape=jax.ShapeDtypeStruct((nb * t_q, W_HEADS), bf16),
        scratch_shapes=[pltpu.VMEM((N_HEADS, tq, LANES), f32), pltpu.VMEM((N_HEADS, tq, LANES), f32)],
        compiler_params=_params(("arbitrary", "arbitrary", "arbitrary")),
        name="fox",
    )(q, k, v)


def _band_kernel(rows, gq, wk, has_prev, *refs):
    if has_prev:
        q_ref, kp_ref, kc_ref, vp_ref, vc_ref, bias_ref, o_ref, k_scr, v_scr = refs
        k_scr[0:WINDOW_A] = kp_ref[...]
        k_scr[WINDOW_A:WINDOW_A + rows] = kc_ref[...]
        v_scr[0:WINDOW_A] = vp_ref[...]
        v_scr[WINDOW_A:WINDOW_A + rows] = vc_ref[...]
        k_src, v_src = k_scr, v_scr
    else:
        q_ref, k_src, v_src, bias_ref, o_ref = refs
    i = pl.program_id(1)
    lane = lax.broadcasted_iota(jnp.int32, (gq, LANES), 1)
    low = lane < HEAD_DIM

    def attend(before_start):
        for g in range(rows // gq):
            r0 = g * gq
            if before_start:
                key_pos = (i - 1) * WINDOW_A + r0 + lax.broadcasted_iota(jnp.int32, (gq, wk), 1)
                vis = key_pos >= 0
            for p in range(N_HEADS // 2):
                cols = slice(p * LANES, (p + 1) * LANES)
                q2 = q_ref[r0:r0 + gq, cols] * SCALE
                kw = k_src[r0:r0 + wk, cols]
                vw = v_src[r0:r0 + wk, cols]
                outs = []
                for half in range(2):
                    h = 2 * p + half
                    qm = jnp.where(low if half == 0 else ~low, q2, jnp.zeros_like(q2))
                    s = lax.dot_general(qm, kw, (((1,), (1,)), ((), ())), preferred_element_type=f32)
                    s = s + bias_ref[h]
                    if before_start:
                        s = jnp.where(vis, s, NEG_INF)
                    m = jnp.max(s, axis=-1, keepdims=True)
                    pr = jnp.exp(s - m)
                    l = jnp.sum(pr, axis=-1, keepdims=True)
                    pv = jnp.dot(pr.astype(bf16), vw, preferred_element_type=f32)
                    outs.append(pv / l)
                o_ref[r0:r0 + gq, cols] = jnp.where(low, outs[0], outs[1]).astype(o_ref.dtype)

    if has_prev:
        pl.when(i == 0)(functools.partial(attend, True))
        pl.when(i > 0)(functools.partial(attend, False))
    else:
        attend(False)


def _rel_bias_table(rel_bias, gq, wk, q_shift):
    period = wk + gq
    j = np.arange(period)
    k = np.where(j < wk, j, j - period)
    line = rel_bias.astype(f32)[:, np.clip(q_shift - k, -REL_CLIP, REL_CLIP) + REL_CLIP]
    tiled = jnp.tile(line, (1, gq))[:, :gq * (period - 1)]
    return tiled.reshape(-1, gq, period - 1)[:, :, :wk]


def _band_bias(rel_bias, gq, wk, q_shift):
    r = np.arange(gq)[:, None]
    s = np.arange(wk)[None, :]
    band0 = (r // CHUNK) * CHUNK + q_shift - WINDOW_A
    ok = (s >= band0) & (s < band0 + WINDOW_A + CHUNK)
    return jnp.where(ok[None], _rel_bias_table(rel_bias, gq, wk, q_shift), NEG_INF)


def _band_attention_prompt(q, k, v, rel_bias, nb, t, gq):
    rows = WINDOW_A
    wk = WINDOW_A + gq
    n_steps = t // rows
    bias = _band_bias(rel_bias, gq, wk, WINDOW_A)
    cur = pl.BlockSpec((rows, W_HEADS), lambda b, i: (b * n_steps + i, 0))
    prev = pl.BlockSpec((rows, W_HEADS), lambda b, i: (b * n_steps + jnp.maximum(i - 1, 0), 0))
    return pl.pallas_call(
        functools.partial(_band_kernel, rows, gq, wk, True),
        grid=(nb, n_steps),
        in_specs=[cur, prev, cur, prev, cur,
                  pl.BlockSpec((N_HEADS, gq, wk), lambda b, i: (0, 0, 0))],
        out_specs=cur,
        out_shape=jax.ShapeDtypeStruct((nb * t, W_HEADS), bf16),
        scratch_shapes=[pltpu.VMEM((2 * rows, W_HEADS), bf16), pltpu.VMEM((2 * rows, W_HEADS), bf16)],
        compiler_params=_params(("arbitrary", "arbitrary")),
        name="band_prompt",
    )(q, k, k, v, v, bias)


def _band_attention_sample(q, kk, vv, rel_bias, nb, s_new, l_cache, q_blk0):
    wk = l_cache + s_new
    bias = _rel_bias_table(rel_bias, s_new, wk, l_cache)
    return pl.pallas_call(
        functools.partial(_band_kernel, s_new, s_new, wk, False),
        grid=(nb, 1),
        in_specs=[pl.BlockSpec((s_new, W_HEADS), lambda b, i: (q_blk0 + b, 0)),
                  pl.BlockSpec((wk, W_HEADS), lambda b, i: (b, 0)),
                  pl.BlockSpec((wk, W_HEADS), lambda b, i: (b, 0)),
                  pl.BlockSpec((N_HEADS, s_new, wk), lambda b, i: (0, 0, 0))],
        out_specs=pl.BlockSpec((s_new, W_HEADS), lambda b, i: (b, 0)),
        out_shape=jax.ShapeDtypeStruct((nb * s_new, W_HEADS), bf16),
        compiler_params=_params(("arbitrary", "arbitrary")),
        name="band_sample",
    )(q, kk, vv, bias)


CONV_STRIP = 32


def _conv_kernel(tt, n_slab, init_ref, u_ref, w_ref, cb_ref, g_ref, b_ref, o_ref, ubuf, acc_scr):
    halo = CONV_HALO * n_slab

    @pl.when(pl.program_id(1) == 0)
    def _():
        ubuf[0:halo] = init_ref[0]

    ubuf[halo:halo + tt * n_slab] = u_ref[...]
    rs = min(CONV_STRIP, tt)
    first = CONV_HALO - (CONV_W - 1)

    def per_step(slab):
        return jnp.broadcast_to(slab[None], (rs, n_slab, LANES)).reshape(rs * n_slab, LANES)

    for s in range(tt // rs):
        acc = per_step(cb_ref[...])
        for j in range(CONV_W):
            r0 = (s * rs + first + j) * n_slab
            acc = acc + per_step(w_ref[j * n_slab:(j + 1) * n_slab, :]) * ubuf[r0:r0 + rs * n_slab, :]
        acc_scr[...] = acc
        rows = jnp.concatenate([acc_scr[pl.ds(q, rs, stride=n_slab), :] for q in range(n_slab)], axis=1)
        mu = jnp.mean(rows, axis=-1, keepdims=True)
        cen = rows - mu
        var = jnp.mean(cen * cen, axis=-1, keepdims=True)
        y = cen * lax.rsqrt(var + EPS) * g_ref[...] + b_ref[...]
        o_ref[s * rs:(s + 1) * rs, :] = (y * _sigmoid(y)).astype(o_ref.dtype)
    if tt >= CONV_HALO:
        ubuf[0:halo] = ubuf[tt * n_slab:tt * n_slab + halo]


def _conv_module(u_slab, init, conv_w, conv_b, ln_g, ln_b, nb, t, tt, blk0):
    c = conv_w.shape[1]
    n_slab = c // LANES
    n_t = t // tt
    vec = pl.BlockSpec((1, c), lambda b, i: (0, 0))
    return pl.pallas_call(
        functools.partial(_conv_kernel, tt, n_slab),
        grid=(nb, n_t),
        in_specs=[pl.BlockSpec((1, CONV_HALO * n_slab, LANES), lambda b, i: (b, 0, 0)),
                  pl.BlockSpec((tt * n_slab, LANES), lambda b, i: (blk0 + b * n_t + i, 0)),
                  pl.BlockSpec((CONV_W * n_slab, LANES), lambda b, i: (0, 0)),
                  pl.BlockSpec((n_slab, LANES), lambda b, i: (0, 0)), vec, vec],
        out_specs=pl.BlockSpec((tt, c), lambda b, i: (b * n_t + i, 0)),
        out_shape=jax.ShapeDtypeStruct((nb * t, c), bf16),
        scratch_shapes=[pltpu.VMEM(((CONV_HALO + tt) * n_slab, LANES), f32),
                        pltpu.VMEM((min(CONV_STRIP, tt) * n_slab, LANES), f32)],
        compiler_params=_params(("arbitrary", "arbitrary")),
        name="conv",
    )(init.reshape(nb, CONV_HALO * n_slab, LANES), u_slab, conv_w.reshape(CONV_W * n_slab, LANES),
      conv_b.reshape(n_slab, LANES), ln_g, ln_b)


def _mix_kernel(bounds, ya_p, ya_s, yb_p, yb_s, c_p, c_s, gates, pa, pb, pc, o_ref):
    d = o_ref.shape[1]

    def go(k):
        ya, yb, c = ((ya_p, yb_p, c_p), (ya_s, yb_s, c_s))[k]
        a = jnp.dot(ya[...], pa[...], preferred_element_type=f32)
        mixed = gates[:, 0:d].astype(f32) * a
        b = jnp.dot(yb[...], pb[...], preferred_element_type=f32)
        mixed = mixed + gates[:, d:2 * d].astype(f32) * b
        cc = jnp.dot(c[...], pc[...], preferred_element_type=f32)
        mixed = mixed + gates[:, 2 * d:3 * d].astype(f32) * cc
        o_ref[...] = mixed.astype(o_ref.dtype)

    _when_segment(pl.program_id(0), bounds, go)


def _mix(ya_p, ya_s, yb_p, yb_s, c_p, c_s, gates, pa, pb, pc, tm):
    m, d3 = gates.shape
    d = d3 // 3
    n_p, n_s = ya_p.shape[0] // tm, ya_s.shape[0] // tm
    bounds = (0, n_p, n_p + n_s)
    c_conv = c_p.shape[1]
    const = lambda shape: pl.BlockSpec(shape, lambda i: (0, 0))
    return pl.pallas_call(
        functools.partial(_mix_kernel, bounds),
        grid=(m // tm,),
        in_specs=[_seg_spec((tm, W_HEADS), 0, n_p), _seg_spec((tm, W_HEADS), n_p, n_s),
                  _seg_spec((tm, W_HEADS), 0, n_p), _seg_spec((tm, W_HEADS), n_p, n_s),
                  _seg_spec((tm, c_conv), 0, n_p), _seg_spec((tm, c_conv), n_p, n_s),
                  pl.BlockSpec((tm, d3), lambda i: (i, 0)),
                  const((W_HEADS, d)), const((W_HEADS, d)), const((c_conv, d))],
        out_specs=pl.BlockSpec((tm, d), lambda i: (i, 0)),
        out_shape=jax.ShapeDtypeStruct((m, d), bf16),
        compiler_params=_params(("arbitrary",)),
        name="mix",
    )(ya_p, ya_s, yb_p, yb_s, c_p, c_s, gates, pa, pb, pc)


def _route(logits):
    shape = logits.shape
    lane = lax.broadcasted_iota(jnp.int32, shape, 1)
    lane_f = lane.astype(f32)
    big = float(LANES)
    gl = jnp.where(lane < N_GROUPS, logits, -jnp.inf)
    g_max = jnp.max(gl, axis=-1, keepdims=True)
    g_idx = jnp.min(jnp.where(gl == g_max, lane_f, big), axis=-1, keepdims=True)
    g_sum = jnp.sum(jnp.exp(gl - g_max), axis=-1, keepdims=True)
    g_w = 1.0 / g_sum
    lo = N_GROUPS + g_idx * EXPERTS_PER_GROUP
    el = jnp.where((lane_f >= lo) & (lane_f < lo + EXPERTS_PER_GROUP), logits, -jnp.inf)
    m1 = jnp.max(el, axis=-1, keepdims=True)
    i1 = jnp.min(jnp.where(el == m1, lane_f, big), axis=-1, keepdims=True)
    el2 = jnp.where(lane_f == i1, -jnp.inf, el)
    m2 = jnp.max(el2, axis=-1, keepdims=True)
    i2 = jnp.min(jnp.where(el2 == m2, lane_f, big), axis=-1, keepdims=True)
    e21 = jnp.exp(m2 - m1)
    den = 1.0 + e21
    w1 = g_w * (1.0 / den)
    w2 = g_w * (e21 / den)
    eid = jnp.where(lane == 0, i1 - N_GROUPS, jnp.where(lane == 1, i2 - N_GROUPS, 0.0)).astype(jnp.int32)
    wgt = jnp.where(lane == 0, w1, jnp.where(lane == 1, w2, 0.0))
    return eid, wgt


def _outproj_kernel(n_slab, n_seg, bounds, *refs):
    xs = refs[:n_seg]
    mixed, wo, g2, wr, br, xo, h2o, eid_o, wgt_o = refs[n_seg:]
    tm = mixed.shape[0]
    i = pl.program_id(0)
    x = xs[0][...]
    for k in range(1, n_seg):
        x = jnp.where(i >= bounds[k], xs[k][...], x)
    xn = x + jnp.dot(mixed[...], wo[...], preferred_element_type=f32)
    xo[...] = xn
    h2 = _rms(xn, g2[...])
    n_word = n_slab // 2
    for s in range(n_word):
        h2o[pl.ds(s, tm, stride=n_word), :] = _pack_bf16_pair(h2[:, s * LANES:(s + 1) * LANES],
                                                              h2[:, (n_word + s) * LANES:(n_word + s + 1) * LANES])
    h_hi = h2.astype(bf16)
    h_lo = (h2 - h_hi.astype(f32)).astype(bf16)
    hi = jnp.dot(h_hi, wr[...], preferred_element_type=f32)
    lo = jnp.dot(h_lo, wr[:, :LANES], preferred_element_type=f32)
    logits = hi[:, :LANES] + (hi[:, LANES:] + lo) + br[...]
    eid, wgt = _route(logits)
    eid_o[...] = eid
    wgt_o[...] = wgt


def _outproj(mixed, x_segs, wo, g2, wr, br, tm):
    m, d = mixed.shape
    n_slab = d // LANES
    counts = [a.shape[0] // tm for a in x_segs]
    bounds = [0]
    for cnt in counts:
        bounds.append(bounds[-1] + cnt)
    const = lambda shape: pl.BlockSpec(shape, lambda i: (0, 0))
    row = lambda w: pl.BlockSpec((tm, w), lambda i: (i, 0))
    sds = jax.ShapeDtypeStruct
    return pl.pallas_call(
        functools.partial(_outproj_kernel, n_slab, len(x_segs), tuple(bounds)),
        grid=(m // tm,),
        in_specs=[_seg_spec((tm, d), bounds[k], counts[k]) for k in range(len(x_segs))] +
                 [row(d), const((d, d)), const((1, d)), const((d, 2 * LANES)), const((1, LANES))],
        out_specs=[row(d), pl.BlockSpec((tm * n_slab // 2, LANES), lambda i: (i, 0)), row(LANES), row(LANES)],
        out_shape=[sds((m, d), f32), sds((m * n_slab // 2, LANES), jnp.uint32), sds((m, LANES), jnp.int32),
                   sds((m, LANES), f32)],
        compiler_params=_params(("arbitrary",)),
        name="outproj",
    )(*x_segs, mixed, wo, g2, wr, br)


TB = 256
TD = 256


def _plan(eid):
    flat_e = eid.reshape(-1)
    n_assign = flat_e.shape[0]
    onehot = (flat_e[:, None] == jnp.arange(N_EXPERTS, dtype=jnp.int32)[None, :]).astype(jnp.int32)
    csum = jnp.cumsum(onehot, axis=0)
    counts = csum[-1]
    rank = jnp.sum(onehot * csum, axis=1) - 1
    n_blk_e = (counts + TB - 1) // TB
    blk_end = jnp.cumsum(n_blk_e)
    blk_start = blk_end - n_blk_e
    dest = blk_start[flat_e] * TB + rank
    n_blocks = -(-n_assign // TB) + N_EXPERTS
    blk_ids = jnp.arange(n_blocks, dtype=jnp.int32)
    blk_expert = jnp.minimum(jnp.sum((blk_end[None, :] <= blk_ids[:, None]).astype(jnp.int32), axis=1),
                             N_EXPERTS - 1)
    last_blk = jnp.where(n_blk_e > 0, blk_end - 1, -1).astype(jnp.int32)
    return dest.astype(jnp.int32), blk_expert, blk_end[-1:].astype(jnp.int32), last_blk, n_blocks


DMA_UNROLL = 16


def _issue_rows(n, copy):
    per_trip = DMA_UNROLL // TOP_K

    def trip(t, carry):
        for r in range(per_trip):
            for k in range(TOP_K):
                copy(t * per_trip + r, k).start(priority=k)
        return carry

    lax.fori_loop(0, n // DMA_UNROLL, trip, 0)


def _dispatch_kernel(n_slab, n_blocks, dest_ref, last_ref, nu_ref, h_ref, xs_ref, zero_scr, stage, sems, zero_sem):
    i = pl.program_id(0)
    last = pl.num_programs(0) - 1
    n = dest_ref.shape[2]
    blk_rows = TB * n_slab
    slot = i % 2

    def wait_slot(s):
        for _ in range(TOP_K):
            pltpu.make_async_copy(stage.at[s], xs_ref.at[pl.ds(0, (n // TOP_K) * n_slab), :], sems.at[s]).wait()

    @pl.when(i == 0)
    def _():
        zero_scr[...] = jnp.zeros_like(zero_scr)

        def zero_block(b):
            rows = pl.ds(pl.multiple_of(b * blk_rows, blk_rows), blk_rows)
            return pltpu.make_async_copy(zero_scr, xs_ref.at[rows, :], zero_sem)

        def over_blocks(act):
            for e in range(N_EXPERTS):
                pl.when(last_ref[e] >= 0)(lambda e=e: act(zero_block(last_ref[e])))
            lax.fori_loop(nu_ref[0], n_blocks, lambda b, c: (act(zero_block(b)), c)[1], 0)

        over_blocks(lambda cp: cp.start())
        over_blocks(lambda cp: cp.wait())

    pl.when(i >= 2)(lambda: wait_slot(slot))
    stage[slot] = h_ref[...]

    def copy(tok, k):
        src = stage.at[slot, pl.ds(pl.multiple_of(tok * n_slab, n_slab), n_slab), :]
        dst = xs_ref.at[pl.ds(pl.multiple_of(dest_ref[0, 0, tok * TOP_K + k] * n_slab, n_slab), n_slab), :]
        return pltpu.make_async_copy(src, dst, sems.at[slot])

    _issue_rows(n, copy)

    @pl.when(i == last)
    def _():
        pl.when(i >= 1)(lambda: wait_slot(1 - slot))
        wait_slot(slot)


def _dispatch(h2_slab, dest, last_blk, n_used, n_blocks, n_slab, td):
    m = h2_slab.shape[0] // n_slab
    n_steps = m // td
    dest3 = dest.reshape(n_steps, 1, td * TOP_K)
    smem = pl.BlockSpec(memory_space=pltpu.SMEM)
    return pl.pallas_call(
        functools.partial(_dispatch_kernel, n_slab, n_blocks),
        grid=(n_steps,),
        in_specs=[pl.BlockSpec((1, 1, td * TOP_K), lambda i: (i, 0, 0), memory_space=pltpu.SMEM),
                  smem, smem,
                  pl.BlockSpec((td * n_slab, LANES), lambda i: (i, 0))],
        out_specs=pl.BlockSpec(memory_space=pl.ANY),
        out_shape=jax.ShapeDtypeStruct((n_blocks * TB * n_slab, LANES), h2_slab.dtype),
        scratch_shapes=[pltpu.VMEM((TB * n_slab, LANES), h2_slab.dtype),
                        pltpu.VMEM((2, td * n_slab, LANES), h2_slab.dtype),
                        pltpu.SemaphoreType.DMA((2,)), pltpu.SemaphoreType.DMA(())],
        compiler_params=_params(("arbitrary",)),
        name="dispatch",
    )(dest3, last_blk, n_used, h2_slab)


def _expert_kernel(n_slab, be_ref, nu_ref, xs_ref, wg_ref, wu_ref, wd_ref, o_ref, wgu_scr, wd_scr):
    i = pl.program_id(0)
    de = wd_ref.shape[1]

    @pl.when(i < nu_ref[0])
    def _():
        @pl.when((i == 0) | (be_ref[i] != be_ref[jnp.maximum(i - 1, 0)]))
        def _():
            wgu_scr[:, :de] = wg_ref[0].astype(bf16)
            wgu_scr[:, de:] = wu_ref[0].astype(bf16)
            wd_scr[...] = wd_ref[0].astype(bf16)

        n_word = n_slab // 2
        halves = [_unpack_bf16_pair(xs_ref[pl.ds(s, TB, stride=n_word), :]) for s in range(n_word)]
        x = jnp.concatenate([lo for lo, _ in halves] + [hi for _, hi in halves], axis=1)
        gu = jnp.dot(x.astype(bf16), wgu_scr[...], preferred_element_type=f32)
        g = gu[:, :de]
        hmid = (g * _sigmoid(g)) * gu[:, de:]
        y = jnp.dot(hmid.astype(bf16), wd_scr[...], preferred_element_type=f32)
        for s in range(n_word):
            o_ref[pl.ds(s, TB, stride=n_word), :] = _pack_bf16_pair(y[:, s * LANES:(s + 1) * LANES],
                                                                   y[:, (n_word + s) * LANES:(n_word + s + 1) * LANES])


def _experts(xs, blk_expert, n_used, w_gate, w_up, w_down, layer, n_blocks, n_slab):
    d, de = w_gate.shape[2], w_gate.shape[3]
    by_expert = lambda i, be, nu: (layer, be[jnp.minimum(i, nu[0] - 1)], 0, 0)
    grid_spec = pltpu.PrefetchScalarGridSpec(
        num_scalar_prefetch=2,
        grid=(n_blocks,),
        in_specs=[pl.BlockSpec((TB * n_slab // 2, LANES), lambda i, be, nu: (jnp.minimum(i, nu[0] - 1), 0)),
                  pl.BlockSpec((None, 1, d, de), by_expert), pl.BlockSpec((None, 1, d, de), by_expert),
                  pl.BlockSpec((None, 1, de, d), by_expert)],
        out_specs=pl.BlockSpec((TB * n_slab // 2, LANES), lambda i, be, nu: (jnp.minimum(i, nu[0] - 1), 0)),
        scratch_shapes=[pltpu.VMEM((d, 2 * de), bf16), pltpu.VMEM((de, d), bf16)],
    )
    return pl.pallas_call(
        functools.partial(_expert_kernel, n_slab),
        grid_spec=grid_spec,
        out_shape=jax.ShapeDtypeStruct(xs.shape, xs.dtype),
        input_output_aliases={2: 0},
        compiler_params=_params(("arbitrary",)),
        name="experts",
    )(blk_expert, n_used, xs, w_gate, w_up, w_down)


def _combine_kernel(n_slab, final, norm_next, bounds, dest_ref, next_ref, x_ref, wgt_ref, g_ref, ys_ref, *rest):
    if norm_next:
        (gn_ref, wf_ref, bf_ref), rest = rest[:3], rest[3:]
        outs, (hn_ref, logf_ref, gbuf, sems) = rest[:-4], rest[-4:]
    else:
        outs, (gbuf, sems) = rest[:-2], rest[-2:]
    i = pl.program_id(0)
    n = dest_ref.shape[2]
    tm = x_ref.shape[0]
    slot = i % 2
    n_word = n_slab // 2

    def gather(idx_ref, s):
        def copy(tok, k):
            src = ys_ref.at[pl.ds(pl.multiple_of(idx_ref[0, 0, tok * TOP_K + k] * n_word, n_word), n_word), :]
            dst = gbuf.at[s, pl.ds(pl.multiple_of((k * tm + tok) * n_word, n_word), n_word), :]
            return pltpu.make_async_copy(src, dst, sems.at[s])
        _issue_rows(n, copy)

    pl.when(i == 0)(lambda: gather(dest_ref, slot))
    pl.when(i + 1 < pl.num_programs(0))(lambda: gather(next_ref, 1 - slot))
    pltpu.make_async_copy(ys_ref.at[pl.ds(0, n * n_word), :], gbuf.at[slot], sems.at[slot]).wait()

    ys = []
    for k in range(TOP_K):
        halves = [_unpack_bf16_pair(gbuf[slot, pl.ds(k * tm * n_word + s, tm, stride=n_word), :])
                  for s in range(n_word)]
        y = jnp.concatenate([lo for lo, _ in halves] + [hi for _, hi in halves], axis=1)
        ys.append(y * wgt_ref[:, k:k + 1])
    x = x_ref[...] + (ys[0] + ys[1])
    if final:
        x = _rms(x, g_ref[...])
    if norm_next:
        hn_ref[...] = _rms(x, gn_ref[...]).astype(bf16)
        logf_ref[...] = _log_forget(hn_ref[...], wf_ref[...], bf_ref[...])

    def store(k):
        outs[k][...] = x

    _when_segment(i, bounds, store)


def _combine(x, ys, dest, wgt, g, n_slab, final, seg_rows, norm_next=None):
    m, d = x.shape
    n_steps = m // TD
    dest3 = dest.reshape(n_steps, 1, TD * TOP_K)
    counts = [r // TD for r in seg_rows]
    bounds = [0]
    for cnt in counts:
        bounds.append(bounds[-1] + cnt)
    out_specs = [_seg_spec((TD, d), bounds[k], counts[k]) for k in range(len(seg_rows))]
    out_shape = [jax.ShapeDtypeStruct((r, d), f32) for r in seg_rows]
    idx_block = (1, 1, TD * TOP_K)
    extra_in, extra_specs = [], []
    if norm_next is not None:
        extra_in = list(norm_next)
        extra_specs = [pl.BlockSpec(a.shape, lambda i: (0, 0)) for a in norm_next]
        out_specs += [pl.BlockSpec((TD, d), lambda i: (i, 0)), pl.BlockSpec((TD, N_HEADS), lambda i: (i, 0))]
        out_shape += [jax.ShapeDtypeStruct((m, d), bf16), jax.ShapeDtypeStruct((m, N_HEADS), f32)]
    return pl.pallas_call(
        functools.partial(_combine_kernel, n_slab, final, norm_next is not None, tuple(bounds)),
        grid=(n_steps,),
        in_specs=[pl.BlockSpec(idx_block, lambda i: (i, 0, 0), memory_space=pltpu.SMEM),
                  pl.BlockSpec(idx_block, lambda i: (jnp.minimum(i + 1, n_steps - 1), 0, 0),
                               memory_space=pltpu.SMEM),
                  pl.BlockSpec((TD, d), lambda i: (i, 0)),
                  pl.BlockSpec((TD, LANES), lambda i: (i, 0)),
                  pl.BlockSpec((1, d), lambda i: (0, 0)),
                  pl.BlockSpec(memory_space=pl.ANY)] + extra_specs,
        out_specs=out_specs, out_shape=out_shape,
        scratch_shapes=[pltpu.VMEM((2, TD * TOP_K * n_slab // 2, LANES), jnp.uint32),
                        pltpu.SemaphoreType.DMA((2,))],
        compiler_params=_params(("arbitrary",)),
        name="combine",
    )(dest3, dest3, x, wgt, g, ys, *extra_in)


def kernel(x_prompt, x_sample, cache_a_k, cache_a_v, cache_b_k, cache_b_v, cache_b_logf, state_conv, norm_mix_g, w_in, b_in, rel_bias, conv_w, conv_b, conv_ln_g, conv_ln_b, w_proj_a, w_proj_b, w_proj_c, w_out, norm_ffn_g, w_router_group, b_router_group, w_router_expert, b_router_expert, w_e_gate, w_e_up, w_e_down, norm_final_g):
    nb_p, t_p, d = x_prompt.shape
    nb_s, t_s, _ = x_sample.shape
    depth = w_in.shape[0]
    past = cache_b_k.shape[2]
    a_rows = cache_a_k.shape[2]
    m_p, m_s = nb_p * t_p, nb_s * t_s
    m = m_p + m_s
    c_conv = d // 2
    n_slab = d // LANES
    tm = _row_tile(np.gcd(m_p, m_s), 512)
    tm_mix = _row_tile(np.gcd(m_p, m_s), 256)
    assert m_p % TD == 0 and m_s % TD == 0 and t_s % 16 == 0 and m_p % t_s == 0

    a_keep = min(WINDOW_A, t_p)
    x_segs = [x_prompt.reshape(m_p, d), x_sample.reshape(m_s, d)]
    kv_states = [jnp.zeros((depth, rows * N_HEADS, HEAD_DIM), f32)
                 for rows in (nb_p * a_keep, nb_p * a_keep, m_p, m_p, m_s, m_s, m_s, m_s)]
    p_states, s_states = [], []
    normed = None
    for l in range(depth):
        (qa, ka16, va16, qb, kb16, vb16, u, gates, logf), kv_states = _inproj(
            x_segs, norm_mix_g[l][None, :], w_in[l], b_in[l], kv_states, l, m_p, t_p, a_keep, tm, normed)

        ya_p = _band_attention_prompt(qa, ka16, va16, rel_bias[l], nb_p, t_p, 4 * CHUNK)
        kk = jnp.concatenate([cache_a_k[l].reshape(nb_s, a_rows, W_HEADS).astype(bf16),
                              ka16[m_p:].reshape(nb_s, t_s, W_HEADS)], axis=1).reshape(-1, W_HEADS)
        vv = jnp.concatenate([cache_a_v[l].reshape(nb_s, a_rows, W_HEADS).astype(bf16),
                              va16[m_p:].reshape(nb_s, t_s, W_HEADS)], axis=1).reshape(-1, W_HEADS)
        ya_s = _band_attention_sample(qa, kk, vv, rel_bias[l], nb_s, t_s, a_rows, m_p // t_s)

        logf_p = logf[:m_p].reshape(nb_p, t_p, N_HEADS)
        logf_s = logf[m_p:].reshape(nb_s, t_s, N_HEADS)
        cum_p = _cumsum_time(logf_p.transpose(0, 2, 1))
        f_p = cum_p.transpose(0, 2, 1).reshape(m_p, N_HEADS)
        yb_p = _fox_attention(*_fox_expand("qkv", (qb, kb16, vb16), f_p, m_p, tm),
                              nb_p, t_p, t_p, _row_tile(t_p, 2048), _row_tile(t_p, 512))
        cum_s = _cumsum_time(jnp.concatenate([cache_b_logf[l].astype(f32), logf_s], axis=1).transpose(0, 2, 1))
        t_ks = past + t_s
        kk = jnp.concatenate([cache_b_k[l].reshape(nb_s, past, W_HEADS).astype(bf16),
                              kb16[m_p:].reshape(nb_s, t_s, W_HEADS)], axis=1).reshape(-1, W_HEADS)
        vv = jnp.concatenate([cache_b_v[l].reshape(nb_s, past, W_HEADS).astype(bf16),
                              vb16[m_p:].reshape(nb_s, t_s, W_HEADS)], axis=1).reshape(-1, W_HEADS)
        f_ks = cum_s.transpose(0, 2, 1)
        yb_s = _fox_attention(
            *_fox_expand("q", (qb,), f_ks[:, past:].reshape(m_s, N_HEADS), m_s, t_s, blk0=m_p // t_s),
            *_fox_expand("kv", (kk, vv), f_ks.reshape(nb_s * t_ks, N_HEADS), nb_s * t_ks, t_ks),
            nb_s, t_s, t_ks, t_s, t_ks)

        conv_args = (conv_w[l], conv_b[l][None, :], conv_ln_g[l][None, :], conv_ln_b[l][None, :])
        c_p = _conv_module(u, jnp.zeros((nb_p, CONV_HALO, c_conv), f32), *conv_args,
                           nb_p, t_p, _row_tile(t_p, 256), 0)
        init_s = jnp.pad(state_conv[l], ((0, 0), (CONV_HALO - (CONV_W - 1), 0), (0, 0)))
        c_s = _conv_module(u, init_s, *conv_args, nb_s, t_s, t_s, m_p // t_s)

        mixed = _mix(ya_p, ya_s, yb_p, yb_s, c_p, c_s, gates, w_proj_a[l].astype(bf16),
                     w_proj_b[l].astype(bf16), w_proj_c[l].astype(bf16), tm_mix)
        wr = jnp.pad(jnp.concatenate([w_router_group[l], w_router_expert[l]], axis=1),
                     ((0, 0), (0, LANES - N_GROUPS - N_EXPERTS)))
        wr_hi = wr.astype(bf16)
        wr_parts = jnp.concatenate([wr_hi, (wr - wr_hi.astype(f32)).astype(bf16)], axis=1)
        br = jnp.pad(jnp.concatenate([b_router_group[l], b_router_expert[l]]),
                     (0, LANES - N_GROUPS - N_EXPERTS))[None, :]
        x_mid, h2_slab, eid, wgt = _outproj(mixed, x_segs, w_out[l].astype(bf16), norm_ffn_g[l][None, :],
                                            wr_parts, br, tm_mix)

        dest, blk_expert, n_used, last_blk, n_blocks = _plan(eid[:, :TOP_K])
        xs = _dispatch(h2_slab, dest, last_blk, n_used, n_blocks, n_slab // 2, tm)
        ys = _experts(xs, blk_expert, n_used, w_e_gate, w_e_up, w_e_down, l, n_blocks, n_slab)
        final = l == depth - 1
        if final:
            x_segs = _combine(x_mid, ys, dest, wgt, norm_final_g[None, :], n_slab, True, (m_p, m_s))
        else:
            *x_segs, h_next, logf_next = _combine(
                x_mid, ys, dest, wgt, norm_final_g[None, :], n_slab, False, (m,),
                norm_next=(norm_mix_g[l + 1][None, :],) + _forget_weights(w_in[l + 1], b_in[l + 1]))
            normed = (h_next, logf_next)

        n_cs = c_conv // LANES
        u_p = jnp.stack([u[((b + 1) * t_p - (CONV_W - 1)) * n_cs:(b + 1) * t_p * n_cs] for b in range(nb_p)])
        u_p = u_p.reshape(nb_p, CONV_W - 1, c_conv)
        u_s = u[m_p * n_cs:].reshape(nb_s, t_s, c_conv)
        p_states.append((logf_p, u_p))
        s_states.append((logf_s, jnp.concatenate([state_conv[l], u_s], axis=1)[:, -(CONV_W - 1):]))

    y_prompt = x_segs[0].reshape(nb_p, t_p, d)
    y_sample = x_segs[1].reshape(nb_s, t_s, d)
    stack = lambda states, k: jnp.stack([st[k] for st in states], axis=0)
    heads = lambda a, nb, t: a.reshape(depth, nb, t, N_HEADS, HEAD_DIM)
    ka_p, va_p, kb_p, vb_p, ka_s, va_s, kb_s, vb_s = kv_states
    return (y_prompt, y_sample,
            heads(ka_p, nb_p, a_keep), heads(va_p, nb_p, a_keep), heads(kb_p, nb_p, t_p), heads(vb_p, nb_p, t_p),
            stack(p_states, 0), stack(p_states, 1),
            heads(ka_s, nb_s, t_s), heads(va_s, nb_s, t_s), heads(kb_s, nb_s, t_s), heads(vb_s, nb_s, t_s),
            stack(s_states, 0), stack(s_states, 1))
```
